```python
import math
import jax, jax.numpy as jnp
from jax import lax
import numpy as np

D_MODEL = 1024
BATCH = 2
SEQ = 8192
DEPTH = 4
DEC_BATCH = 32
DEC_SEQ = 8
PAST_LEN = 8192
PAGE_SIZE = 128

N_A_LAYERS = DEPTH // 2
N_B_LAYERS = DEPTH - N_A_LAYERS
HG_HEADS = 8
HG_DK = D_MODEL // HG_HEADS
HG_DV = D_MODEL // HG_HEADS
HG_CHUNK = 64
FOX_HEADS = 16
FOX_HEAD_DIM = D_MODEL // FOX_HEADS
Q_BLOCK = 128
N_EXPERTS = 16
N_GROUPS = 4
TOPK_GROUPS = 1
TOP_K = 2
D_EXPERT = 512
EPS = 1e-6
NEG_BIG = -1e30
TINY = 1e-30

kernel_name = 'yoco_hgrn2_fox_moe_step'


def rmsnorm(x, g):
    x32 = x.astype(jnp.float32)
    y = x32 * lax.rsqrt(jnp.mean(x32 * x32, axis=-1, keepdims=True) + EPS)
    return (y * g.astype(jnp.float32)).astype(x.dtype)


def ada_params(c, w, b, n):
    m = jax.nn.silu(c) @ w + b
    return [t[:, None, :] for t in jnp.split(m, n, axis=-1)]


def modulate(xn, shift, scale):
    return xn * (1 + scale) + shift


def hgrn2_lower_bounds(lb_logits):
    p = jax.nn.softmax(lb_logits.astype(jnp.float32), axis=0)
    return jnp.cumsum(p, axis=0) - p[0]


def hgrn2_scan(q, k, v, log_f, s0):
    b, t, h, _ = q.shape
    dv = v.shape[-1]
    c = t if t <= HG_CHUNK else math.gcd(t, HG_CHUNK)
    nc = t // c

    def chunks(a):
        return a.reshape(b, nc, c, h, a.shape[-1]).swapaxes(0, 1)

    causal = jnp.tril(jnp.ones((c, c), dtype=bool))[None, :, :, None, None]

    def step(s, inp):
        qc, kc, vc, gc = inp
        bc = jnp.cumsum(gc, axis=1)
        o_inter = jnp.einsum('bthk,bhkv->bthv', qc * jnp.exp(bc), s)
        diff = bc[:, :, None] - bc[:, None, :]
        decay = jnp.exp(jnp.where(causal, diff, NEG_BIG))
        a = jnp.einsum('bthk,bshk,btshk->bhts', qc, kc, decay)
        o_intra = jnp.einsum('bhts,bshv->bthv', a, vc)
        bl = bc[:, -1]
        s_new = jnp.exp(bl)[..., None] * s + jnp.einsum('bshk,bshv->bhkv', kc * jnp.exp(bl[:, None] - bc), vc)
        return s_new, o_inter + o_intra

    s_t, o = lax.scan(step, s0, (chunks(q), chunks(k), chunks(v), chunks(log_f)))
    return o.swapaxes(0, 1).reshape(b, t, h, dv), s_t


def hgrn2_mixer(hn, w_in, lb, g_norm, w_out, s0):
    b, t, _ = hn.shape
    kw = HG_HEADS * HG_DK
    vw = HG_HEADS * HG_DV
    proj = hn @ w_in
    q = proj[..., :kw]
    zf = proj[..., kw:2 * kw].astype(jnp.float32)
    i = proj[..., 2 * kw:2 * kw + vw]
    g = proj[..., 2 * kw + vw:]
    log_f = jnp.logaddexp(jnp.log(jnp.maximum(lb, TINY)), jnp.log1p(-lb) + jax.nn.log_sigmoid(zf))
    k = (1.0 - lb) * jax.nn.sigmoid(-zf)

    def heads(a, d):
        return a.astype(jnp.float32).reshape(b, t, HG_HEADS, d)

    o, s_t = hgrn2_scan(heads(q, HG_DK), heads(k, HG_DK), heads(i, HG_DV), heads(log_f, HG_DK), s0)
    o = rmsnorm(o, g_norm).reshape(b, t, vw) * jax.nn.silu(g.astype(jnp.float32))
    return o.astype(hn.dtype) @ w_out, s_t


def shared_kv(h, c, norm_kv, w_ada_kv, b_ada_kv, w_kv, b_fgate, k_norm):
    b, t, _ = h.shape
    hd = FOX_HEADS * FOX_HEAD_DIM
    shift, scale = ada_params(c, w_ada_kv, b_ada_kv, 2)
    kvf = modulate(rmsnorm(h, norm_kv), shift, scale) @ w_kv
    k = rmsnorm(kvf[..., :hd].reshape(b, t, FOX_HEADS, FOX_HEAD_DIM), k_norm)
    v = kvf[..., hd:2 * hd].reshape(b, t, FOX_HEADS, FOX_HEAD_DIM)
    log_f = jax.nn.log_sigmoid((kvf[..., 2 * hd:] + b_fgate).astype(jnp.float32))
    return k, v, log_f


def fox_attention(q, k, v, log_f, n_past):
    b, t, h, dh = q.shape
    length = k.shape[1]
    f_cum = jnp.cumsum(log_f.astype(jnp.float32), axis=1).transpose(0, 2, 1)
    f_q = f_cum[:, :, n_past:]
    qb = t if t <= Q_BLOCK else math.gcd(t, Q_BLOCK)
    k_pos = jnp.arange(length)
    scale = dh ** -0.5

    def block(i):
        s0 = i * qb
        qi = lax.dynamic_slice_in_dim(q, s0, qb, axis=1)
        fqi = lax.dynamic_slice_in_dim(f_q, s0, qb, axis=2)
        q_pos = n_past + s0 + jnp.arange(qb)
        logits = jnp.einsum('bqhd,bkhd->bhqk', qi, k).astype(jnp.float32) * scale
        logits = logits + fqi[..., None] - f_cum[:, :, None, :]
        logits = jnp.where(k_pos[None, :] <= q_pos[:, None], logits, NEG_BIG)
        p = jax.nn.softmax(logits, axis=-1)
        return jnp.einsum('bhqk,bkhd->bqhd', p.astype(v.dtype), v)

    out = lax.map(block, jnp.arange(t // qb))
    return out.swapaxes(0, 1).reshape(b, t, h, dh)


def moe(hn, w_router, b_router, w_in, w_out):
    logits = (hn @ w_router).astype(jnp.float32) + b_router.astype(jnp.float32)
    probs = jax.nn.softmax(logits, axis=-1)
    grouped = probs.reshape(probs.shape[:-1] + (N_GROUPS, N_EXPERTS // N_GROUPS))
    group_score = lax.top_k(grouped, TOP_K)[0].sum(-1)
    _, g_idx = lax.top_k(group_score, TOPK_GROUPS)
    g_mask = jax.nn.one_hot(g_idx, N_GROUPS, dtype=jnp.float32).sum(-2)
    masked = (grouped * g_mask[..., None]).reshape(probs.shape)
    top_p, top_i = lax.top_k(masked, TOP_K)
    gates = top_p / top_p.sum(-1, keepdims=True)
    dense_gate = jnp.einsum('...k,...ke->...e', gates, jax.nn.one_hot(top_i, N_EXPERTS, dtype=jnp.float32))
    out = jnp.zeros(hn.shape, jnp.float32)
    for e in range(N_EXPERTS):
        a, u = jnp.split(hn @ w_in[e], 2, axis=-1)
        out = out + dense_gate[..., e:e + 1] * ((jax.nn.silu(a) * u) @ w_out[e]).astype(jnp.float32)
    return out.astype(hn.dtype)


def trunk(x, c, s0, past, p):
    lbs = hgrn2_lower_bounds(p['lb_logits'])
    b, t, _ = x.shape
    h = x
    states = []
    for l in range(DEPTH):
        sh1, sc1, g1, sh2, sc2, g2 = ada_params(c, p['w_ada'][l], p['b_ada'][l], 6)
        hn = modulate(rmsnorm(h, p['norm_mix'][l]), sh1, sc1)
        if l < N_A_LAYERS:
            mix, s_t = hgrn2_mixer(hn, p['w_in_a'][l], lbs[l], p['gnorm_a'][l], p['w_out_a'][l], s0[l])
            states.append(s_t)
        else:
            j = l - N_A_LAYERS
            q = rmsnorm((hn @ p['w_q_b'][j]).reshape(b, t, FOX_HEADS, FOX_HEAD_DIM), p['q_norm'][j])
            att = fox_attention(q, k_all, v_all, logf_all, n_past)
            mix = att.reshape(b, t, FOX_HEADS * FOX_HEAD_DIM) @ p['w_out_b'][j]
        h = h + g1 * mix
        hn = modulate(rmsnorm(h, p['norm_ffn'][l]), sh2, sc2)
        h = h + g2 * moe(hn, p['w_router'], p['b_router'], p['w_exp_in'][l], p['w_exp_out'][l])
        if l == N_A_LAYERS - 1:
            k_new, v_new, logf_new = shared_kv(h, c, p['norm_kv'], p['w_ada_kv'], p['b_ada_kv'],
                                               p['w_kv'], p['b_fgate'], p['k_norm'])
            if past is None:
                k_all, v_all, logf_all, n_past = k_new, v_new, logf_new, 0
            else:
                pk, pv, plf = past
                n_past = pk.shape[1]
                k_all = jnp.concatenate([pk, k_new.astype(pk.dtype)], axis=1)
                v_all = jnp.concatenate([pv, v_new.astype(pv.dtype)], axis=1)
                logf_all = jnp.concatenate([plf.astype(jnp.float32), logf_new], axis=1)
    return h, jnp.stack(states), k_new, v_new, logf_new


def setup_inputs(seed: int = 0) -> dict:
    key = jax.random.key(seed)
    ks = jax.random.split(key, 40)
    f32 = jnp.float32
    d = D_MODEL
    kw = HG_HEADS * HG_DK
    vw = HG_HEADS * HG_DV
    hd = FOX_HEADS * FOX_HEAD_DIM
    n_pages = PAST_LEN // PAGE_SIZE
    n_used = DEC_BATCH * n_pages
    n_pool = n_used + max(1, n_used // 4)

    def nrm(k, shape, scale):
        return jax.random.normal(k, shape, f32) * scale

    def gain(k, shape):
        return 1.0 + nrm(k, shape, 0.02)

    page_table = jax.random.permutation(ks[0], n_pool)[:n_used].reshape(DEC_BATCH, n_pages).astype(jnp.int32)
    w_kv = jnp.concatenate([nrm(ks[1], (d, 2 * hd), d ** -0.5), nrm(ks[2], (d, FOX_HEADS), 0.1 * d ** -0.5)], axis=1)
    return {
        'x_prompt': nrm(ks[3], (BATCH, SEQ, d), 1.0),
        'x_sample': nrm(ks[4], (DEC_BATCH, DEC_SEQ, d), 1.0),
        'cache_k': nrm(ks[5], (n_pool, PAGE_SIZE, FOX_HEADS, FOX_HEAD_DIM), 1.0),
        'cache_v': nrm(ks[6], (n_pool, PAGE_SIZE, FOX_HEADS, FOX_HEAD_DIM), 1.0),
        'cache_logf': jax.nn.log_sigmoid(4.5 + nrm(ks[7], (n_pool, PAGE_SIZE, FOX_HEADS), 1.0)),
        'state_hgrn': nrm(ks[8], (N_A_LAYERS, DEC_BATCH, HG_HEADS, HG_DK, HG_DV), 0.5),
        'page_table': page_table,
        'c_prompt': nrm(ks[9], (BATCH, d), 1.0),
        'c_sample': nrm(ks[10], (DEC_BATCH, d), 1.0),
        'w_ada': nrm(ks[11], (DEPTH, d, 6 * d), 0.5 * d ** -0.5),
        'b_ada': nrm(ks[12], (DEPTH, 6 * d), 0.02),
        'norm_mix': gain(ks[13], (DEPTH, d)),
        'norm_ffn': gain(ks[14], (DEPTH, d)),
        'w_in_a': nrm(ks[15], (N_A_LAYERS, d, 2 * kw + 2 * vw), d ** -0.5),
        'lb_logits': nrm(ks[16], (N_A_LAYERS, kw), 1.0),
        'gnorm_a': gain(ks[17], (N_A_LAYERS, HG_DV)),
        'w_out_a': nrm(ks[18], (N_A_LAYERS, vw, d), vw ** -0.5),
        'norm_kv': gain(ks[19], (d,)),
        'w_ada_kv': nrm(ks[20], (d, 2 * d), 0.5 * d ** -0.5),
        'b_ada_kv': nrm(ks[21], (2 * d,), 0.02),
        'w_kv': w_kv,
        'b_fgate': jax.random.uniform(ks[22], (FOX_HEADS,), f32, 3.0, 6.0),
        'k_norm': gain(ks[23], (FOX_HEAD_DIM,)),
        'w_q_b': nrm(ks[24], (N_B_LAYERS, d, hd), d ** -0.5),
        'q_norm': gain(ks[25], (N_B_LAYERS, FOX_HEAD_DIM)),
        'w_out_b': nrm(ks[26], (N_B_LAYERS, hd, d), hd ** -0.5),
        'w_router': nrm(ks[27], (d, N_EXPERTS), d ** -0.5),
        'b_router': nrm(ks[28], (N_EXPERTS,), 0.01),
        'w_exp_in': nrm(ks[29], (DEPTH, N_EXPERTS, d, 2 * D_EXPERT), d ** -0.5),
        'w_exp_out': nrm(ks[30], (DEPTH, N_EXPERTS, D_EXPERT, d), D_EXPERT ** -0.5),
    }


def reference(x_prompt, x_sample, cache_k, cache_v, cache_logf, state_hgrn, page_table, c_prompt, c_sample,
              w_ada, b_ada, norm_mix, norm_ffn, w_in_a, lb_logits, gnorm_a, w_out_a, norm_kv, w_ada_kv,
              b_ada_kv, w_kv, b_fgate, k_norm, w_q_b, q_norm, w_out_b, w_router, b_router, w_exp_in, w_exp_out):
    p = dict(w_ada=w_ada, b_ada=b_ada, norm_mix=norm_mix, norm_ffn=norm_ffn, w_in_a=w_in_a,
             lb_logits=lb_logits, gnorm_a=gnorm_a, w_out_a=w_out_a, norm_kv=norm_kv, w_ada_kv=w_ada_kv,
             b_ada_kv=b_ada_kv, w_kv=w_kv, b_fgate=b_fgate, k_norm=k_norm, w_q_b=w_q_b, q_norm=q_norm,
             w_out_b=w_out_b, w_router=w_router, b_router=b_router, w_exp_in=w_exp_in, w_exp_out=w_exp_out)
    s0_p = jnp.zeros((N_A_LAYERS, x_prompt.shape[0], HG_HEADS, HG_DK, HG_DV), jnp.float32)
    y_p, st_p, k_p, v_p, lf_p = trunk(x_prompt, c_prompt, s0_p, None, p)
    n_seq, n_pages = page_table.shape
    page = cache_k.shape[1]
    past_k = cache_k[page_table].reshape(n_seq, n_pages * page, FOX_HEADS, FOX_HEAD_DIM)
    past_v = cache_v[page_table].reshape(n_seq, n_pages * page, FOX_HEADS, FOX_HEAD_DIM)
    past_lf = cache_logf[page_table].reshape(n_seq, n_pages * page, FOX_HEADS)
    y_s, st_s, k_s, v_s, lf_s = trunk(x_sample, c_sample, state_hgrn.astype(jnp.float32),
                                      (past_k, past_v, past_lf), p)
    return (y_p, y_s, st_p.astype(state_hgrn.dtype), st_s.astype(state_hgrn.dtype),
            k_p.astype(cache_k.dtype), v_p.astype(cache_v.dtype), lf_p.astype(cache_logf.dtype),
            k_s.astype(cache_k.dtype), v_s.astype(cache_v.dtype), lf_s.astype(cache_logf.dtype))
```

```python
import functools
import math

import jax
import jax.numpy as jnp
from jax import lax
from jax.experimental import pallas as pl
from jax.experimental.pallas import tpu as pltpu

F32 = jnp.float32
BF16 = jnp.bfloat16
HIGHEST = lax.Precision.HIGHEST

HG_HEADS = 8
FOX_HEADS = 16
N_EXPERTS = 16
N_GROUPS = 4
EXPERTS_PER_GROUP = N_EXPERTS // N_GROUPS
EPS = 1e-6
NEG_BIG = -1e30
TINY = 1e-30

LANES = 128
SUBLANES = 8
VMEM_LIMIT_BYTES = 56 * 1024 * 1024

SCAN_CHUNK = 128
ATTN_BLOCK = 512
PAGES_PER_STEP = 8


def _params(sem):
    return pltpu.CompilerParams(dimension_semantics=sem, vmem_limit_bytes=VMEM_LIMIT_BYTES)


def _nt_dot(a, b):
    return lax.dot_general(a, b, (((1,), (1,)), ((), ())), preferred_element_type=F32)


def _sigmoid(x):
    return 1.0 / (1.0 + jnp.exp(-x))


def _log_sigmoid(x):
    return jnp.minimum(x, 0.0) - jnp.log1p(jnp.exp(-jnp.abs(x)))


def _norm_mod(x, gain, shift, scale):
    ms = jnp.mean(x * x, axis=-1, keepdims=True)
    y = x * lax.rsqrt(ms + EPS) * gain
    return y * (1.0 + scale) + shift


class _Mod:
    def __init__(self, arr, d, tm, tiles_per_seq):
        self.arr = arr
        self.d = d
        self.tm = tm
        self.tiles_per_seq = tiles_per_seq
        self.per_token = arr.ndim == 2

    def spec(self, chunk):
        d, tps = self.d, self.tiles_per_seq
        if self.per_token:
            return pl.BlockSpec((self.tm, d), lambda i, *_: (i, chunk))
        return pl.BlockSpec((None, 1, d), lambda i, *_: (i // tps, 0, chunk))


def _ada_body(c_ref, w_ref, b_ref, o_ref):
    c = c_ref[...]
    a = c * _sigmoid(c)
    o_ref[0] = jnp.dot(a, w_ref[0], precision=HIGHEST, preferred_element_type=F32) + b_ref[0]


def _ada(c, w, b):
    n_layers, d, n = w.shape
    r = c.shape[0]
    tn = 1536 if n % 1536 == 0 else 1024
    return pl.pallas_call(
        _ada_body,
        grid=(n_layers, n // tn),
        in_specs=[pl.BlockSpec((r, d), lambda l, j: (0, 0)),
                  pl.BlockSpec((1, d, tn), lambda l, j: (l, 0, j)),
                  pl.BlockSpec((1, 1, tn), lambda l, j: (l, 0, j))],
        out_specs=pl.BlockSpec((1, r, tn), lambda l, j: (l, 0, j)),
        out_shape=jax.ShapeDtypeStruct((n_layers, r, n), F32),
        compiler_params=_params(("arbitrary", "arbitrary")),
        name="ada",
    )(c, w, b.reshape(n_layers, 1, n))


def _hgrn_in_body(x_ref, gain_ref, sh_ref, sc_ref, wq_ref, wf_ref, wi_ref, wg_ref,
                  q_ref, zf_ref, i_ref, g_ref):
    xn = _norm_mod(x_ref[...], gain_ref[...], sh_ref[...], sc_ref[...]).astype(BF16)
    q_ref[...] = jnp.dot(xn, wq_ref[...], preferred_element_type=F32).astype(q_ref.dtype)
    zf_ref[...] = jnp.dot(xn, wf_ref[...], preferred_element_type=F32)
    i_ref[...] = jnp.dot(xn, wi_ref[...], preferred_element_type=F32).astype(i_ref.dtype)
    g_ref[...] = jnp.dot(xn, wg_ref[...], preferred_element_type=F32).astype(g_ref.dtype)


def _hgrn_in_proj(h, gain, mod, w4, tm):
    t, d = h.shape
    row = pl.BlockSpec((tm, d), lambda i: (i, 0))
    full = pl.BlockSpec((d, d), lambda i: (0, 0))
    return pl.pallas_call(
        _hgrn_in_body,
        grid=(t // tm,),
        in_specs=[row, pl.BlockSpec((1, d), lambda i: (0, 0)), mod.spec(0), mod.spec(1),
                  full, full, full, full],
        out_specs=[row, row, row, row],
        out_shape=[jax.ShapeDtypeStruct((t, d), BF16), jax.ShapeDtypeStruct((t, d), F32),
                   jax.ShapeDtypeStruct((t, d), BF16), jax.ShapeDtypeStruct((t, d), BF16)],
        compiler_params=_params(("arbitrary",)),
        name="hgrn_in_proj",
    )(h, gain.reshape(1, d), mod.arr, mod.arr, *w4)


def _split3(x):
    hi = x.astype(BF16)
    r1 = x - hi.astype(F32)
    mid = r1.astype(BF16)
    lo = (r1 - mid.astype(F32)).astype(BF16)
    return hi, mid, lo


def _hgrn_scan_body(q_ref, zf_ref, i_ref, g_ref, lb_ref, gn_ref, s0_ref, o_ref, st_ref,
                    *, chunk, n_valid, dk):
    c = pl.program_id(1)

    @pl.when(c == 0)
    def _():
        st_ref[...] = s0_ref[...]

    row = lax.broadcasted_iota(jnp.int32, (chunk, chunk), 0)
    col = lax.broadcasted_iota(jnp.int32, (chunk, chunk), 1)
    tri = jnp.where(row >= col, 1.0, 0.0).astype(BF16)
    n_levels = chunk.bit_length() - 1
    differ = row ^ col
    on_diag = differ == 0
    at_level = [lax.shift_right_logical(differ, lv) == 1 for lv in range(n_levels)]
    tok = lax.broadcasted_iota(jnp.int32, (chunk, dk), 0)
    if n_valid < chunk:
        valid = tok < n_valid
    for h in range(HG_HEADS):
        hs = slice(h * dk, (h + 1) * dk)
        q = q_ref[:, hs].astype(F32)
        zf = zf_ref[:, hs]
        v = i_ref[:, hs]
        lb = lb_ref[:, hs]
        a1 = jnp.log(jnp.maximum(lb, TINY))
        a2 = jnp.log1p(-lb) + _log_sigmoid(zf)
        log_f = jnp.maximum(a1, a2) + jnp.log1p(jnp.exp(-jnp.abs(a1 - a2)))
        k = (1.0 - lb) / (1.0 + jnp.exp(zf))
        if n_valid < chunk:
            log_f = jnp.where(valid, log_f, 0.0)
            k = jnp.where(valid, k, 0.0)
        g_hi, g_mid, g_lo = _split3(log_f)
        b = (jnp.dot(tri, g_hi, preferred_element_type=F32)
             + jnp.dot(tri, g_mid, preferred_element_type=F32)
             + jnp.dot(tri, g_lo, preferred_element_type=F32))
        b_end = b[chunk - 1:chunk, :]
        a = jnp.where(on_diag, _nt_dot(q.astype(BF16), k.astype(BF16)), 0.0)
        b_dn1 = pltpu.roll(b, 1, 0)
        b_up1 = pltpu.roll(b, chunk - 1, 0)
        for lv in range(n_levels):
            half = 1 << lv
            if lv == 0:
                d_q, d_k = b - b_dn1, jnp.zeros_like(b)
            elif lv == 1:
                d_q = jnp.where((tok & 3) == 2, b - b_dn1, b - pltpu.roll(b, 2, 0))
                d_k = jnp.where((tok & 3) == 0, b_up1 - b, 0.0)
            else:
                nblk = chunk // (2 * half)
                edge = b.reshape(nblk, 2 * half, dk)[:, half - 1:half, :]
                edge = jnp.broadcast_to(edge, (nblk, 2 * half, dk)).reshape(chunk, dk)
                d_q, d_k = b - edge, edge - b
            right = (lax.shift_right_logical(tok, lv) & 1) == 1
            e_q = jnp.where(right, jnp.exp(jnp.minimum(d_q, 0.0)), 0.0)
            e_k = jnp.where(right, 0.0, jnp.exp(jnp.minimum(d_k, 0.0)))
            a_lv = _nt_dot((q * e_q).astype(BF16), (k * e_k).astype(BF16))
            a = jnp.where(at_level[lv], a_lv, a)
        s = st_ref[0, h]
        o = (jnp.dot(a.astype(BF16), v, preferred_element_type=F32)
             + jnp.dot((q * jnp.exp(b)).astype(BF16), s.astype(BF16), preferred_element_type=F32))
        k_d = (k * jnp.exp(b_end - b)).T.astype(BF16)
        decay = jnp.broadcast_to(jnp.exp(b_end), (dk, dk)).T
        st_ref[0, h] = decay * s + jnp.dot(k_d, v, preferred_element_type=F32)
        ms = jnp.mean(o * o, axis=-1, keepdims=True)
        gate = g_ref[:, hs].astype(F32)
        o = o * lax.rsqrt(ms + EPS) * gn_ref[...] * (gate * _sigmoid(gate))
        o_ref[:, hs] = o.astype(o_ref.dtype)


def _hgrn_scan(q, zf, i, g, lb, gnorm, s0, nseq, n_valid):
    tt, d = q.shape
    tp = tt // nseq
    nc = tp // SCAN_CHUNK
    dk = d // HG_HEADS
    row = pl.BlockSpec((SCAN_CHUNK, d), lambda b, c: (b * nc + c, 0))
    st = pl.BlockSpec((1, HG_HEADS, dk, dk), lambda b, c: (b, 0, 0, 0))
    body = functools.partial(_hgrn_scan_body, chunk=SCAN_CHUNK, n_valid=min(n_valid, SCAN_CHUNK), dk=dk)
    return pl.pallas_call(
        body,
        grid=(nseq, nc),
        in_specs=[row, row, row, row,
                  pl.BlockSpec((1, d), lambda b, c: (0, 0)),
                  pl.BlockSpec((1, dk), lambda b, c: (0, 0)),
                  st],
        out_specs=[row, st],
        out_shape=[jax.ShapeDtypeStruct((tt, d), BF16), jax.ShapeDtypeStruct(s0.shape, F32)],
        compiler_params=_params(("arbitrary", "arbitrary")),
        name="hgrn_scan",
    )(q, zf, i, g, lb.reshape(1, d), gnorm.reshape(1, dk), s0)


def _route_rows(lt):
    rows = [lt[e:e + 1, :] for e in range(N_EXPERTS)]
    m = functools.reduce(jnp.maximum, rows)
    ex = [jnp.exp(r - m) for r in rows]
    inv = 1.0 / functools.reduce(lambda x, y: x + y, ex)
    pr = [e * inv for e in ex]
    scores = []
    for gi in range(N_GROUPS):
        p4 = pr[gi * EXPERTS_PER_GROUP:(gi + 1) * EXPERTS_PER_GROUP]
        pairs = [p4[x] + p4[y] for x in range(EXPERTS_PER_GROUP) for y in range(x + 1, EXPERTS_PER_GROUP)]
        scores.append(functools.reduce(jnp.maximum, pairs))
    best = scores[0]
    sel = jnp.zeros(best.shape, jnp.int32)
    for gi in range(1, N_GROUPS):
        better = scores[gi] > best
        sel = jnp.where(better, gi, sel)
        best = jnp.where(better, scores[gi], best)
    cand = []
    for j in range(EXPERTS_PER_GROUP):
        cj = pr[(N_GROUPS - 1) * EXPERTS_PER_GROUP + j]
        for gi in range(N_GROUPS - 2, -1, -1):
            cj = jnp.where(sel == gi, pr[gi * EXPERTS_PER_GROUP + j], cj)
        cand.append(cj)

    def argmax4(vals):
        p, idx = vals[0], jnp.zeros(vals[0].shape, jnp.int32)
        for j in range(1, len(vals)):
            better = vals[j] > p
            idx = jnp.where(better, j, idx)
            p = jnp.where(better, vals[j], p)
        return p, idx

    p1, i1 = argmax4(cand)
    p2, i2 = argmax4([jnp.where(i1 == j, -1.0, cand[j]) for j in range(EXPERTS_PER_GROUP)])
    den = p1 + p2
    base = sel * EXPERTS_PER_GROUP
    return base + i1, base + i2, p1 / den, p2 / den


def _mix_router_body(a_ref, w_ref, h_ref, g1_ref, gain_ref, sh_ref, sc_ref, wrh_ref, wrl_ref, br_ref,
                     h_out, hn_out, gate_out):
    mix = jnp.dot(a_ref[...].astype(BF16), w_ref[...], preferred_element_type=F32)
    h = h_ref[...] + g1_ref[...] * mix
    h_out[...] = h
    hn = _norm_mod(h, gain_ref[...], sh_ref[...], sc_ref[...])
    x_hi = hn.astype(BF16)
    hn_out[...] = x_hi
    x_lo = (hn - x_hi.astype(F32)).astype(BF16)
    lt = (_nt_dot(wrh_ref[...], x_hi) + _nt_dot(wrl_ref[...], x_hi) + _nt_dot(wrh_ref[...], x_lo)
          + br_ref[...])
    e1, e2, g1, g2 = _route_rows(lt)
    tm = lt.shape[1]
    eid = lax.broadcasted_iota(jnp.int32, (LANES, tm), 0)
    dense_t = jnp.where(eid == e1, g1, 0.0) + jnp.where(eid == e2, g2, 0.0)
    gate_out[...] = dense_t.T


def _mix_router(a, w_out, h, gain, mod, wr_hi, wr_lo, b_router, tm):
    t, d = h.shape
    row = pl.BlockSpec((tm, d), lambda i: (i, 0))
    const = lambda shape: pl.BlockSpec(shape, lambda i: (0,) * len(shape))
    return pl.pallas_call(
        _mix_router_body,
        grid=(t // tm,),
        in_specs=[row, const((d, d)), row, mod.spec(2), const((1, d)), mod.spec(3), mod.spec(4),
                  const((N_EXPERTS, d)), const((N_EXPERTS, d)), const((N_EXPERTS, 1))],
        out_specs=[row, row, pl.BlockSpec((tm, LANES), lambda i: (i, 0))],
        out_shape=[jax.ShapeDtypeStruct((t, d), F32), jax.ShapeDtypeStruct((t, d), BF16),
                   jax.ShapeDtypeStruct((t, LANES), F32)],
        compiler_params=_params(("arbitrary",)),
        name="mix_router",
    )(a, w_out, h, mod.arr, gain.reshape(1, d), mod.arr, mod.arr, wr_hi, wr_lo,
      b_router.reshape(N_EXPERTS, 1).astype(F32))


def _moe_body(x_ref, win_ref, wout_ref, gate_ref, h_ref, g2_ref, o_ref, acc_ref):
    e = pl.program_id(1)

    @pl.when(e == 0)
    def _():
        acc_ref[...] = jnp.zeros_like(acc_ref)

    hid = jnp.dot(x_ref[...], win_ref[0], preferred_element_type=F32)
    de = hid.shape[1] // 2
    a, u = hid[:, :de], hid[:, de:]
    act = (a * _sigmoid(a) * u).astype(BF16)
    y = jnp.dot(act, wout_ref[0], preferred_element_type=F32)
    lane = lax.broadcasted_iota(jnp.int32, gate_ref.shape, 1)
    gcol = jnp.sum(jnp.where(lane == e, gate_ref[...], 0.0), axis=1, keepdims=True)
    acc_ref[...] += gcol * y

    @pl.when(e == pl.num_programs(1) - 1)
    def _():
        o_ref[...] = h_ref[...] + g2_ref[...] * acc_ref[...]


def _moe(hn, gates, h, mod, w_in, w_out, tm):
    t, d = h.shape
    n_e, _, d2 = w_in.shape
    row = pl.BlockSpec((tm, d), lambda i, e: (i, 0))
    return pl.pallas_call(
        _moe_body,
        grid=(t // tm, n_e),
        in_specs=[row,
                  pl.BlockSpec((1, d, d2), lambda i, e: (e, 0, 0)),
                  pl.BlockSpec((1, d2 // 2, d), lambda i, e: (e, 0, 0)),
                  pl.BlockSpec((tm, LANES), lambda i, e: (i, 0)),
                  row, mod.spec(5)],
        out_specs=row,
        out_shape=jax.ShapeDtypeStruct((t, d), F32),
        scratch_shapes=[pltpu.VMEM((tm, d), F32)],
        compiler_params=_params(("arbitrary", "arbitrary")),
        name="moe",
    )(hn, w_in, w_out, gates, h, mod.arr)


def _head_norm(y, gsum_ref, gain_ref, head_dim):
    sq = y * y
    hi = sq.astype(BF16)
    lo = (sq - hi.astype(F32)).astype(BF16)
    ssum = (jnp.dot(hi, gsum_ref[...], preferred_element_type=F32)
            + jnp.dot(lo, gsum_ref[...], preferred_element_type=F32))
    return y * lax.rsqrt(ssum * (1.0 / head_dim) + EPS) * gain_ref[...]


def _head_sum_matrix(d, head_dim):
    r = jnp.arange(d) // head_dim
    return (r[:, None] == r[None, :]).astype(BF16)


def _q_proj_body(x_ref, gain_ref, sh_ref, sc_ref, w_ref, gsum_ref, qn_ref, q_ref, *, head_dim):
    xn = _norm_mod(x_ref[...], gain_ref[...], sh_ref[...], sc_ref[...]).astype(BF16)
    y = jnp.dot(xn, w_ref[...], preferred_element_type=F32)
    q = _head_norm(y, gsum_ref, qn_ref, head_dim) * (head_dim ** -0.5)
    q_ref[...] = q.astype(q_ref.dtype)


def _q_proj(h, gain, mod, w, gsum, q_norm, tm, out_dtype):
    t, d = h.shape
    head_dim = d // FOX_HEADS
    row = pl.BlockSpec((tm, d), lambda i: (i, 0))
    const = lambda shape: pl.BlockSpec(shape, lambda i: (0,) * len(shape))
    return pl.pallas_call(
        functools.partial(_q_proj_body, head_dim=head_dim),
        grid=(t // tm,),
        in_specs=[row, const((1, d)), mod.spec(0), mod.spec(1), const((d, d)), const((d, d)), const((1, d))],
        out_specs=row,
        out_shape=jax.ShapeDtypeStruct((t, d), out_dtype),
        compiler_params=_params(("arbitrary",)),
        name="q_proj",
    )(h, gain.reshape(1, d), mod.arr, mod.arr, w, gsum, jnp.tile(q_norm, FOX_HEADS).reshape(1, d))


def _kv_proj_body(x_ref, gain_ref, sh_ref, sc_ref, wk_ref, wv_ref, wf_ref, gsum_ref, kn_ref, bf_ref,
                  k_ref, v_ref, kb_ref, vb_ref, lf_ref, *, head_dim):
    xn = _norm_mod(x_ref[...], gain_ref[...], sh_ref[...], sc_ref[...]).astype(BF16)
    k = _head_norm(jnp.dot(xn, wk_ref[...], preferred_element_type=F32), gsum_ref, kn_ref, head_dim)
    k_ref[...] = k
    kb_ref[...] = k.astype(BF16)
    v = jnp.dot(xn, wv_ref[...], preferred_element_type=F32)
    v_ref[...] = v
    vb_ref[...] = v.astype(BF16)
    zf = jnp.dot(xn, wf_ref[...], preferred_element_type=F32) + bf_ref[...]
    lf_ref[...] = _log_sigmoid(zf)[:, :lf_ref.shape[1]]


def _kv_proj(h, gain, mod, wk, wv, wf, gsum, k_norm, b_fgate, tm):
    t, d = h.shape
    head_dim = d // FOX_HEADS
    row = pl.BlockSpec((tm, d), lambda i: (i, 0))
    const = lambda shape: pl.BlockSpec(shape, lambda i: (0,) * len(shape))
    bf = jnp.pad(b_fgate.astype(F32), (0, LANES - FOX_HEADS)).reshape(1, LANES)
    return pl.pallas_call(
        functools.partial(_kv_proj_body, head_dim=head_dim),
        grid=(t // tm,),
        in_specs=[row, const((1, d)), mod.spec(0), mod.spec(1), const((d, d)), const((d, d)),
                  const((d, LANES)), const((d, d)), const((1, d)), const((1, LANES))],
        out_specs=[row, row, row, row, pl.BlockSpec((tm, FOX_HEADS), lambda i: (i, 0))],
        out_shape=[jax.ShapeDtypeStruct((t, d), F32), jax.ShapeDtypeStruct((t, d), F32),
                   jax.ShapeDtypeStruct((t, d), BF16), jax.ShapeDtypeStruct((t, d), BF16),
                   jax.ShapeDtypeStruct((t, FOX_HEADS), F32)],
        compiler_params=_params(("arbitrary",)),
        name="kv_proj",
    )(h, gain.reshape(1, d), mod.arr, mod.arr, wk, wv, wf, gsum,
      jnp.tile(k_norm, FOX_HEADS).reshape(1, d), bf)


def _cumsum_lanes_body(x_ref, o_ref, carry_ref, *, carry_rows):
    j = pl.program_id(1)

    @pl.when(j == 0)
    def _():
        carry_ref[...] = jnp.zeros_like(carry_ref)

    w = x_ref.shape[-1]
    r = lax.broadcasted_iota(jnp.int32, (w, w), 0)
    c = lax.broadcasted_iota(jnp.int32, (w, w), 1)
    upper = jnp.where(r <= c, 1.0, 0.0).astype(F32)
    y = jnp.dot(x_ref[0], upper, precision=HIGHEST, preferred_element_type=F32)
    if carry_rows:
        y = y + carry_ref[...]
        carry_ref[...] = y[:, w - 1:w]
    o_ref[0] = y


def _cumsum_lanes(x, width, carry):
    b, r, t = x.shape
    spec = pl.BlockSpec((1, r, width), lambda i, j: (i, 0, j))
    return pl.pallas_call(
        functools.partial(_cumsum_lanes_body, carry_rows=carry),
        grid=(b, t // width),
        in_specs=[spec],
        out_specs=spec,
        out_shape=jax.ShapeDtypeStruct(x.shape, F32),
        scratch_shapes=[pltpu.VMEM((r, 1), F32)],
        compiler_params=_params(("arbitrary", "arbitrary")),
        name="cumsum_lanes",
    )(x)


def _attn_prompt_body(qt_ref, kt_ref, q_ref, k_ref, v_ref, f_ref, o_ref, m_ref, l_ref, acc_ref,
                      *, blk, head_dim):
    p = pl.program_id(2)
    qi = qt_ref[p]
    ki = kt_ref[p]

    @pl.when(ki == 0)
    def _():
        m_ref[...] = jnp.full(m_ref.shape, NEG_BIG, F32)
        l_ref[...] = jnp.zeros_like(l_ref)
        acc_ref[...] = jnp.zeros_like(acc_ref)

    q = q_ref[...]
    k = k_ref[...]
    v = v_ref[...]
    f = f_ref[...]
    lane = lax.broadcasted_iota(jnp.int32, q.shape, 1)
    first = lane < head_dim
    row = lax.broadcasted_iota(jnp.int32, (blk, blk), 0) + qi * blk
    col = lax.broadcasted_iota(jnp.int32, (blk, blk), 1) + ki * blk
    visible = col <= row
    zero = jnp.zeros_like(q)
    alphas, pvs = [], []
    for hh, qh in enumerate((jnp.where(first, q, zero), jnp.where(first, zero, q))):
        s = _nt_dot(qh, k) - f[hh:hh + 1, :]
        s = jnp.where(visible, s, NEG_BIG)
        m_old = m_ref[hh]
        m_new = jnp.maximum(m_old, jnp.max(s, axis=1, keepdims=True))
        alpha = jnp.exp(m_old - m_new)
        pr = jnp.exp(s - m_new)
        l_ref[hh] = alpha * l_ref[hh] + jnp.sum(pr, axis=1, keepdims=True)
        m_ref[hh] = m_new
        alphas.append(alpha)
        pvs.append(jnp.dot(pr.astype(BF16), v, preferred_element_type=F32))
    acc_ref[...] = (jnp.where(first, alphas[0], alphas[1]) * acc_ref[...]
                    + jnp.where(first, pvs[0], pvs[1]))

    @pl.when(ki == qi)
    def _():
        inv = jnp.where(first, 1.0 / l_ref[0], 1.0 / l_ref[1])
        o_ref[...] = (acc_ref[...] * inv).astype(o_ref.dtype)


def _attn_prompt(q, k, v, f_pairs, nseq):
    tt, d = q.shape
    t = tt // nseq
    blk = min(ATTN_BLOCK, t)
    nb = t // blk
    head_dim = d // FOX_HEADS
    pairs = [(qi, ki) for qi in range(nb) for ki in range(qi + 1)]
    q_tab = jnp.asarray([p[0] for p in pairs], jnp.int32)
    k_tab = jnp.asarray([p[1] for p in pairs], jnp.int32)
    grid_spec = pltpu.PrefetchScalarGridSpec(
        num_scalar_prefetch=2,
        grid=(nseq, d // LANES, len(pairs)),
        in_specs=[pl.BlockSpec((blk, LANES), lambda b, hp, p, qt, kt: (b * nb + qt[p], hp)),
                  pl.BlockSpec((blk, LANES), lambda b, hp, p, qt, kt: (b * nb + kt[p], hp)),
                  pl.BlockSpec((blk, LANES), lambda b, hp, p, qt, kt: (b * nb + kt[p], hp)),
                  pl.BlockSpec((None, None, 2, blk), lambda b, hp, p, qt, kt: (b, hp, 0, kt[p]))],
        out_specs=pl.BlockSpec((blk, LANES), lambda b, hp, p, qt, kt: (b * nb + qt[p], hp)),
        scratch_shapes=[pltpu.VMEM((2, blk, 1), F32), pltpu.VMEM((2, blk, 1), F32),
                        pltpu.VMEM((blk, LANES), F32)])
    return pl.pallas_call(
        functools.partial(_attn_prompt_body, blk=blk, head_dim=head_dim),
        grid_spec=grid_spec,
        out_shape=jax.ShapeDtypeStruct((tt, d), BF16),
        compiler_params=_params(("arbitrary", "arbitrary", "arbitrary")),
        name="attn_prompt",
    )(q_tab, k_tab, q, k, v, f_pairs)


def _gather_cum_body(pt_ref, cp_ref, new_ref, o_ref, *, n_pages, page):
    b = pl.program_id(0)

    def step(j, carry):
        idx = pt_ref[b * n_pages + j]
        blk = cp_ref[idx] + carry
        o_ref[0, :, pl.ds(pl.multiple_of(j * page, page), page)] = blk
        return blk[:, page - 1:page]

    carry = lax.fori_loop(0, n_pages, step, jnp.zeros((cp_ref.shape[1], 1), F32))
    o_ref[0, :, n_pages * page:] = new_ref[0] + carry


def _gather_cum(page_table, cum_pages, cum_new):
    nseq, n_pages = page_table.shape
    n_pool, hh, page = cum_pages.shape
    grid_spec = pltpu.PrefetchScalarGridSpec(
        num_scalar_prefetch=1,
        grid=(nseq,),
        in_specs=[pl.BlockSpec((n_pool, hh, page), lambda b, pt: (0, 0, 0)),
                  pl.BlockSpec((1, hh, page), lambda b, pt: (b, 0, 0))],
        out_specs=pl.BlockSpec((1, hh, (n_pages + 1) * page), lambda b, pt: (b, 0, 0)))
    return pl.pallas_call(
        functools.partial(_gather_cum_body, n_pages=n_pages, page=page),
        grid_spec=grid_spec,
        out_shape=jax.ShapeDtypeStruct((nseq, hh, (n_pages + 1) * page), F32),
        compiler_params=_params(("arbitrary",)),
        name="gather_cum",
    )(page_table.reshape(-1), cum_pages, cum_new)


def _attn_paged_body(pt_ref, *refs, n_q, page, pps, n_steps, head_dim):
    q_ref = refs[0]
    k_refs = refs[1:1 + pps]
    v_refs = refs[1 + pps:1 + 2 * pps]
    kn_ref, vn_ref, f_ref, o_ref, qbd_ref, m_ref, l_ref, acc_ref = refs[1 + 2 * pps:]
    j = pl.program_id(1)
    d = q_ref.shape[1]
    rows = FOX_HEADS * n_q

    @pl.when(j == 0)
    def _():
        q = q_ref[...]
        lane_head = lax.broadcasted_iota(jnp.int32, q.shape, 1) // head_dim
        for h in range(FOX_HEADS):
            qbd_ref[h * n_q:(h + 1) * n_q, :] = jnp.where(lane_head == h, q, 0.0).astype(BF16)
        m_ref[...] = jnp.full(m_ref.shape, NEG_BIG, F32)
        l_ref[...] = jnp.zeros_like(l_ref)
        acc_ref[...] = jnp.zeros_like(acc_ref)

    def attend(k, v, f, mask):
        bias = jnp.concatenate([jnp.broadcast_to(f[h:h + 1, :], (n_q, page)) for h in range(FOX_HEADS)], axis=0)
        s = _nt_dot(qbd_ref[...], k.astype(BF16)) - bias
        if mask is not None:
            s = jnp.where(mask, s, NEG_BIG)
        m_old = m_ref[...]
        m_new = jnp.maximum(m_old, jnp.max(s, axis=1, keepdims=True))
        alpha = jnp.exp(m_old - m_new)
        pr = jnp.exp(s - m_new)
        l_ref[...] = alpha * l_ref[...] + jnp.sum(pr, axis=1, keepdims=True)
        m_ref[...] = m_new
        acc_ref[...] = alpha * acc_ref[...] + jnp.dot(pr.astype(BF16), v.astype(BF16),
                                                      preferred_element_type=F32)

    @pl.when(j < n_steps - 1)
    def _():
        for u in range(pps):
            off = pl.multiple_of((j * pps + u) * page, page)
            attend(k_refs[u][...], v_refs[u][...], f_ref[0, :, pl.ds(off, page)], None)

    @pl.when(j == n_steps - 1)
    def _():
        pad = jnp.zeros((page - n_q, d), F32)
        k = jnp.concatenate([kn_ref[...], pad], axis=0)
        v = jnp.concatenate([vn_ref[...], pad], axis=0)
        t_of_row = lax.broadcasted_iota(jnp.int32, (rows, page), 0) % n_q
        key = lax.broadcasted_iota(jnp.int32, (rows, page), 1)
        attend(k, v, f_ref[0, :, (n_steps - 1) * pps * page:], key <= t_of_row)
        acc = acc_ref[...] * (1.0 / l_ref[...])
        lane_head = lax.broadcasted_iota(jnp.int32, (n_q, d), 1) // head_dim
        out = jnp.zeros((n_q, d), F32)
        for h in range(FOX_HEADS):
            out = out + jnp.where(lane_head == h, acc[h * n_q:(h + 1) * n_q, :], 0.0)
        o_ref[...] = out


def _attn_paged(q, cache_k, cache_v, k_new, v_new, f_all, page_table):
    tt, d = q.shape
    nseq, n_pages = page_table.shape
    n_q = tt // nseq
    page = cache_k.shape[1]
    pps = PAGES_PER_STEP
    n_steps = n_pages // pps + 1
    head_dim = d // FOX_HEADS
    rows = FOX_HEADS * n_q

    def page_spec(u):
        def index(b, j, pt):
            return (pt[b * n_pages + jnp.minimum(j, n_steps - 2) * pps + u], 0, 0)
        return pl.BlockSpec((None, page, d), index)

    tok = pl.BlockSpec((n_q, d), lambda b, j, pt: (b, 0))
    grid_spec = pltpu.PrefetchScalarGridSpec(
        num_scalar_prefetch=1,
        grid=(nseq, n_steps),
        in_specs=([tok] + [page_spec(u) for u in range(pps)] + [page_spec(u) for u in range(pps)]
                  + [tok, tok, pl.BlockSpec((1, FOX_HEADS, f_all.shape[2]), lambda b, j, pt: (b, 0, 0))]),
        out_specs=tok,
        scratch_shapes=[pltpu.VMEM((rows, d), BF16), pltpu.VMEM((rows, 1), F32),
                        pltpu.VMEM((rows, 1), F32), pltpu.VMEM((rows, d), F32)])
    body = functools.partial(_attn_paged_body, n_q=n_q, page=page, pps=pps, n_steps=n_steps,
                             head_dim=head_dim)
    return pl.pallas_call(
        body,
        grid_spec=grid_spec,
        out_shape=jax.ShapeDtypeStruct((tt, d), F32),
        compiler_params=_params(("arbitrary", "arbitrary")),
        name="attn_paged",
    )(page_table.reshape(-1), q, *([cache_k] * pps), *([cache_v] * pps), k_new, v_new, f_all)


def _lower_bounds(lb_logits):
    p = jax.nn.softmax(lb_logits.astype(F32), axis=0)
    return jnp.cumsum(p, axis=0) - p[0]


def _trunk(x, nseq, mods, mod_kv, s0, past, w):
    tt, d = x.shape
    t = tt // nseq
    tm = min(512, tt)
    tiles_per_seq = max(t // tm, 1)
    n_a = w["w_in_a"].shape[0]
    depth = w["w_ada"].shape[0]
    head_dim = d // FOX_HEADS
    lbs = _lower_bounds(w["lb_logits"])
    mk = lambda arr: _Mod(arr, d, tm, tiles_per_seq)

    h = x
    states = []
    for l in range(depth):
        mod = mk(mods[l])
        if l < n_a:
            q, zf, i, g = _hgrn_in_proj(h, w["norm_mix"][l], mod, w["w_in_a4"][l], tm)
            tp = -(-t // SCAN_CHUNK) * SCAN_CHUNK
            if tp != t:
                padr = lambda a: jnp.pad(a.reshape(nseq, t, d), ((0, 0), (0, tp - t), (0, 0))).reshape(nseq * tp, d)
                q, zf, i, g = padr(q), padr(zf), padr(i), padr(g)
            o, s_t = _hgrn_scan(q, zf, i, g, lbs[l], w["gnorm_a"][l], s0[l], nseq, t)
            if tp != t:
                o = o.reshape(nseq, tp, d)[:, :t].reshape(tt, d)
            states.append(s_t)
            mix_in, w_out = o, w["w_out_a"][l]
        else:
            j = l - n_a
            if past is None:
                q = _q_proj(h, w["norm_mix"][l], mod, w["w_q_b"][j], w["gsum"], w["q_norm"][j], tm, BF16)
                mix_in = _attn_prompt(q, k_bf, v_bf, f_keys, nseq)
            else:
                q = _q_proj(h, w["norm_mix"][l], mod, w["w_q_b"][j], w["gsum"], w["q_norm"][j], tm, F32)
                mix_in = _attn_paged(q, past[0], past[1], k_new, v_new, f_keys, past[3])
            w_out = w["w_out_b"][j]
        h, hn, gates = _mix_router(mix_in, w_out, h, w["norm_ffn"][l], mod, w["wr_hi"], w["wr_lo"],
                                   w["b_router"], tm)
        h = _moe(hn, gates, h, mod, w["w_exp_in"][l], w["w_exp_out"][l], tm)
        if l == n_a - 1:
            k_new, v_new, k_bf, v_bf, logf_new = _kv_proj(
                h, w["norm_kv"], mk(mod_kv), w["w_k"], w["w_v"], w["w_f"], w["gsum"], w["k_norm"],
                w["b_fgate"], tm)
            lf_t = logf_new.reshape(nseq, t, FOX_HEADS).transpose(0, 2, 1)
            if past is None:
                f_cum = _cumsum_lanes(lf_t, min(512, t), carry=True)
                f_keys = f_cum.reshape(nseq, FOX_HEADS // 2, 2, t)
            else:
                cache_logf, page_table = past[2], past[3]
                page = cache_logf.shape[1]
                pool_t = cache_logf.astype(F32).transpose(0, 2, 1)
                n_pool = pool_t.shape[0]
                rows = 64 if n_pool % 64 == 0 else 1
                cum_pages = _cumsum_lanes(pool_t.reshape(n_pool // rows, rows * FOX_HEADS, page), page,
                                          carry=False).reshape(n_pool, FOX_HEADS, page)
                new_pad = jnp.pad(lf_t, ((0, 0), (0, 0), (0, page - t)))
                cum_new = _cumsum_lanes(new_pad, page, carry=False)
                f_keys = _gather_cum(page_table, cum_pages, cum_new)
    return h, jnp.stack(states), k_new, v_new, logf_new


def kernel(x_prompt, x_sample, cache_k, cache_v, cache_logf, state_hgrn, page_table, c_prompt, c_sample,
           w_ada, b_ada, norm_mix, norm_ffn, w_in_a, lb_logits, gnorm_a, w_out_a, norm_kv, w_ada_kv,
           b_ada_kv, w_kv, b_fgate, k_norm, w_q_b, q_norm, w_out_b, w_router, b_router, w_exp_in, w_exp_out):
    nb, seq, d = x_prompt.shape
    ns, dseq, _ = x_sample.shape
    n_a = w_in_a.shape[0]
    depth = w_ada.shape[0]
    hd = FOX_HEADS * (d // FOX_HEADS)
    dk = d // HG_HEADS

    nrow = nb + ns
    nrow_pad = -(-nrow // SUBLANES) * SUBLANES
    c_all = jnp.pad(jnp.concatenate([c_prompt, c_sample], axis=0).astype(F32), ((0, nrow_pad - nrow), (0, 0)))
    mod_all = _ada(c_all, w_ada, b_ada)
    mod_kv_all = _ada(c_all, w_ada_kv[None], b_ada_kv[None])[0]
    mods_p = [mod_all[l, :nb].reshape(nb, 1, 6 * d) for l in range(depth)]
    mods_s = [jnp.repeat(mod_all[l, nb:nrow], dseq, axis=0) for l in range(depth)]
    mod_kv_p = mod_kv_all[:nb].reshape(nb, 1, 2 * d)
    mod_kv_s = jnp.repeat(mod_kv_all[nb:nrow], dseq, axis=0)

    wr_t = w_router.astype(F32).T
    wr_hi = wr_t.astype(BF16)
    wr_lo = (wr_t - wr_hi.astype(F32)).astype(BF16)
    w_in_b = w_in_a.astype(BF16)
    w = dict(
        w_ada=w_ada, lb_logits=lb_logits, norm_mix=norm_mix, norm_ffn=norm_ffn, gnorm_a=gnorm_a,
        w_in_a=w_in_a,
        w_in_a4=[[w_in_b[l, :, s * d:(s + 1) * d] for s in range(4)] for l in range(n_a)],
        w_out_a=w_out_a.astype(BF16), norm_kv=norm_kv,
        w_k=w_kv[:, :hd].astype(BF16), w_v=w_kv[:, hd:2 * hd].astype(BF16),
        w_f=jnp.pad(w_kv[:, 2 * hd:], ((0, 0), (0, LANES - FOX_HEADS))).astype(BF16),
        b_fgate=b_fgate, k_norm=k_norm, w_q_b=w_q_b.astype(BF16), q_norm=q_norm,
        w_out_b=w_out_b.astype(BF16), wr_hi=wr_hi, wr_lo=wr_lo, b_router=b_router,
        w_exp_in=w_exp_in.astype(BF16), w_exp_out=w_exp_out.astype(BF16),
        gsum=_head_sum_matrix(d, d // FOX_HEADS),
    )

    s0_p = jnp.zeros((n_a, nb, HG_HEADS, dk, dk), F32)
    y_p, st_p, k_p, v_p, lf_p = _trunk(x_prompt.reshape(nb * seq, d), nb, mods_p, mod_kv_p, s0_p, None, w)

    n_pool, page = cache_k.shape[0], cache_k.shape[1]
    past = (cache_k.reshape(n_pool, page, hd), cache_v.reshape(n_pool, page, hd), cache_logf, page_table)
    y_s, st_s, k_s, v_s, lf_s = _trunk(x_sample.reshape(ns * dseq, d), ns, mods_s, mod_kv_s,
                                       state_hgrn.astype(F32), past, w)

    hs = (FOX_HEADS, d // FOX_HEADS)
    return (y_p.reshape(nb, seq, d), y_s.reshape(ns, dseq, d),
            st_p.astype(state_hgrn.dtype), st_s.astype(state_hgrn.dtype),
            k_p.reshape(nb, seq, *hs).astype(cache_k.dtype), v_p.reshape(nb, seq, *hs).astype(cache_v.dtype),
            lf_p.reshape(nb, seq, FOX_HEADS).astype(cache_logf.dtype),
            k_s.reshape(ns, dseq, *hs).astype(cache_k.dtype), v_s.reshape(ns, dseq, *hs).astype(cache_v.dtype),
            lf_s.reshape(ns, dseq, FOX_HEADS).astype(cache_logf.dtype))
```

```python
import functools
import math

import jax
import jax.numpy as jnp
from jax import lax
from jax.experimental import pallas as pl
from jax.experimental.pallas import tpu as pltpu

F32 = jnp.float32
BF16 = jnp.bfloat16
HIGHEST = lax.Precision.HIGHEST

HG_HEADS = 8
FOX_HEADS = 16
N_EXPERTS = 16
N_GROUPS = 4
EXPERTS_PER_GROUP = N_EXPERTS // N_GROUPS
EPS = 1e-6
NEG_BIG = -1e30
TINY = 1e-30

LANES = 128
SUBLANES = 8
VMEM_LIMIT_BYTES = 56 * 1024 * 1024

SCAN_CHUNK = 128
ATTN_BLOCK = 512
ATTN_ROWS = 256
LOG2E = math.log2(math.e)
PAGES_PER_STEP = 16


def _params(sem):
    return pltpu.CompilerParams(dimension_semantics=sem, vmem_limit_bytes=VMEM_LIMIT_BYTES)


def _nt_dot(a, b):
    return lax.dot_general(a, b, (((1,), (1,)), ((), ())), preferred_element_type=F32)


def _sigmoid(x):
    return 1.0 / (1.0 + jnp.exp(-x))


def _log_sigmoid(x):
    return jnp.minimum(x, 0.0) - jnp.log1p(jnp.exp(-jnp.abs(x)))


def _norm_mod(x, gain, shift, scale):
    ms = jnp.mean(x * x, axis=-1, keepdims=True)
    y = x * lax.rsqrt(ms + EPS) * gain
    return y * (1.0 + scale) + shift


class _Mod:
    def __init__(self, arr, d, tm, tiles_per_seq):
        self.arr = arr
        self.d = d
        self.tm = tm
        self.tiles_per_seq = tiles_per_seq
        self.per_token = arr.ndim == 2

    def spec(self, chunk):
        d, tps = self.d, self.tiles_per_seq
        if self.per_token:
            return pl.BlockSpec((self.tm, d), lambda i, *_: (i, chunk))
        return pl.BlockSpec((None, 1, d), lambda i, *_: (i // tps, 0, chunk))


def _ada_body(c_ref, w_ref, b_ref, o_ref):
    c = c_ref[...]
    a = c * _sigmoid(c)
    o_ref[0] = jnp.dot(a, w_ref[0], precision=HIGHEST, preferred_element_type=F32) + b_ref[0]


def _ada(c, w, b):
    n_layers, d, n = w.shape
    r = c.shape[0]
    tn = 1536 if n % 1536 == 0 else 1024
    return pl.pallas_call(
        _ada_body,
        grid=(n_layers, n // tn),
        in_specs=[pl.BlockSpec((r, d), lambda l, j: (0, 0)),
                  pl.BlockSpec((1, d, tn), lambda l, j: (l, 0, j)),
                  pl.BlockSpec((1, 1, tn), lambda l, j: (l, 0, j))],
        out_specs=pl.BlockSpec((1, r, tn), lambda l, j: (l, 0, j)),
        out_shape=jax.ShapeDtypeStruct((n_layers, r, n), F32),
        compiler_params=_params(("arbitrary", "arbitrary")),
        name="ada",
    )(c, w, b.reshape(n_layers, 1, n))


def _hgrn_in_body(x_ref, gain_ref, sh_ref, sc_ref, wq_ref, wf_ref, wi_ref, wg_ref,
                  q_ref, zf_ref, i_ref, g_ref):
    xn = _norm_mod(x_ref[...], gain_ref[...], sh_ref[...], sc_ref[...]).astype(BF16)
    q_ref[...] = jnp.dot(xn, wq_ref[...], preferred_element_type=F32).astype(q_ref.dtype)
    zf_ref[...] = jnp.dot(xn, wf_ref[...], preferred_element_type=F32)
    i_ref[...] = jnp.dot(xn, wi_ref[...], preferred_element_type=F32).astype(i_ref.dtype)
    g_ref[...] = jnp.dot(xn, wg_ref[...], preferred_element_type=F32).astype(g_ref.dtype)


def _hgrn_in_proj(h, gain, mod, w4, tm):
    t, d = h.shape
    row = pl.BlockSpec((tm, d), lambda i: (i, 0))
    full = pl.BlockSpec((d, d), lambda i: (0, 0))
    return pl.pallas_call(
        _hgrn_in_body,
        grid=(t // tm,),
        in_specs=[row, pl.BlockSpec((1, d), lambda i: (0, 0)), mod.spec(0), mod.spec(1),
                  full, full, full, full],
        out_specs=[row, row, row, row],
        out_shape=[jax.ShapeDtypeStruct((t, d), BF16), jax.ShapeDtypeStruct((t, d), F32),
                   jax.ShapeDtypeStruct((t, d), BF16), jax.ShapeDtypeStruct((t, d), BF16)],
        compiler_params=_params(("arbitrary",)),
        name="hgrn_in_proj",
    )(h, gain.reshape(1, d), mod.arr, mod.arr, *w4)


def _split3(x):
    hi = x.astype(BF16)
    r1 = x - hi.astype(F32)
    mid = r1.astype(BF16)
    lo = (r1 - mid.astype(F32)).astype(BF16)
    return hi, mid, lo


def _hgrn_scan_body(q_ref, zf_ref, i_ref, g_ref, lb_ref, gn_ref, s0_ref, o_ref, st_ref,
                    *, chunk, n_valid, dk):
    c = pl.program_id(1)

    @pl.when(c == 0)
    def _():
        st_ref[...] = s0_ref[...]

    row = lax.broadcasted_iota(jnp.int32, (chunk, chunk), 0)
    col = lax.broadcasted_iota(jnp.int32, (chunk, chunk), 1)
    tri = jnp.where(row >= col, 1.0, 0.0).astype(BF16)
    n_levels = chunk.bit_length() - 1
    differ = row ^ col
    on_diag = differ == 0
    at_level = [lax.shift_right_logical(differ, lv) == 1 for lv in range(n_levels)]
    tok = lax.broadcasted_iota(jnp.int32, (chunk, dk), 0)
    if n_valid < chunk:
        valid = tok < n_valid
    for h in range(HG_HEADS):
        hs = slice(h * dk, (h + 1) * dk)
        q = q_ref[:, hs].astype(F32)
        zf = zf_ref[:, hs]
        v = i_ref[:, hs]
        lb = lb_ref[:, hs]
        a1 = jnp.log(jnp.maximum(lb, TINY))
        a2 = jnp.log1p(-lb) + _log_sigmoid(zf)
        log_f = jnp.maximum(a1, a2) + jnp.log1p(jnp.exp(-jnp.abs(a1 - a2)))
        k = (1.0 - lb) / (1.0 + jnp.exp(zf))
        if n_valid < chunk:
            log_f = jnp.where(valid, log_f, 0.0)
            k = jnp.where(valid, k, 0.0)
        g_hi, g_mid, g_lo = _split3(log_f)
        b = (jnp.dot(tri, g_hi, preferred_element_type=F32)
             + jnp.dot(tri, g_mid, preferred_element_type=F32)
             + jnp.dot(tri, g_lo, preferred_element_type=F32))
        b_end = b[chunk - 1:chunk, :]
        a = jnp.where(on_diag, _nt_dot(q.astype(BF16), k.astype(BF16)), 0.0)
        b_dn1 = pltpu.roll(b, 1, 0)
        b_up1 = pltpu.roll(b, chunk - 1, 0)
        for lv in range(n_levels):
            half = 1 << lv
            if lv == 0:
                d_q, d_k = b - b_dn1, jnp.zeros_like(b)
            elif lv == 1:
                d_q = jnp.where((tok & 3) == 2, b - b_dn1, b - pltpu.roll(b, 2, 0))
                d_k = jnp.where((tok & 3) == 0, b_up1 - b, 0.0)
            else:
                nblk = chunk // (2 * half)
                edge = b.reshape(nblk, 2 * half, dk)[:, half - 1:half, :]
                edge = jnp.broadcast_to(edge, (nblk, 2 * half, dk)).reshape(chunk, dk)
                d_q, d_k = b - edge, edge - b
            right = (lax.shift_right_logical(tok, lv) & 1) == 1
            e_q = jnp.where(right, jnp.exp(jnp.minimum(d_q, 0.0)), 0.0)
            e_k = jnp.where(right, 0.0, jnp.exp(jnp.minimum(d_k, 0.0)))
            a_lv = _nt_dot((q * e_q).astype(BF16), (k * e_k).astype(BF16))
            a = jnp.where(at_level[lv], a_lv, a)
        s = st_ref[0, h]
        o = (jnp.dot(a.astype(BF16), v, preferred_element_type=F32)
             + jnp.dot((q * jnp.exp(b)).astype(BF16), s.astype(BF16), preferred_element_type=F32))
        k_d = (k * jnp.exp(b_end - b)).T.astype(BF16)
        decay = jnp.broadcast_to(jnp.exp(b_end), (dk, dk)).T
        st_ref[0, h] = decay * s + jnp.dot(k_d, v, preferred_element_type=F32)
        ms = jnp.mean(o * o, axis=-1, keepdims=True)
        gate = g_ref[:, hs].astype(F32)
        o = o * lax.rsqrt(ms + EPS) * gn_ref[...] * (gate * _sigmoid(gate))
        o_ref[:, hs] = o.astype(o_ref.dtype)


def _hgrn_scan(q, zf, i, g, lb, gnorm, s0, nseq, n_valid):
    tt, d = q.shape
    tp = tt // nseq
    nc = tp // SCAN_CHUNK
    dk = d // HG_HEADS
    row = pl.BlockSpec((SCAN_CHUNK, d), lambda b, c: (b * nc + c, 0))
    st = pl.BlockSpec((1, HG_HEADS, dk, dk), lambda b, c: (b, 0, 0, 0))
    body = functools.partial(_hgrn_scan_body, chunk=SCAN_CHUNK, n_valid=min(n_valid, SCAN_CHUNK), dk=dk)
    return pl.pallas_call(
        body,
        grid=(nseq, nc),
        in_specs=[row, row, row, row,
                  pl.BlockSpec((1, d), lambda b, c: (0, 0)),
                  pl.BlockSpec((1, dk), lambda b, c: (0, 0)),
                  st],
        out_specs=[row, st],
        out_shape=[jax.ShapeDtypeStruct((tt, d), BF16), jax.ShapeDtypeStruct(s0.shape, F32)],
        compiler_params=_params(("arbitrary", "arbitrary")),
        name="hgrn_scan",
    )(q, zf, i, g, lb.reshape(1, d), gnorm.reshape(1, dk), s0)


def _route_rows(lt):
    rows = [lt[e:e + 1, :] for e in range(N_EXPERTS)]
    m = functools.reduce(jnp.maximum, rows)
    ex = [jnp.exp(r - m) for r in rows]
    inv = 1.0 / functools.reduce(lambda x, y: x + y, ex)
    pr = [e * inv for e in ex]
    scores = []
    for gi in range(N_GROUPS):
        p4 = pr[gi * EXPERTS_PER_GROUP:(gi + 1) * EXPERTS_PER_GROUP]
        pairs = [p4[x] + p4[y] for x in range(EXPERTS_PER_GROUP) for y in range(x + 1, EXPERTS_PER_GROUP)]
        scores.append(functools.reduce(jnp.maximum, pairs))
    best = scores[0]
    sel = jnp.zeros(best.shape, jnp.int32)
    for gi in range(1, N_GROUPS):
        better = scores[gi] > best
        sel = jnp.where(better, gi, sel)
        best = jnp.where(better, scores[gi], best)
    cand = []
    for j in range(EXPERTS_PER_GROUP):
        cj = pr[(N_GROUPS - 1) * EXPERTS_PER_GROUP + j]
        for gi in range(N_GROUPS - 2, -1, -1):
            cj = jnp.where(sel == gi, pr[gi * EXPERTS_PER_GROUP + j], cj)
        cand.append(cj)

    def argmax4(vals):
        p, idx = vals[0], jnp.zeros(vals[0].shape, jnp.int32)
        for j in range(1, len(vals)):
            better = vals[j] > p
            idx = jnp.where(better, j, idx)
            p = jnp.where(better, vals[j], p)
        return p, idx

    p1, i1 = argmax4(cand)
    p2, i2 = argmax4([jnp.where(i1 == j, -1.0, cand[j]) for j in range(EXPERTS_PER_GROUP)])
    den = p1 + p2
    base = sel * EXPERTS_PER_GROUP
    return base + i1, base + i2, p1 / den, p2 / den


def _mix_router_body(a_ref, w_ref, h_ref, g1_ref, gain_ref, sh_ref, sc_ref, wrh_ref, wrl_ref, br_ref,
                     h_out, hn_out, gate_out):
    mix = jnp.dot(a_ref[...].astype(BF16), w_ref[...], preferred_element_type=F32)
    h = h_ref[...] + g1_ref[...] * mix
    h_out[...] = h
    hn = _norm_mod(h, gain_ref[...], sh_ref[...], sc_ref[...])
    x_hi = hn.astype(BF16)
    hn_out[...] = x_hi
    x_lo = (hn - x_hi.astype(F32)).astype(BF16)
    lt = (_nt_dot(wrh_ref[...], x_hi) + _nt_dot(wrl_ref[...], x_hi) + _nt_dot(wrh_ref[...], x_lo)
          + br_ref[...])
    e1, e2, g1, g2 = _route_rows(lt)
    tm = lt.shape[1]
    eid = lax.broadcasted_iota(jnp.int32, (LANES, tm), 0)
    dense_t = jnp.where(eid == e1, g1, 0.0) + jnp.where(eid == e2, g2, 0.0)
    gate_out[...] = dense_t.T


def _mix_router(a, w_out, h, gain, mod, wr_hi, wr_lo, b_router, tm):
    t, d = h.shape
    row = pl.BlockSpec((tm, d), lambda i: (i, 0))
    const = lambda shape: pl.BlockSpec(shape, lambda i: (0,) * len(shape))
    return pl.pallas_call(
        _mix_router_body,
        grid=(t // tm,),
        in_specs=[row, const((d, d)), row, mod.spec(2), const((1, d)), mod.spec(3), mod.spec(4),
                  const((N_EXPERTS, d)), const((N_EXPERTS, d)), const((N_EXPERTS, 1))],
        out_specs=[row, row, pl.BlockSpec((tm, LANES), lambda i: (i, 0))],
        out_shape=[jax.ShapeDtypeStruct((t, d), F32), jax.ShapeDtypeStruct((t, d), BF16),
                   jax.ShapeDtypeStruct((t, LANES), F32)],
        compiler_params=_params(("arbitrary",)),
        name="mix_router",
    )(a, w_out, h, mod.arr, gain.reshape(1, d), mod.arr, mod.arr, wr_hi, wr_lo,
      b_router.reshape(N_EXPERTS, 1).astype(F32))


def _moe_body(x_ref, win_ref, wout_ref, gate_ref, h_ref, g2_ref, o_ref, acc_ref):
    e = pl.program_id(1)

    @pl.when(e == 0)
    def _():
        acc_ref[...] = jnp.zeros_like(acc_ref)

    hid = jnp.dot(x_ref[...], win_ref[0], preferred_element_type=F32)
    de = hid.shape[1] // 2
    a, u = hid[:, :de], hid[:, de:]
    act = (a * _sigmoid(a) * u).astype(BF16)
    y = jnp.dot(act, wout_ref[0], preferred_element_type=F32)
    lane = lax.broadcasted_iota(jnp.int32, gate_ref.shape, 1)
    gcol = jnp.sum(jnp.where(lane == e, gate_ref[...], 0.0), axis=1, keepdims=True)
    acc_ref[...] += gcol * y

    @pl.when(e == pl.num_programs(1) - 1)
    def _():
        o_ref[...] = h_ref[...] + g2_ref[...] * acc_ref[...]


def _moe(hn, gates, h, mod, w_in, w_out, tm):
    t, d = h.shape
    n_e, _, d2 = w_in.shape
    row = pl.BlockSpec((tm, d), lambda i, e: (i, 0))
    return pl.pallas_call(
        _moe_body,
        grid=(t // tm, n_e),
        in_specs=[row,
                  pl.BlockSpec((1, d, d2), lambda i, e: (e, 0, 0)),
                  pl.BlockSpec((1, d2 // 2, d), lambda i, e: (e, 0, 0)),
                  pl.BlockSpec((tm, LANES), lambda i, e: (i, 0)),
                  row, mod.spec(5)],
        out_specs=row,
        out_shape=jax.ShapeDtypeStruct((t, d), F32),
        scratch_shapes=[pltpu.VMEM((tm, d), F32)],
        compiler_params=_params(("arbitrary", "arbitrary")),
        name="moe",
    )(hn, w_in, w_out, gates, h, mod.arr)


def _head_norm(y, gsum_ref, gain_ref, head_dim):
    sq = y * y
    hi = sq.astype(BF16)
    lo = (sq - hi.astype(F32)).astype(BF16)
    ssum = (jnp.dot(hi, gsum_ref[...], preferred_element_type=F32)
            + jnp.dot(lo, gsum_ref[...], preferred_element_type=F32))
    return y * lax.rsqrt(ssum * (1.0 / head_dim) + EPS) * gain_ref[...]


def _head_sum_matrix(d, head_dim):
    r = jnp.arange(d) // head_dim
    return (r[:, None] == r[None, :]).astype(BF16)


def _q_proj_body(x_ref, gain_ref, sh_ref, sc_ref, w_ref, gsum_ref, qn_ref, q_ref, *, head_dim, q_scale):
    xn = _norm_mod(x_ref[...], gain_ref[...], sh_ref[...], sc_ref[...]).astype(BF16)
    y = jnp.dot(xn, w_ref[...], preferred_element_type=F32)
    q = _head_norm(y, gsum_ref, qn_ref, head_dim) * q_scale
    q_ref[...] = q.astype(q_ref.dtype)


def _q_proj(h, gain, mod, w, gsum, q_norm, tm, out_dtype, q_scale):
    t, d = h.shape
    head_dim = d // FOX_HEADS
    row = pl.BlockSpec((tm, d), lambda i: (i, 0))
    const = lambda shape: pl.BlockSpec(shape, lambda i: (0,) * len(shape))
    return pl.pallas_call(
        functools.partial(_q_proj_body, head_dim=head_dim, q_scale=q_scale),
        grid=(t // tm,),
        in_specs=[row, const((1, d)), mod.spec(0), mod.spec(1), const((d, d)), const((d, d)), const((1, d))],
        out_specs=row,
        out_shape=jax.ShapeDtypeStruct((t, d), out_dtype),
        compiler_params=_params(("arbitrary",)),
        name="q_proj",
    )(h, gain.reshape(1, d), mod.arr, mod.arr, w, gsum, jnp.tile(q_norm, FOX_HEADS).reshape(1, d))


def _kv_proj_body(x_ref, gain_ref, sh_ref, sc_ref, wk_ref, wv_ref, wf_ref, gsum_ref, kn_ref, bf_ref,
                  k_ref, v_ref, kb_ref, vb_ref, lf_ref, *, head_dim):
    xn = _norm_mod(x_ref[...], gain_ref[...], sh_ref[...], sc_ref[...]).astype(BF16)
    k = _head_norm(jnp.dot(xn, wk_ref[...], preferred_element_type=F32), gsum_ref, kn_ref, head_dim)
    k_ref[...] = k
    kb_ref[...] = k.astype(BF16)
    v = jnp.dot(xn, wv_ref[...], preferred_element_type=F32)
    v_ref[...] = v
    vb_ref[...] = v.astype(BF16)
    zf = jnp.dot(xn, wf_ref[...], preferred_element_type=F32) + bf_ref[...]
    lf_ref[...] = _log_sigmoid(zf)[:, :lf_ref.shape[1]]


def _kv_proj(h, gain, mod, wk, wv, wf, gsum, k_norm, b_fgate, tm):
    t, d = h.shape
    head_dim = d // FOX_HEADS
    row = pl.BlockSpec((tm, d), lambda i: (i, 0))
    const = lambda shape: pl.BlockSpec(shape, lambda i: (0,) * len(shape))
    bf = jnp.pad(b_fgate.astype(F32), (0, LANES - FOX_HEADS)).reshape(1, LANES)
    return pl.pallas_call(
        functools.partial(_kv_proj_body, head_dim=head_dim),
        grid=(t // tm,),
        in_specs=[row, const((1, d)), mod.spec(0), mod.spec(1), const((d, d)), const((d, d)),
                  const((d, LANES)), const((d, d)), const((1, d)), const((1, LANES))],
        out_specs=[row, row, row, row, pl.BlockSpec((tm, FOX_HEADS), lambda i: (i, 0))],
        out_shape=[jax.ShapeDtypeStruct((t, d), F32), jax.ShapeDtypeStruct((t, d), F32),
                   jax.ShapeDtypeStruct((t, d), BF16), jax.ShapeDtypeStruct((t, d), BF16),
                   jax.ShapeDtypeStruct((t, FOX_HEADS), F32)],
        compiler_params=_params(("arbitrary",)),
        name="kv_proj",
    )(h, gain.reshape(1, d), mod.arr, mod.arr, wk, wv, wf, gsum,
      jnp.tile(k_norm, FOX_HEADS).reshape(1, d), bf)


def _cumsum_lanes_body(x_ref, o_ref, carry_ref, *, carry_rows):
    j = pl.program_id(1)

    @pl.when(j == 0)
    def _():
        carry_ref[...] = jnp.zeros_like(carry_ref)

    w = x_ref.shape[-1]
    r = lax.broadcasted_iota(jnp.int32, (w, w), 0)
    c = lax.broadcasted_iota(jnp.int32, (w, w), 1)
    upper = jnp.where(r <= c, 1.0, 0.0).astype(F32)
    y = jnp.dot(x_ref[0], upper, precision=HIGHEST, preferred_element_type=F32)
    if carry_rows:
        y = y + carry_ref[...]
        carry_ref[...] = y[:, w - 1:w]
    o_ref[0] = y


def _cumsum_lanes(x, width, carry):
    b, r, t = x.shape
    spec = pl.BlockSpec((1, r, width), lambda i, j: (i, 0, j))
    return pl.pallas_call(
        functools.partial(_cumsum_lanes_body, carry_rows=carry),
        grid=(b, t // width),
        in_specs=[spec],
        out_specs=spec,
        out_shape=jax.ShapeDtypeStruct(x.shape, F32),
        scratch_shapes=[pltpu.VMEM((r, 1), F32)],
        compiler_params=_params(("arbitrary", "arbitrary")),
        name="cumsum_lanes",
    )(x)


def _attn_prompt_body(qt_ref, kt_ref, q_ref, k_ref, v_ref, f_ref, o_ref, m_ref, acc_ref,
                      *, blk, rows, head_dim):
    p = pl.program_id(2)
    qi = qt_ref[p]
    ki = kt_ref[p]

    @pl.when(ki == 0)
    def _():
        m_ref[...] = jnp.full(m_ref.shape, NEG_BIG, F32)
        acc_ref[...] = jnp.zeros_like(acc_ref)

    def process(masked):
        k = k_ref[...]
        v = v_ref[...]
        f = f_ref[...]
        first_v = lax.broadcasted_iota(jnp.int32, v.shape, 1) < head_dim
        ones = jnp.ones_like(v)
        v_heads = (jnp.where(first_v, v, ones), jnp.where(first_v, ones, v))
        for r0 in range(0, blk, rows):
            q = q_ref[r0:r0 + rows, :]
            first_q = lax.broadcasted_iota(jnp.int32, q.shape, 1) < head_dim
            zero = jnp.zeros_like(q)
            for hh, qh in enumerate((jnp.where(first_q, q, zero), jnp.where(first_q, zero, q))):
                s = _nt_dot(qh, k) - f[hh:hh + 1, :]
                if masked:
                    row = lax.broadcasted_iota(jnp.int32, s.shape, 0) + r0
                    col = lax.broadcasted_iota(jnp.int32, s.shape, 1)
                    s = jnp.where(col <= row, s, NEG_BIG)
                m_old = m_ref[hh, r0:r0 + rows, :]
                m_new = jnp.maximum(m_old, jnp.max(s, axis=1, keepdims=True))
                alpha = jnp.exp2(m_old - m_new)
                pr = jnp.concatenate([jnp.exp2(s[:, c * LANES:(c + 1) * LANES] - m_new)
                                      for c in range(blk // LANES)], axis=1).astype(BF16)
                acc_ref[hh, r0:r0 + rows, :] = (alpha * acc_ref[hh, r0:r0 + rows, :]
                                                + jnp.dot(pr, v_heads[hh], preferred_element_type=F32))
                m_ref[hh, r0:r0 + rows, :] = m_new

    @pl.when(ki < qi)
    def _():
        process(False)

    @pl.when(ki == qi)
    def _():
        process(True)
        a0 = acc_ref[0]
        a1 = acc_ref[1]
        first = lax.broadcasted_iota(jnp.int32, a0.shape, 1) < head_dim
        inv0 = 1.0 / a0[:, head_dim:head_dim + 1]
        inv1 = 1.0 / a1[:, 0:1]
        o_ref[...] = jnp.where(first, a0 * inv0, a1 * inv1).astype(o_ref.dtype)


def _attn_prompt(q, k, v, f_pairs, nseq):
    tt, d = q.shape
    t = tt // nseq
    blk = min(ATTN_BLOCK, t)
    rows = min(ATTN_ROWS, blk)
    nb = t // blk
    head_dim = d // FOX_HEADS
    pairs = [(qi, ki) for qi in range(nb) for ki in range(qi + 1)]
    q_tab = jnp.asarray([p[0] for p in pairs], jnp.int32)
    k_tab = jnp.asarray([p[1] for p in pairs], jnp.int32)
    grid_spec = pltpu.PrefetchScalarGridSpec(
        num_scalar_prefetch=2,
        grid=(nseq, d // LANES, len(pairs)),
        in_specs=[pl.BlockSpec((blk, LANES), lambda b, hp, p, qt, kt: (b * nb + qt[p], hp)),
                  pl.BlockSpec((blk, LANES), lambda b, hp, p, qt, kt: (b * nb + kt[p], hp)),
                  pl.BlockSpec((blk, LANES), lambda b, hp, p, qt, kt: (b * nb + kt[p], hp)),
                  pl.BlockSpec((None, None, 2, blk), lambda b, hp, p, qt, kt: (b, hp, 0, kt[p]))],
        out_specs=pl.BlockSpec((blk, LANES), lambda b, hp, p, qt, kt: (b * nb + qt[p], hp)),
        scratch_shapes=[pltpu.VMEM((2, blk, LANES), F32), pltpu.VMEM((2, blk, LANES), F32)])
    return pl.pallas_call(
        functools.partial(_attn_prompt_body, blk=blk, rows=rows, head_dim=head_dim),
        grid_spec=grid_spec,
        out_shape=jax.ShapeDtypeStruct((tt, d), BF16),
        compiler_params=_params(("arbitrary", "arbitrary", "arbitrary")),
        name="attn_prompt",
    )(q_tab, k_tab, q, k, v, f_pairs)


def _gather_cum_body(pt_ref, cp_ref, new_ref, o_ref, *, n_pages, page):
    b = pl.program_id(0)

    def step(j, carry):
        idx = pt_ref[b * n_pages + j]
        blk = cp_ref[idx] + carry
        o_ref[0, :, pl.ds(pl.multiple_of(j * page, page), page)] = blk
        return blk[:, page - 1:page]

    carry = lax.fori_loop(0, n_pages, step, jnp.zeros((cp_ref.shape[1], 1), F32))
    o_ref[0, :, n_pages * page:] = new_ref[0] + carry


def _gather_cum(page_table, cum_pages, cum_new):
    nseq, n_pages = page_table.shape
    n_pool, hh, page = cum_pages.shape
    grid_spec = pltpu.PrefetchScalarGridSpec(
        num_scalar_prefetch=1,
        grid=(nseq,),
        in_specs=[pl.BlockSpec((n_pool, hh, page), lambda b, pt: (0, 0, 0)),
                  pl.BlockSpec((1, hh, page), lambda b, pt: (b, 0, 0))],
        out_specs=pl.BlockSpec((1, hh, (n_pages + 1) * page), lambda b, pt: (b, 0, 0)))
    return pl.pallas_call(
        functools.partial(_gather_cum_body, n_pages=n_pages, page=page),
        grid_spec=grid_spec,
        out_shape=jax.ShapeDtypeStruct((nseq, hh, (n_pages + 1) * page), F32),
        compiler_params=_params(("arbitrary",)),
        name="gather_cum",
    )(page_table.reshape(-1), cum_pages, cum_new)


def _attn_paged_body(pt_ref, *refs, n_q, page, pps, n_steps, head_dim):
    q_ref = refs[0]
    k_refs = refs[1:1 + pps]
    v_refs = refs[1 + pps:1 + 2 * pps]
    kn_ref, vn_ref, f_ref, o_ref, qbd_ref, m_ref, l_ref, acc_ref = refs[1 + 2 * pps:]
    j = pl.program_id(1)
    d = q_ref.shape[1]
    rows = FOX_HEADS * n_q

    @pl.when(j == 0)
    def _():
        q = q_ref[...]
        lane_head = lax.broadcasted_iota(jnp.int32, q.shape, 1) // head_dim
        for h in range(FOX_HEADS):
            qbd_ref[h * n_q:(h + 1) * n_q, :] = jnp.where(lane_head == h, q, 0.0).astype(BF16)
        m_ref[...] = jnp.full(m_ref.shape, NEG_BIG, F32)
        l_ref[...] = jnp.zeros_like(l_ref)
        acc_ref[...] = jnp.zeros_like(acc_ref)

    def attend(k, v, f, mask):
        bias = jnp.concatenate([jnp.broadcast_to(f[h:h + 1, :], (n_q, page)) for h in range(FOX_HEADS)], axis=0)
        s = _nt_dot(qbd_ref[...], k) - bias
        if mask is not None:
            s = jnp.where(mask, s, NEG_BIG)
        m_old = m_ref[...]
        m_new = jnp.maximum(m_old, jnp.max(s, axis=1, keepdims=True))
        alpha = jnp.exp(m_old - m_new)
        pr = jnp.exp(s - m_new)
        l_ref[...] = alpha * l_ref[...] + jnp.sum(pr, axis=1, keepdims=True)
        m_ref[...] = m_new
        acc_ref[...] = alpha * acc_ref[...] + jnp.dot(pr.astype(BF16), v, preferred_element_type=F32)

    @pl.when(j < n_steps - 1)
    def _():
        for u in range(pps):
            off = pl.multiple_of((j * pps + u) * page, page)
            attend(k_refs[u][...], v_refs[u][...], f_ref[0, :, pl.ds(off, page)], None)

    @pl.when(j == n_steps - 1)
    def _():
        pad = jnp.zeros((page - n_q, d), F32)
        k = jnp.concatenate([kn_ref[...], pad], axis=0).astype(BF16)
        v = jnp.concatenate([vn_ref[...], pad], axis=0).astype(BF16)
        t_of_row = lax.broadcasted_iota(jnp.int32, (rows, page), 0) % n_q
        key = lax.broadcasted_iota(jnp.int32, (rows, page), 1)
        attend(k, v, f_ref[0, :, (n_steps - 1) * pps * page:], key <= t_of_row)
        acc = acc_ref[...] * (1.0 / l_ref[...])
        lane_head = lax.broadcasted_iota(jnp.int32, (n_q, d), 1) // head_dim
        out = jnp.zeros((n_q, d), F32)
        for h in range(FOX_HEADS):
            out = out + jnp.where(lane_head == h, acc[h * n_q:(h + 1) * n_q, :], 0.0)
        o_ref[...] = out


def _attn_paged(q, cache_k, cache_v, k_new, v_new, f_all, page_table):
    tt, d = q.shape
    nseq, n_pages = page_table.shape
    n_q = tt // nseq
    page = cache_k.shape[1]
    pps = PAGES_PER_STEP
    n_steps = n_pages // pps + 1
    head_dim = d // FOX_HEADS
    rows = FOX_HEADS * n_q

    def page_spec(u):
        def index(b, j, pt):
            return (pt[b * n_pages + jnp.minimum(j, n_steps - 2) * pps + u], 0, 0)
        return pl.BlockSpec((None, page, d), index)

    tok = pl.BlockSpec((n_q, d), lambda b, j, pt: (b, 0))
    grid_spec = pltpu.PrefetchScalarGridSpec(
        num_scalar_prefetch=1,
        grid=(nseq, n_steps),
        in_specs=([tok] + [page_spec(u) for u in range(pps)] + [page_spec(u) for u in range(pps)]
                  + [tok, tok, pl.BlockSpec((1, FOX_HEADS, f_all.shape[2]), lambda b, j, pt: (b, 0, 0))]),
        out_specs=tok,
        scratch_shapes=[pltpu.VMEM((rows, d), BF16), pltpu.VMEM((rows, 1), F32),
                        pltpu.VMEM((rows, 1), F32), pltpu.VMEM((rows, d), F32)])
    body = functools.partial(_attn_paged_body, n_q=n_q, page=page, pps=pps, n_steps=n_steps,
                             head_dim=head_dim)
    return pl.pallas_call(
        body,
        grid_spec=grid_spec,
        out_shape=jax.ShapeDtypeStruct((tt, d), F32),
        compiler_params=_params(("arbitrary", "arbitrary")),
        name="attn_paged",
    )(page_table.reshape(-1), q, *([cache_k] * pps), *([cache_v] * pps), k_new, v_new, f_all)


def _lower_bounds(lb_logits):
    p = jax.nn.softmax(lb_logits.astype(F32), axis=0)
    return jnp.cumsum(p, axis=0) - p[0]


def _trunk(x, nseq, mods, mod_kv, s0, past, w):
    tt, d = x.shape
    t = tt // nseq
    tm = min(512, tt)
    tiles_per_seq = max(t // tm, 1)
    n_a = w["w_in_a"].shape[0]
    depth = w["w_ada"].shape[0]
    head_dim = d // FOX_HEADS
    lbs = _lower_bounds(w["lb_logits"])
    mk = lambda arr: _Mod(arr, d, tm, tiles_per_seq)

    h = x
    states = []
    for l in range(depth):
        mod = mk(mods[l])
        if l < n_a:
            q, zf, i, g = _hgrn_in_proj(h, w["norm_mix"][l], mod, w["w_in_a4"][l], tm)
            tp = -(-t // SCAN_CHUNK) * SCAN_CHUNK
            if tp != t:
                padr = lambda a: jnp.pad(a.reshape(nseq, t, d), ((0, 0), (0, tp - t), (0, 0))).reshape(nseq * tp, d)
                q, zf, i, g = padr(q), padr(zf), padr(i), padr(g)
            o, s_t = _hgrn_scan(q, zf, i, g, lbs[l], w["gnorm_a"][l], s0[l], nseq, t)
            if tp != t:
                o = o.reshape(nseq, tp, d)[:, :t].reshape(tt, d)
            states.append(s_t)
            mix_in, w_out = o, w["w_out_a"][l]
        else:
            j = l - n_a
            if past is None:
                q = _q_proj(h, w["norm_mix"][l], mod, w["w_q_b"][j], w["gsum"], w["q_norm"][j], tm, BF16,
                            head_dim ** -0.5 * LOG2E)
                mix_in = _attn_prompt(q, k_bf, v_bf, f_keys, nseq)
            else:
                q = _q_proj(h, w["norm_mix"][l], mod, w["w_q_b"][j], w["gsum"], w["q_norm"][j], tm, F32,
                            head_dim ** -0.5)
                mix_in = _attn_paged(q, past[0], past[1], k_new, v_new, f_keys, past[3])
            w_out = w["w_out_b"][j]
        h, hn, gates = _mix_router(mix_in, w_out, h, w["norm_ffn"][l], mod, w["wr_hi"], w["wr_lo"],
                                   w["b_router"], tm)
        h = _moe(hn, gates, h, mod, w["w_exp_in"][l], w["w_exp_out"][l], tm)
        if l == n_a - 1:
            k_new, v_new, k_bf, v_bf, logf_new = _kv_proj(
                h, w["norm_kv"], mk(mod_kv), w["w_k"], w["w_v"], w["w_f"], w["gsum"], w["k_norm"],
                w["b_fgate"], tm)
            lf_t = logf_new.reshape(nseq, t, FOX_HEADS).transpose(0, 2, 1)
            if past is None:
                f_cum = _cumsum_lanes(lf_t, min(512, t), carry=True)
                f_keys = (f_cum * LOG2E).reshape(nseq, FOX_HEADS // 2, 2, t)
            else:
                cache_logf, page_table = past[2], past[3]
                page = cache_logf.shape[1]
                pool_t = cache_logf.astype(F32).transpose(0, 2, 1)
                n_pool = pool_t.shape[0]
                rows = 64 if n_pool % 64 == 0 else 1
                cum_pages = _cumsum_lanes(pool_t.reshape(n_pool // rows, rows * FOX_HEADS, page), page,
                                          carry=False).reshape(n_pool, FOX_HEADS, page)
                new_pad = jnp.pad(lf_t, ((0, 0), (0, 0), (0, page - t)))
                cum_new = _cumsum_lanes(new_pad, page, carry=False)
                f_keys = _gather_cum(page_table, cum_pages, cum_new)
    return h, jnp.stack(states), k_new, v_new, logf_new


def kernel(x_prompt, x_sample, cache_k, cache_v, cache_logf, state_hgrn, page_table, c_prompt, c_sample,
           w_ada, b_ada, norm_mix, norm_ffn, w_in_a, lb_logits, gnorm_a, w_out_a, norm_kv, w_ada_kv,
           b_ada_kv, w_kv, b_fgate, k_norm, w_q_b, q_norm, w_out_b, w_router, b_router, w_exp_in, w_exp_out):
    nb, seq, d = x_prompt.shape
    ns, dseq, _ = x_sample.shape
    n_a = w_in_a.shape[0]
    depth = w_ada.shape[0]
    hd = FOX_HEADS * (d // FOX_HEADS)
    dk = d // HG_HEADS

    nrow = nb + ns
    nrow_pad = -(-nrow // SUBLANES) * SUBLANES
    c_all = jnp.pad(jnp.concatenate([c_prompt, c_sample], axis=0).astype(F32), ((0, nrow_pad - nrow), (0, 0)))
    mod_all = _ada(c_all, w_ada, b_ada)
    mod_kv_all = _ada(c_all, w_ada_kv[None], b_ada_kv[None])[0]
    mods_p = [mod_all[l, :nb].reshape(nb, 1, 6 * d) for l in range(depth)]
    mods_s = [jnp.repeat(mod_all[l, nb:nrow], dseq, axis=0) for l in range(depth)]
    mod_kv_p = mod_kv_all[:nb].reshape(nb, 1, 2 * d)
    mod_kv_s = jnp.repeat(mod_kv_all[nb:nrow], dseq, axis=0)

    wr_t = w_router.astype(F32).T
    wr_hi = wr_t.astype(BF16)
    wr_lo = (wr_t - wr_hi.astype(F32)).astype(BF16)
    w_in_b = w_in_a.astype(BF16)
    w = dict(
        w_ada=w_ada, lb_logits=lb_logits, norm_mix=norm_mix, norm_ffn=norm_ffn, gnorm_a=gnorm_a,
        w_in_a=w_in_a,
        w_in_a4=[[w_in_b[l, :, s * d:(s + 1) * d] for s in range(4)] for l in range(n_a)],
        w_out_a=w_out_a.astype(BF16), norm_kv=norm_kv,
        w_k=w_kv[:, :hd].astype(BF16), w_v=w_kv[:, hd:2 * hd].astype(BF16),
        w_f=jnp.pad(w_kv[:, 2 * hd:], ((0, 0), (0, LANES - FOX_HEADS))).astype(BF16),
        b_fgate=b_fgate, k_norm=k_norm, w_q_b=w_q_b.astype(BF16), q_norm=q_norm,
        w_out_b=w_out_b.astype(BF16), wr_hi=wr_hi, wr_lo=wr_lo, b_router=b_router,
        w_exp_in=w_exp_in.astype(BF16), w_exp_out=w_exp_out.astype(BF16),
        gsum=_head_sum_matrix(d, d // FOX_HEADS),
    )

    s0_p = jnp.zeros((n_a, nb, HG_HEADS, dk, dk), F32)
    y_p, st_p, k_p, v_p, lf_p = _trunk(x_prompt.reshape(nb * seq, d), nb, mods_p, mod_kv_p, s0_p, None, w)

    n_pool, page = cache_k.shape[0], cache_k.shape[1]
    past = (cache_k.astype(BF16).reshape(n_pool, page, hd), cache_v.astype(BF16).reshape(n_pool, page, hd),
            cache_logf, page_table)
    y_s, st_s, k_s, v_s, lf_s = _trunk(x_sample.reshape(ns * dseq, d), ns, mods_s, mod_kv_s,
                                       state_hgrn.astype(F32), past, w)

    hs = (FOX_HEADS, d // FOX_HEADS)
    return (y_p.reshape(nb, seq, d), y_s.reshape(ns, dseq, d),
            st_p.astype(state_hgrn.dtype), st_s.astype(state_hgrn.dtype),
            k_p.reshape(nb, seq, *hs).astype(cache_k.dtype), v_p.reshape(nb, seq, *hs).astype(cache_v.dtype),
            lf_p.reshape(nb, seq, FOX_HEADS).astype(cache_logf.dtype),
            k_s.reshape(ns, dseq, *hs).astype(cache_k.dtype), v_s.reshape(ns, dseq, *hs).astype(cache_v.dtype),
            lf_s.reshape(ns, dseq, FOX_HEADS).astype(cache_logf.dtype))
```

```python
import functools
import math

import jax
import jax.numpy as jnp
from jax import lax
from jax.experimental import pallas as pl
from jax.experimental.pallas import tpu as pltpu

F32 = jnp.float32
BF16 = jnp.bfloat16
HIGHEST = lax.Precision.HIGHEST

HG_HEADS = 8
FOX_HEADS = 16
N_EXPERTS = 16
N_GROUPS = 4
EXPERTS_PER_GROUP = N_EXPERTS // N_GROUPS
EPS = 1e-6
NEG_BIG = -1e30
TINY = 1e-30

LANES = 128
SUBLANES = 8
VMEM_LIMIT_BYTES = 56 * 1024 * 1024

SCAN_CHUNK = 128
ATTN_BLOCK = 512
ATTN_ROWS = 256
LOG2E = math.log2(math.e)
PAGES_PER_STEP = 8
EXPERT_TILE = 256
SMEM_I32_CHUNK = 1024


def _params(sem):
    return pltpu.CompilerParams(dimension_semantics=sem, vmem_limit_bytes=VMEM_LIMIT_BYTES)


def _nt_dot(a, b):
    return lax.dot_general(a, b, (((1,), (1,)), ((), ())), preferred_element_type=F32)


def _sigmoid(x):
    return 1.0 / (1.0 + jnp.exp(-x))


def _log_sigmoid(x):
    return jnp.minimum(x, 0.0) - jnp.log1p(jnp.exp(-jnp.abs(x)))


def _norm_mod(x, gain, shift, scale):
    ms = jnp.mean(x * x, axis=-1, keepdims=True)
    y = x * lax.rsqrt(ms + EPS) * gain
    return y * (1.0 + scale) + shift


class _Mod:
    def __init__(self, arr, d, tm, tiles_per_seq):
        self.arr = arr
        self.d = d
        self.tm = tm
        self.tiles_per_seq = tiles_per_seq
        self.per_token = arr.ndim == 2

    def spec(self, chunk):
        d, tps = self.d, self.tiles_per_seq
        if self.per_token:
            return pl.BlockSpec((self.tm, d), lambda i, *_: (i, chunk))
        return pl.BlockSpec((None, 1, d), lambda i, *_: (i // tps, 0, chunk))


def _ada_body(c_ref, w_ref, b_ref, o_ref):
    c = c_ref[...]
    a = c * _sigmoid(c)
    o_ref[0] = jnp.dot(a, w_ref[0], precision=HIGHEST, preferred_element_type=F32) + b_ref[0]


def _ada(c, w, b):
    n_layers, d, n = w.shape
    r = c.shape[0]
    tn = 1536 if n % 1536 == 0 else 1024
    return pl.pallas_call(
        _ada_body,
        grid=(n_layers, n // tn),
        in_specs=[pl.BlockSpec((r, d), lambda l, j: (0, 0)),
                  pl.BlockSpec((1, d, tn), lambda l, j: (l, 0, j)),
                  pl.BlockSpec((1, 1, tn), lambda l, j: (l, 0, j))],
        out_specs=pl.BlockSpec((1, r, tn), lambda l, j: (l, 0, j)),
        out_shape=jax.ShapeDtypeStruct((n_layers, r, n), F32),
        compiler_params=_params(("arbitrary", "arbitrary")),
        name="ada",
    )(c, w, b.reshape(n_layers, 1, n))


def _hgrn_in_body(x_ref, gain_ref, sh_ref, sc_ref, wq_ref, wf_ref, wi_ref, wg_ref,
                  q_ref, zf_ref, i_ref, g_ref):
    xn = _norm_mod(x_ref[...], gain_ref[...], sh_ref[...], sc_ref[...]).astype(BF16)
    q_ref[...] = jnp.dot(xn, wq_ref[...], preferred_element_type=F32).astype(q_ref.dtype)
    zf_ref[...] = jnp.dot(xn, wf_ref[...], preferred_element_type=F32)
    i_ref[...] = jnp.dot(xn, wi_ref[...], preferred_element_type=F32).astype(i_ref.dtype)
    g_ref[...] = jnp.dot(xn, wg_ref[...], preferred_element_type=F32).astype(g_ref.dtype)


def _hgrn_in_proj(h, gain, mod, w4, tm):
    t, d = h.shape
    row = pl.BlockSpec((tm, d), lambda i: (i, 0))
    full = pl.BlockSpec((d, d), lambda i: (0, 0))
    return pl.pallas_call(
        _hgrn_in_body,
        grid=(t // tm,),
        in_specs=[row, pl.BlockSpec((1, d), lambda i: (0, 0)), mod.spec(0), mod.spec(1),
                  full, full, full, full],
        out_specs=[row, row, row, row],
        out_shape=[jax.ShapeDtypeStruct((t, d), BF16), jax.ShapeDtypeStruct((t, d), F32),
                   jax.ShapeDtypeStruct((t, d), BF16), jax.ShapeDtypeStruct((t, d), BF16)],
        compiler_params=_params(("arbitrary",)),
        name="hgrn_in_proj",
    )(h, gain.reshape(1, d), mod.arr, mod.arr, *w4)


def _split3(x):
    hi = x.astype(BF16)
    r1 = x - hi.astype(F32)
    mid = r1.astype(BF16)
    lo = (r1 - mid.astype(F32)).astype(BF16)
    return hi, mid, lo


def _hgrn_scan_body(q_ref, zf_ref, i_ref, g_ref, lb_ref, gn_ref, s0_ref, o_ref, st_ref,
                    *, chunk, n_valid, dk):
    c = pl.program_id(1)

    @pl.when(c == 0)
    def _():
        st_ref[...] = s0_ref[...]

    row = lax.broadcasted_iota(jnp.int32, (chunk, chunk), 0)
    col = lax.broadcasted_iota(jnp.int32, (chunk, chunk), 1)
    tri = jnp.where(row >= col, 1.0, 0.0).astype(BF16)
    n_levels = chunk.bit_length() - 1
    differ = row ^ col
    on_diag = differ == 0
    at_level = [lax.shift_right_logical(differ, lv) == 1 for lv in range(n_levels)]
    tok = lax.broadcasted_iota(jnp.int32, (chunk, dk), 0)
    if n_valid < chunk:
        valid = tok < n_valid
    for h in range(HG_HEADS):
        hs = slice(h * dk, (h + 1) * dk)
        q = q_ref[:, hs].astype(F32)
        zf = zf_ref[:, hs]
        v = i_ref[:, hs]
        lb = lb_ref[:, hs]
        a1 = jnp.log(jnp.maximum(lb, TINY))
        a2 = jnp.log1p(-lb) + _log_sigmoid(zf)
        log_f = jnp.maximum(a1, a2) + jnp.log1p(jnp.exp(-jnp.abs(a1 - a2)))
        k = (1.0 - lb) / (1.0 + jnp.exp(zf))
        if n_valid < chunk:
            log_f = jnp.where(valid, log_f, 0.0)
            k = jnp.where(valid, k, 0.0)
        g_hi, g_mid, g_lo = _split3(log_f)
        b = (jnp.dot(tri, g_hi, preferred_element_type=F32)
             + jnp.dot(tri, g_mid, preferred_element_type=F32)
             + jnp.dot(tri, g_lo, preferred_element_type=F32))
        b_end = b[chunk - 1:chunk, :]
        a = jnp.where(on_diag, _nt_dot(q.astype(BF16), k.astype(BF16)), 0.0)
        b_dn1 = pltpu.roll(b, 1, 0)
        b_up1 = pltpu.roll(b, chunk - 1, 0)
        for lv in range(n_levels):
            half = 1 << lv
            if lv == 0:
                d_q, d_k = b - b_dn1, jnp.zeros_like(b)
            elif lv == 1:
                d_q = jnp.where((tok & 3) == 2, b - b_dn1, b - pltpu.roll(b, 2, 0))
                d_k = jnp.where((tok & 3) == 0, b_up1 - b, 0.0)
            else:
                nblk = chunk // (2 * half)
                edge = b.reshape(nblk, 2 * half, dk)[:, half - 1:half, :]
                edge = jnp.broadcast_to(edge, (nblk, 2 * half, dk)).reshape(chunk, dk)
                d_q, d_k = b - edge, edge - b
            right = (lax.shift_right_logical(tok, lv) & 1) == 1
            e_q = jnp.where(right, jnp.exp(jnp.minimum(d_q, 0.0)), 0.0)
            e_k = jnp.where(right, 0.0, jnp.exp(jnp.minimum(d_k, 0.0)))
            a_lv = _nt_dot((q * e_q).astype(BF16), (k * e_k).astype(BF16))
            a = jnp.where(at_level[lv], a_lv, a)
        s = st_ref[0, h]
        o = (jnp.dot(a.astype(BF16), v, preferred_element_type=F32)
             + jnp.dot((q * jnp.exp(b)).astype(BF16), s.astype(BF16), preferred_element_type=F32))
        k_d = (k * jnp.exp(b_end - b)).T.astype(BF16)
        decay = jnp.broadcast_to(jnp.exp(b_end), (dk, dk)).T
        st_ref[0, h] = decay * s + jnp.dot(k_d, v, preferred_element_type=F32)
        ms = jnp.mean(o * o, axis=-1, keepdims=True)
        gate = g_ref[:, hs].astype(F32)
        o = o * lax.rsqrt(ms + EPS) * gn_ref[...] * (gate * _sigmoid(gate))
        o_ref[:, hs] = o.astype(o_ref.dtype)


def _hgrn_scan(q, zf, i, g, lb, gnorm, s0, nseq, n_valid):
    tt, d = q.shape
    tp = tt // nseq
    nc = tp // SCAN_CHUNK
    dk = d // HG_HEADS
    row = pl.BlockSpec((SCAN_CHUNK, d), lambda b, c: (b * nc + c, 0))
    st = pl.BlockSpec((1, HG_HEADS, dk, dk), lambda b, c: (b, 0, 0, 0))
    body = functools.partial(_hgrn_scan_body, chunk=SCAN_CHUNK, n_valid=min(n_valid, SCAN_CHUNK), dk=dk)
    return pl.pallas_call(
        body,
        grid=(nseq, nc),
        in_specs=[row, row, row, row,
                  pl.BlockSpec((1, d), lambda b, c: (0, 0)),
                  pl.BlockSpec((1, dk), lambda b, c: (0, 0)),
                  st],
        out_specs=[row, st],
        out_shape=[jax.ShapeDtypeStruct((tt, d), BF16), jax.ShapeDtypeStruct(s0.shape, F32)],
        compiler_params=_params(("arbitrary", "arbitrary")),
        name="hgrn_scan",
    )(q, zf, i, g, lb.reshape(1, d), gnorm.reshape(1, dk), s0)


def _route_rows(lt):
    rows = [lt[e:e + 1, :] for e in range(N_EXPERTS)]
    m = functools.reduce(jnp.maximum, rows)
    ex = [jnp.exp(r - m) for r in rows]
    inv = 1.0 / functools.reduce(lambda x, y: x + y, ex)
    pr = [e * inv for e in ex]
    scores = []
    for gi in range(N_GROUPS):
        p4 = pr[gi * EXPERTS_PER_GROUP:(gi + 1) * EXPERTS_PER_GROUP]
        pairs = [p4[x] + p4[y] for x in range(EXPERTS_PER_GROUP) for y in range(x + 1, EXPERTS_PER_GROUP)]
        scores.append(functools.reduce(jnp.maximum, pairs))
    best = scores[0]
    sel = jnp.zeros(best.shape, jnp.int32)
    for gi in range(1, N_GROUPS):
        better = scores[gi] > best
        sel = jnp.where(better, gi, sel)
        best = jnp.where(better, scores[gi], best)
    cand = []
    for j in range(EXPERTS_PER_GROUP):
        cj = pr[(N_GROUPS - 1) * EXPERTS_PER_GROUP + j]
        for gi in range(N_GROUPS - 2, -1, -1):
            cj = jnp.where(sel == gi, pr[gi * EXPERTS_PER_GROUP + j], cj)
        cand.append(cj)

    def argmax4(vals):
        p, idx = vals[0], jnp.zeros(vals[0].shape, jnp.int32)
        for j in range(1, len(vals)):
            better = vals[j] > p
            idx = jnp.where(better, j, idx)
            p = jnp.where(better, vals[j], p)
        return p, idx

    p1, i1 = argmax4(cand)
    p2, i2 = argmax4([jnp.where(i1 == j, -1.0, cand[j]) for j in range(EXPERTS_PER_GROUP)])
    den = p1 + p2
    base = sel * EXPERTS_PER_GROUP
    return base + i1, base + i2, p1 / den, p2 / den


def _mix_router_body(a_ref, w_ref, h_ref, g1_ref, gain_ref, sh_ref, sc_ref, wrh_ref, wrl_ref, br_ref, upper_ref,
                     h_out, hn_out, gate_out, route_out, count_out):
    @pl.when(pl.program_id(0) == 0)
    def _():
        count_out[...] = jnp.zeros_like(count_out)

    mix = jnp.dot(a_ref[...].astype(BF16), w_ref[...], preferred_element_type=F32)
    h = h_ref[...] + g1_ref[...] * mix
    h_out[...] = h
    hn = _norm_mod(h, gain_ref[...], sh_ref[...], sc_ref[...])
    hn_out[...] = hn
    x_hi = hn.astype(BF16)
    x_lo = (hn - x_hi.astype(F32)).astype(BF16)
    lt = (_nt_dot(wrh_ref[...], x_hi) + _nt_dot(wrl_ref[...], x_hi) + _nt_dot(wrh_ref[...], x_lo)
          + br_ref[...])
    e1, e2, g1, g2 = _route_rows(lt)
    tm = lt.shape[1]
    sub = lax.broadcasted_iota(jnp.int32, (LANES, tm), 0)
    gate_out[...] = (jnp.where(sub == 0, g1, 0.0) + jnp.where(sub == 1, g2, 0.0)).T
    eid = lax.broadcasted_iota(jnp.int32, (N_EXPERTS, tm), 0)
    pick1 = eid == e1
    pick2 = eid == e2
    hits = jnp.where(pick1, 1.0, 0.0) + jnp.where(pick2, 1.0, 0.0)
    before = jnp.dot(hits.astype(BF16), upper_ref[...], preferred_element_type=F32) + count_out[...]
    rank1 = jnp.sum(jnp.where(pick1, before, 0.0), axis=0, keepdims=True)
    rank2 = jnp.sum(jnp.where(pick2, before, 0.0), axis=0, keepdims=True)
    count_out[...] += jnp.sum(hits, axis=1, keepdims=True)
    field = lax.broadcasted_iota(jnp.int32, route_out.shape, 0)
    route_out[...] = jnp.where(field == 0, e1, jnp.where(field == 1, e2, jnp.where(
        field == 2, rank1.astype(jnp.int32), rank2.astype(jnp.int32))))


def _mix_router(a, w_out, h, gain, mod, wr_hi, wr_lo, b_router, tm):
    t, d = h.shape
    row = pl.BlockSpec((tm, d), lambda i: (i, 0))
    const = lambda shape: pl.BlockSpec(shape, lambda i: (0,) * len(shape))
    upper = (jnp.arange(tm)[:, None] < jnp.arange(tm)[None, :]).astype(BF16)
    return pl.pallas_call(
        _mix_router_body,
        grid=(t // tm,),
        in_specs=[row, const((d, d)), row, mod.spec(2), const((1, d)), mod.spec(3), mod.spec(4),
                  const((N_EXPERTS, d)), const((N_EXPERTS, d)), const((N_EXPERTS, 1)), const((tm, tm))],
        out_specs=[row, row, pl.BlockSpec((tm, LANES), lambda i: (i, 0)),
                   pl.BlockSpec((4, tm), lambda i: (0, i)), const((N_EXPERTS, 1))],
        out_shape=[jax.ShapeDtypeStruct((t, d), F32), jax.ShapeDtypeStruct((t, d), F32),
                   jax.ShapeDtypeStruct((t, LANES), F32), jax.ShapeDtypeStruct((4, t), jnp.int32),
                   jax.ShapeDtypeStruct((N_EXPERTS, 1), F32)],
        compiler_params=_params(("arbitrary",)),
        name="mix_router",
    )(a, w_out, h, mod.arr, gain.reshape(1, d), mod.arr, mod.arr, wr_hi, wr_lo,
      b_router.reshape(N_EXPERTS, 1).astype(F32), upper)


def _row_copy(src, src_row, dst, dst_row, sem):
    return pltpu.make_async_copy(src.at[pl.ds(src_row, 1)], dst.at[pl.ds(dst_row, 1)], sem)


def _index_slot(tm):
    return -(-2 * tm // SMEM_I32_CHUNK) * SMEM_I32_CHUNK


def _load_positions(pos_hbm, idx_ref, sem, step):
    n = idx_ref.shape[0]
    cp = pltpu.make_async_copy(pos_hbm.at[pl.ds(pl.multiple_of(step * n, n), n)], idx_ref, sem)
    cp.start()
    cp.wait()


def _dispatch_body(pos_hbm, x_ref, xs_in, xs_out, idx_ref, idx_sem, row_sem, *, tm):
    del xs_in
    _load_positions(pos_hbm, idx_ref, idx_sem, pl.program_id(0))

    def start(t, carry):
        _row_copy(x_ref, t, xs_out, idx_ref[t], row_sem).start()
        _row_copy(x_ref, t, xs_out, idx_ref[tm + t], row_sem).start()
        return carry

    lax.fori_loop(0, tm, start, 0)
    for _ in range(2):
        pltpu.make_async_copy(x_ref, xs_out.at[pl.ds(0, tm)], row_sem).wait()


def _dispatch(hn, pos_tiles, n_rows, tm):
    t, d = hn.shape
    return pl.pallas_call(
        functools.partial(_dispatch_body, tm=tm),
        grid=(t // tm,),
        in_specs=[pl.BlockSpec(memory_space=pl.ANY), pl.BlockSpec((tm, d), lambda i: (i, 0)),
                  pl.BlockSpec(memory_space=pl.ANY)],
        out_specs=pl.BlockSpec(memory_space=pl.ANY),
        out_shape=jax.ShapeDtypeStruct((n_rows, d), F32),
        scratch_shapes=[pltpu.SMEM((_index_slot(tm),), jnp.int32), pltpu.SemaphoreType.DMA, pltpu.SemaphoreType.DMA],
        input_output_aliases={2: 0},
        compiler_params=_params(("arbitrary",)),
        name="moe_dispatch",
    )(pos_tiles, hn, jnp.zeros((n_rows, d), F32))


def _expert_body(te_ref, nu_ref, x_ref, win_ref, wout_ref, y_ref):
    @pl.when(pl.program_id(0) < nu_ref[0])
    def _():
        hid = jnp.dot(x_ref[...].astype(BF16), win_ref[0], preferred_element_type=F32)
        de = hid.shape[1] // 2
        a, u = hid[:, :de], hid[:, de:]
        act = (a * _sigmoid(a) * u).astype(BF16)
        y_ref[...] = jnp.dot(act, wout_ref[0], preferred_element_type=F32)

    @pl.when(pl.program_id(0) >= nu_ref[0])
    def _():
        y_ref[...] = jnp.zeros_like(y_ref)


def _experts(xs, tile_expert, n_used, w_in, w_out):
    n_rows, d = xs.shape
    n_e, _, d2 = w_in.shape
    tile = lambda i, te, nu: jnp.minimum(i, nu[0] - 1)
    grid_spec = pltpu.PrefetchScalarGridSpec(
        num_scalar_prefetch=2,
        grid=(n_rows // EXPERT_TILE,),
        in_specs=[pl.BlockSpec((EXPERT_TILE, d), lambda i, te, nu: (tile(i, te, nu), 0)),
                  pl.BlockSpec((1, d, d2), lambda i, te, nu: (te[tile(i, te, nu)], 0, 0)),
                  pl.BlockSpec((1, d2 // 2, d), lambda i, te, nu: (te[tile(i, te, nu)], 0, 0))],
        out_specs=pl.BlockSpec((EXPERT_TILE, d), lambda i, te, nu: (i, 0)))
    return pl.pallas_call(
        _expert_body,
        grid_spec=grid_spec,
        out_shape=jax.ShapeDtypeStruct((n_rows, d), F32),
        compiler_params=_params(("arbitrary",)),
        name="moe_experts",
    )(tile_expert, n_used, xs, w_in, w_out)


def _combine_body(pos_hbm, ys_hbm, gate_ref, h_ref, g2_ref, o_ref, idx_ref, y_ref, idx_sem, row_sem, *, tm):
    _load_positions(pos_hbm, idx_ref, idx_sem, pl.program_id(0))

    def start(t, carry):
        _row_copy(ys_hbm, idx_ref[t], y_ref.at[0], t, row_sem).start()
        _row_copy(ys_hbm, idx_ref[tm + t], y_ref.at[1], t, row_sem).start()
        return carry

    lax.fori_loop(0, tm, start, 0)
    for c in range(2):
        pltpu.make_async_copy(ys_hbm.at[pl.ds(0, tm)], y_ref.at[c], row_sem).wait()
    gates = gate_ref[...]
    moe = gates[:, 0:1] * y_ref[0] + gates[:, 1:2] * y_ref[1]
    o_ref[...] = h_ref[...] + g2_ref[...] * moe


def _combine(ys, pos_tiles, gates, h, mod, tm):
    t, d = h.shape
    row = pl.BlockSpec((tm, d), lambda i: (i, 0))
    return pl.pallas_call(
        functools.partial(_combine_body, tm=tm),
        grid=(t // tm,),
        in_specs=[pl.BlockSpec(memory_space=pl.ANY), pl.BlockSpec(memory_space=pl.ANY),
                  pl.BlockSpec((tm, LANES), lambda i: (i, 0)), row, mod.spec(5)],
        out_specs=row,
        out_shape=jax.ShapeDtypeStruct((t, d), F32),
        scratch_shapes=[pltpu.SMEM((_index_slot(tm),), jnp.int32), pltpu.VMEM((2, tm, d), F32),
                        pltpu.SemaphoreType.DMA, pltpu.SemaphoreType.DMA],
        compiler_params=_params(("arbitrary",)),
        name="moe_combine",
    )(pos_tiles, ys, gates, h, mod.arr)


def _moe(hn, gates, route, counts, h, mod, w_in, w_out, tm):
    t, d = h.shape
    n_tiles = 2 * t // EXPERT_TILE + N_EXPERTS
    n_rows = n_tiles * EXPERT_TILE
    cnt = counts.reshape(N_EXPERTS).astype(jnp.int32)
    padded = (cnt + (EXPERT_TILE - 1)) // EXPERT_TILE * EXPERT_TILE
    seg_end = jnp.cumsum(padded)
    seg_start = seg_end - padded
    pos = jnp.take(seg_start, route[:2], axis=0) + route[2:]
    pos_tiles = pos.reshape(2, t // tm, tm).transpose(1, 0, 2).reshape(t // tm, 2 * tm)
    pos_tiles = jnp.pad(pos_tiles, ((0, 0), (0, _index_slot(tm) - 2 * tm))).reshape(-1)
    tile_expert = jnp.minimum(
        jnp.searchsorted(seg_end, jnp.arange(n_tiles, dtype=jnp.int32) * EXPERT_TILE, side="right"),
        N_EXPERTS - 1).astype(jnp.int32)
    n_used = (seg_end[-1:] // EXPERT_TILE).astype(jnp.int32)
    xs = _dispatch(hn, pos_tiles, n_rows, tm)
    ys = _experts(xs, tile_expert, n_used, w_in, w_out)
    return _combine(ys, pos_tiles, gates, h, mod, tm)


def _head_norm(y, gsum_ref, gain_ref, head_dim):
    sq = y * y
    hi = sq.astype(BF16)
    lo = (sq - hi.astype(F32)).astype(BF16)
    ssum = (jnp.dot(hi, gsum_ref[...], preferred_element_type=F32)
            + jnp.dot(lo, gsum_ref[...], preferred_element_type=F32))
    return y * lax.rsqrt(ssum * (1.0 / head_dim) + EPS) * gain_ref[...]


def _head_sum_matrix(d, head_dim):
    r = jnp.arange(d) // head_dim
    return (r[:, None] == r[None, :]).astype(BF16)


def _q_proj_body(x_ref, gain_ref, sh_ref, sc_ref, w_ref, gsum_ref, qn_ref, q_ref, *, head_dim, q_scale):
    xn = _norm_mod(x_ref[...], gain_ref[...], sh_ref[...], sc_ref[...]).astype(BF16)
    y = jnp.dot(xn, w_ref[...], preferred_element_type=F32)
    q = _head_norm(y, gsum_ref, qn_ref, head_dim) * q_scale
    q_ref[...] = q.astype(q_ref.dtype)


def _q_proj(h, gain, mod, w, gsum, q_norm, tm, out_dtype, q_scale):
    t, d = h.shape
    head_dim = d // FOX_HEADS
    row = pl.BlockSpec((tm, d), lambda i: (i, 0))
    const = lambda shape: pl.BlockSpec(shape, lambda i: (0,) * len(shape))
    return pl.pallas_call(
        functools.partial(_q_proj_body, head_dim=head_dim, q_scale=q_scale),
        grid=(t // tm,),
        in_specs=[row, const((1, d)), mod.spec(0), mod.spec(1), const((d, d)), const((d, d)), const((1, d))],
        out_specs=row,
        out_shape=jax.ShapeDtypeStruct((t, d), out_dtype),
        compiler_params=_params(("arbitrary",)),
        name="q_proj",
    )(h, gain.reshape(1, d), mod.arr, mod.arr, w, gsum, jnp.tile(q_norm, FOX_HEADS).reshape(1, d))


def _kv_proj_body(x_ref, gain_ref, sh_ref, sc_ref, wk_ref, wv_ref, wf_ref, gsum_ref, kn_ref, bf_ref,
                  k_ref, v_ref, kb_ref, vb_ref, lf_ref, *, head_dim):
    xn = _norm_mod(x_ref[...], gain_ref[...], sh_ref[...], sc_ref[...]).astype(BF16)
    k = _head_norm(jnp.dot(xn, wk_ref[...], preferred_element_type=F32), gsum_ref, kn_ref, head_dim)
    k_ref[...] = k
    kb_ref[...] = k.astype(BF16)
    v = jnp.dot(xn, wv_ref[...], preferred_element_type=F32)
    v_ref[...] = v
    vb_ref[...] = v.astype(BF16)
    zf = jnp.dot(xn, wf_ref[...], preferred_element_type=F32) + bf_ref[...]
    lf_ref[...] = _log_sigmoid(zf)[:, :lf_ref.shape[1]]


def _kv_proj(h, gain, mod, wk, wv, wf, gsum, k_norm, b_fgate, tm):
    t, d = h.shape
    head_dim = d // FOX_HEADS
    row = pl.BlockSpec((tm, d), lambda i: (i, 0))
    const = lambda shape: pl.BlockSpec(shape, lambda i: (0,) * len(shape))
    bf = jnp.pad(b_fgate.astype(F32), (0, LANES - FOX_HEADS)).reshape(1, LANES)
    return pl.pallas_call(
        functools.partial(_kv_proj_body, head_dim=head_dim),
        grid=(t // tm,),
        in_specs=[row, const((1, d)), mod.spec(0), mod.spec(1), const((d, d)), const((d, d)),
                  const((d, LANES)), const((d, d)), const((1, d)), const((1, LANES))],
        out_specs=[row, row, row, row, pl.BlockSpec((tm, FOX_HEADS), lambda i: (i, 0))],
        out_shape=[jax.ShapeDtypeStruct((t, d), F32), jax.ShapeDtypeStruct((t, d), F32),
                   jax.ShapeDtypeStruct((t, d), BF16), jax.ShapeDtypeStruct((t, d), BF16),
                   jax.ShapeDtypeStruct((t, FOX_HEADS), F32)],
        compiler_params=_params(("arbitrary",)),
        name="kv_proj",
    )(h, gain.reshape(1, d), mod.arr, mod.arr, wk, wv, wf, gsum,
      jnp.tile(k_norm, FOX_HEADS).reshape(1, d), bf)


def _cumsum_lanes_body(x_ref, o_ref, carry_ref, *, carry_rows):
    j = pl.program_id(1)

    @pl.when(j == 0)
    def _():
        carry_ref[...] = jnp.zeros_like(carry_ref)

    w = x_ref.shape[-1]
    r = lax.broadcasted_iota(jnp.int32, (w, w), 0)
    c = lax.broadcasted_iota(jnp.int32, (w, w), 1)
    upper = jnp.where(r <= c, 1.0, 0.0).astype(F32)
    y = jnp.dot(x_ref[0], upper, precision=HIGHEST, preferred_element_type=F32)
    if carry_rows:
        y = y + carry_ref[...]
        carry_ref[...] = y[:, w - 1:w]
    o_ref[0] = y


def _cumsum_lanes(x, width, carry):
    b, r, t = x.shape
    spec = pl.BlockSpec((1, r, width), lambda i, j: (i, 0, j))
    return pl.pallas_call(
        functools.partial(_cumsum_lanes_body, carry_rows=carry),
        grid=(b, t // width),
        in_specs=[spec],
        out_specs=spec,
        out_shape=jax.ShapeDtypeStruct(x.shape, F32),
        scratch_shapes=[pltpu.VMEM((r, 1), F32)],
        compiler_params=_params(("arbitrary", "arbitrary")),
        name="cumsum_lanes",
    )(x)


def _attn_prompt_body(qt_ref, kt_ref, q_ref, k_ref, v_ref, f_ref, o_ref, m_ref, acc_ref,
                      *, blk, rows, head_dim):
    p = pl.program_id(2)
    qi = qt_ref[p]
    ki = kt_ref[p]

    @pl.when(ki == 0)
    def _():
        m_ref[...] = jnp.full(m_ref.shape, NEG_BIG, F32)
        acc_ref[...] = jnp.zeros_like(acc_ref)

    def process(masked):
        k = k_ref[...]
        v = v_ref[...]
        f = f_ref[...]
        first_v = lax.broadcasted_iota(jnp.int32, v.shape, 1) < head_dim
        ones = jnp.ones_like(v)
        v_heads = (jnp.where(first_v, v, ones), jnp.where(first_v, ones, v))
        for r0 in range(0, blk, rows):
            q = q_ref[r0:r0 + rows, :]
            first_q = lax.broadcasted_iota(jnp.int32, q.shape, 1) < head_dim
            zero = jnp.zeros_like(q)
            for hh, qh in enumerate((jnp.where(first_q, q, zero), jnp.where(first_q, zero, q))):
                s = _nt_dot(qh, k) - f[hh:hh + 1, :]
                if masked:
                    row = lax.broadcasted_iota(jnp.int32, s.shape, 0) + r0
                    col = lax.broadcasted_iota(jnp.int32, s.shape, 1)
                    s = jnp.where(col <= row, s, NEG_BIG)
                m_old = m_ref[hh, r0:r0 + rows, :]
                m_new = jnp.maximum(m_old, jnp.max(s, axis=1, keepdims=True))
                alpha = jnp.exp2(m_old - m_new)
                pr = jnp.concatenate([jnp.exp2(s[:, c * LANES:(c + 1) * LANES] - m_new)
                                      for c in range(blk // LANES)], axis=1).astype(BF16)
                acc_ref[hh, r0:r0 + rows, :] = (alpha * acc_ref[hh, r0:r0 + rows, :]
                                                + jnp.dot(pr, v_heads[hh], preferred_element_type=F32))
                m_ref[hh, r0:r0 + rows, :] = m_new

    @pl.when(ki < qi)
    def _():
        process(False)

    @pl.when(ki == qi)
    def _():
        process(True)
        a0 = acc_ref[0]
        a1 = acc_ref[1]
        first = lax.broadcasted_iota(jnp.int32, a0.shape, 1) < head_dim
        inv0 = 1.0 / a0[:, head_dim:head_dim + 1]
        inv1 = 1.0 / a1[:, 0:1]
        o_ref[...] = jnp.where(first, a0 * inv0, a1 * inv1).astype(o_ref.dtype)


def _attn_prompt(q, k, v, f_pairs, nseq):
    tt, d = q.shape
    t = tt // nseq
    blk = min(ATTN_BLOCK, t)
    rows = min(ATTN_ROWS, blk)
    nb = t // blk
    head_dim = d // FOX_HEADS
    pairs = [(qi, ki) for qi in range(nb) for ki in range(qi + 1)]
    q_tab = jnp.asarray([p[0] for p in pairs], jnp.int32)
    k_tab = jnp.asarray([p[1] for p in pairs], jnp.int32)
    grid_spec = pltpu.PrefetchScalarGridSpec(
        num_scalar_prefetch=2,
        grid=(nseq, d // LANES, len(pairs)),
        in_specs=[pl.BlockSpec((blk, LANES), lambda b, hp, p, qt, kt: (b * nb + qt[p], hp)),
                  pl.BlockSpec((blk, LANES), lambda b, hp, p, qt, kt: (b * nb + kt[p], hp)),
                  pl.BlockSpec((blk, LANES), lambda b, hp, p, qt, kt: (b * nb + kt[p], hp)),
                  pl.BlockSpec((None, None, 2, blk), lambda b, hp, p, qt, kt: (b, hp, 0, kt[p]))],
        out_specs=pl.BlockSpec((blk, LANES), lambda b, hp, p, qt, kt: (b * nb + qt[p], hp)),
        scratch_shapes=[pltpu.VMEM((2, blk, LANES), F32), pltpu.VMEM((2, blk, LANES), F32)])
    return pl.pallas_call(
        functools.partial(_attn_prompt_body, blk=blk, rows=rows, head_dim=head_dim),
        grid_spec=grid_spec,
        out_shape=jax.ShapeDtypeStruct((tt, d), BF16),
        compiler_params=_params(("arbitrary", "arbitrary", "arbitrary")),
        name="attn_prompt",
    )(q_tab, k_tab, q, k, v, f_pairs)


def _gather_cum_body(pt_ref, cp_ref, new_ref, o_ref, *, n_pages, page):
    b = pl.program_id(0)

    def step(j, carry):
        idx = pt_ref[b * n_pages + j]
        blk = cp_ref[idx] + carry
        o_ref[0, :, pl.ds(pl.multiple_of(j * page, page), page)] = blk
        return blk[:, page - 1:page]

    carry = lax.fori_loop(0, n_pages, step, jnp.zeros((cp_ref.shape[1], 1), F32))
    o_ref[0, :, n_pages * page:] = new_ref[0] + carry


def _gather_cum(page_table, cum_pages, cum_new):
    nseq, n_pages = page_table.shape
    n_pool, hh, page = cum_pages.shape
    grid_spec = pltpu.PrefetchScalarGridSpec(
        num_scalar_prefetch=1,
        grid=(nseq,),
        in_specs=[pl.BlockSpec((n_pool, hh, page), lambda b, pt: (0, 0, 0)),
                  pl.BlockSpec((1, hh, page), lambda b, pt: (b, 0, 0))],
        out_specs=pl.BlockSpec((1, hh, (n_pages + 1) * page), lambda b, pt: (b, 0, 0)))
    return pl.pallas_call(
        functools.partial(_gather_cum_body, n_pages=n_pages, page=page),
        grid_spec=grid_spec,
        out_shape=jax.ShapeDtypeStruct((nseq, hh, (n_pages + 1) * page), F32),
        compiler_params=_params(("arbitrary",)),
        name="gather_cum",
    )(page_table.reshape(-1), cum_pages, cum_new)


def _attn_paged_body(pt_ref, *refs, n_q, page, pps, n_steps, head_dim):
    q_ref = refs[0]
    k_refs = refs[1:1 + pps]
    v_refs = refs[1 + pps:1 + 2 * pps]
    kn_ref, vn_ref, f_ref, o_ref, qbd_ref, m_ref, l_ref, acc_ref = refs[1 + 2 * pps:]
    j = pl.program_id(1)
    d = q_ref.shape[1]
    rows = FOX_HEADS * n_q

    @pl.when(j == 0)
    def _():
        q = q_ref[...]
        lane_head = lax.broadcasted_iota(jnp.int32, q.shape, 1) // head_dim
        for h in range(FOX_HEADS):
            qbd_ref[h * n_q:(h + 1) * n_q, :] = jnp.where(lane_head == h, q, 0.0).astype(BF16)
        m_ref[...] = jnp.full(m_ref.shape, NEG_BIG, F32)
        l_ref[...] = jnp.zeros_like(l_ref)
        acc_ref[...] = jnp.zeros_like(acc_ref)

    def attend(k, v, f, mask):
        bias = jnp.concatenate([jnp.broadcast_to(f[h:h + 1, :], (n_q, page)) for h in range(FOX_HEADS)], axis=0)
        s = _nt_dot(qbd_ref[...], k) - bias
        if mask is not None:
            s = jnp.where(mask, s, NEG_BIG)
        m_old = m_ref[...]
        m_new = jnp.maximum(m_old, jnp.max(s, axis=1, keepdims=True))
        alpha = jnp.exp(m_old - m_new)
        pr = jnp.exp(s - m_new)
        l_ref[...] = alpha * l_ref[...] + jnp.sum(pr, axis=1, keepdims=True)
        m_ref[...] = m_new
        acc_ref[...] = alpha * acc_ref[...] + jnp.dot(pr.astype(BF16), v, preferred_element_type=F32)

    @pl.when(j < n_steps - 1)
    def _():
        for u in range(pps):
            off = pl.multiple_of((j * pps + u) * page, page)
            attend(k_refs[u][...].astype(BF16), v_refs[u][...].astype(BF16), f_ref[0, :, pl.ds(off, page)], None)

    @pl.when(j == n_steps - 1)
    def _():
        pad = jnp.zeros((page - n_q, d), F32)
        k = jnp.concatenate([kn_ref[...], pad], axis=0).astype(BF16)
        v = jnp.concatenate([vn_ref[...], pad], axis=0).astype(BF16)
        t_of_row = lax.broadcasted_iota(jnp.int32, (rows, page), 0) % n_q
        key = lax.broadcasted_iota(jnp.int32, (rows, page), 1)
        attend(k, v, f_ref[0, :, (n_steps - 1) * pps * page:], key <= t_of_row)
        acc = acc_ref[...] * (1.0 / l_ref[...])
        lane_head = lax.broadcasted_iota(jnp.int32, (n_q, d), 1) // head_dim
        out = jnp.zeros((n_q, d), F32)
        for h in range(FOX_HEADS):
            out = out + jnp.where(lane_head == h, acc[h * n_q:(h + 1) * n_q, :], 0.0)
        o_ref[...] = out


def _attn_paged(q, cache_k, cache_v, k_new, v_new, f_all, page_table):
    tt, d = q.shape
    nseq, n_pages = page_table.shape
    n_q = tt // nseq
    page = cache_k.shape[1]
    pps = PAGES_PER_STEP
    n_steps = n_pages // pps + 1
    head_dim = d // FOX_HEADS
    rows = FOX_HEADS * n_q

    def page_spec(u):
        def index(b, j, pt):
            return (pt[b * n_pages + jnp.minimum(j, n_steps - 2) * pps + u], 0, 0)
        return pl.BlockSpec((None, page, d), index)

    tok = pl.BlockSpec((n_q, d), lambda b, j, pt: (b, 0))
    grid_spec = pltpu.PrefetchScalarGridSpec(
        num_scalar_prefetch=1,
        grid=(nseq, n_steps),
        in_specs=([tok] + [page_spec(u) for u in range(pps)] + [page_spec(u) for u in range(pps)]
                  + [tok, tok, pl.BlockSpec((1, FOX_HEADS, f_all.shape[2]), lambda b, j, pt: (b, 0, 0))]),
        out_specs=tok,
        scratch_shapes=[pltpu.VMEM((rows, d), BF16), pltpu.VMEM((rows, 1), F32),
                        pltpu.VMEM((rows, 1), F32), pltpu.VMEM((rows, d), F32)])
    body = functools.partial(_attn_paged_body, n_q=n_q, page=page, pps=pps, n_steps=n_steps,
                             head_dim=head_dim)
    return pl.pallas_call(
        body,
        grid_spec=grid_spec,
        out_shape=jax.ShapeDtypeStruct((tt, d), F32),
        compiler_params=_params(("arbitrary", "arbitrary")),
        name="attn_paged",
    )(page_table.reshape(-1), q, *([cache_k] * pps), *([cache_v] * pps), k_new, v_new, f_all)


def _lower_bounds(lb_logits):
    p = jax.nn.softmax(lb_logits.astype(F32), axis=0)
    return jnp.cumsum(p, axis=0) - p[0]


def _trunk(x, nseq, mods, mod_kv, s0, past, w):
    tt, d = x.shape
    t = tt // nseq
    tm = min(512, tt)
    tiles_per_seq = max(t // tm, 1)
    n_a = w["w_in_a"].shape[0]
    depth = w["w_ada"].shape[0]
    head_dim = d // FOX_HEADS
    lbs = _lower_bounds(w["lb_logits"])
    mk = lambda arr: _Mod(arr, d, tm, tiles_per_seq)

    h = x
    states = []
    for l in range(depth):
        mod = mk(mods[l])
        if l < n_a:
            q, zf, i, g = _hgrn_in_proj(h, w["norm_mix"][l], mod, w["w_in_a4"][l], tm)
            tp = -(-t // SCAN_CHUNK) * SCAN_CHUNK
            if tp != t:
                padr = lambda a: jnp.pad(a.reshape(nseq, t, d), ((0, 0), (0, tp - t), (0, 0))).reshape(nseq * tp, d)
                q, zf, i, g = padr(q), padr(zf), padr(i), padr(g)
            o, s_t = _hgrn_scan(q, zf, i, g, lbs[l], w["gnorm_a"][l], s0[l], nseq, t)
            if tp != t:
                o = o.reshape(nseq, tp, d)[:, :t].reshape(tt, d)
            states.append(s_t)
            mix_in, w_out = o, w["w_out_a"][l]
        else:
            j = l - n_a
            if past is None:
                q = _q_proj(h, w["norm_mix"][l], mod, w["w_q_b"][j], w["gsum"], w["q_norm"][j], tm, BF16,
                            head_dim ** -0.5 * LOG2E)
                mix_in = _attn_prompt(q, k_bf, v_bf, f_keys, nseq)
            else:
                q = _q_proj(h, w["norm_mix"][l], mod, w["w_q_b"][j], w["gsum"], w["q_norm"][j], tm, F32,
                            head_dim ** -0.5)
                mix_in = _attn_paged(q, past[0], past[1], k_new, v_new, f_keys, past[3])
            w_out = w["w_out_b"][j]
        h, hn, gates, route, counts = _mix_router(mix_in, w_out, h, w["norm_ffn"][l], mod, w["wr_hi"],
                                                  w["wr_lo"], w["b_router"], tm)
        h = _moe(hn, gates, route, counts, h, mod, w["w_exp_in"][l], w["w_exp_out"][l], tm)
        if l == n_a - 1:
            k_new, v_new, k_bf, v_bf, logf_new = _kv_proj(
                h, w["norm_kv"], mk(mod_kv), w["w_k"], w["w_v"], w["w_f"], w["gsum"], w["k_norm"],
                w["b_fgate"], tm)
            lf_t = logf_new.reshape(nseq, t, FOX_HEADS).transpose(0, 2, 1)
            if past is None:
                f_cum = _cumsum_lanes(lf_t, min(512, t), carry=True)
                f_keys = (f_cum * LOG2E).reshape(nseq, FOX_HEADS // 2, 2, t)
            else:
                cache_logf, page_table = past[2], past[3]
                page = cache_logf.shape[1]
                pool_t = cache_logf.astype(F32).transpose(0, 2, 1)
                n_pool = pool_t.shape[0]
                rows = 64 if n_pool % 64 == 0 else 1
                cum_pages = _cumsum_lanes(pool_t.reshape(n_pool // rows, rows * FOX_HEADS, page), page,
                                          carry=False).reshape(n_pool, FOX_HEADS, page)
                new_pad = jnp.pad(lf_t, ((0, 0), (0, 0), (0, page - t)))
                cum_new = _cumsum_lanes(new_pad, page, carry=False)
                f_keys = _gather_cum(page_table, cum_pages, cum_new)
    return h, jnp.stack(states), k_new, v_new, logf_new


def kernel(x_prompt, x_sample, cache_k, cache_v, cache_logf, state_hgrn, page_table, c_prompt, c_sample,
           w_ada, b_ada, norm_mix, norm_ffn, w_in_a, lb_logits, gnorm_a, w_out_a, norm_kv, w_ada_kv,
           b_ada_kv, w_kv, b_fgate, k_norm, w_q_b, q_norm, w_out_b, w_router, b_router, w_exp_in, w_exp_out):
    nb, seq, d = x_prompt.shape
    ns, dseq, _ = x_sample.shape
    n_a = w_in_a.shape[0]
    depth = w_ada.shape[0]
    hd = FOX_HEADS * (d // FOX_HEADS)
    dk = d // HG_HEADS

    nrow = nb + ns
    nrow_pad = -(-nrow // SUBLANES) * SUBLANES
    c_all = jnp.pad(jnp.concatenate([c_prompt, c_sample], axis=0).astype(F32), ((0, nrow_pad - nrow), (0, 0)))
    mod_all = _ada(c_all, w_ada, b_ada)
    mod_kv_all = _ada(c_all, w_ada_kv[None], b_ada_kv[None])[0]
    mods_p = [mod_all[l, :nb].reshape(nb, 1, 6 * d) for l in range(depth)]
    mods_s = [jnp.repeat(mod_all[l, nb:nrow], dseq, axis=0) for l in range(depth)]
    mod_kv_p = mod_kv_all[:nb].reshape(nb, 1, 2 * d)
    mod_kv_s = jnp.repeat(mod_kv_all[nb:nrow], dseq, axis=0)

    wr_t = w_router.astype(F32).T
    wr_hi = wr_t.astype(BF16)
    wr_lo = (wr_t - wr_hi.astype(F32)).astype(BF16)
    w_in_b = w_in_a.astype(BF16)
    w = dict(
        w_ada=w_ada, lb_logits=lb_logits, norm_mix=norm_mix, norm_ffn=norm_ffn, gnorm_a=gnorm_a,
        w_in_a=w_in_a,
        w_in_a4=[[w_in_b[l, :, s * d:(s + 1) * d] for s in range(4)] for l in range(n_a)],
        w_out_a=w_out_a.astype(BF16), norm_kv=norm_kv,
        w_k=w_kv[:, :hd].astype(BF16), w_v=w_kv[:, hd:2 * hd].astype(BF16),
        w_f=jnp.pad(w_kv[:, 2 * hd:], ((0, 0), (0, LANES - FOX_HEADS))).astype(BF16),
        b_fgate=b_fgate, k_norm=k_norm, w_q_b=w_q_b.astype(BF16), q_norm=q_norm,
        w_out_b=w_out_b.astype(BF16), wr_hi=wr_hi, wr_lo=wr_lo, b_router=b_router,
        w_exp_in=w_exp_in.astype(BF16), w_exp_out=w_exp_out.astype(BF16),
        gsum=_head_sum_matrix(d, d // FOX_HEADS),
    )

    s0_p = jnp.zeros((n_a, nb, HG_HEADS, dk, dk), F32)
    y_p, st_p, k_p, v_p, lf_p = _trunk(x_prompt.reshape(nb * seq, d), nb, mods_p, mod_kv_p, s0_p, None, w)

    n_pool, page = cache_k.shape[0], cache_k.shape[1]
    past = (cache_k.reshape(n_pool, page, hd), cache_v.reshape(n_pool, page, hd), cache_logf, page_table)
    y_s, st_s, k_s, v_s, lf_s = _trunk(x_sample.reshape(ns * dseq, d), ns, mods_s, mod_kv_s,
                                       state_hgrn.astype(F32), past, w)

    hs = (FOX_HEADS, d // FOX_HEADS)
    return (y_p.reshape(nb, seq, d), y_s.reshape(ns, dseq, d),
            st_p.astype(state_hgrn.dtype), st_s.astype(state_hgrn.dtype),
            k_p.reshape(nb, seq, *hs).astype(cache_k.dtype), v_p.reshape(nb, seq, *hs).astype(cache_v.dtype),
            lf_p.reshape(nb, seq, FOX_HEADS).astype(cache_logf.dtype),
            k_s.reshape(ns, dseq, *hs).astype(cache_k.dtype), v_s.reshape(ns, dseq, *hs).astype(cache_v.dtype),
            lf_s.reshape(ns, dseq, FOX_HEADS).astype(cache_logf.dtype))
```

```python
import functools
import math

import jax
import jax.numpy as jnp
from jax import lax
from jax.experimental import pallas as pl
from jax.experimental.pallas import tpu as pltpu

F32 = jnp.float32
BF16 = jnp.bfloat16
HIGHEST = lax.Precision.HIGHEST

HG_HEADS = 8
FOX_HEADS = 16
N_EXPERTS = 16
N_GROUPS = 4
EXPERTS_PER_GROUP = N_EXPERTS // N_GROUPS
EPS = 1e-6
NEG_BIG = -1e30
TINY = 1e-30

LANES = 128
SUBLANES = 8
VMEM_LIMIT_BYTES = 56 * 1024 * 1024

SCAN_CHUNK = 128
ATTN_BLOCK = 1024
ATTN_ROWS = 1024
LOG2E = math.log2(math.e)
PAGES_PER_STEP = 8
EXPERT_TILE = 256
SMEM_I32_CHUNK = 1024


def _params(sem):
    return pltpu.CompilerParams(dimension_semantics=sem, vmem_limit_bytes=VMEM_LIMIT_BYTES)


def _nt_dot(a, b):
    return lax.dot_general(a, b, (((1,), (1,)), ((), ())), preferred_element_type=F32)


def _sigmoid(x):
    return 1.0 / (1.0 + jnp.exp(-x))


def _log_sigmoid(x):
    return jnp.minimum(x, 0.0) - jnp.log1p(jnp.exp(-jnp.abs(x)))


def _norm_mod(x, gain, shift, scale):
    ms = jnp.mean(x * x, axis=-1, keepdims=True)
    y = x * lax.rsqrt(ms + EPS) * gain
    return y * (1.0 + scale) + shift


class _Mod:
    def __init__(self, arr, d, tm, tiles_per_seq):
        self.arr = arr
        self.d = d
        self.tm = tm
        self.tiles_per_seq = tiles_per_seq
        self.per_token = arr.ndim == 2

    def spec(self, chunk):
        d, tps = self.d, self.tiles_per_seq
        if self.per_token:
            return pl.BlockSpec((self.tm, d), lambda i, *_: (i, chunk))
        return pl.BlockSpec((None, 1, d), lambda i, *_: (i // tps, 0, chunk))


def _ada_body(c_ref, w_ref, b_ref, o_ref):
    c = c_ref[...]
    a = c * _sigmoid(c)
    o_ref[0] = jnp.dot(a, w_ref[0], precision=HIGHEST, preferred_element_type=F32) + b_ref[0]


def _ada(c, w, b):
    n_layers, d, n = w.shape
    r = c.shape[0]
    tn = 1536 if n % 1536 == 0 else 1024
    return pl.pallas_call(
        _ada_body,
        grid=(n_layers, n // tn),
        in_specs=[pl.BlockSpec((r, d), lambda l, j: (0, 0)),
                  pl.BlockSpec((1, d, tn), lambda l, j: (l, 0, j)),
                  pl.BlockSpec((1, 1, tn), lambda l, j: (l, 0, j))],
        out_specs=pl.BlockSpec((1, r, tn), lambda l, j: (l, 0, j)),
        out_shape=jax.ShapeDtypeStruct((n_layers, r, n), F32),
        compiler_params=_params(("arbitrary", "arbitrary")),
        name="ada",
    )(c, w, b.reshape(n_layers, 1, n))


def _hgrn_in_body(x_ref, gain_ref, sh_ref, sc_ref, wq_ref, wf_ref, wi_ref, wg_ref,
                  q_ref, zf_ref, i_ref, g_ref):
    xn = _norm_mod(x_ref[...], gain_ref[...], sh_ref[...], sc_ref[...]).astype(BF16)
    q_ref[...] = jnp.dot(xn, wq_ref[...], preferred_element_type=F32).astype(q_ref.dtype)
    zf_ref[...] = jnp.dot(xn, wf_ref[...], preferred_element_type=F32)
    i_ref[...] = jnp.dot(xn, wi_ref[...], preferred_element_type=F32).astype(i_ref.dtype)
    g_ref[...] = jnp.dot(xn, wg_ref[...], preferred_element_type=F32).astype(g_ref.dtype)


def _hgrn_in_proj(h, gain, mod, w4, tm):
    t, d = h.shape
    row = pl.BlockSpec((tm, d), lambda i: (i, 0))
    full = pl.BlockSpec((d, d), lambda i: (0, 0))
    return pl.pallas_call(
        _hgrn_in_body,
        grid=(t // tm,),
        in_specs=[row, pl.BlockSpec((1, d), lambda i: (0, 0)), mod.spec(0), mod.spec(1),
                  full, full, full, full],
        out_specs=[row, row, row, row],
        out_shape=[jax.ShapeDtypeStruct((t, d), BF16), jax.ShapeDtypeStruct((t, d), F32),
                   jax.ShapeDtypeStruct((t, d), BF16), jax.ShapeDtypeStruct((t, d), BF16)],
        compiler_params=_params(("arbitrary",)),
        name="hgrn_in_proj",
    )(h, gain.reshape(1, d), mod.arr, mod.arr, *w4)


def _split3(x):
    hi = x.astype(BF16)
    r1 = x - hi.astype(F32)
    mid = r1.astype(BF16)
    lo = (r1 - mid.astype(F32)).astype(BF16)
    return hi, mid, lo


def _hgrn_scan_body(q_ref, zf_ref, i_ref, g_ref, lb_ref, gn_ref, s0_ref, o_ref, st_ref,
                    *, chunk, n_valid, dk):
    c = pl.program_id(1)

    @pl.when(c == 0)
    def _():
        st_ref[...] = s0_ref[...]

    row = lax.broadcasted_iota(jnp.int32, (chunk, chunk), 0)
    col = lax.broadcasted_iota(jnp.int32, (chunk, chunk), 1)
    tri = jnp.where(row >= col, 1.0, 0.0).astype(BF16)
    n_levels = chunk.bit_length() - 1
    differ = row ^ col
    on_diag = differ == 0
    at_level = [lax.shift_right_logical(differ, lv) == 1 for lv in range(n_levels)]
    tok = lax.broadcasted_iota(jnp.int32, (chunk, dk), 0)
    if n_valid < chunk:
        valid = tok < n_valid
    for h in range(HG_HEADS):
        hs = slice(h * dk, (h + 1) * dk)
        q = q_ref[:, hs].astype(F32)
        zf = zf_ref[:, hs]
        v = i_ref[:, hs]
        lb = lb_ref[:, hs]
        a1 = jnp.log(jnp.maximum(lb, TINY))
        a2 = jnp.log1p(-lb) + _log_sigmoid(zf)
        log_f = jnp.maximum(a1, a2) + jnp.log1p(jnp.exp(-jnp.abs(a1 - a2)))
        k = (1.0 - lb) / (1.0 + jnp.exp(zf))
        if n_valid < chunk:
            log_f = jnp.where(valid, log_f, 0.0)
            k = jnp.where(valid, k, 0.0)
        g_hi, g_mid, g_lo = _split3(log_f)
        b = (jnp.dot(tri, g_hi, preferred_element_type=F32)
             + jnp.dot(tri, g_mid, preferred_element_type=F32)
             + jnp.dot(tri, g_lo, preferred_element_type=F32))
        b_end = b[chunk - 1:chunk, :]
        a = jnp.where(on_diag, _nt_dot(q.astype(BF16), k.astype(BF16)), 0.0)
        b_dn1 = pltpu.roll(b, 1, 0)
        b_up1 = pltpu.roll(b, chunk - 1, 0)
        for lv in range(n_levels):
            half = 1 << lv
            if lv == 0:
                d_q, d_k = b - b_dn1, jnp.zeros_like(b)
            elif lv == 1:
                d_q = jnp.where((tok & 3) == 2, b - b_dn1, b - pltpu.roll(b, 2, 0))
                d_k = jnp.where((tok & 3) == 0, b_up1 - b, 0.0)
            else:
                nblk = chunk // (2 * half)
                edge = b.reshape(nblk, 2 * half, dk)[:, half - 1:half, :]
                edge = jnp.broadcast_to(edge, (nblk, 2 * half, dk)).reshape(chunk, dk)
                d_q, d_k = b - edge, edge - b
            right = (lax.shift_right_logical(tok, lv) & 1) == 1
            e_q = jnp.where(right, jnp.exp(jnp.minimum(d_q, 0.0)), 0.0)
            e_k = jnp.where(right, 0.0, jnp.exp(jnp.minimum(d_k, 0.0)))
            a_lv = _nt_dot((q * e_q).astype(BF16), (k * e_k).astype(BF16))
            a = jnp.where(at_level[lv], a_lv, a)
        s = st_ref[0, h]
        o = (jnp.dot(a.astype(BF16), v, preferred_element_type=F32)
             + jnp.dot((q * jnp.exp(b)).astype(BF16), s.astype(BF16), preferred_element_type=F32))
        k_d = (k * jnp.exp(b_end - b)).T.astype(BF16)
        decay = jnp.broadcast_to(jnp.exp(b_end), (dk, dk)).T
        st_ref[0, h] = decay * s + jnp.dot(k_d, v, preferred_element_type=F32)
        ms = jnp.mean(o * o, axis=-1, keepdims=True)
        gate = g_ref[:, hs].astype(F32)
        o = o * lax.rsqrt(ms + EPS) * gn_ref[...] * (gate * _sigmoid(gate))
        o_ref[:, hs] = o.astype(o_ref.dtype)


def _hgrn_scan(q, zf, i, g, lb, gnorm, s0, nseq, n_valid):
    tt, d = q.shape
    tp = tt // nseq
    nc = tp // SCAN_CHUNK
    dk = d // HG_HEADS
    row = pl.BlockSpec((SCAN_CHUNK, d), lambda b, c: (b * nc + c, 0))
    st = pl.BlockSpec((1, HG_HEADS, dk, dk), lambda b, c: (b, 0, 0, 0))
    body = functools.partial(_hgrn_scan_body, chunk=SCAN_CHUNK, n_valid=min(n_valid, SCAN_CHUNK), dk=dk)
    return pl.pallas_call(
        body,
        grid=(nseq, nc),
        in_specs=[row, row, row, row,
                  pl.BlockSpec((1, d), lambda b, c: (0, 0)),
                  pl.BlockSpec((1, dk), lambda b, c: (0, 0)),
                  st],
        out_specs=[row, st],
        out_shape=[jax.ShapeDtypeStruct((tt, d), BF16), jax.ShapeDtypeStruct(s0.shape, F32)],
        compiler_params=_params(("arbitrary", "arbitrary")),
        name="hgrn_scan",
    )(q, zf, i, g, lb.reshape(1, d), gnorm.reshape(1, dk), s0)


def _route_rows(lt):
    rows = [lt[e:e + 1, :] for e in range(N_EXPERTS)]
    m = functools.reduce(jnp.maximum, rows)
    ex = [jnp.exp(r - m) for r in rows]
    inv = 1.0 / functools.reduce(lambda x, y: x + y, ex)
    pr = [e * inv for e in ex]
    scores = []
    for gi in range(N_GROUPS):
        p4 = pr[gi * EXPERTS_PER_GROUP:(gi + 1) * EXPERTS_PER_GROUP]
        pairs = [p4[x] + p4[y] for x in range(EXPERTS_PER_GROUP) for y in range(x + 1, EXPERTS_PER_GROUP)]
        scores.append(functools.reduce(jnp.maximum, pairs))
    best = scores[0]
    sel = jnp.zeros(best.shape, jnp.int32)
    for gi in range(1, N_GROUPS):
        better = scores[gi] > best
        sel = jnp.where(better, gi, sel)
        best = jnp.where(better, scores[gi], best)
    cand = []
    for j in range(EXPERTS_PER_GROUP):
        cj = pr[(N_GROUPS - 1) * EXPERTS_PER_GROUP + j]
        for gi in range(N_GROUPS - 2, -1, -1):
            cj = jnp.where(sel == gi, pr[gi * EXPERTS_PER_GROUP + j], cj)
        cand.append(cj)

    def argmax4(vals):
        p, idx = vals[0], jnp.zeros(vals[0].shape, jnp.int32)
        for j in range(1, len(vals)):
            better = vals[j] > p
            idx = jnp.where(better, j, idx)
            p = jnp.where(better, vals[j], p)
        return p, idx

    p1, i1 = argmax4(cand)
    p2, i2 = argmax4([jnp.where(i1 == j, -1.0, cand[j]) for j in range(EXPERTS_PER_GROUP)])
    den = p1 + p2
    base = sel * EXPERTS_PER_GROUP
    return base + i1, base + i2, p1 / den, p2 / den


def _mix_router_body(a_ref, w_ref, h_ref, g1_ref, gain_ref, sh_ref, sc_ref, wrh_ref, wrl_ref, br_ref, upper_ref,
                     h_out, hn_out, gate_out, route_out, count_out):
    @pl.when(pl.program_id(0) == 0)
    def _():
        count_out[...] = jnp.zeros_like(count_out)

    mix = jnp.dot(a_ref[...].astype(BF16), w_ref[...], preferred_element_type=F32)
    h = h_ref[...] + g1_ref[...] * mix
    h_out[...] = h
    hn = _norm_mod(h, gain_ref[...], sh_ref[...], sc_ref[...])
    hn_out[...] = hn
    x_hi = hn.astype(BF16)
    x_lo = (hn - x_hi.astype(F32)).astype(BF16)
    lt = (_nt_dot(wrh_ref[...], x_hi) + _nt_dot(wrl_ref[...], x_hi) + _nt_dot(wrh_ref[...], x_lo)
          + br_ref[...])
    e1, e2, g1, g2 = _route_rows(lt)
    tm = lt.shape[1]
    sub = lax.broadcasted_iota(jnp.int32, (LANES, tm), 0)
    gate_out[...] = (jnp.where(sub == 0, g1, 0.0) + jnp.where(sub == 1, g2, 0.0)).T
    eid = lax.broadcasted_iota(jnp.int32, (N_EXPERTS, tm), 0)
    pick1 = eid == e1
    pick2 = eid == e2
    hits = jnp.where(pick1, 1.0, 0.0) + jnp.where(pick2, 1.0, 0.0)
    before = jnp.dot(hits.astype(BF16), upper_ref[...], preferred_element_type=F32) + count_out[...]
    rank1 = jnp.sum(jnp.where(pick1, before, 0.0), axis=0, keepdims=True)
    rank2 = jnp.sum(jnp.where(pick2, before, 0.0), axis=0, keepdims=True)
    count_out[...] += jnp.sum(hits, axis=1, keepdims=True)
    field = lax.broadcasted_iota(jnp.int32, route_out.shape[1:], 0)
    route_out[0] = jnp.where(field == 0, e1, jnp.where(field == 1, e2, jnp.where(
        field == 2, rank1.astype(jnp.int32), rank2.astype(jnp.int32))))


def _mix_router(a, w_out, h, gain, mod, wr_hi, wr_lo, b_router, tm):
    t, d = h.shape
    row = pl.BlockSpec((tm, d), lambda i: (i, 0))
    const = lambda shape: pl.BlockSpec(shape, lambda i: (0,) * len(shape))
    upper = (jnp.arange(tm)[:, None] < jnp.arange(tm)[None, :]).astype(BF16)
    return pl.pallas_call(
        _mix_router_body,
        grid=(t // tm,),
        in_specs=[row, const((d, d)), row, mod.spec(2), const((1, d)), mod.spec(3), mod.spec(4),
                  const((N_EXPERTS, d)), const((N_EXPERTS, d)), const((N_EXPERTS, 1)), const((tm, tm))],
        out_specs=[row, row, pl.BlockSpec((tm, LANES), lambda i: (i, 0)),
                   pl.BlockSpec((1, 4, tm), lambda i: (i, 0, 0)), const((N_EXPERTS, 1))],
        out_shape=[jax.ShapeDtypeStruct((t, d), F32), jax.ShapeDtypeStruct((t, d), F32),
                   jax.ShapeDtypeStruct((t, LANES), F32), jax.ShapeDtypeStruct((t // tm, 4, tm), jnp.int32),
                   jax.ShapeDtypeStruct((N_EXPERTS, 1), F32)],
        compiler_params=_params(("arbitrary",)),
        name="mix_router",
    )(a, w_out, h, mod.arr, gain.reshape(1, d), mod.arr, mod.arr, wr_hi, wr_lo,
      b_router.reshape(N_EXPERTS, 1).astype(F32), upper)


def _row_copy(src, src_row, dst, dst_row, sem):
    return pltpu.make_async_copy(src.at[pl.ds(src_row, 1)], dst.at[pl.ds(dst_row, 1)], sem)


def _index_slot(tm):
    return -(-2 * tm // SMEM_I32_CHUNK) * SMEM_I32_CHUNK


def _load_positions(pos_hbm, idx_ref, sem, step):
    n = idx_ref.shape[0]
    cp = pltpu.make_async_copy(pos_hbm.at[pl.ds(pl.multiple_of(step * n, n), n)], idx_ref, sem)
    cp.start()
    cp.wait()


def _dispatch_body(seg_end_ref, pos_hbm, x_ref, xs_out, idx_ref, zero_ref, idx_sem, row_sem, *, tm):
    @pl.when(pl.program_id(0) == 0)
    def _():
        zero_ref[...] = jnp.zeros_like(zero_ref)

        def last_tile(e):
            end = pl.multiple_of(seg_end_ref[e], EXPERT_TILE)
            return pltpu.make_async_copy(zero_ref, xs_out.at[pl.ds(end - EXPERT_TILE, EXPERT_TILE)], row_sem)

        def nonempty(e):
            return seg_end_ref[e] > (seg_end_ref[e - 1] if e else 0)

        for e in range(N_EXPERTS):
            pl.when(nonempty(e))(lambda e=e: last_tile(e).start())
        for e in range(N_EXPERTS):
            pl.when(nonempty(e))(lambda e=e: last_tile(e).wait())

    _load_positions(pos_hbm, idx_ref, idx_sem, pl.program_id(0))

    def start(t, carry):
        _row_copy(x_ref, t, xs_out, idx_ref[t], row_sem).start()
        _row_copy(x_ref, t, xs_out, idx_ref[tm + t], row_sem).start()
        return carry

    lax.fori_loop(0, tm, start, 0)
    for _ in range(2):
        pltpu.make_async_copy(x_ref, xs_out.at[pl.ds(0, tm)], row_sem).wait()


def _dispatch(hn, pos_tiles, seg_end, n_rows, tm):
    t, d = hn.shape
    grid_spec = pltpu.PrefetchScalarGridSpec(
        num_scalar_prefetch=1,
        grid=(t // tm,),
        in_specs=[pl.BlockSpec(memory_space=pl.ANY), pl.BlockSpec((tm, d), lambda i, se: (i, 0))],
        out_specs=pl.BlockSpec(memory_space=pl.ANY),
        scratch_shapes=[pltpu.SMEM((_index_slot(tm),), jnp.int32), pltpu.VMEM((EXPERT_TILE, d), F32),
                        pltpu.SemaphoreType.DMA, pltpu.SemaphoreType.DMA])
    return pl.pallas_call(
        functools.partial(_dispatch_body, tm=tm),
        grid_spec=grid_spec,
        out_shape=jax.ShapeDtypeStruct((n_rows, d), F32),
        compiler_params=_params(("arbitrary",)),
        name="moe_dispatch",
    )(seg_end, pos_tiles, hn)


def _expert_body(te_ref, nu_ref, x_ref, win_ref, wout_ref, y_ref):
    @pl.when(pl.program_id(0) < nu_ref[0])
    def _():
        hid = jnp.dot(x_ref[...].astype(BF16), win_ref[0], preferred_element_type=F32)
        de = hid.shape[1] // 2
        a, u = hid[:, :de], hid[:, de:]
        act = (a * _sigmoid(a) * u).astype(BF16)
        y_ref[...] = jnp.dot(act, wout_ref[0], preferred_element_type=F32)

    @pl.when(pl.program_id(0) >= nu_ref[0])
    def _():
        y_ref[...] = jnp.zeros_like(y_ref)


def _experts(xs, tile_expert, n_used, w_in, w_out):
    n_rows, d = xs.shape
    n_e, _, d2 = w_in.shape
    tile = lambda i, te, nu: jnp.minimum(i, nu[0] - 1)
    grid_spec = pltpu.PrefetchScalarGridSpec(
        num_scalar_prefetch=2,
        grid=(n_rows // EXPERT_TILE,),
        in_specs=[pl.BlockSpec((EXPERT_TILE, d), lambda i, te, nu: (tile(i, te, nu), 0)),
                  pl.BlockSpec((1, d, d2), lambda i, te, nu: (te[tile(i, te, nu)], 0, 0)),
                  pl.BlockSpec((1, d2 // 2, d), lambda i, te, nu: (te[tile(i, te, nu)], 0, 0))],
        out_specs=pl.BlockSpec((EXPERT_TILE, d), lambda i, te, nu: (i, 0)))
    return pl.pallas_call(
        _expert_body,
        grid_spec=grid_spec,
        out_shape=jax.ShapeDtypeStruct((n_rows, d), F32),
        compiler_params=_params(("arbitrary",)),
        name="moe_experts",
    )(tile_expert, n_used, xs, w_in, w_out)


def _combine_body(pos_hbm, ys_hbm, gate_ref, h_ref, g2_ref, o_ref, idx_ref, y_ref, idx_sem, row_sem, *, tm):
    _load_positions(pos_hbm, idx_ref, idx_sem, pl.program_id(0))

    def start(t, carry):
        _row_copy(ys_hbm, idx_ref[t], y_ref.at[0], t, row_sem).start()
        _row_copy(ys_hbm, idx_ref[tm + t], y_ref.at[1], t, row_sem).start()
        return carry

    lax.fori_loop(0, tm, start, 0)
    for c in range(2):
        pltpu.make_async_copy(ys_hbm.at[pl.ds(0, tm)], y_ref.at[c], row_sem).wait()
    gates = gate_ref[...]
    moe = gates[:, 0:1] * y_ref[0] + gates[:, 1:2] * y_ref[1]
    o_ref[...] = h_ref[...] + g2_ref[...] * moe


def _combine(ys, pos_tiles, gates, h, mod, tm):
    t, d = h.shape
    row = pl.BlockSpec((tm, d), lambda i: (i, 0))
    return pl.pallas_call(
        functools.partial(_combine_body, tm=tm),
        grid=(t // tm,),
        in_specs=[pl.BlockSpec(memory_space=pl.ANY), pl.BlockSpec(memory_space=pl.ANY),
                  pl.BlockSpec((tm, LANES), lambda i: (i, 0)), row, mod.spec(5)],
        out_specs=row,
        out_shape=jax.ShapeDtypeStruct((t, d), F32),
        scratch_shapes=[pltpu.SMEM((_index_slot(tm),), jnp.int32), pltpu.VMEM((2, tm, d), F32),
                        pltpu.SemaphoreType.DMA, pltpu.SemaphoreType.DMA],
        compiler_params=_params(("arbitrary",)),
        name="moe_combine",
    )(pos_tiles, ys, gates, h, mod.arr)


def _moe(hn, gates, route, counts, h, mod, w_in, w_out, tm):
    t, d = h.shape
    n_tiles = 2 * t // EXPERT_TILE + N_EXPERTS
    n_rows = n_tiles * EXPERT_TILE
    cnt = counts.reshape(N_EXPERTS).astype(jnp.int32)
    padded = (cnt + (EXPERT_TILE - 1)) // EXPERT_TILE * EXPERT_TILE
    experts = jnp.arange(N_EXPERTS, dtype=jnp.int32)
    seg_end = jnp.sum(jnp.where(experts[None, :] <= experts[:, None], padded[None, :], 0), axis=1)
    seg_start = seg_end - padded
    first_row = jnp.sum(jnp.where(route[:, :2, :, None] == experts, seg_start, 0), axis=-1)
    pos_tiles = (first_row + route[:, 2:]).reshape(t // tm, 2 * tm)
    if _index_slot(tm) != 2 * tm:
        pos_tiles = jnp.pad(pos_tiles, ((0, 0), (0, _index_slot(tm) - 2 * tm)))
    pos_tiles = pos_tiles.reshape(-1)
    tile_row = jnp.arange(n_tiles, dtype=jnp.int32) * EXPERT_TILE
    tile_expert = jnp.minimum(jnp.sum((seg_end[None, :] <= tile_row[:, None]).astype(jnp.int32), axis=1),
                              N_EXPERTS - 1)
    n_used = seg_end[-1:] // EXPERT_TILE
    xs = _dispatch(hn, pos_tiles, seg_end, n_rows, tm)
    ys = _experts(xs, tile_expert, n_used, w_in, w_out)
    return _combine(ys, pos_tiles, gates, h, mod, tm)


def _head_norm(y, gsum_ref, gain_ref, head_dim):
    sq = y * y
    hi = sq.astype(BF16)
    lo = (sq - hi.astype(F32)).astype(BF16)
    ssum = (jnp.dot(hi, gsum_ref[...], preferred_element_type=F32)
            + jnp.dot(lo, gsum_ref[...], preferred_element_type=F32))
    return y * lax.rsqrt(ssum * (1.0 / head_dim) + EPS) * gain_ref[...]


def _head_sum_matrix(d, head_dim):
    r = jnp.arange(d) // head_dim
    return (r[:, None] == r[None, :]).astype(BF16)


def _q_proj_body(x_ref, gain_ref, sh_ref, sc_ref, w_ref, gsum_ref, qn_ref, q_ref, *, head_dim, q_scale):
    xn = _norm_mod(x_ref[...], gain_ref[...], sh_ref[...], sc_ref[...]).astype(BF16)
    y = jnp.dot(xn, w_ref[...], preferred_element_type=F32)
    q = _head_norm(y, gsum_ref, qn_ref, head_dim) * q_scale
    q_ref[...] = q.astype(q_ref.dtype)


def _q_proj(h, gain, mod, w, gsum, q_norm, tm, out_dtype, q_scale):
    t, d = h.shape
    head_dim = d // FOX_HEADS
    row = pl.BlockSpec((tm, d), lambda i: (i, 0))
    const = lambda shape: pl.BlockSpec(shape, lambda i: (0,) * len(shape))
    return pl.pallas_call(
        functools.partial(_q_proj_body, head_dim=head_dim, q_scale=q_scale),
        grid=(t // tm,),
        in_specs=[row, const((1, d)), mod.spec(0), mod.spec(1), const((d, d)), const((d, d)), const((1, d))],
        out_specs=row,
        out_shape=jax.ShapeDtypeStruct((t, d), out_dtype),
        compiler_params=_params(("arbitrary",)),
        name="q_proj",
    )(h, gain.reshape(1, d), mod.arr, mod.arr, w, gsum, jnp.tile(q_norm, FOX_HEADS).reshape(1, d))


def _kv_proj_body(x_ref, gain_ref, sh_ref, sc_ref, wk_ref, wv_ref, wf_ref, gsum_ref, kn_ref, bf_ref,
                  k_ref, v_ref, kb_ref, vb_ref, lf_ref, *, head_dim):
    xn = _norm_mod(x_ref[...], gain_ref[...], sh_ref[...], sc_ref[...]).astype(BF16)
    k = _head_norm(jnp.dot(xn, wk_ref[...], preferred_element_type=F32), gsum_ref, kn_ref, head_dim)
    k_ref[...] = k
    kb_ref[...] = k.astype(BF16)
    v = jnp.dot(xn, wv_ref[...], preferred_element_type=F32)
    v_ref[...] = v
    vb_ref[...] = v.astype(BF16)
    zf = jnp.dot(xn, wf_ref[...], preferred_element_type=F32) + bf_ref[...]
    lf_ref[...] = _log_sigmoid(zf)[:, :lf_ref.shape[1]]


def _kv_proj(h, gain, mod, wk, wv, wf, gsum, k_norm, b_fgate, tm):
    t, d = h.shape
    head_dim = d // FOX_HEADS
    row = pl.BlockSpec((tm, d), lambda i: (i, 0))
    const = lambda shape: pl.BlockSpec(shape, lambda i: (0,) * len(shape))
    bf = jnp.pad(b_fgate.astype(F32), (0, LANES - FOX_HEADS)).reshape(1, LANES)
    return pl.pallas_call(
        functools.partial(_kv_proj_body, head_dim=head_dim),
        grid=(t // tm,),
        in_specs=[row, const((1, d)), mod.spec(0), mod.spec(1), const((d, d)), const((d, d)),
                  const((d, LANES)), const((d, d)), const((1, d)), const((1, LANES))],
        out_specs=[row, row, row, row, pl.BlockSpec((tm, FOX_HEADS), lambda i: (i, 0))],
        out_shape=[jax.ShapeDtypeStruct((t, d), F32), jax.ShapeDtypeStruct((t, d), F32),
                   jax.ShapeDtypeStruct((t, d), BF16), jax.ShapeDtypeStruct((t, d), BF16),
                   jax.ShapeDtypeStruct((t, FOX_HEADS), F32)],
        compiler_params=_params(("arbitrary",)),
        name="kv_proj",
    )(h, gain.reshape(1, d), mod.arr, mod.arr, wk, wv, wf, gsum,
      jnp.tile(k_norm, FOX_HEADS).reshape(1, d), bf)


def _cumsum_lanes_body(x_ref, o_ref, carry_ref, *, carry_rows):
    j = pl.program_id(1)

    @pl.when(j == 0)
    def _():
        carry_ref[...] = jnp.zeros_like(carry_ref)

    w = x_ref.shape[-1]
    r = lax.broadcasted_iota(jnp.int32, (w, w), 0)
    c = lax.broadcasted_iota(jnp.int32, (w, w), 1)
    upper = jnp.where(r <= c, 1.0, 0.0).astype(F32)
    y = jnp.dot(x_ref[0], upper, precision=HIGHEST, preferred_element_type=F32)
    if carry_rows:
        y = y + carry_ref[...]
        carry_ref[...] = y[:, w - 1:w]
    o_ref[0] = y


def _cumsum_lanes(x, width, carry):
    b, r, t = x.shape
    spec = pl.BlockSpec((1, r, width), lambda i, j: (i, 0, j))
    return pl.pallas_call(
        functools.partial(_cumsum_lanes_body, carry_rows=carry),
        grid=(b, t // width),
        in_specs=[spec],
        out_specs=spec,
        out_shape=jax.ShapeDtypeStruct(x.shape, F32),
        scratch_shapes=[pltpu.VMEM((r, 1), F32)],
        compiler_params=_params(("arbitrary", "arbitrary")),
        name="cumsum_lanes",
    )(x)


def _attn_prompt_body(qt_ref, kt_ref, q_ref, k_ref, v_ref, f_ref, o_ref, m_ref, acc_ref,
                      *, blk, rows, head_dim):
    p = pl.program_id(2)
    qi = qt_ref[p]
    ki = kt_ref[p]

    @pl.when(ki == 0)
    def _():
        m_ref[...] = jnp.full(m_ref.shape, NEG_BIG, F32)
        acc_ref[...] = jnp.zeros_like(acc_ref)

    def process(masked):
        k = k_ref[...]
        v = v_ref[...]
        f = f_ref[...]
        first_v = lax.broadcasted_iota(jnp.int32, v.shape, 1) < head_dim
        ones = jnp.ones_like(v)
        v_heads = (jnp.where(first_v, v, ones), jnp.where(first_v, ones, v))
        m_olds = {(hh, r0): m_ref[hh, r0:r0 + rows, :] for hh in range(2) for r0 in range(0, blk, rows)}
        acc_olds = {(hh, r0): acc_ref[hh, r0:r0 + rows, :] for hh in range(2) for r0 in range(0, blk, rows)}
        m_news, acc_news = {}, {}
        for r0 in range(0, blk, rows):
            q = q_ref[r0:r0 + rows, :]
            first_q = lax.broadcasted_iota(jnp.int32, q.shape, 1) < head_dim
            zero = jnp.zeros_like(q)
            for hh, qh in enumerate((jnp.where(first_q, q, zero), jnp.where(first_q, zero, q))):
                s = _nt_dot(qh, k) - f[hh:hh + 1, :]
                if masked:
                    row = lax.broadcasted_iota(jnp.int32, s.shape, 0) + r0
                    col = lax.broadcasted_iota(jnp.int32, s.shape, 1)
                    s = jnp.where(col <= row, s, NEG_BIG)
                m_old = m_olds[hh, r0]
                m_new = jnp.maximum(m_old, jnp.max(s, axis=1, keepdims=True))
                alpha = jnp.exp2(m_old - m_new)
                pr = jnp.concatenate([jnp.exp2(s[:, c * LANES:(c + 1) * LANES] - m_new)
                                      for c in range(blk // LANES)], axis=1).astype(BF16)
                acc_news[hh, r0] = alpha * acc_olds[hh, r0] + jnp.dot(pr, v_heads[hh], preferred_element_type=F32)
                m_news[hh, r0] = m_new
        for (hh, r0), m_new in m_news.items():
            m_ref[hh, r0:r0 + rows, :] = m_new
            acc_ref[hh, r0:r0 + rows, :] = acc_news[hh, r0]

    @pl.when(ki < qi)
    def _():
        process(False)

    @pl.when(ki == qi)
    def _():
        process(True)
        a0 = acc_ref[0]
        a1 = acc_ref[1]
        first = lax.broadcasted_iota(jnp.int32, a0.shape, 1) < head_dim
        inv0 = 1.0 / a0[:, head_dim:head_dim + 1]
        inv1 = 1.0 / a1[:, 0:1]
        o_ref[...] = jnp.where(first, a0 * inv0, a1 * inv1).astype(o_ref.dtype)


def _attn_prompt(q, k, v, f_pairs, nseq):
    tt, d = q.shape
    t = tt // nseq
    blk = min(ATTN_BLOCK, t)
    rows = min(ATTN_ROWS, blk)
    nb = t // blk
    head_dim = d // FOX_HEADS
    pairs = [(qi, ki) for qi in range(nb) for ki in range(qi + 1)]
    q_tab = jnp.asarray([p[0] for p in pairs], jnp.int32)
    k_tab = jnp.asarray([p[1] for p in pairs], jnp.int32)
    grid_spec = pltpu.PrefetchScalarGridSpec(
        num_scalar_prefetch=2,
        grid=(nseq, d // LANES, len(pairs)),
        in_specs=[pl.BlockSpec((blk, LANES), lambda b, hp, p, qt, kt: (b * nb + qt[p], hp)),
                  pl.BlockSpec((blk, LANES), lambda b, hp, p, qt, kt: (b * nb + kt[p], hp)),
                  pl.BlockSpec((blk, LANES), lambda b, hp, p, qt, kt: (b * nb + kt[p], hp)),
                  pl.BlockSpec((None, None, 2, blk), lambda b, hp, p, qt, kt: (b, hp, 0, kt[p]))],
        out_specs=pl.BlockSpec((blk, LANES), lambda b, hp, p, qt, kt: (b * nb + qt[p], hp)),
        scratch_shapes=[pltpu.VMEM((2, blk, LANES), F32), pltpu.VMEM((2, blk, LANES), F32)])
    return pl.pallas_call(
        functools.partial(_attn_prompt_body, blk=blk, rows=rows, head_dim=head_dim),
        grid_spec=grid_spec,
        out_shape=jax.ShapeDtypeStruct((tt, d), BF16),
        compiler_params=_params(("arbitrary", "arbitrary", "arbitrary")),
        name="attn_prompt",
    )(q_tab, k_tab, q, k, v, f_pairs)


def _gather_cum_body(pt_ref, cp_ref, new_ref, o_ref, *, n_pages, page):
    b = pl.program_id(0)

    def step(j, carry):
        idx = pt_ref[b * n_pages + j]
        blk = cp_ref[idx] + carry
        o_ref[0, :, pl.ds(pl.multiple_of(j * page, page), page)] = blk
        return blk[:, page - 1:page]

    carry = lax.fori_loop(0, n_pages, step, jnp.zeros((cp_ref.shape[1], 1), F32))
    o_ref[0, :, n_pages * page:] = new_ref[0] + carry


def _gather_cum(page_table, cum_pages, cum_new):
    nseq, n_pages = page_table.shape
    n_pool, hh, page = cum_pages.shape
    grid_spec = pltpu.PrefetchScalarGridSpec(
        num_scalar_prefetch=1,
        grid=(nseq,),
        in_specs=[pl.BlockSpec((n_pool, hh, page), lambda b, pt: (0, 0, 0)),
                  pl.BlockSpec((1, hh, page), lambda b, pt: (b, 0, 0))],
        out_specs=pl.BlockSpec((1, hh, (n_pages + 1) * page), lambda b, pt: (b, 0, 0)))
    return pl.pallas_call(
        functools.partial(_gather_cum_body, n_pages=n_pages, page=page),
        grid_spec=grid_spec,
        out_shape=jax.ShapeDtypeStruct((nseq, hh, (n_pages + 1) * page), F32),
        compiler_params=_params(("arbitrary",)),
        name="gather_cum",
    )(page_table.reshape(-1), cum_pages, cum_new)


def _attn_paged_body(pt_ref, *refs, n_q, page, pps, n_steps, head_dim):
    q_ref = refs[0]
    k_refs = refs[1:1 + pps]
    v_refs = refs[1 + pps:1 + 2 * pps]
    kn_ref, vn_ref, f_ref, o_ref, qbd_ref, m_ref, l_ref, acc_ref = refs[1 + 2 * pps:]
    j = pl.program_id(1)
    d = q_ref.shape[1]
    rows = FOX_HEADS * n_q

    @pl.when(j == 0)
    def _():
        q = q_ref[...]
        lane_head = lax.broadcasted_iota(jnp.int32, q.shape, 1) // head_dim
        for h in range(FOX_HEADS):
            qbd_ref[h * n_q:(h + 1) * n_q, :] = jnp.where(lane_head == h, q, 0.0).astype(BF16)
        m_ref[...] = jnp.full(m_ref.shape, NEG_BIG, F32)
        l_ref[...] = jnp.zeros_like(l_ref)
        acc_ref[...] = jnp.zeros_like(acc_ref)

    def attend(k, v, f, mask):
        bias = jnp.concatenate([jnp.broadcast_to(f[h:h + 1, :], (n_q, page)) for h in range(FOX_HEADS)], axis=0)
        s = _nt_dot(qbd_ref[...], k) - bias
        if mask is not None:
            s = jnp.where(mask, s, NEG_BIG)
        m_old = m_ref[...]
        m_new = jnp.maximum(m_old, jnp.max(s, axis=1, keepdims=True))
        alpha = jnp.exp(m_old - m_new)
        pr = jnp.exp(s - m_new)
        l_ref[...] = alpha * l_ref[...] + jnp.sum(pr, axis=1, keepdims=True)
        m_ref[...] = m_new
        acc_ref[...] = alpha * acc_ref[...] + jnp.dot(pr.astype(BF16), v, preferred_element_type=F32)

    @pl.when(j < n_steps - 1)
    def _():
        for u in range(pps):
            off = pl.multiple_of((j * pps + u) * page, page)
            attend(k_refs[u][...].astype(BF16), v_refs[u][...].astype(BF16), f_ref[0, :, pl.ds(off, page)], None)

    @pl.when(j == n_steps - 1)
    def _():
        pad = jnp.zeros((page - n_q, d), F32)
        k = jnp.concatenate([kn_ref[...], pad], axis=0).astype(BF16)
        v = jnp.concatenate([vn_ref[...], pad], axis=0).astype(BF16)
        t_of_row = lax.broadcasted_iota(jnp.int32, (rows, page), 0) % n_q
        key = lax.broadcasted_iota(jnp.int32, (rows, page), 1)
        attend(k, v, f_ref[0, :, (n_steps - 1) * pps * page:], key <= t_of_row)
        acc = acc_ref[...] * (1.0 / l_ref[...])
        lane_head = lax.broadcasted_iota(jnp.int32, (n_q, d), 1) // head_dim
        out = jnp.zeros((n_q, d), F32)
        for h in range(FOX_HEADS):
            out = out + jnp.where(lane_head == h, acc[h * n_q:(h + 1) * n_q, :], 0.0)
        o_ref[...] = out


def _attn_paged(q, cache_k, cache_v, k_new, v_new, f_all, page_table):
    tt, d = q.shape
    nseq, n_pages = page_table.shape
    n_q = tt // nseq
    page = cache_k.shape[1]
    pps = PAGES_PER_STEP
    n_steps = n_pages // pps + 1
    head_dim = d // FOX_HEADS
    rows = FOX_HEADS * n_q

    def page_spec(u):
        def index(b, j, pt):
            return (pt[b * n_pages + jnp.minimum(j, n_steps - 2) * pps + u], 0, 0)
        return pl.BlockSpec((None, page, d), index)

    tok = pl.BlockSpec((n_q, d), lambda b, j, pt: (b, 0))
    grid_spec = pltpu.PrefetchScalarGridSpec(
        num_scalar_prefetch=1,
        grid=(nseq, n_steps),
        in_specs=([tok] + [page_spec(u) for u in range(pps)] + [page_spec(u) for u in range(pps)]
                  + [tok, tok, pl.BlockSpec((1, FOX_HEADS, f_all.shape[2]), lambda b, j, pt: (b, 0, 0))]),
        out_specs=tok,
        scratch_shapes=[pltpu.VMEM((rows, d), BF16), pltpu.VMEM((rows, 1), F32),
                        pltpu.VMEM((rows, 1), F32), pltpu.VMEM((rows, d), F32)])
    body = functools.partial(_attn_paged_body, n_q=n_q, page=page, pps=pps, n_steps=n_steps,
                             head_dim=head_dim)
    return pl.pallas_call(
        body,
        grid_spec=grid_spec,
        out_shape=jax.ShapeDtypeStruct((tt, d), F32),
        compiler_params=_params(("arbitrary", "arbitrary")),
        name="attn_paged",
    )(page_table.reshape(-1), q, *([cache_k] * pps), *([cache_v] * pps), k_new, v_new, f_all)


def _lower_bounds(lb_logits):
    p = jax.nn.softmax(lb_logits.astype(F32), axis=0)
    return jnp.cumsum(p, axis=0) - p[0]


def _trunk(x, nseq, mods, mod_kv, s0, past, w):
    tt, d = x.shape
    t = tt // nseq
    tm = min(512, tt)
    tiles_per_seq = max(t // tm, 1)
    n_a = w["w_in_a"].shape[0]
    depth = w["w_ada"].shape[0]
    head_dim = d // FOX_HEADS
    lbs = _lower_bounds(w["lb_logits"])
    mk = lambda arr: _Mod(arr, d, tm, tiles_per_seq)

    h = x
    states = []
    for l in range(depth):
        mod = mk(mods[l])
        if l < n_a:
            q, zf, i, g = _hgrn_in_proj(h, w["norm_mix"][l], mod, w["w_in_a4"][l], tm)
            tp = -(-t // SCAN_CHUNK) * SCAN_CHUNK
            if tp != t:
                padr = lambda a: jnp.pad(a.reshape(nseq, t, d), ((0, 0), (0, tp - t), (0, 0))).reshape(nseq * tp, d)
                q, zf, i, g = padr(q), padr(zf), padr(i), padr(g)
            o, s_t = _hgrn_scan(q, zf, i, g, lbs[l], w["gnorm_a"][l], s0[l], nseq, t)
            if tp != t:
                o = o.reshape(nseq, tp, d)[:, :t].reshape(tt, d)
            states.append(s_t)
            mix_in, w_out = o, w["w_out_a"][l]
        else:
            j = l - n_a
            if past is None:
                q = _q_proj(h, w["norm_mix"][l], mod, w["w_q_b"][j], w["gsum"], w["q_norm"][j], tm, BF16,
                            head_dim ** -0.5 * LOG2E)
                mix_in = _attn_prompt(q, k_bf, v_bf, f_keys, nseq)
            else:
                q = _q_proj(h, w["norm_mix"][l], mod, w["w_q_b"][j], w["gsum"], w["q_norm"][j], tm, F32,
                            head_dim ** -0.5)
                mix_in = _attn_paged(q, past[0], past[1], k_new, v_new, f_keys, past[3])
            w_out = w["w_out_b"][j]
        h, hn, gates, route, counts = _mix_router(mix_in, w_out, h, w["norm_ffn"][l], mod, w["wr_hi"],
                                                  w["wr_lo"], w["b_router"], tm)
        h = _moe(hn, gates, route, counts, h, mod, w["w_exp_in"][l], w["w_exp_out"][l], tm)
        if l == n_a - 1:
            k_new, v_new, k_bf, v_bf, logf_new = _kv_proj(
                h, w["norm_kv"], mk(mod_kv), w["w_k"], w["w_v"], w["w_f"], w["gsum"], w["k_norm"],
                w["b_fgate"], tm)
            lf_t = logf_new.reshape(nseq, t, FOX_HEADS).transpose(0, 2, 1)
            if past is None:
                f_cum = _cumsum_lanes(lf_t, min(512, t), carry=True)
                f_keys = (f_cum * LOG2E).reshape(nseq, FOX_HEADS // 2, 2, t)
            else:
                cache_logf, page_table = past[2], past[3]
                page = cache_logf.shape[1]
                pool_t = cache_logf.astype(F32).transpose(0, 2, 1)
                n_pool = pool_t.shape[0]
                rows = 64 if n_pool % 64 == 0 else 1
                cum_pages = _cumsum_lanes(pool_t.reshape(n_pool // rows, rows * FOX_HEADS, page), page,
                                          carry=False).reshape(n_pool, FOX_HEADS, page)
                new_pad = jnp.pad(lf_t, ((0, 0), (0, 0), (0, page - t)))
                cum_new = _cumsum_lanes(new_pad, page, carry=False)
                f_keys = _gather_cum(page_table, cum_pages, cum_new)
    return h, jnp.stack(states), k_new, v_new, logf_new


def kernel(x_prompt, x_sample, cache_k, cache_v, cache_logf, state_hgrn, page_table, c_prompt, c_sample,
           w_ada, b_ada, norm_mix, norm_ffn, w_in_a, lb_logits, gnorm_a, w_out_a, norm_kv, w_ada_kv,
           b_ada_kv, w_kv, b_fgate, k_norm, w_q_b, q_norm, w_out_b, w_router, b_router, w_exp_in, w_exp_out):
    nb, seq, d = x_prompt.shape
    ns, dseq, _ = x_sample.shape
    n_a = w_in_a.shape[0]
    depth = w_ada.shape[0]
    hd = FOX_HEADS * (d // FOX_HEADS)
    dk = d // HG_HEADS

    nrow = nb + ns
    nrow_pad = -(-nrow // SUBLANES) * SUBLANES
    c_all = jnp.pad(jnp.concatenate([c_prompt, c_sample], axis=0).astype(F32), ((0, nrow_pad - nrow), (0, 0)))
    mod_all = _ada(c_all, w_ada, b_ada)
    mod_kv_all = _ada(c_all, w_ada_kv[None], b_ada_kv[None])[0]
    mods_p = [mod_all[l, :nb].reshape(nb, 1, 6 * d) for l in range(depth)]
    mods_s = [jnp.repeat(mod_all[l, nb:nrow], dseq, axis=0) for l in range(depth)]
    mod_kv_p = mod_kv_all[:nb].reshape(nb, 1, 2 * d)
    mod_kv_s = jnp.repeat(mod_kv_all[nb:nrow], dseq, axis=0)

    wr_t = w_router.astype(F32).T
    wr_hi = wr_t.astype(BF16)
    wr_lo = (wr_t - wr_hi.astype(F32)).astype(BF16)
    w_in_b = w_in_a.astype(BF16)
    w = dict(
        w_ada=w_ada, lb_logits=lb_logits, norm_mix=norm_mix, norm_ffn=norm_ffn, gnorm_a=gnorm_a,
        w_in_a=w_in_a,
        w_in_a4=[[w_in_b[l, :, s * d:(s + 1) * d] for s in range(4)] for l in range(n_a)],
        w_out_a=w_out_a.astype(BF16), norm_kv=norm_kv,
        w_k=w_kv[:, :hd].astype(BF16), w_v=w_kv[:, hd:2 * hd].astype(BF16),
        w_f=jnp.pad(w_kv[:, 2 * hd:], ((0, 0), (0, LANES - FOX_HEADS))).astype(BF16),
        b_fgate=b_fgate, k_norm=k_norm, w_q_b=w_q_b.astype(BF16), q_norm=q_norm,
        w_out_b=w_out_b.astype(BF16), wr_hi=wr_hi, wr_lo=wr_lo, b_router=b_router,
        w_exp_in=w_exp_in.astype(BF16), w_exp_out=w_exp_out.astype(BF16),
        gsum=_head_sum_matrix(d, d // FOX_HEADS),
    )

    s0_p = jnp.zeros((n_a, nb, HG_HEADS, dk, dk), F32)
    y_p, st_p, k_p, v_p, lf_p = _trunk(x_prompt.reshape(nb * seq, d), nb, mods_p, mod_kv_p, s0_p, None, w)

    n_pool, page = cache_k.shape[0], cache_k.shape[1]
    past = (cache_k.reshape(n_pool, page, hd), cache_v.reshape(n_pool, page, hd), cache_logf, page_table)
    y_s, st_s, k_s, v_s, lf_s = _trunk(x_sample.reshape(ns * dseq, d), ns, mods_s, mod_kv_s,
                                       state_hgrn.astype(F32), past, w)

    hs = (FOX_HEADS, d // FOX_HEADS)
    return (y_p.reshape(nb, seq, d), y_s.reshape(ns, dseq, d),
            st_p.astype(state_hgrn.dtype), st_s.astype(state_hgrn.dtype),
            k_p.reshape(nb, seq, *hs).astype(cache_k.dtype), v_p.reshape(nb, seq, *hs).astype(cache_v.dtype),
            lf_p.reshape(nb, seq, FOX_HEADS).astype(cache_logf.dtype),
            k_s.reshape(ns, dseq, *hs).astype(cache_k.dtype), v_s.reshape(ns, dseq, *hs).astype(cache_v.dtype),
            lf_s.reshape(ns, dseq, FOX_HEADS).astype(cache_logf.dtype))
```

```python
import functools
import math

import jax
import jax.numpy as jnp
from jax import lax
from jax.experimental import pallas as pl
from jax.experimental.pallas import tpu as pltpu

F32 = jnp.float32
BF16 = jnp.bfloat16
HIGHEST = lax.Precision.HIGHEST

HG_HEADS = 8
FOX_HEADS = 16
N_EXPERTS = 16
N_GROUPS = 4
EXPERTS_PER_GROUP = N_EXPERTS // N_GROUPS
EPS = 1e-6
NEG_BIG = -1e30
TINY = 1e-30

LANES = 128
SUBLANES = 8
VMEM_LIMIT_BYTES = 56 * 1024 * 1024

SCAN_CHUNK = 128
ATTN_BLOCK = 1024
ATTN_ROWS = 1024
LOG2E = math.log2(math.e)
PAGES_PER_STEP = 8
EXPERT_TILE = 256
SMEM_I32_CHUNK = 1024
ROW_COPY_UNROLL = 8


def _params(sem):
    return pltpu.CompilerParams(dimension_semantics=sem, vmem_limit_bytes=VMEM_LIMIT_BYTES)


def _nt_dot(a, b):
    return lax.dot_general(a, b, (((1,), (1,)), ((), ())), preferred_element_type=F32)


def _sigmoid(x):
    return 1.0 / (1.0 + jnp.exp(-x))


def _log_sigmoid(x):
    return jnp.minimum(x, 0.0) - jnp.log1p(jnp.exp(-jnp.abs(x)))


def _norm_mod(x, gain, shift, scale):
    ms = jnp.mean(x * x, axis=-1, keepdims=True)
    y = x * lax.rsqrt(ms + EPS) * gain
    return y * (1.0 + scale) + shift


class _Mod:
    def __init__(self, arr, d, tm, tiles_per_seq):
        self.arr = arr
        self.d = d
        self.tm = tm
        self.tiles_per_seq = tiles_per_seq
        self.per_token = arr.ndim == 2

    def spec(self, chunk):
        d, tps = self.d, self.tiles_per_seq
        if self.per_token:
            return pl.BlockSpec((self.tm, d), lambda i, *_: (i, chunk))
        return pl.BlockSpec((None, 1, d), lambda i, *_: (i // tps, 0, chunk))


def _ada_body(c_ref, w_ref, b_ref, o_ref):
    c = c_ref[...]
    a = c * _sigmoid(c)
    o_ref[0] = jnp.dot(a, w_ref[0], precision=HIGHEST, preferred_element_type=F32) + b_ref[0]


def _ada(c, w, b):
    n_layers, d, n = w.shape
    r = c.shape[0]
    tn = 1536 if n % 1536 == 0 else 1024
    return pl.pallas_call(
        _ada_body,
        grid=(n_layers, n // tn),
        in_specs=[pl.BlockSpec((r, d), lambda l, j: (0, 0)),
                  pl.BlockSpec((1, d, tn), lambda l, j: (l, 0, j)),
                  pl.BlockSpec((1, 1, tn), lambda l, j: (l, 0, j))],
        out_specs=pl.BlockSpec((1, r, tn), lambda l, j: (l, 0, j)),
        out_shape=jax.ShapeDtypeStruct((n_layers, r, n), F32),
        compiler_params=_params(("arbitrary", "arbitrary")),
        name="ada",
    )(c, w, b.reshape(n_layers, 1, n))


def _hgrn_in_body(x_ref, gain_ref, sh_ref, sc_ref, wq_ref, wf_ref, wi_ref, wg_ref,
                  q_ref, zf_ref, i_ref, g_ref):
    xn = _norm_mod(x_ref[...], gain_ref[...], sh_ref[...], sc_ref[...]).astype(BF16)
    q_ref[...] = jnp.dot(xn, wq_ref[...], preferred_element_type=F32).astype(q_ref.dtype)
    zf_ref[...] = jnp.dot(xn, wf_ref[...], preferred_element_type=F32)
    i_ref[...] = jnp.dot(xn, wi_ref[...], preferred_element_type=F32).astype(i_ref.dtype)
    g_ref[...] = jnp.dot(xn, wg_ref[...], preferred_element_type=F32).astype(g_ref.dtype)


def _hgrn_in_proj(h, gain, mod, w4, tm):
    t, d = h.shape
    row = pl.BlockSpec((tm, d), lambda i: (i, 0))
    full = pl.BlockSpec((d, d), lambda i: (0, 0))
    return pl.pallas_call(
        _hgrn_in_body,
        grid=(t // tm,),
        in_specs=[row, pl.BlockSpec((1, d), lambda i: (0, 0)), mod.spec(0), mod.spec(1),
                  full, full, full, full],
        out_specs=[row, row, row, row],
        out_shape=[jax.ShapeDtypeStruct((t, d), BF16), jax.ShapeDtypeStruct((t, d), F32),
                   jax.ShapeDtypeStruct((t, d), BF16), jax.ShapeDtypeStruct((t, d), BF16)],
        compiler_params=_params(("arbitrary",)),
        name="hgrn_in_proj",
    )(h, gain.reshape(1, d), mod.arr, mod.arr, *w4)


def _split3(x):
    hi = x.astype(BF16)
    r1 = x - hi.astype(F32)
    mid = r1.astype(BF16)
    lo = (r1 - mid.astype(F32)).astype(BF16)
    return hi, mid, lo


def _hgrn_scan_body(q_ref, zf_ref, i_ref, g_ref, lb_ref, gn_ref, s0_ref, o_ref, st_ref,
                    *, chunk, n_valid, dk):
    c = pl.program_id(1)

    @pl.when(c == 0)
    def _():
        st_ref[...] = s0_ref[...]

    row = lax.broadcasted_iota(jnp.int32, (chunk, chunk), 0)
    col = lax.broadcasted_iota(jnp.int32, (chunk, chunk), 1)
    tri = jnp.where(row >= col, 1.0, 0.0).astype(BF16)
    n_levels = chunk.bit_length() - 1
    differ = row ^ col
    on_diag = differ == 0
    at_level = [(lax.shift_right_logical(differ, lv) == 1) & (row > col) for lv in range(n_levels)]
    tok = lax.broadcasted_iota(jnp.int32, (chunk, dk), 0)
    in_right = [(lax.shift_right_logical(tok, lv) & 1) == 1 for lv in range(n_levels)]
    quad = tok & 3
    if n_valid < chunk:
        valid = tok < n_valid
    for h in range(HG_HEADS):
        hs = slice(h * dk, (h + 1) * dk)
        q = q_ref[:, hs].astype(F32)
        zf = zf_ref[:, hs]
        v = i_ref[:, hs]
        lb = lb_ref[:, hs]
        e_abs = jnp.exp(-jnp.abs(zf))
        big = 1.0 / (1.0 + e_abs)
        small = e_abs * big
        nonneg = zf >= 0.0
        log_f = jnp.log(jnp.maximum(lb, TINY) + (1.0 - lb) * jnp.where(nonneg, big, small))
        k = (1.0 - lb) * jnp.where(nonneg, small, big)
        if n_valid < chunk:
            log_f = jnp.where(valid, log_f, 0.0)
            k = jnp.where(valid, k, 0.0)
        g_hi, g_mid, g_lo = _split3(log_f)
        b = (jnp.dot(tri, g_hi, preferred_element_type=F32)
             + jnp.dot(tri, g_mid, preferred_element_type=F32)
             + jnp.dot(tri, g_lo, preferred_element_type=F32))
        b_end = b[chunk - 1:chunk, :]
        a = jnp.where(on_diag, _nt_dot(q.astype(BF16), k.astype(BF16)), 0.0)
        for lv in range(n_levels):
            half = 1 << lv
            if lv == 0:
                dist = jnp.where(in_right[0], log_f, 0.0)
            elif lv == 1:
                edge = jnp.where(quad == 0, pltpu.roll(b, chunk - 1, 0),
                                 jnp.where(quad == 2, pltpu.roll(b, 1, 0),
                                           jnp.where(quad == 3, pltpu.roll(b, 2, 0), b)))
                dist = -jnp.abs(b - edge)
            else:
                nblk = chunk // (2 * half)
                edge = b.reshape(nblk, 2 * half, dk)[:, half - 1:half, :]
                edge = jnp.broadcast_to(edge, (nblk, 2 * half, dk)).reshape(chunk, dk)
                dist = -jnp.abs(b - edge)
            z = (jnp.where(in_right[lv], q, k) * jnp.exp(dist)).astype(BF16)
            a = jnp.where(at_level[lv], _nt_dot(z, z), a)
        s = st_ref[0, h]
        o = (jnp.dot(a.astype(BF16), v, preferred_element_type=F32)
             + jnp.dot((q * jnp.exp(b)).astype(BF16), s.astype(BF16), preferred_element_type=F32))
        k_d = (k * jnp.exp(b_end - b)).T.astype(BF16)
        decay = jnp.broadcast_to(jnp.exp(b_end), (dk, dk)).T
        st_ref[0, h] = decay * s + jnp.dot(k_d, v, preferred_element_type=F32)
        ms = jnp.mean(o * o, axis=-1, keepdims=True)
        gate = g_ref[:, hs].astype(F32)
        o = o * lax.rsqrt(ms + EPS) * gn_ref[...] * (gate * _sigmoid(gate))
        o_ref[:, hs] = o.astype(o_ref.dtype)


def _hgrn_scan(q, zf, i, g, lb, gnorm, s0, nseq, n_valid):
    tt, d = q.shape
    tp = tt // nseq
    nc = tp // SCAN_CHUNK
    dk = d // HG_HEADS
    row = pl.BlockSpec((SCAN_CHUNK, d), lambda b, c: (b * nc + c, 0))
    st = pl.BlockSpec((1, HG_HEADS, dk, dk), lambda b, c: (b, 0, 0, 0))
    body = functools.partial(_hgrn_scan_body, chunk=SCAN_CHUNK, n_valid=min(n_valid, SCAN_CHUNK), dk=dk)
    return pl.pallas_call(
        body,
        grid=(nseq, nc),
        in_specs=[row, row, row, row,
                  pl.BlockSpec((1, d), lambda b, c: (0, 0)),
                  pl.BlockSpec((1, dk), lambda b, c: (0, 0)),
                  st],
        out_specs=[row, st],
        out_shape=[jax.ShapeDtypeStruct((tt, d), BF16), jax.ShapeDtypeStruct(s0.shape, F32)],
        compiler_params=_params(("arbitrary", "arbitrary")),
        name="hgrn_scan",
    )(q, zf, i, g, lb.reshape(1, d), gnorm.reshape(1, dk), s0)


def _route_rows(lt):
    rows = [lt[e:e + 1, :] for e in range(N_EXPERTS)]
    m = functools.reduce(jnp.maximum, rows)
    ex = [jnp.exp(r - m) for r in rows]
    inv = 1.0 / functools.reduce(lambda x, y: x + y, ex)
    pr = [e * inv for e in ex]
    scores = []
    for gi in range(N_GROUPS):
        p4 = pr[gi * EXPERTS_PER_GROUP:(gi + 1) * EXPERTS_PER_GROUP]
        pairs = [p4[x] + p4[y] for x in range(EXPERTS_PER_GROUP) for y in range(x + 1, EXPERTS_PER_GROUP)]
        scores.append(functools.reduce(jnp.maximum, pairs))
    best = scores[0]
    sel = jnp.zeros(best.shape, jnp.int32)
    for gi in range(1, N_GROUPS):
        better = scores[gi] > best
        sel = jnp.where(better, gi, sel)
        best = jnp.where(better, scores[gi], best)
    cand = []
    for j in range(EXPERTS_PER_GROUP):
        cj = pr[(N_GROUPS - 1) * EXPERTS_PER_GROUP + j]
        for gi in range(N_GROUPS - 2, -1, -1):
            cj = jnp.where(sel == gi, pr[gi * EXPERTS_PER_GROUP + j], cj)
        cand.append(cj)

    def argmax4(vals):
        p, idx = vals[0], jnp.zeros(vals[0].shape, jnp.int32)
        for j in range(1, len(vals)):
            better = vals[j] > p
            idx = jnp.where(better, j, idx)
            p = jnp.where(better, vals[j], p)
        return p, idx

    p1, i1 = argmax4(cand)
    p2, i2 = argmax4([jnp.where(i1 == j, -1.0, cand[j]) for j in range(EXPERTS_PER_GROUP)])
    den = p1 + p2
    base = sel * EXPERTS_PER_GROUP
    return base + i1, base + i2, p1 / den, p2 / den


def _mix_router_body(a_ref, w_ref, h_ref, g1_ref, gain_ref, sh_ref, sc_ref, wrh_ref, wrl_ref, br_ref, upper_ref,
                     h_out, hn_out, gate_out, route_out, count_out):
    @pl.when(pl.program_id(0) == 0)
    def _():
        count_out[...] = jnp.zeros_like(count_out)

    mix = jnp.dot(a_ref[...].astype(BF16), w_ref[...], preferred_element_type=F32)
    h = h_ref[...] + g1_ref[...] * mix
    h_out[...] = h
    hn = _norm_mod(h, gain_ref[...], sh_ref[...], sc_ref[...])
    _store_row_tiles(hn_out, hn)
    x_hi = hn.astype(BF16)
    x_lo = (hn - x_hi.astype(F32)).astype(BF16)
    lt = (_nt_dot(wrh_ref[...], x_hi) + _nt_dot(wrl_ref[...], x_hi) + _nt_dot(wrh_ref[...], x_lo)
          + br_ref[...])
    e1, e2, g1, g2 = _route_rows(lt)
    tm = lt.shape[1]
    sub = lax.broadcasted_iota(jnp.int32, (LANES, tm), 0)
    gate_out[...] = (jnp.where(sub == 0, g1, 0.0) + jnp.where(sub == 1, g2, 0.0)).T
    eid = lax.broadcasted_iota(jnp.int32, (N_EXPERTS, tm), 0)
    pick1 = eid == e1
    pick2 = eid == e2
    hits = jnp.where(pick1, 1.0, 0.0) + jnp.where(pick2, 1.0, 0.0)
    before = jnp.dot(hits.astype(BF16), upper_ref[...], preferred_element_type=F32) + count_out[...]
    rank1 = jnp.sum(jnp.where(pick1, before, 0.0), axis=0, keepdims=True)
    rank2 = jnp.sum(jnp.where(pick2, before, 0.0), axis=0, keepdims=True)
    count_out[...] += jnp.sum(hits, axis=1, keepdims=True)
    field = lax.broadcasted_iota(jnp.int32, route_out.shape[1:], 0)
    route_out[0] = jnp.where(field == 0, e1, jnp.where(field == 1, e2, jnp.where(
        field == 2, rank1.astype(jnp.int32), rank2.astype(jnp.int32))))


def _mix_router(a, w_out, h, gain, mod, wr_hi, wr_lo, b_router, tm):
    t, d = h.shape
    row = pl.BlockSpec((tm, d), lambda i: (i, 0))
    const = lambda shape: pl.BlockSpec(shape, lambda i: (0,) * len(shape))
    upper = (jnp.arange(tm)[:, None] < jnp.arange(tm)[None, :]).astype(BF16)
    return pl.pallas_call(
        _mix_router_body,
        grid=(t // tm,),
        in_specs=[row, const((d, d)), row, mod.spec(2), const((1, d)), mod.spec(3), mod.spec(4),
                  const((N_EXPERTS, d)), const((N_EXPERTS, d)), const((N_EXPERTS, 1)), const((tm, tm))],
        out_specs=[row, pl.BlockSpec((tm * SUBLANES, LANES), lambda i: (i, 0)),
                   pl.BlockSpec((tm, LANES), lambda i: (i, 0)),
                   pl.BlockSpec((1, 4, tm), lambda i: (i, 0, 0)), const((N_EXPERTS, 1))],
        out_shape=[jax.ShapeDtypeStruct((t, d), F32), jax.ShapeDtypeStruct((t * SUBLANES, LANES), F32),
                   jax.ShapeDtypeStruct((t, LANES), F32), jax.ShapeDtypeStruct((t // tm, 4, tm), jnp.int32),
                   jax.ShapeDtypeStruct((N_EXPERTS, 1), F32)],
        compiler_params=_params(("arbitrary",)),
        name="mix_router",
    )(a, w_out, h, mod.arr, gain.reshape(1, d), mod.arr, mod.arr, wr_hi, wr_lo,
      b_router.reshape(N_EXPERTS, 1).astype(F32), upper)


def _store_row_tiles(ref, x):
    n, d = x.shape
    assert d == SUBLANES * LANES
    for c in range(SUBLANES):
        ref[pl.ds(c, n, stride=SUBLANES), :] = x[:, c * LANES:(c + 1) * LANES]


def _load_row_tiles(ref):
    n = ref.shape[0] // SUBLANES
    return jnp.concatenate([ref[pl.ds(c, n, stride=SUBLANES), :] for c in range(SUBLANES)], axis=1)


def _tile_rows(first_row, n_rows):
    return pl.ds(pl.multiple_of(first_row * SUBLANES, SUBLANES), n_rows * SUBLANES)


def _row_copy(src, src_row, dst, dst_row, sem):
    return pltpu.make_async_copy(src.at[_tile_rows(src_row, 1)], dst.at[_tile_rows(dst_row, 1)], sem)


def _index_slot(tm):
    return -(-2 * tm // SMEM_I32_CHUNK) * SMEM_I32_CHUNK


def _load_positions(pos_hbm, idx_ref, sem, step):
    n = idx_ref.shape[0]
    cp = pltpu.make_async_copy(pos_hbm.at[pl.ds(pl.multiple_of(step * n, n), n)], idx_ref, sem)
    cp.start()
    cp.wait()


def _dispatch_body(seg_end_ref, pos_hbm, x_ref, xs_out, idx_ref, zero_ref, idx_sem, row_sem, *, tm):
    @pl.when(pl.program_id(0) == 0)
    def _():
        zero_ref[...] = jnp.zeros_like(zero_ref)

        def last_tile(e):
            return pltpu.make_async_copy(
                zero_ref, xs_out.at[_tile_rows(seg_end_ref[e] - EXPERT_TILE, EXPERT_TILE)], row_sem)

        def nonempty(e):
            return seg_end_ref[e] > (seg_end_ref[e - 1] if e else 0)

        for e in range(N_EXPERTS):
            pl.when(nonempty(e))(lambda e=e: last_tile(e).start())
        for e in range(N_EXPERTS):
            pl.when(nonempty(e))(lambda e=e: last_tile(e).wait())

        def unused_tile(i, carry):
            cp = pltpu.make_async_copy(zero_ref, xs_out.at[_tile_rows(i * EXPERT_TILE, EXPERT_TILE)], row_sem)
            cp.start()
            cp.wait()
            return carry

        n_tiles = xs_out.shape[0] // (EXPERT_TILE * SUBLANES)
        lax.fori_loop(seg_end_ref[N_EXPERTS - 1] // EXPERT_TILE, n_tiles, unused_tile, 0)

    _load_positions(pos_hbm, idx_ref, idx_sem, pl.program_id(0))

    def start(t, carry):
        _row_copy(x_ref, t, xs_out, idx_ref[t], row_sem).start()
        _row_copy(x_ref, t, xs_out, idx_ref[tm + t], row_sem).start()
        return carry

    lax.fori_loop(0, tm, start, 0, unroll=ROW_COPY_UNROLL)
    for _ in range(2):
        pltpu.make_async_copy(x_ref, xs_out.at[_tile_rows(0, tm)], row_sem).wait()


def _dispatch(hn, pos_tiles, seg_end, n_rows, tm):
    t = hn.shape[0] // SUBLANES
    grid_spec = pltpu.PrefetchScalarGridSpec(
        num_scalar_prefetch=1,
        grid=(t // tm,),
        in_specs=[pl.BlockSpec(memory_space=pl.ANY), pl.BlockSpec((tm * SUBLANES, LANES), lambda i, se: (i, 0))],
        out_specs=pl.BlockSpec(memory_space=pl.ANY),
        scratch_shapes=[pltpu.SMEM((_index_slot(tm),), jnp.int32),
                        pltpu.VMEM((EXPERT_TILE * SUBLANES, LANES), F32),
                        pltpu.SemaphoreType.DMA, pltpu.SemaphoreType.DMA])
    return pl.pallas_call(
        functools.partial(_dispatch_body, tm=tm),
        grid_spec=grid_spec,
        out_shape=jax.ShapeDtypeStruct((n_rows * SUBLANES, LANES), F32),
        compiler_params=_params(("arbitrary",)),
        name="moe_dispatch",
    )(seg_end, pos_tiles, hn)


def _expert_body(te_ref, nu_ref, x_ref, win_ref, wout_ref, y_ref):
    @pl.when(pl.program_id(0) < nu_ref[0])
    def _():
        hid = jnp.dot(_load_row_tiles(x_ref).astype(BF16), win_ref[0], preferred_element_type=F32)
        de = hid.shape[1] // 2
        a, u = hid[:, :de], hid[:, de:]
        act = (a * _sigmoid(a) * u).astype(BF16)
        _store_row_tiles(y_ref, jnp.dot(act, wout_ref[0], preferred_element_type=F32))

    @pl.when(pl.program_id(0) >= nu_ref[0])
    def _():
        y_ref[...] = jnp.zeros_like(y_ref)


def _experts(xs, tile_expert, n_used, w_in, w_out):
    n_e, d, d2 = w_in.shape
    n_tiles = xs.shape[0] // (EXPERT_TILE * SUBLANES)
    tile = lambda i, te, nu: jnp.minimum(i, nu[0] - 1)
    rows = (EXPERT_TILE * SUBLANES, LANES)
    grid_spec = pltpu.PrefetchScalarGridSpec(
        num_scalar_prefetch=2,
        grid=(n_tiles,),
        in_specs=[pl.BlockSpec(rows, lambda i, te, nu: (tile(i, te, nu), 0)),
                  pl.BlockSpec((1, d, d2), lambda i, te, nu: (te[tile(i, te, nu)], 0, 0)),
                  pl.BlockSpec((1, d2 // 2, d), lambda i, te, nu: (te[tile(i, te, nu)], 0, 0))],
        out_specs=pl.BlockSpec(rows, lambda i, te, nu: (i, 0)))
    return pl.pallas_call(
        _expert_body,
        grid_spec=grid_spec,
        out_shape=jax.ShapeDtypeStruct(xs.shape, F32),
        compiler_params=_params(("arbitrary",)),
        name="moe_experts",
    )(tile_expert, n_used, xs, w_in, w_out)


def _combine_body(pos_hbm, ys_hbm, gate_ref, h_ref, g2_ref, o_ref, idx_ref, y1_ref, y2_ref, idx_sem, row_sem,
                  *, tm):
    _load_positions(pos_hbm, idx_ref, idx_sem, pl.program_id(0))

    def start(t, carry):
        _row_copy(ys_hbm, idx_ref[t], y1_ref, t, row_sem).start()
        _row_copy(ys_hbm, idx_ref[tm + t], y2_ref, t, row_sem).start()
        return carry

    lax.fori_loop(0, tm, start, 0, unroll=ROW_COPY_UNROLL)
    for y_ref in (y1_ref, y2_ref):
        pltpu.make_async_copy(ys_hbm.at[_tile_rows(0, tm)], y_ref, row_sem).wait()
    gates = gate_ref[...]
    moe = gates[:, 0:1] * _load_row_tiles(y1_ref) + gates[:, 1:2] * _load_row_tiles(y2_ref)
    o_ref[...] = h_ref[...] + g2_ref[...] * moe


def _combine(ys, pos_tiles, gates, h, mod, tm):
    t, d = h.shape
    row = pl.BlockSpec((tm, d), lambda i: (i, 0))
    return pl.pallas_call(
        functools.partial(_combine_body, tm=tm),
        grid=(t // tm,),
        in_specs=[pl.BlockSpec(memory_space=pl.ANY), pl.BlockSpec(memory_space=pl.ANY),
                  pl.BlockSpec((tm, LANES), lambda i: (i, 0)), row, mod.spec(5)],
        out_specs=row,
        out_shape=jax.ShapeDtypeStruct((t, d), F32),
        scratch_shapes=[pltpu.SMEM((_index_slot(tm),), jnp.int32),
                        pltpu.VMEM((tm * SUBLANES, LANES), F32), pltpu.VMEM((tm * SUBLANES, LANES), F32),
                        pltpu.SemaphoreType.DMA, pltpu.SemaphoreType.DMA],
        compiler_params=_params(("arbitrary",)),
        name="moe_combine",
    )(pos_tiles, ys, gates, h, mod.arr)


def _moe(hn, gates, route, counts, h, mod, w_in, w_out, tm):
    t, d = h.shape
    n_tiles = 2 * t // EXPERT_TILE + N_EXPERTS
    n_rows = n_tiles * EXPERT_TILE
    cnt = counts.reshape(N_EXPERTS).astype(jnp.int32)
    padded = (cnt + (EXPERT_TILE - 1)) // EXPERT_TILE * EXPERT_TILE
    experts = jnp.arange(N_EXPERTS, dtype=jnp.int32)
    seg_end = jnp.sum(jnp.where(experts[None, :] <= experts[:, None], padded[None, :], 0), axis=1)
    seg_start = seg_end - padded
    first_row = jnp.sum(jnp.where(route[:, :2, :, None] == experts, seg_start, 0), axis=-1)
    pos_tiles = (first_row + route[:, 2:]).reshape(t // tm, 2 * tm)
    if _index_slot(tm) != 2 * tm:
        pos_tiles = jnp.pad(pos_tiles, ((0, 0), (0, _index_slot(tm) - 2 * tm)))
    pos_tiles = pos_tiles.reshape(-1)
    tile_row = jnp.arange(n_tiles, dtype=jnp.int32) * EXPERT_TILE
    tile_expert = jnp.minimum(jnp.sum((seg_end[None, :] <= tile_row[:, None]).astype(jnp.int32), axis=1),
                              N_EXPERTS - 1)
    n_used = seg_end[-1:] // EXPERT_TILE
    xs = _dispatch(hn, pos_tiles, seg_end, n_rows, tm)
    ys = _experts(xs, tile_expert, n_used, w_in, w_out)
    return _combine(ys, pos_tiles, gates, h, mod, tm)


def _head_norm(y, gsum_ref, gain_ref, head_dim):
    sq = y * y
    hi = sq.astype(BF16)
    lo = (sq - hi.astype(F32)).astype(BF16)
    ssum = (jnp.dot(hi, gsum_ref[...], preferred_element_type=F32)
            + jnp.dot(lo, gsum_ref[...], preferred_element_type=F32))
    return y * lax.rsqrt(ssum * (1.0 / head_dim) + EPS) * gain_ref[...]


def _head_sum_matrix(d, head_dim):
    r = jnp.arange(d) // head_dim
    return (r[:, None] == r[None, :]).astype(BF16)


def _q_proj_body(x_ref, gain_ref, sh_ref, sc_ref, w_ref, gsum_ref, qn_ref, q_ref, *, head_dim, q_scale):
    xn = _norm_mod(x_ref[...], gain_ref[...], sh_ref[...], sc_ref[...]).astype(BF16)
    y = jnp.dot(xn, w_ref[...], preferred_element_type=F32)
    q = _head_norm(y, gsum_ref, qn_ref, head_dim) * q_scale
    q_ref[...] = q.astype(q_ref.dtype)


def _q_proj(h, gain, mod, w, gsum, q_norm, tm, out_dtype, q_scale):
    t, d = h.shape
    head_dim = d // FOX_HEADS
    row = pl.BlockSpec((tm, d), lambda i: (i, 0))
    const = lambda shape: pl.BlockSpec(shape, lambda i: (0,) * len(shape))
    return pl.pallas_call(
        functools.partial(_q_proj_body, head_dim=head_dim, q_scale=q_scale),
        grid=(t // tm,),
        in_specs=[row, const((1, d)), mod.spec(0), mod.spec(1), const((d, d)), const((d, d)), const((1, d))],
        out_specs=row,
        out_shape=jax.ShapeDtypeStruct((t, d), out_dtype),
        compiler_params=_params(("arbitrary",)),
        name="q_proj",
    )(h, gain.reshape(1, d), mod.arr, mod.arr, w, gsum, jnp.tile(q_norm, FOX_HEADS).reshape(1, d))


def _kv_proj_body(x_ref, gain_ref, sh_ref, sc_ref, wk_ref, wv_ref, wf_ref, gsum_ref, kn_ref, bf_ref,
                  k_ref, v_ref, kb_ref, vb_ref, lf_ref, *, head_dim):
    xn = _norm_mod(x_ref[...], gain_ref[...], sh_ref[...], sc_ref[...]).astype(BF16)
    k = _head_norm(jnp.dot(xn, wk_ref[...], preferred_element_type=F32), gsum_ref, kn_ref, head_dim)
    k_ref[...] = k
    kb_ref[...] = k.astype(BF16)
    v = jnp.dot(xn, wv_ref[...], preferred_element_type=F32)
    v_ref[...] = v
    vb_ref[...] = v.astype(BF16)
    zf = jnp.dot(xn, wf_ref[...], preferred_element_type=F32) + bf_ref[...]
    lf_ref[...] = _log_sigmoid(zf)[:, :lf_ref.shape[1]]


def _kv_proj(h, gain, mod, wk, wv, wf, gsum, k_norm, b_fgate, tm):
    t, d = h.shape
    head_dim = d // FOX_HEADS
    row = pl.BlockSpec((tm, d), lambda i: (i, 0))
    const = lambda shape: pl.BlockSpec(shape, lambda i: (0,) * len(shape))
    bf = jnp.pad(b_fgate.astype(F32), (0, LANES - FOX_HEADS)).reshape(1, LANES)
    return pl.pallas_call(
        functools.partial(_kv_proj_body, head_dim=head_dim),
        grid=(t // tm,),
        in_specs=[row, const((1, d)), mod.spec(0), mod.spec(1), const((d, d)), const((d, d)),
                  const((d, LANES)), const((d, d)), const((1, d)), const((1, LANES))],
        out_specs=[row, row, row, row, pl.BlockSpec((tm, FOX_HEADS), lambda i: (i, 0))],
        out_shape=[jax.ShapeDtypeStruct((t, d), F32), jax.ShapeDtypeStruct((t, d), F32),
                   jax.ShapeDtypeStruct((t, d), BF16), jax.ShapeDtypeStruct((t, d), BF16),
                   jax.ShapeDtypeStruct((t, FOX_HEADS), F32)],
        compiler_params=_params(("arbitrary",)),
        name="kv_proj",
    )(h, gain.reshape(1, d), mod.arr, mod.arr, wk, wv, wf, gsum,
      jnp.tile(k_norm, FOX_HEADS).reshape(1, d), bf)


def _cumsum_lanes_body(x_ref, o_ref, carry_ref, *, carry_rows):
    j = pl.program_id(1)

    @pl.when(j == 0)
    def _():
        carry_ref[...] = jnp.zeros_like(carry_ref)

    w = x_ref.shape[-1]
    r = lax.broadcasted_iota(jnp.int32, (w, w), 0)
    c = lax.broadcasted_iota(jnp.int32, (w, w), 1)
    upper = jnp.where(r <= c, 1.0, 0.0).astype(F32)
    y = jnp.dot(x_ref[0], upper, precision=HIGHEST, preferred_element_type=F32)
    if carry_rows:
        y = y + carry_ref[...]
        carry_ref[...] = y[:, w - 1:w]
    o_ref[0] = y


def _cumsum_lanes(x, width, carry):
    b, r, t = x.shape
    spec = pl.BlockSpec((1, r, width), lambda i, j: (i, 0, j))
    return pl.pallas_call(
        functools.partial(_cumsum_lanes_body, carry_rows=carry),
        grid=(b, t // width),
        in_specs=[spec],
        out_specs=spec,
        out_shape=jax.ShapeDtypeStruct(x.shape, F32),
        scratch_shapes=[pltpu.VMEM((r, 1), F32)],
        compiler_params=_params(("arbitrary", "arbitrary")),
        name="cumsum_lanes",
    )(x)


def _attn_prompt_body(qt_ref, kt_ref, q_ref, k_ref, v_ref, f_ref, o_ref, m_ref, acc_ref,
                      *, blk, rows, head_dim):
    p = pl.program_id(2)
    qi = qt_ref[p]
    ki = kt_ref[p]

    @pl.when(ki == 0)
    def _():
        m_ref[...] = jnp.full(m_ref.shape, NEG_BIG, F32)
        acc_ref[...] = jnp.zeros_like(acc_ref)

    def process(masked):
        k = k_ref[...]
        v = v_ref[...]
        f = f_ref[...]
        first_v = lax.broadcasted_iota(jnp.int32, v.shape, 1) < head_dim
        ones = jnp.ones_like(v)
        v_heads = (jnp.where(first_v, v, ones), jnp.where(first_v, ones, v))
        m_olds = {(hh, r0): m_ref[hh, r0:r0 + rows, :] for hh in range(2) for r0 in range(0, blk, rows)}
        acc_olds = {(hh, r0): acc_ref[hh, r0:r0 + rows, :] for hh in range(2) for r0 in range(0, blk, rows)}
        m_news, acc_news = {}, {}
        for r0 in range(0, blk, rows):
            q = q_ref[r0:r0 + rows, :]
            first_q = lax.broadcasted_iota(jnp.int32, q.shape, 1) < head_dim
            zero = jnp.zeros_like(q)
            for hh, qh in enumerate((jnp.where(first_q, q, zero), jnp.where(first_q, zero, q))):
                s = _nt_dot(qh, k) - f[hh:hh + 1, :]
                if masked:
                    row = lax.broadcasted_iota(jnp.int32, s.shape, 0) + r0
                    col = lax.broadcasted_iota(jnp.int32, s.shape, 1)
                    s = jnp.where(col <= row, s, NEG_BIG)
                m_old = m_olds[hh, r0]
                m_new = jnp.maximum(m_old, jnp.max(s, axis=1, keepdims=True))
                alpha = jnp.exp2(m_old - m_new)
                pr = jnp.concatenate([jnp.exp2(s[:, c * LANES:(c + 1) * LANES] - m_new)
                                      for c in range(blk // LANES)], axis=1).astype(BF16)
                acc_news[hh, r0] = alpha * acc_olds[hh, r0] + jnp.dot(pr, v_heads[hh], preferred_element_type=F32)
                m_news[hh, r0] = m_new
        for (hh, r0), m_new in m_news.items():
            m_ref[hh, r0:r0 + rows, :] = m_new
            acc_ref[hh, r0:r0 + rows, :] = acc_news[hh, r0]

    @pl.when(ki < qi)
    def _():
        process(False)

    @pl.when(ki == qi)
    def _():
        process(True)
        a0 = acc_ref[0]
        a1 = acc_ref[1]
        first = lax.broadcasted_iota(jnp.int32, a0.shape, 1) < head_dim
        inv0 = 1.0 / a0[:, head_dim:head_dim + 1]
        inv1 = 1.0 / a1[:, 0:1]
        o_ref[...] = jnp.where(first, a0 * inv0, a1 * inv1).astype(o_ref.dtype)


def _attn_prompt(q, k, v, f_pairs, nseq):
    tt, d = q.shape
    t = tt // nseq
    blk = min(ATTN_BLOCK, t)
    rows = min(ATTN_ROWS, blk)
    nb = t // blk
    head_dim = d // FOX_HEADS
    pairs = [(qi, ki) for qi in range(nb) for ki in range(qi + 1)]
    q_tab = jnp.asarray([p[0] for p in pairs], jnp.int32)
    k_tab = jnp.asarray([p[1] for p in pairs], jnp.int32)
    grid_spec = pltpu.PrefetchScalarGridSpec(
        num_scalar_prefetch=2,
        grid=(nseq, d // LANES, len(pairs)),
        in_specs=[pl.BlockSpec((blk, LANES), lambda b, hp, p, qt, kt: (b * nb + qt[p], hp)),
                  pl.BlockSpec((blk, LANES), lambda b, hp, p, qt, kt: (b * nb + kt[p], hp)),
                  pl.BlockSpec((blk, LANES), lambda b, hp, p, qt, kt: (b * nb + kt[p], hp)),
                  pl.BlockSpec((None, None, 2, blk), lambda b, hp, p, qt, kt: (b, hp, 0, kt[p]))],
        out_specs=pl.BlockSpec((blk, LANES), lambda b, hp, p, qt, kt: (b * nb + qt[p], hp)),
        scratch_shapes=[pltpu.VMEM((2, blk, LANES), F32), pltpu.VMEM((2, blk, LANES), F32)])
    return pl.pallas_call(
        functools.partial(_attn_prompt_body, blk=blk, rows=rows, head_dim=head_dim),
        grid_spec=grid_spec,
        out_shape=jax.ShapeDtypeStruct((tt, d), BF16),
        compiler_params=_params(("arbitrary", "arbitrary", "arbitrary")),
        name="attn_prompt",
    )(q_tab, k_tab, q, k, v, f_pairs)


def _gather_cum_body(pt_ref, cp_ref, new_ref, o_ref, *, n_pages, page):
    b = pl.program_id(0)

    def step(j, carry):
        idx = pt_ref[b * n_pages + j]
        blk = cp_ref[idx] + carry
        o_ref[0, :, pl.ds(pl.multiple_of(j * page, page), page)] = blk
        return blk[:, page - 1:page]

    carry = lax.fori_loop(0, n_pages, step, jnp.zeros((cp_ref.shape[1], 1), F32))
    o_ref[0, :, n_pages * page:] = new_ref[0] + carry


def _gather_cum(page_table, cum_pages, cum_new):
    nseq, n_pages = page_table.shape
    n_pool, hh, page = cum_pages.shape
    grid_spec = pltpu.PrefetchScalarGridSpec(
        num_scalar_prefetch=1,
        grid=(nseq,),
        in_specs=[pl.BlockSpec((n_pool, hh, page), lambda b, pt: (0, 0, 0)),
                  pl.BlockSpec((1, hh, page), lambda b, pt: (b, 0, 0))],
        out_specs=pl.BlockSpec((1, hh, (n_pages + 1) * page), lambda b, pt: (b, 0, 0)))
    return pl.pallas_call(
        functools.partial(_gather_cum_body, n_pages=n_pages, page=page),
        grid_spec=grid_spec,
        out_shape=jax.ShapeDtypeStruct((nseq, hh, (n_pages + 1) * page), F32),
        compiler_params=_params(("arbitrary",)),
        name="gather_cum",
    )(page_table.reshape(-1), cum_pages, cum_new)


def _attn_paged_body(pt_ref, *refs, n_q, page, pps, n_steps, head_dim):
    q_ref = refs[0]
    k_refs = refs[1:1 + pps]
    v_refs = refs[1 + pps:1 + 2 * pps]
    kn_ref, vn_ref, f_ref, o_ref, qbd_ref, m_ref, l_ref, acc_ref = refs[1 + 2 * pps:]
    j = pl.program_id(1)
    d = q_ref.shape[1]
    rows = FOX_HEADS * n_q

    @pl.when(j == 0)
    def _():
        q = q_ref[...]
        lane_head = lax.broadcasted_iota(jnp.int32, q.shape, 1) // head_dim
        for h in range(FOX_HEADS):
            qbd_ref[h * n_q:(h + 1) * n_q, :] = jnp.where(lane_head == h, q, 0.0).astype(BF16)
        m_ref[...] = jnp.full(m_ref.shape, NEG_BIG, F32)
        l_ref[...] = jnp.zeros_like(l_ref)
        acc_ref[...] = jnp.zeros_like(acc_ref)

    def attend(k, v, f, mask):
        bias = jnp.concatenate([jnp.broadcast_to(f[h:h + 1, :], (n_q, page)) for h in range(FOX_HEADS)], axis=0)
        s = _nt_dot(qbd_ref[...], k) - bias
        if mask is not None:
            s = jnp.where(mask, s, NEG_BIG)
        m_old = m_ref[...]
        m_new = jnp.maximum(m_old, jnp.max(s, axis=1, keepdims=True))
        alpha = jnp.exp(m_old - m_new)
        pr = jnp.exp(s - m_new)
        l_ref[...] = alpha * l_ref[...] + jnp.sum(pr, axis=1, keepdims=True)
        m_ref[...] = m_new
        acc_ref[...] = alpha * acc_ref[...] + jnp.dot(pr.astype(BF16), v, preferred_element_type=F32)

    @pl.when(j < n_steps - 1)
    def _():
        for u in range(pps):
            off = pl.multiple_of((j * pps + u) * page, page)
            attend(k_refs[u][...].astype(BF16), v_refs[u][...].astype(BF16), f_ref[0, :, pl.ds(off, page)], None)

    @pl.when(j == n_steps - 1)
    def _():
        pad = jnp.zeros((page - n_q, d), F32)
        k = jnp.concatenate([kn_ref[...], pad], axis=0).astype(BF16)
        v = jnp.concatenate([vn_ref[...], pad], axis=0).astype(BF16)
        t_of_row = lax.broadcasted_iota(jnp.int32, (rows, page), 0) % n_q
        key = lax.broadcasted_iota(jnp.int32, (rows, page), 1)
        attend(k, v, f_ref[0, :, (n_steps - 1) * pps * page:], key <= t_of_row)
        acc = acc_ref[...] * (1.0 / l_ref[...])
        lane_head = lax.broadcasted_iota(jnp.int32, (n_q, d), 1) // head_dim
        out = jnp.zeros((n_q, d), F32)
        for h in range(FOX_HEADS):
            out = out + jnp.where(lane_head == h, acc[h * n_q:(h + 1) * n_q, :], 0.0)
        o_ref[...] = out


def _attn_paged(q, cache_k, cache_v, k_new, v_new, f_all, page_table):
    tt, d = q.shape
    nseq, n_pages = page_table.shape
    n_q = tt // nseq
    page = cache_k.shape[1]
    pps = PAGES_PER_STEP
    n_steps = n_pages // pps + 1
    head_dim = d // FOX_HEADS
    rows = FOX_HEADS * n_q

    def page_spec(u):
        def index(b, j, pt):
            return (pt[b * n_pages + jnp.minimum(j, n_steps - 2) * pps + u], 0, 0)
        return pl.BlockSpec((None, page, d), index)

    tok = pl.BlockSpec((n_q, d), lambda b, j, pt: (b, 0))
    grid_spec = pltpu.PrefetchScalarGridSpec(
        num_scalar_prefetch=1,
        grid=(nseq, n_steps),
        in_specs=([tok] + [page_spec(u) for u in range(pps)] + [page_spec(u) for u in range(pps)]
                  + [tok, tok, pl.BlockSpec((1, FOX_HEADS, f_all.shape[2]), lambda b, j, pt: (b, 0, 0))]),
        out_specs=tok,
        scratch_shapes=[pltpu.VMEM((rows, d), BF16), pltpu.VMEM((rows, 1), F32),
                        pltpu.VMEM((rows, 1), F32), pltpu.VMEM((rows, d), F32)])
    body = functools.partial(_attn_paged_body, n_q=n_q, page=page, pps=pps, n_steps=n_steps,
                             head_dim=head_dim)
    return pl.pallas_call(
        body,
        grid_spec=grid_spec,
        out_shape=jax.ShapeDtypeStruct((tt, d), F32),
        compiler_params=_params(("arbitrary", "arbitrary")),
        name="attn_paged",
    )(page_table.reshape(-1), q, *([cache_k] * pps), *([cache_v] * pps), k_new, v_new, f_all)


def _lower_bounds(lb_logits):
    p = jax.nn.softmax(lb_logits.astype(F32), axis=0)
    return jnp.cumsum(p, axis=0) - p[0]


def _trunk(x, nseq, mods, mod_kv, s0, past, w):
    tt, d = x.shape
    t = tt // nseq
    tm = min(512, tt)
    tiles_per_seq = max(t // tm, 1)
    n_a = w["w_in_a"].shape[0]
    depth = w["w_ada"].shape[0]
    head_dim = d // FOX_HEADS
    lbs = _lower_bounds(w["lb_logits"])
    mk = lambda arr: _Mod(arr, d, tm, tiles_per_seq)

    h = x
    states = []
    for l in range(depth):
        mod = mk(mods[l])
        if l < n_a:
            q, zf, i, g = _hgrn_in_proj(h, w["norm_mix"][l], mod, w["w_in_a4"][l], tm)
            tp = -(-t // SCAN_CHUNK) * SCAN_CHUNK
            if tp != t:
                padr = lambda a: jnp.pad(a.reshape(nseq, t, d), ((0, 0), (0, tp - t), (0, 0))).reshape(nseq * tp, d)
                q, zf, i, g = padr(q), padr(zf), padr(i), padr(g)
            o, s_t = _hgrn_scan(q, zf, i, g, lbs[l], w["gnorm_a"][l], s0[l], nseq, t)
            if tp != t:
                o = o.reshape(nseq, tp, d)[:, :t].reshape(tt, d)
            states.append(s_t)
            mix_in, w_out = o, w["w_out_a"][l]
        else:
            j = l - n_a
            if past is None:
                q = _q_proj(h, w["norm_mix"][l], mod, w["w_q_b"][j], w["gsum"], w["q_norm"][j], tm, BF16,
                            head_dim ** -0.5 * LOG2E)
                mix_in = _attn_prompt(q, k_bf, v_bf, f_keys, nseq)
            else:
                q = _q_proj(h, w["norm_mix"][l], mod, w["w_q_b"][j], w["gsum"], w["q_norm"][j], tm, F32,
                            head_dim ** -0.5)
                mix_in = _attn_paged(q, past[0], past[1], k_new, v_new, f_keys, past[3])
            w_out = w["w_out_b"][j]
        h, hn, gates, route, counts = _mix_router(mix_in, w_out, h, w["norm_ffn"][l], mod, w["wr_hi"],
                                                  w["wr_lo"], w["b_router"], tm)
        h = _moe(hn, gates, route, counts, h, mod, w["w_exp_in"][l], w["w_exp_out"][l], tm)
        if l == n_a - 1:
            k_new, v_new, k_bf, v_bf, logf_new = _kv_proj(
                h, w["norm_kv"], mk(mod_kv), w["w_k"], w["w_v"], w["w_f"], w["gsum"], w["k_norm"],
                w["b_fgate"], tm)
            lf_t = logf_new.reshape(nseq, t, FOX_HEADS).transpose(0, 2, 1)
            if past is None:
                f_cum = _cumsum_lanes(lf_t, min(512, t), carry=True)
                f_keys = (f_cum * LOG2E).reshape(nseq, FOX_HEADS // 2, 2, t)
            else:
                cache_logf, page_table = past[2], past[3]
                page = cache_logf.shape[1]
                pool_t = cache_logf.astype(F32).transpose(0, 2, 1)
                n_pool = pool_t.shape[0]
                rows = 64 if n_pool % 64 == 0 else 1
                cum_pages = _cumsum_lanes(pool_t.reshape(n_pool // rows, rows * FOX_HEADS, page), page,
                                          carry=False).reshape(n_pool, FOX_HEADS, page)
                new_pad = jnp.pad(lf_t, ((0, 0), (0, 0), (0, page - t)))
                cum_new = _cumsum_lanes(new_pad, page, carry=False)
                f_keys = _gather_cum(page_table, cum_pages, cum_new)
    return h, jnp.stack(states), k_new, v_new, logf_new


def kernel(x_prompt, x_sample, cache_k, cache_v, cache_logf, state_hgrn, page_table, c_prompt, c_sample,
           w_ada, b_ada, norm_mix, norm_ffn, w_in_a, lb_logits, gnorm_a, w_out_a, norm_kv, w_ada_kv,
           b_ada_kv, w_kv, b_fgate, k_norm, w_q_b, q_norm, w_out_b, w_router, b_router, w_exp_in, w_exp_out):
    nb, seq, d = x_prompt.shape
    ns, dseq, _ = x_sample.shape
    n_a = w_in_a.shape[0]
    depth = w_ada.shape[0]
    hd = FOX_HEADS * (d // FOX_HEADS)
    dk = d // HG_HEADS

    nrow = nb + ns
    nrow_pad = -(-nrow // SUBLANES) * SUBLANES
    c_all = jnp.pad(jnp.concatenate([c_prompt, c_sample], axis=0).astype(F32), ((0, nrow_pad - nrow), (0, 0)))
    mod_all = _ada(c_all, w_ada, b_ada)
    mod_kv_all = _ada(c_all, w_ada_kv[None], b_ada_kv[None])[0]
    mods_p = [mod_all[l, :nb].reshape(nb, 1, 6 * d) for l in range(depth)]
    mods_s = [jnp.repeat(mod_all[l, nb:nrow], dseq, axis=0) for l in range(depth)]
    mod_kv_p = mod_kv_all[:nb].reshape(nb, 1, 2 * d)
    mod_kv_s = jnp.repeat(mod_kv_all[nb:nrow], dseq, axis=0)

    wr_t = w_router.astype(F32).T
    wr_hi = wr_t.astype(BF16)
    wr_lo = (wr_t - wr_hi.astype(F32)).astype(BF16)
    w_in_b = w_in_a.astype(BF16)
    w = dict(
        w_ada=w_ada, lb_logits=lb_logits, norm_mix=norm_mix, norm_ffn=norm_ffn, gnorm_a=gnorm_a,
        w_in_a=w_in_a,
        w_in_a4=[[w_in_b[l, :, s * d:(s + 1) * d] for s in range(4)] for l in range(n_a)],
        w_out_a=w_out_a.astype(BF16), norm_kv=norm_kv,
        w_k=w_kv[:, :hd].astype(BF16), w_v=w_kv[:, hd:2 * hd].astype(BF16),
        w_f=jnp.pad(w_kv[:, 2 * hd:], ((0, 0), (0, LANES - FOX_HEADS))).astype(BF16),
        b_fgate=b_fgate, k_norm=k_norm, w_q_b=w_q_b.astype(BF16), q_norm=q_norm,
        w_out_b=w_out_b.astype(BF16), wr_hi=wr_hi, wr_lo=wr_lo, b_router=b_router,
        w_exp_in=w_exp_in.astype(BF16), w_exp_out=w_exp_out.astype(BF16),
        gsum=_head_sum_matrix(d, d // FOX_HEADS),
    )

    s0_p = jnp.zeros((n_a, nb, HG_HEADS, dk, dk), F32)
    y_p, st_p, k_p, v_p, lf_p = _trunk(x_prompt.reshape(nb * seq, d), nb, mods_p, mod_kv_p, s0_p, None, w)

    n_pool, page = cache_k.shape[0], cache_k.shape[1]
    past = (cache_k.reshape(n_pool, page, hd), cache_v.reshape(n_pool, page, hd), cache_logf, page_table)
    y_s, st_s, k_s, v_s, lf_s = _trunk(x_sample.reshape(ns * dseq, d), ns, mods_s, mod_kv_s,
                                       state_hgrn.astype(F32), past, w)

    hs = (FOX_HEADS, d // FOX_HEADS)
    return (y_p.reshape(nb, seq, d), y_s.reshape(ns, dseq, d),
            st_p.astype(state_hgrn.dtype), st_s.astype(state_hgrn.dtype),
            k_p.reshape(nb, seq, *hs).astype(cache_k.dtype), v_p.reshape(nb, seq, *hs).astype(cache_v.dtype),
            lf_p.reshape(nb, seq, FOX_HEADS).astype(cache_logf.dtype),
            k_s.reshape(ns, dseq, *hs).astype(cache_k.dtype), v_s.reshape(ns, dseq, *hs).astype(cache_v.dtype),
            lf_s.reshape(ns, dseq, FOX_HEADS).astype(cache_logf.dtype))
```

```python
import functools
import math

import jax
import jax.numpy as jnp
from jax import lax
from jax.experimental import pallas as pl
from jax.experimental.pallas import tpu as pltpu

F32 = jnp.float32
BF16 = jnp.bfloat16
HIGHEST = lax.Precision.HIGHEST

HG_HEADS = 8
FOX_HEADS = 16
N_EXPERTS = 16
N_GROUPS = 4
EXPERTS_PER_GROUP = N_EXPERTS // N_GROUPS
EPS = 1e-6
NEG_BIG = -1e30
TINY = 1e-30

LANES = 128
SUBLANES = 8
VMEM_LIMIT_BYTES = 56 * 1024 * 1024

SCAN_CHUNK = 128
ATTN_BLOCK = 1024
ATTN_ROWS = 1024
LOG2E = math.log2(math.e)
PAGES_PER_STEP = 8
EXPERT_TILE = 256
SMEM_I32_CHUNK = 1024
ROW_COPY_UNROLL = 8


def _params(sem):
    return pltpu.CompilerParams(dimension_semantics=sem, vmem_limit_bytes=VMEM_LIMIT_BYTES)


def _nt_dot(a, b):
    return lax.dot_general(a, b, (((1,), (1,)), ((), ())), preferred_element_type=F32)


def _sigmoid(x):
    return 1.0 / (1.0 + jnp.exp(-x))


def _log_sigmoid(x):
    return jnp.minimum(x, 0.0) - jnp.log1p(jnp.exp(-jnp.abs(x)))


def _norm_mod(x, gain, shift, scale):
    ms = jnp.mean(x * x, axis=-1, keepdims=True)
    y = x * lax.rsqrt(ms + EPS) * gain
    return y * (1.0 + scale) + shift


class _Mod:
    def __init__(self, arr, d, tm, tiles_per_seq):
        self.arr = arr
        self.d = d
        self.tm = tm
        self.tiles_per_seq = tiles_per_seq
        self.per_token = arr.ndim == 2

    def spec(self, chunk):
        d, tps = self.d, self.tiles_per_seq
        if self.per_token:
            return pl.BlockSpec((self.tm, d), lambda i, *_: (i, chunk))
        return pl.BlockSpec((None, 1, d), lambda i, *_: (i // tps, 0, chunk))


def _ada_body(c_ref, w_ref, b_ref, o_ref):
    c = c_ref[...]
    a = c * _sigmoid(c)
    o_ref[0] = jnp.dot(a, w_ref[0], precision=HIGHEST, preferred_element_type=F32) + b_ref[0]


def _ada(c, w, b):
    n_layers, d, n = w.shape
    r = c.shape[0]
    tn = 1536 if n % 1536 == 0 else 1024
    return pl.pallas_call(
        _ada_body,
        grid=(n_layers, n // tn),
        in_specs=[pl.BlockSpec((r, d), lambda l, j: (0, 0)),
                  pl.BlockSpec((1, d, tn), lambda l, j: (l, 0, j)),
                  pl.BlockSpec((1, 1, tn), lambda l, j: (l, 0, j))],
        out_specs=pl.BlockSpec((1, r, tn), lambda l, j: (l, 0, j)),
        out_shape=jax.ShapeDtypeStruct((n_layers, r, n), F32),
        compiler_params=_params(("arbitrary", "arbitrary")),
        name="ada",
    )(c, w, b.reshape(n_layers, 1, n))


def _hgrn_in_body(x_ref, gain_ref, sh_ref, sc_ref, wq_ref, wf_ref, wi_ref, wg_ref,
                  q_ref, zf_ref, i_ref, g_ref):
    xn = _norm_mod(x_ref[...], gain_ref[...], sh_ref[...], sc_ref[...]).astype(BF16)
    q_ref[...] = jnp.dot(xn, wq_ref[...], preferred_element_type=F32).astype(q_ref.dtype)
    zf_ref[...] = jnp.dot(xn, wf_ref[...], preferred_element_type=F32)
    i_ref[...] = jnp.dot(xn, wi_ref[...], preferred_element_type=F32).astype(i_ref.dtype)
    g_ref[...] = jnp.dot(xn, wg_ref[...], preferred_element_type=F32).astype(g_ref.dtype)


def _hgrn_in_proj(h, gain, mod, w4, tm):
    t, d = h.shape
    row = pl.BlockSpec((tm, d), lambda i: (i, 0))
    full = pl.BlockSpec((d, d), lambda i: (0, 0))
    return pl.pallas_call(
        _hgrn_in_body,
        grid=(t // tm,),
        in_specs=[row, pl.BlockSpec((1, d), lambda i: (0, 0)), mod.spec(0), mod.spec(1),
                  full, full, full, full],
        out_specs=[row, row, row, row],
        out_shape=[jax.ShapeDtypeStruct((t, d), BF16), jax.ShapeDtypeStruct((t, d), F32),
                   jax.ShapeDtypeStruct((t, d), BF16), jax.ShapeDtypeStruct((t, d), BF16)],
        compiler_params=_params(("arbitrary",)),
        name="hgrn_in_proj",
    )(h, gain.reshape(1, d), mod.arr, mod.arr, *w4)


def _split3(x):
    hi = x.astype(BF16)
    r1 = x - hi.astype(F32)
    mid = r1.astype(BF16)
    lo = (r1 - mid.astype(F32)).astype(BF16)
    return hi, mid, lo


def _hgrn_scan_body(q_ref, zf_ref, i_ref, g_ref, lb_ref, gn_ref, s0_ref, o_ref, st_ref,
                    *, chunk, n_valid, dk):
    c = pl.program_id(1)

    @pl.when(c == 0)
    def _():
        st_ref[...] = s0_ref[...]

    row = lax.broadcasted_iota(jnp.int32, (chunk, chunk), 0)
    col = lax.broadcasted_iota(jnp.int32, (chunk, chunk), 1)
    tri = jnp.where(row >= col, 1.0, 0.0).astype(BF16)
    n_levels = chunk.bit_length() - 1
    differ = row ^ col
    on_diag = differ == 0
    at_level = [(lax.shift_right_logical(differ, lv) == 1) & (row > col) for lv in range(n_levels)]
    tok = lax.broadcasted_iota(jnp.int32, (chunk, dk), 0)
    in_right = [(lax.shift_right_logical(tok, lv) & 1) == 1 for lv in range(n_levels)]
    quad = tok & 3
    if n_valid < chunk:
        valid = tok < n_valid
    for h in range(HG_HEADS):
        hs = slice(h * dk, (h + 1) * dk)
        q = q_ref[:, hs].astype(F32)
        zf = zf_ref[:, hs]
        v = i_ref[:, hs]
        lb = lb_ref[:, hs]
        e_abs = jnp.exp(-jnp.abs(zf))
        big = 1.0 / (1.0 + e_abs)
        small = e_abs * big
        nonneg = zf >= 0.0
        log_f = jnp.log(jnp.maximum(lb, TINY) + (1.0 - lb) * jnp.where(nonneg, big, small))
        k = (1.0 - lb) * jnp.where(nonneg, small, big)
        if n_valid < chunk:
            log_f = jnp.where(valid, log_f, 0.0)
            k = jnp.where(valid, k, 0.0)
        g_hi, g_mid, g_lo = _split3(log_f)
        b = (jnp.dot(tri, g_hi, preferred_element_type=F32)
             + jnp.dot(tri, g_mid, preferred_element_type=F32)
             + jnp.dot(tri, g_lo, preferred_element_type=F32))
        b_end = b[chunk - 1:chunk, :]
        a = jnp.where(on_diag, _nt_dot(q.astype(BF16), k.astype(BF16)), 0.0)
        for lv in range(n_levels):
            half = 1 << lv
            if lv == 0:
                dist = jnp.where(in_right[0], log_f, 0.0)
            elif lv == 1:
                edge = jnp.where(quad == 0, pltpu.roll(b, chunk - 1, 0),
                                 jnp.where(quad == 2, pltpu.roll(b, 1, 0),
                                           jnp.where(quad == 3, pltpu.roll(b, 2, 0), b)))
                dist = -jnp.abs(b - edge)
            else:
                nblk = chunk // (2 * half)
                edge = b.reshape(nblk, 2 * half, dk)[:, half - 1:half, :]
                edge = jnp.broadcast_to(edge, (nblk, 2 * half, dk)).reshape(chunk, dk)
                dist = -jnp.abs(b - edge)
            z = (jnp.where(in_right[lv], q, k) * jnp.exp(dist)).astype(BF16)
            a = jnp.where(at_level[lv], _nt_dot(z, z), a)
        s = st_ref[0, h]
        o = (jnp.dot(a.astype(BF16), v, preferred_element_type=F32)
             + jnp.dot((q * jnp.exp(b)).astype(BF16), s.astype(BF16), preferred_element_type=F32))
        k_d = (k * jnp.exp(b_end - b)).T.astype(BF16)
        decay = jnp.broadcast_to(jnp.exp(b_end), (dk, dk)).T
        st_ref[0, h] = decay * s + jnp.dot(k_d, v, preferred_element_type=F32)
        ms = jnp.mean(o * o, axis=-1, keepdims=True)
        gate = g_ref[:, hs].astype(F32)
        o = o * lax.rsqrt(ms + EPS) * gn_ref[...] * (gate * _sigmoid(gate))
        o_ref[:, hs] = o.astype(o_ref.dtype)


def _hgrn_scan(q, zf, i, g, lb, gnorm, s0, nseq, n_valid):
    tt, d = q.shape
    tp = tt // nseq
    nc = tp // SCAN_CHUNK
    dk = d // HG_HEADS
    row = pl.BlockSpec((SCAN_CHUNK, d), lambda b, c: (b * nc + c, 0))
    st = pl.BlockSpec((1, HG_HEADS, dk, dk), lambda b, c: (b, 0, 0, 0))
    body = functools.partial(_hgrn_scan_body, chunk=SCAN_CHUNK, n_valid=min(n_valid, SCAN_CHUNK), dk=dk)
    return pl.pallas_call(
        body,
        grid=(nseq, nc),
        in_specs=[row, row, row, row,
                  pl.BlockSpec((1, d), lambda b, c: (0, 0)),
                  pl.BlockSpec((1, dk), lambda b, c: (0, 0)),
                  st],
        out_specs=[row, st],
        out_shape=[jax.ShapeDtypeStruct((tt, d), BF16), jax.ShapeDtypeStruct(s0.shape, F32)],
        compiler_params=_params(("arbitrary", "arbitrary")),
        name="hgrn_scan",
    )(q, zf, i, g, lb.reshape(1, d), gnorm.reshape(1, dk), s0)


def _route_rows(lt):
    rows = [lt[e:e + 1, :] for e in range(N_EXPERTS)]
    m = functools.reduce(jnp.maximum, rows)
    ex = [jnp.exp(r - m) for r in rows]
    inv = 1.0 / functools.reduce(lambda x, y: x + y, ex)
    pr = [e * inv for e in ex]
    scores = []
    for gi in range(N_GROUPS):
        p4 = pr[gi * EXPERTS_PER_GROUP:(gi + 1) * EXPERTS_PER_GROUP]
        pairs = [p4[x] + p4[y] for x in range(EXPERTS_PER_GROUP) for y in range(x + 1, EXPERTS_PER_GROUP)]
        scores.append(functools.reduce(jnp.maximum, pairs))
    best = scores[0]
    sel = jnp.zeros(best.shape, jnp.int32)
    for gi in range(1, N_GROUPS):
        better = scores[gi] > best
        sel = jnp.where(better, gi, sel)
        best = jnp.where(better, scores[gi], best)
    cand = []
    for j in range(EXPERTS_PER_GROUP):
        cj = pr[(N_GROUPS - 1) * EXPERTS_PER_GROUP + j]
        for gi in range(N_GROUPS - 2, -1, -1):
            cj = jnp.where(sel == gi, pr[gi * EXPERTS_PER_GROUP + j], cj)
        cand.append(cj)

    def argmax4(vals):
        p, idx = vals[0], jnp.zeros(vals[0].shape, jnp.int32)
        for j in range(1, len(vals)):
            better = vals[j] > p
            idx = jnp.where(better, j, idx)
            p = jnp.where(better, vals[j], p)
        return p, idx

    p1, i1 = argmax4(cand)
    p2, i2 = argmax4([jnp.where(i1 == j, -1.0, cand[j]) for j in range(EXPERTS_PER_GROUP)])
    den = p1 + p2
    base = sel * EXPERTS_PER_GROUP
    return base + i1, base + i2, p1 / den, p2 / den


def _mix_router_body(a_ref, w_ref, h_ref, g1_ref, gain_ref, sh_ref, sc_ref, wrh_ref, wrl_ref, br_ref, upper_ref,
                     h_out, hn_out, gate_out, route_out, count_out):
    @pl.when(pl.program_id(0) == 0)
    def _():
        count_out[...] = jnp.zeros_like(count_out)

    mix = jnp.dot(a_ref[...].astype(BF16), w_ref[...], preferred_element_type=F32)
    h = h_ref[...] + g1_ref[...] * mix
    h_out[...] = h
    hn = _norm_mod(h, gain_ref[...], sh_ref[...], sc_ref[...])
    _store_row_tiles(hn_out, hn)
    x_hi = hn.astype(BF16)
    x_lo = (hn - x_hi.astype(F32)).astype(BF16)
    lt = (_nt_dot(wrh_ref[...], x_hi) + _nt_dot(wrl_ref[...], x_hi) + _nt_dot(wrh_ref[...], x_lo)
          + br_ref[...])
    e1, e2, g1, g2 = _route_rows(lt)
    tm = lt.shape[1]
    sub = lax.broadcasted_iota(jnp.int32, (LANES, tm), 0)
    gate_out[...] = (jnp.where(sub == 0, g1, 0.0) + jnp.where(sub == 1, g2, 0.0)).T
    eid = lax.broadcasted_iota(jnp.int32, (N_EXPERTS, tm), 0)
    pick1 = eid == e1
    pick2 = eid == e2
    hits = jnp.where(pick1, 1.0, 0.0) + jnp.where(pick2, 1.0, 0.0)
    before = jnp.dot(hits.astype(BF16), upper_ref[...], preferred_element_type=F32) + count_out[...]
    rank1 = jnp.sum(jnp.where(pick1, before, 0.0), axis=0, keepdims=True)
    rank2 = jnp.sum(jnp.where(pick2, before, 0.0), axis=0, keepdims=True)
    count_out[...] += jnp.sum(hits, axis=1, keepdims=True)
    field = lax.broadcasted_iota(jnp.int32, route_out.shape[1:], 0)
    route_out[0] = jnp.where(field == 0, e1, jnp.where(field == 1, e2, jnp.where(
        field == 2, rank1.astype(jnp.int32), rank2.astype(jnp.int32))))


def _mix_router(a, w_out, h, gain, mod, wr_hi, wr_lo, b_router, tm):
    t, d = h.shape
    row = pl.BlockSpec((tm, d), lambda i: (i, 0))
    const = lambda shape: pl.BlockSpec(shape, lambda i: (0,) * len(shape))
    upper = (jnp.arange(tm)[:, None] < jnp.arange(tm)[None, :]).astype(BF16)
    return pl.pallas_call(
        _mix_router_body,
        grid=(t // tm,),
        in_specs=[row, const((d, d)), row, mod.spec(2), const((1, d)), mod.spec(3), mod.spec(4),
                  const((N_EXPERTS, d)), const((N_EXPERTS, d)), const((N_EXPERTS, 1)), const((tm, tm))],
        out_specs=[row, pl.BlockSpec((tm * SUBLANES, LANES), lambda i: (i, 0)),
                   pl.BlockSpec((tm, LANES), lambda i: (i, 0)),
                   pl.BlockSpec((1, 4, tm), lambda i: (i, 0, 0)), const((N_EXPERTS, 1))],
        out_shape=[jax.ShapeDtypeStruct((t, d), F32), jax.ShapeDtypeStruct((t * SUBLANES, LANES), F32),
                   jax.ShapeDtypeStruct((t, LANES), F32), jax.ShapeDtypeStruct((t // tm, 4, tm), jnp.int32),
                   jax.ShapeDtypeStruct((N_EXPERTS, 1), F32)],
        compiler_params=_params(("arbitrary",)),
        name="mix_router",
    )(a, w_out, h, mod.arr, gain.reshape(1, d), mod.arr, mod.arr, wr_hi, wr_lo,
      b_router.reshape(N_EXPERTS, 1).astype(F32), upper)


def _store_row_tiles(ref, x):
    n, d = x.shape
    assert d == SUBLANES * LANES
    for c in range(SUBLANES):
        ref[pl.ds(c, n, stride=SUBLANES), :] = x[:, c * LANES:(c + 1) * LANES]


def _load_row_tiles(ref):
    n = ref.shape[0] // SUBLANES
    return jnp.concatenate([ref[pl.ds(c, n, stride=SUBLANES), :] for c in range(SUBLANES)], axis=1)


def _tile_rows(first_row, n_rows):
    return pl.ds(pl.multiple_of(first_row * SUBLANES, SUBLANES), n_rows * SUBLANES)


def _row_copy(src, src_row, dst, dst_row, sem):
    return pltpu.make_async_copy(src.at[_tile_rows(src_row, 1)], dst.at[_tile_rows(dst_row, 1)], sem)


def _index_slot(tm):
    return -(-2 * tm // SMEM_I32_CHUNK) * SMEM_I32_CHUNK


def _load_positions(pos_hbm, idx_ref, sem, step):
    n = idx_ref.shape[0]
    cp = pltpu.make_async_copy(pos_hbm.at[pl.ds(pl.multiple_of(step * n, n), n)], idx_ref, sem)
    cp.start()
    cp.wait()


def _dispatch_body(seg_end_ref, pos_hbm, x_ref, xs_out, idx_ref, zero_ref, idx_sem, row_sem, *, tm):
    @pl.when(pl.program_id(0) == 0)
    def _():
        zero_ref[...] = jnp.zeros_like(zero_ref)

        def last_tile(e):
            return pltpu.make_async_copy(
                zero_ref, xs_out.at[_tile_rows(seg_end_ref[e] - EXPERT_TILE, EXPERT_TILE)], row_sem)

        def nonempty(e):
            return seg_end_ref[e] > (seg_end_ref[e - 1] if e else 0)

        for e in range(N_EXPERTS):
            pl.when(nonempty(e))(lambda e=e: last_tile(e).start())
        for e in range(N_EXPERTS):
            pl.when(nonempty(e))(lambda e=e: last_tile(e).wait())

        def unused_tile(i, carry):
            cp = pltpu.make_async_copy(zero_ref, xs_out.at[_tile_rows(i * EXPERT_TILE, EXPERT_TILE)], row_sem)
            cp.start()
            cp.wait()
            return carry

        n_tiles = xs_out.shape[0] // (EXPERT_TILE * SUBLANES)
        lax.fori_loop(seg_end_ref[N_EXPERTS - 1] // EXPERT_TILE, n_tiles, unused_tile, 0)

    _load_positions(pos_hbm, idx_ref, idx_sem, pl.program_id(0))

    def start(t, carry):
        _row_copy(x_ref, t, xs_out, idx_ref[t], row_sem).start(priority=0)
        _row_copy(x_ref, t, xs_out, idx_ref[tm + t], row_sem).start(priority=1)
        return carry

    lax.fori_loop(0, tm, start, 0, unroll=ROW_COPY_UNROLL)
    for _ in range(2):
        pltpu.make_async_copy(x_ref, xs_out.at[_tile_rows(0, tm)], row_sem).wait()


def _dispatch(hn, pos_tiles, seg_end, n_rows, tm):
    t = hn.shape[0] // SUBLANES
    grid_spec = pltpu.PrefetchScalarGridSpec(
        num_scalar_prefetch=1,
        grid=(t // tm,),
        in_specs=[pl.BlockSpec(memory_space=pl.ANY), pl.BlockSpec((tm * SUBLANES, LANES), lambda i, se: (i, 0))],
        out_specs=pl.BlockSpec(memory_space=pl.ANY),
        scratch_shapes=[pltpu.SMEM((_index_slot(tm),), jnp.int32),
                        pltpu.VMEM((EXPERT_TILE * SUBLANES, LANES), F32),
                        pltpu.SemaphoreType.DMA, pltpu.SemaphoreType.DMA])
    return pl.pallas_call(
        functools.partial(_dispatch_body, tm=tm),
        grid_spec=grid_spec,
        out_shape=jax.ShapeDtypeStruct((n_rows * SUBLANES, LANES), F32),
        compiler_params=_params(("arbitrary",)),
        name="moe_dispatch",
    )(seg_end, pos_tiles, hn)


def _expert_body(te_ref, nu_ref, x_ref, win_ref, wout_ref, y_ref):
    @pl.when(pl.program_id(0) < nu_ref[0])
    def _():
        hid = jnp.dot(_load_row_tiles(x_ref).astype(BF16), win_ref[0], preferred_element_type=F32)
        de = hid.shape[1] // 2
        a, u = hid[:, :de], hid[:, de:]
        act = (a * _sigmoid(a) * u).astype(BF16)
        _store_row_tiles(y_ref, jnp.dot(act, wout_ref[0], preferred_element_type=F32))

    @pl.when(pl.program_id(0) >= nu_ref[0])
    def _():
        y_ref[...] = jnp.zeros_like(y_ref)


def _experts(xs, tile_expert, n_used, w_in, w_out):
    n_e, d, d2 = w_in.shape
    n_tiles = xs.shape[0] // (EXPERT_TILE * SUBLANES)
    tile = lambda i, te, nu: jnp.minimum(i, nu[0] - 1)
    rows = (EXPERT_TILE * SUBLANES, LANES)
    grid_spec = pltpu.PrefetchScalarGridSpec(
        num_scalar_prefetch=2,
        grid=(n_tiles,),
        in_specs=[pl.BlockSpec(rows, lambda i, te, nu: (tile(i, te, nu), 0)),
                  pl.BlockSpec((1, d, d2), lambda i, te, nu: (te[tile(i, te, nu)], 0, 0)),
                  pl.BlockSpec((1, d2 // 2, d), lambda i, te, nu: (te[tile(i, te, nu)], 0, 0))],
        out_specs=pl.BlockSpec(rows, lambda i, te, nu: (i, 0)))
    return pl.pallas_call(
        _expert_body,
        grid_spec=grid_spec,
        out_shape=jax.ShapeDtypeStruct(xs.shape, F32),
        compiler_params=_params(("arbitrary",)),
        name="moe_experts",
    )(tile_expert, n_used, xs, w_in, w_out)


def _combine_body(pos_hbm, ys_hbm, gate_ref, h_ref, g2_ref, o_ref, idx_ref, y1_ref, y2_ref, idx_sem, row_sem,
                  *, tm):
    _load_positions(pos_hbm, idx_ref, idx_sem, pl.program_id(0))

    def start(t, carry):
        _row_copy(ys_hbm, idx_ref[t], y1_ref, t, row_sem).start(priority=0)
        _row_copy(ys_hbm, idx_ref[tm + t], y2_ref, t, row_sem).start(priority=1)
        return carry

    lax.fori_loop(0, tm, start, 0, unroll=ROW_COPY_UNROLL)
    for y_ref in (y1_ref, y2_ref):
        pltpu.make_async_copy(ys_hbm.at[_tile_rows(0, tm)], y_ref, row_sem).wait()
    gates = gate_ref[...]
    moe = gates[:, 0:1] * _load_row_tiles(y1_ref) + gates[:, 1:2] * _load_row_tiles(y2_ref)
    o_ref[...] = h_ref[...] + g2_ref[...] * moe


def _combine(ys, pos_tiles, gates, h, mod, tm):
    t, d = h.shape
    row = pl.BlockSpec((tm, d), lambda i: (i, 0))
    return pl.pallas_call(
        functools.partial(_combine_body, tm=tm),
        grid=(t // tm,),
        in_specs=[pl.BlockSpec(memory_space=pl.ANY), pl.BlockSpec(memory_space=pl.ANY),
                  pl.BlockSpec((tm, LANES), lambda i: (i, 0)), row, mod.spec(5)],
        out_specs=row,
        out_shape=jax.ShapeDtypeStruct((t, d), F32),
        scratch_shapes=[pltpu.SMEM((_index_slot(tm),), jnp.int32),
                        pltpu.VMEM((tm * SUBLANES, LANES), F32), pltpu.VMEM((tm * SUBLANES, LANES), F32),
                        pltpu.SemaphoreType.DMA, pltpu.SemaphoreType.DMA],
        compiler_params=_params(("arbitrary",)),
        name="moe_combine",
    )(pos_tiles, ys, gates, h, mod.arr)


def _moe(hn, gates, route, counts, h, mod, w_in, w_out, tm):
    t, d = h.shape
    n_tiles = 2 * t // EXPERT_TILE + N_EXPERTS
    n_rows = n_tiles * EXPERT_TILE
    cnt = counts.reshape(N_EXPERTS).astype(jnp.int32)
    padded = (cnt + (EXPERT_TILE - 1)) // EXPERT_TILE * EXPERT_TILE
    experts = jnp.arange(N_EXPERTS, dtype=jnp.int32)
    seg_end = jnp.sum(jnp.where(experts[None, :] <= experts[:, None], padded[None, :], 0), axis=1)
    seg_start = seg_end - padded
    first_row = jnp.sum(jnp.where(route[:, :2, :, None] == experts, seg_start, 0), axis=-1)
    pos_tiles = (first_row + route[:, 2:]).reshape(t // tm, 2 * tm)
    if _index_slot(tm) != 2 * tm:
        pos_tiles = jnp.pad(pos_tiles, ((0, 0), (0, _index_slot(tm) - 2 * tm)))
    pos_tiles = pos_tiles.reshape(-1)
    tile_row = jnp.arange(n_tiles, dtype=jnp.int32) * EXPERT_TILE
    tile_expert = jnp.minimum(jnp.sum((seg_end[None, :] <= tile_row[:, None]).astype(jnp.int32), axis=1),
                              N_EXPERTS - 1)
    n_used = seg_end[-1:] // EXPERT_TILE
    xs = _dispatch(hn, pos_tiles, seg_end, n_rows, tm)
    ys = _experts(xs, tile_expert, n_used, w_in, w_out)
    return _combine(ys, pos_tiles, gates, h, mod, tm)


def _head_norm(y, gsum_ref, gain_ref, head_dim):
    ssum = jnp.dot((y * y).astype(BF16), gsum_ref[...], preferred_element_type=F32)
    return y * lax.rsqrt(ssum * (1.0 / head_dim) + EPS) * gain_ref[...]


def _head_sum_matrix(d, head_dim):
    r = jnp.arange(d) // head_dim
    return (r[:, None] == r[None, :]).astype(BF16)


def _q_proj_body(x_ref, gain_ref, sh_ref, sc_ref, w_ref, gsum_ref, qn_ref, q_ref, *, head_dim, q_scale):
    xn = _norm_mod(x_ref[...], gain_ref[...], sh_ref[...], sc_ref[...]).astype(BF16)
    y = jnp.dot(xn, w_ref[...], preferred_element_type=F32)
    q = _head_norm(y, gsum_ref, qn_ref, head_dim) * q_scale
    q_ref[...] = q.astype(q_ref.dtype)


def _q_proj(h, gain, mod, w, gsum, q_norm, tm, out_dtype, q_scale):
    t, d = h.shape
    head_dim = d // FOX_HEADS
    row = pl.BlockSpec((tm, d), lambda i: (i, 0))
    const = lambda shape: pl.BlockSpec(shape, lambda i: (0,) * len(shape))
    return pl.pallas_call(
        functools.partial(_q_proj_body, head_dim=head_dim, q_scale=q_scale),
        grid=(t // tm,),
        in_specs=[row, const((1, d)), mod.spec(0), mod.spec(1), const((d, d)), const((d, d)), const((1, d))],
        out_specs=row,
        out_shape=jax.ShapeDtypeStruct((t, d), out_dtype),
        compiler_params=_params(("arbitrary",)),
        name="q_proj",
    )(h, gain.reshape(1, d), mod.arr, mod.arr, w, gsum, jnp.tile(q_norm, FOX_HEADS).reshape(1, d))


def _kv_proj_body(x_ref, gain_ref, sh_ref, sc_ref, wk_ref, wv_ref, wf_ref, gsum_ref, kn_ref, bf_ref,
                  k_ref, v_ref, kb_ref, vb_ref, lf_ref, *, head_dim):
    xn = _norm_mod(x_ref[...], gain_ref[...], sh_ref[...], sc_ref[...]).astype(BF16)
    k = _head_norm(jnp.dot(xn, wk_ref[...], preferred_element_type=F32), gsum_ref, kn_ref, head_dim)
    k_ref[...] = k
    kb_ref[...] = k.astype(BF16)
    v = jnp.dot(xn, wv_ref[...], preferred_element_type=F32)
    v_ref[...] = v
    vb_ref[...] = v.astype(BF16)
    zf = jnp.dot(xn, wf_ref[...], preferred_element_type=F32) + bf_ref[...]
    lf_ref[...] = _log_sigmoid(zf)[:, :lf_ref.shape[1]]


def _kv_proj(h, gain, mod, wk, wv, wf, gsum, k_norm, b_fgate, tm):
    t, d = h.shape
    head_dim = d // FOX_HEADS
    row = pl.BlockSpec((tm, d), lambda i: (i, 0))
    const = lambda shape: pl.BlockSpec(shape, lambda i: (0,) * len(shape))
    bf = jnp.pad(b_fgate.astype(F32), (0, LANES - FOX_HEADS)).reshape(1, LANES)
    return pl.pallas_call(
        functools.partial(_kv_proj_body, head_dim=head_dim),
        grid=(t // tm,),
        in_specs=[row, const((1, d)), mod.spec(0), mod.spec(1), const((d, d)), const((d, d)),
                  const((d, LANES)), const((d, d)), const((1, d)), const((1, LANES))],
        out_specs=[row, row, row, row, pl.BlockSpec((tm, FOX_HEADS), lambda i: (i, 0))],
        out_shape=[jax.ShapeDtypeStruct((t, d), F32), jax.ShapeDtypeStruct((t, d), F32),
                   jax.ShapeDtypeStruct((t, d), BF16), jax.ShapeDtypeStruct((t, d), BF16),
                   jax.ShapeDtypeStruct((t, FOX_HEADS), F32)],
        compiler_params=_params(("arbitrary",)),
        name="kv_proj",
    )(h, gain.reshape(1, d), mod.arr, mod.arr, wk, wv, wf, gsum,
      jnp.tile(k_norm, FOX_HEADS).reshape(1, d), bf)


def _cumsum_lanes_body(x_ref, o_ref, carry_ref, *, carry_rows):
    j = pl.program_id(1)

    @pl.when(j == 0)
    def _():
        carry_ref[...] = jnp.zeros_like(carry_ref)

    w = x_ref.shape[-1]
    r = lax.broadcasted_iota(jnp.int32, (w, w), 0)
    c = lax.broadcasted_iota(jnp.int32, (w, w), 1)
    upper = jnp.where(r <= c, 1.0, 0.0).astype(F32)
    y = jnp.dot(x_ref[0], upper, precision=HIGHEST, preferred_element_type=F32)
    if carry_rows:
        y = y + carry_ref[...]
        carry_ref[...] = y[:, w - 1:w]
    o_ref[0] = y


def _cumsum_lanes(x, width, carry):
    b, r, t = x.shape
    spec = pl.BlockSpec((1, r, width), lambda i, j: (i, 0, j))
    return pl.pallas_call(
        functools.partial(_cumsum_lanes_body, carry_rows=carry),
        grid=(b, t // width),
        in_specs=[spec],
        out_specs=spec,
        out_shape=jax.ShapeDtypeStruct(x.shape, F32),
        scratch_shapes=[pltpu.VMEM((r, 1), F32)],
        compiler_params=_params(("arbitrary", "arbitrary")),
        name="cumsum_lanes",
    )(x)


def _attn_prompt_body(qt_ref, kt_ref, q_ref, k_ref, v_ref, f_ref, o_ref, m_ref, acc_ref,
                      *, blk, rows, head_dim):
    p = pl.program_id(2)
    qi = qt_ref[p]
    ki = kt_ref[p]

    @pl.when(ki == 0)
    def _():
        m_ref[...] = jnp.full(m_ref.shape, NEG_BIG, F32)
        acc_ref[...] = jnp.zeros_like(acc_ref)

    def process(masked):
        k = k_ref[...]
        v = v_ref[...]
        f = f_ref[...]
        first_v = lax.broadcasted_iota(jnp.int32, v.shape, 1) < head_dim
        ones = jnp.ones_like(v)
        v_heads = (jnp.where(first_v, v, ones), jnp.where(first_v, ones, v))
        m_olds = {(hh, r0): m_ref[hh, r0:r0 + rows, :] for hh in range(2) for r0 in range(0, blk, rows)}
        acc_olds = {(hh, r0): acc_ref[hh, r0:r0 + rows, :] for hh in range(2) for r0 in range(0, blk, rows)}
        m_news, acc_news = {}, {}
        for r0 in range(0, blk, rows):
            q = q_ref[r0:r0 + rows, :]
            first_q = lax.broadcasted_iota(jnp.int32, q.shape, 1) < head_dim
            zero = jnp.zeros_like(q)
            for hh, qh in enumerate((jnp.where(first_q, q, zero), jnp.where(first_q, zero, q))):
                s = _nt_dot(qh, k) - f[hh:hh + 1, :]
                if masked:
                    row = lax.broadcasted_iota(jnp.int32, s.shape, 0) + r0
                    col = lax.broadcasted_iota(jnp.int32, s.shape, 1)
                    s = jnp.where(col <= row, s, NEG_BIG)
                m_old = m_olds[hh, r0]
                m_new = jnp.maximum(m_old, jnp.max(s, axis=1, keepdims=True))
                alpha = jnp.exp2(m_old - m_new)
                pr = jnp.concatenate([jnp.exp2(s[:, c * LANES:(c + 1) * LANES] - m_new)
                                      for c in range(blk // LANES)], axis=1).astype(BF16)
                acc_news[hh, r0] = alpha * acc_olds[hh, r0] + jnp.dot(pr, v_heads[hh], preferred_element_type=F32)
                m_news[hh, r0] = m_new
        for (hh, r0), m_new in m_news.items():
            m_ref[hh, r0:r0 + rows, :] = m_new
            acc_ref[hh, r0:r0 + rows, :] = acc_news[hh, r0]

    @pl.when(ki < qi)
    def _():
        process(False)

    @pl.when(ki == qi)
    def _():
        process(True)
        a0 = acc_ref[0]
        a1 = acc_ref[1]
        first = lax.broadcasted_iota(jnp.int32, a0.shape, 1) < head_dim
        inv0 = 1.0 / a0[:, head_dim:head_dim + 1]
        inv1 = 1.0 / a1[:, 0:1]
        o_ref[...] = jnp.where(first, a0 * inv0, a1 * inv1).astype(o_ref.dtype)


def _attn_prompt(q, k, v, f_pairs, nseq):
    tt, d = q.shape
    t = tt // nseq
    blk = min(ATTN_BLOCK, t)
    rows = min(ATTN_ROWS, blk)
    nb = t // blk
    head_dim = d // FOX_HEADS
    pairs = [(qi, ki) for qi in range(nb) for ki in range(qi + 1)]
    q_tab = jnp.asarray([p[0] for p in pairs], jnp.int32)
    k_tab = jnp.asarray([p[1] for p in pairs], jnp.int32)
    grid_spec = pltpu.PrefetchScalarGridSpec(
        num_scalar_prefetch=2,
        grid=(nseq, d // LANES, len(pairs)),
        in_specs=[pl.BlockSpec((blk, LANES), lambda b, hp, p, qt, kt: (b * nb + qt[p], hp)),
                  pl.BlockSpec((blk, LANES), lambda b, hp, p, qt, kt: (b * nb + kt[p], hp)),
                  pl.BlockSpec((blk, LANES), lambda b, hp, p, qt, kt: (b * nb + kt[p], hp)),
                  pl.BlockSpec((None, None, 2, blk), lambda b, hp, p, qt, kt: (b, hp, 0, kt[p]))],
        out_specs=pl.BlockSpec((blk, LANES), lambda b, hp, p, qt, kt: (b * nb + qt[p], hp)),
        scratch_shapes=[pltpu.VMEM((2, blk, LANES), F32), pltpu.VMEM((2, blk, LANES), F32)])
    return pl.pallas_call(
        functools.partial(_attn_prompt_body, blk=blk, rows=rows, head_dim=head_dim),
        grid_spec=grid_spec,
        out_shape=jax.ShapeDtypeStruct((tt, d), BF16),
        compiler_params=_params(("arbitrary", "arbitrary", "arbitrary")),
        name="attn_prompt",
    )(q_tab, k_tab, q, k, v, f_pairs)


def _gather_cum_body(pt_ref, cp_ref, new_ref, o_ref, *, n_pages, page):
    b = pl.program_id(0)

    def step(j, carry):
        blk = cp_ref[pt_ref[b * n_pages + j]]
        o_ref[0, :, pl.ds(pl.multiple_of(j * page, page), page)] = blk + carry
        return carry + blk[:, page - 1:page]

    carry = lax.fori_loop(0, n_pages, step, jnp.zeros((cp_ref.shape[1], 1), F32), unroll=8)
    o_ref[0, :, n_pages * page:] = new_ref[0] + carry


def _gather_cum(page_table, cum_pages, cum_new):
    nseq, n_pages = page_table.shape
    n_pool, hh, page = cum_pages.shape
    grid_spec = pltpu.PrefetchScalarGridSpec(
        num_scalar_prefetch=1,
        grid=(nseq,),
        in_specs=[pl.BlockSpec((n_pool, hh, page), lambda b, pt: (0, 0, 0)),
                  pl.BlockSpec((1, hh, page), lambda b, pt: (b, 0, 0))],
        out_specs=pl.BlockSpec((1, hh, (n_pages + 1) * page), lambda b, pt: (b, 0, 0)))
    return pl.pallas_call(
        functools.partial(_gather_cum_body, n_pages=n_pages, page=page),
        grid_spec=grid_spec,
        out_shape=jax.ShapeDtypeStruct((nseq, hh, (n_pages + 1) * page), F32),
        compiler_params=_params(("arbitrary",)),
        name="gather_cum",
    )(page_table.reshape(-1), cum_pages, cum_new)


def _attn_paged_body(pt_ref, *refs, n_q, page, pps, n_steps, head_dim):
    q_ref = refs[0]
    k_refs = refs[1:1 + pps]
    v_refs = refs[1 + pps:1 + 2 * pps]
    kn_ref, vn_ref, f_ref, o_ref, qbd_ref, m_ref, l_ref, acc_ref = refs[1 + 2 * pps:]
    j = pl.program_id(1)
    d = q_ref.shape[1]
    rows = FOX_HEADS * n_q

    @pl.when(j == 0)
    def _():
        q = q_ref[...]
        lane_head = lax.broadcasted_iota(jnp.int32, q.shape, 1) // head_dim
        for h in range(FOX_HEADS):
            qbd_ref[h * n_q:(h + 1) * n_q, :] = jnp.where(lane_head == h, q, 0.0).astype(BF16)
        m_ref[...] = jnp.full(m_ref.shape, NEG_BIG, F32)
        l_ref[...] = jnp.zeros_like(l_ref)
        acc_ref[...] = jnp.zeros_like(acc_ref)

    def attend(k, v, f, mask):
        bias = jnp.concatenate([jnp.broadcast_to(f[h:h + 1, :], (n_q, page)) for h in range(FOX_HEADS)], axis=0)
        s = _nt_dot(qbd_ref[...], k) - bias
        if mask is not None:
            s = jnp.where(mask, s, NEG_BIG)
        m_old = m_ref[...]
        m_new = jnp.maximum(m_old, jnp.max(s, axis=1, keepdims=True))
        alpha = jnp.exp(m_old - m_new)
        pr = jnp.exp(s - m_new)
        l_ref[...] = alpha * l_ref[...] + jnp.sum(pr, axis=1, keepdims=True)
        m_ref[...] = m_new
        acc_ref[...] = alpha * acc_ref[...] + jnp.dot(pr.astype(BF16), v, preferred_element_type=F32)

    @pl.when(j < n_steps - 1)
    def _():
        for u in range(pps):
            off = pl.multiple_of((j * pps + u) * page, page)
            attend(k_refs[u][...].astype(BF16), v_refs[u][...].astype(BF16), f_ref[0, :, pl.ds(off, page)], None)

    @pl.when(j == n_steps - 1)
    def _():
        pad = jnp.zeros((page - n_q, d), F32)
        k = jnp.concatenate([kn_ref[...], pad], axis=0).astype(BF16)
        v = jnp.concatenate([vn_ref[...], pad], axis=0).astype(BF16)
        t_of_row = lax.broadcasted_iota(jnp.int32, (rows, page), 0) % n_q
        key = lax.broadcasted_iota(jnp.int32, (rows, page), 1)
        attend(k, v, f_ref[0, :, (n_steps - 1) * pps * page:], key <= t_of_row)
        acc = acc_ref[...] * (1.0 / l_ref[...])
        lane_head = lax.broadcasted_iota(jnp.int32, (n_q, d), 1) // head_dim
        out = jnp.zeros((n_q, d), F32)
        for h in range(FOX_HEADS):
            out = out + jnp.where(lane_head == h, acc[h * n_q:(h + 1) * n_q, :], 0.0)
        o_ref[...] = out


def _attn_paged(q, cache_k, cache_v, k_new, v_new, f_all, page_table):
    tt, d = q.shape
    nseq, n_pages = page_table.shape
    n_q = tt // nseq
    page = cache_k.shape[1]
    pps = PAGES_PER_STEP
    n_steps = n_pages // pps + 1
    head_dim = d // FOX_HEADS
    rows = FOX_HEADS * n_q

    def page_spec(u):
        def index(b, j, pt):
            return (pt[b * n_pages + jnp.minimum(j, n_steps - 2) * pps + u], 0, 0)
        return pl.BlockSpec((None, page, d), index)

    tok = pl.BlockSpec((n_q, d), lambda b, j, pt: (b, 0))
    grid_spec = pltpu.PrefetchScalarGridSpec(
        num_scalar_prefetch=1,
        grid=(nseq, n_steps),
        in_specs=([tok] + [page_spec(u) for u in range(pps)] + [page_spec(u) for u in range(pps)]
                  + [tok, tok, pl.BlockSpec((1, FOX_HEADS, f_all.shape[2]), lambda b, j, pt: (b, 0, 0))]),
        out_specs=tok,
        scratch_shapes=[pltpu.VMEM((rows, d), BF16), pltpu.VMEM((rows, 1), F32),
                        pltpu.VMEM((rows, 1), F32), pltpu.VMEM((rows, d), F32)])
    body = functools.partial(_attn_paged_body, n_q=n_q, page=page, pps=pps, n_steps=n_steps,
                             head_dim=head_dim)
    return pl.pallas_call(
        body,
        grid_spec=grid_spec,
        out_shape=jax.ShapeDtypeStruct((tt, d), F32),
        compiler_params=_params(("arbitrary", "arbitrary")),
        name="attn_paged",
    )(page_table.reshape(-1), q, *([cache_k] * pps), *([cache_v] * pps), k_new, v_new, f_all)


def _lower_bounds(lb_logits):
    p = jax.nn.softmax(lb_logits.astype(F32), axis=0)
    return jnp.cumsum(p, axis=0) - p[0]


def _trunk(x, nseq, mods, mod_kv, s0, past, w):
    tt, d = x.shape
    t = tt // nseq
    tm = min(512, tt)
    tiles_per_seq = max(t // tm, 1)
    n_a = w["w_in_a"].shape[0]
    depth = w["w_ada"].shape[0]
    head_dim = d // FOX_HEADS
    lbs = _lower_bounds(w["lb_logits"])
    mk = lambda arr: _Mod(arr, d, tm, tiles_per_seq)

    h = x
    states = []
    for l in range(depth):
        mod = mk(mods[l])
        if l < n_a:
            q, zf, i, g = _hgrn_in_proj(h, w["norm_mix"][l], mod, w["w_in_a4"][l], tm)
            tp = -(-t // SCAN_CHUNK) * SCAN_CHUNK
            if tp != t:
                padr = lambda a: jnp.pad(a.reshape(nseq, t, d), ((0, 0), (0, tp - t), (0, 0))).reshape(nseq * tp, d)
                q, zf, i, g = padr(q), padr(zf), padr(i), padr(g)
            o, s_t = _hgrn_scan(q, zf, i, g, lbs[l], w["gnorm_a"][l], s0[l], nseq, t)
            if tp != t:
                o = o.reshape(nseq, tp, d)[:, :t].reshape(tt, d)
            states.append(s_t)
            mix_in, w_out = o, w["w_out_a"][l]
        else:
            j = l - n_a
            if past is None:
                q = _q_proj(h, w["norm_mix"][l], mod, w["w_q_b"][j], w["gsum"], w["q_norm"][j], tm, BF16,
                            head_dim ** -0.5 * LOG2E)
                mix_in = _attn_prompt(q, k_bf, v_bf, f_keys, nseq)
            else:
                q = _q_proj(h, w["norm_mix"][l], mod, w["w_q_b"][j], w["gsum"], w["q_norm"][j], tm, F32,
                            head_dim ** -0.5)
                mix_in = _attn_paged(q, past[0], past[1], k_new, v_new, f_keys, past[3])
            w_out = w["w_out_b"][j]
        h, hn, gates, route, counts = _mix_router(mix_in, w_out, h, w["norm_ffn"][l], mod, w["wr_hi"],
                                                  w["wr_lo"], w["b_router"], tm)
        h = _moe(hn, gates, route, counts, h, mod, w["w_exp_in"][l], w["w_exp_out"][l], tm)
        if l == n_a - 1:
            k_new, v_new, k_bf, v_bf, logf_new = _kv_proj(
                h, w["norm_kv"], mk(mod_kv), w["w_k"], w["w_v"], w["w_f"], w["gsum"], w["k_norm"],
                w["b_fgate"], tm)
            lf_t = logf_new.reshape(nseq, t, FOX_HEADS).transpose(0, 2, 1)
            if past is None:
                f_cum = _cumsum_lanes(lf_t, min(512, t), carry=True)
                f_keys = (f_cum * LOG2E).reshape(nseq, FOX_HEADS // 2, 2, t)
            else:
                cache_logf, page_table = past[2], past[3]
                page = cache_logf.shape[1]
                pool_t = cache_logf.astype(F32).transpose(0, 2, 1)
                n_pool = pool_t.shape[0]
                rows = 64 if n_pool % 64 == 0 else 1
                cum_pages = _cumsum_lanes(pool_t.reshape(n_pool // rows, rows * FOX_HEADS, page), page,
                                          carry=False).reshape(n_pool, FOX_HEADS, page)
                new_pad = jnp.pad(lf_t, ((0, 0), (0, 0), (0, page - t)))
                cum_new = _cumsum_lanes(new_pad, page, carry=False)
                f_keys = _gather_cum(page_table, cum_pages, cum_new)
    return h, jnp.stack(states), k_new, v_new, logf_new


def kernel(x_prompt, x_sample, cache_k, cache_v, cache_logf, state_hgrn, page_table, c_prompt, c_sample,
           w_ada, b_ada, norm_mix, norm_ffn, w_in_a, lb_logits, gnorm_a, w_out_a, norm_kv, w_ada_kv,
           b_ada_kv, w_kv, b_fgate, k_norm, w_q_b, q_norm, w_out_b, w_router, b_router, w_exp_in, w_exp_out):
    nb, seq, d = x_prompt.shape
    ns, dseq, _ = x_sample.shape
    n_a = w_in_a.shape[0]
    depth = w_ada.shape[0]
    hd = FOX_HEADS * (d // FOX_HEADS)
    dk = d // HG_HEADS

    nrow = nb + ns
    nrow_pad = -(-nrow // SUBLANES) * SUBLANES
    c_all = jnp.pad(jnp.concatenate([c_prompt, c_sample], axis=0).astype(F32), ((0, nrow_pad - nrow), (0, 0)))
    mod_all = _ada(c_all, w_ada, b_ada)
    mod_kv_all = _ada(c_all, w_ada_kv[None], b_ada_kv[None])[0]
    mods_p = [mod_all[l, :nb].reshape(nb, 1, 6 * d) for l in range(depth)]
    mods_s = [jnp.repeat(mod_all[l, nb:nrow], dseq, axis=0) for l in range(depth)]
    mod_kv_p = mod_kv_all[:nb].reshape(nb, 1, 2 * d)
    mod_kv_s = jnp.repeat(mod_kv_all[nb:nrow], dseq, axis=0)

    wr_t = w_router.astype(F32).T
    wr_hi = wr_t.astype(BF16)
    wr_lo = (wr_t - wr_hi.astype(F32)).astype(BF16)
    w_in_b = w_in_a.astype(BF16)
    w = dict(
        w_ada=w_ada, lb_logits=lb_logits, norm_mix=norm_mix, norm_ffn=norm_ffn, gnorm_a=gnorm_a,
        w_in_a=w_in_a,
        w_in_a4=[[w_in_b[l, :, s * d:(s + 1) * d] for s in range(4)] for l in range(n_a)],
        w_out_a=w_out_a.astype(BF16), norm_kv=norm_kv,
        w_k=w_kv[:, :hd].astype(BF16), w_v=w_kv[:, hd:2 * hd].astype(BF16),
        w_f=jnp.pad(w_kv[:, 2 * hd:], ((0, 0), (0, LANES - FOX_HEADS))).astype(BF16),
        b_fgate=b_fgate, k_norm=k_norm, w_q_b=w_q_b.astype(BF16), q_norm=q_norm,
        w_out_b=w_out_b.astype(BF16), wr_hi=wr_hi, wr_lo=wr_lo, b_router=b_router,
        w_exp_in=w_exp_in.astype(BF16), w_exp_out=w_exp_out.astype(BF16),
        gsum=_head_sum_matrix(d, d // FOX_HEADS),
    )

    s0_p = jnp.zeros((n_a, nb, HG_HEADS, dk, dk), F32)
    y_p, st_p, k_p, v_p, lf_p = _trunk(x_prompt.reshape(nb * seq, d), nb, mods_p, mod_kv_p, s0_p, None, w)

    n_pool, page = cache_k.shape[0], cache_k.shape[1]
    past = (cache_k.reshape(n_pool, page, hd), cache_v.reshape(n_pool, page, hd), cache_logf, page_table)
    y_s, st_s, k_s, v_s, lf_s = _trunk(x_sample.reshape(ns * dseq, d), ns, mods_s, mod_kv_s,
                                       state_hgrn.astype(F32), past, w)

    hs = (FOX_HEADS, d // FOX_HEADS)
    return (y_p.reshape(nb, seq, d), y_s.reshape(ns, dseq, d),
            st_p.astype(state_hgrn.dtype), st_s.astype(state_hgrn.dtype),
            k_p.reshape(nb, seq, *hs).astype(cache_k.dtype), v_p.reshape(nb, seq, *hs).astype(cache_v.dtype),
            lf_p.reshape(nb, seq, FOX_HEADS).astype(cache_logf.dtype),
            k_s.reshape(ns, dseq, *hs).astype(cache_k.dtype), v_s.reshape(ns, dseq, *hs).astype(cache_v.dtype),
            lf_s.reshape(ns, dseq, FOX_HEADS).astype(cache_logf.dtype))
```

```python
import functools
import math

import jax
import jax.numpy as jnp
from jax import lax
from jax.experimental import pallas as pl
from jax.experimental.pallas import tpu as pltpu

F32 = jnp.float32
BF16 = jnp.bfloat16
HIGHEST = lax.Precision.HIGHEST

HG_HEADS = 8
FOX_HEADS = 16
N_EXPERTS = 16
N_GROUPS = 4
EXPERTS_PER_GROUP = N_EXPERTS // N_GROUPS
EPS = 1e-6
NEG_BIG = -1e30
TINY = 1e-30

LANES = 128
SUBLANES = 8
VMEM_LIMIT_BYTES = 56 * 1024 * 1024

SCAN_CHUNK = 128
ATTN_BLOCK = 1024
ATTN_ROWS = 1024
LOG2E = math.log2(math.e)
PAGES_PER_STEP = 8
POOL_PAGES_PER_STEP = 4
EXPERT_TILE = 256
SMEM_I32_CHUNK = 1024
ROW_COPY_UNROLL = 8


def _params(sem):
    return pltpu.CompilerParams(dimension_semantics=sem, vmem_limit_bytes=VMEM_LIMIT_BYTES)


def _nt_dot(a, b):
    return lax.dot_general(a, b, (((1,), (1,)), ((), ())), preferred_element_type=F32)


def _sigmoid(x):
    return 1.0 / (1.0 + jnp.exp(-x))


def _log_sigmoid(x):
    return jnp.minimum(x, 0.0) - jnp.log1p(jnp.exp(-jnp.abs(x)))


def _norm_mod(x, gain, shift, scale):
    ms = jnp.mean(x * x, axis=-1, keepdims=True)
    y = x * lax.rsqrt(ms + EPS) * gain
    return y * (1.0 + scale) + shift


class _Mod:
    def __init__(self, arr, d, tm, tiles_per_seq):
        self.arr = arr
        self.d = d
        self.tm = tm
        self.tiles_per_seq = tiles_per_seq
        self.per_token = arr.ndim == 2

    def spec(self, chunk):
        d, tps = self.d, self.tiles_per_seq
        if self.per_token:
            return pl.BlockSpec((self.tm, d), lambda i, *_: (i, chunk))
        return pl.BlockSpec((None, 1, d), lambda i, *_: (i // tps, 0, chunk))


def _ada_body(c_ref, w_ref, b_ref, o_ref):
    c = c_ref[...]
    a = c * _sigmoid(c)
    o_ref[0] = jnp.dot(a, w_ref[0], precision=HIGHEST, preferred_element_type=F32) + b_ref[0]


def _ada(c, w, b):
    n_layers, d, n = w.shape
    r = c.shape[0]
    tn = 1536 if n % 1536 == 0 else 1024
    return pl.pallas_call(
        _ada_body,
        grid=(n_layers, n // tn),
        in_specs=[pl.BlockSpec((r, d), lambda l, j: (0, 0)),
                  pl.BlockSpec((1, d, tn), lambda l, j: (l, 0, j)),
                  pl.BlockSpec((1, 1, tn), lambda l, j: (l, 0, j))],
        out_specs=pl.BlockSpec((1, r, tn), lambda l, j: (l, 0, j)),
        out_shape=jax.ShapeDtypeStruct((n_layers, r, n), F32),
        compiler_params=_params(("arbitrary", "arbitrary")),
        name="ada",
    )(c, w, b.reshape(n_layers, 1, n))


def _hgrn_in_body(x_ref, gain_ref, sh_ref, sc_ref, wq_ref, wf_ref, wi_ref, wg_ref,
                  q_ref, zf_ref, i_ref, g_ref):
    xn = _norm_mod(x_ref[...], gain_ref[...], sh_ref[...], sc_ref[...]).astype(BF16)
    q_ref[...] = jnp.dot(xn, wq_ref[...], preferred_element_type=F32).astype(q_ref.dtype)
    zf_ref[...] = jnp.dot(xn, wf_ref[...], preferred_element_type=F32)
    i_ref[...] = jnp.dot(xn, wi_ref[...], preferred_element_type=F32).astype(i_ref.dtype)
    g_ref[...] = jnp.dot(xn, wg_ref[...], preferred_element_type=F32).astype(g_ref.dtype)


def _hgrn_in_proj(h, gain, mod, w4, tm):
    t, d = h.shape
    row = pl.BlockSpec((tm, d), lambda i: (i, 0))
    full = pl.BlockSpec((d, d), lambda i: (0, 0))
    return pl.pallas_call(
        _hgrn_in_body,
        grid=(t // tm,),
        in_specs=[row, pl.BlockSpec((1, d), lambda i: (0, 0)), mod.spec(0), mod.spec(1),
                  full, full, full, full],
        out_specs=[row, row, row, row],
        out_shape=[jax.ShapeDtypeStruct((t, d), BF16), jax.ShapeDtypeStruct((t, d), F32),
                   jax.ShapeDtypeStruct((t, d), BF16), jax.ShapeDtypeStruct((t, d), BF16)],
        compiler_params=_params(("arbitrary",)),
        name="hgrn_in_proj",
    )(h, gain.reshape(1, d), mod.arr, mod.arr, *w4)


def _split3(x):
    hi = x.astype(BF16)
    r1 = x - hi.astype(F32)
    mid = r1.astype(BF16)
    lo = (r1 - mid.astype(F32)).astype(BF16)
    return hi, mid, lo


def _hgrn_scan_body(q_ref, zf_ref, i_ref, g_ref, lb_ref, gn_ref, s0_ref, o_ref, st_ref,
                    *, chunk, n_valid, dk):
    c = pl.program_id(1)

    @pl.when(c == 0)
    def _():
        st_ref[...] = s0_ref[...]

    row = lax.broadcasted_iota(jnp.int32, (chunk, chunk), 0)
    col = lax.broadcasted_iota(jnp.int32, (chunk, chunk), 1)
    tri = jnp.where(row >= col, 1.0, 0.0).astype(BF16)
    n_levels = chunk.bit_length() - 1
    differ = row ^ col
    on_diag = differ == 0
    at_level = [(lax.shift_right_logical(differ, lv) == 1) & (row > col) for lv in range(n_levels)]
    tok = lax.broadcasted_iota(jnp.int32, (chunk, dk), 0)
    in_right = [(lax.shift_right_logical(tok, lv) & 1) == 1 for lv in range(n_levels)]
    quad = tok & 3
    if n_valid < chunk:
        valid = tok < n_valid
    for h in range(HG_HEADS):
        hs = slice(h * dk, (h + 1) * dk)
        q = q_ref[:, hs].astype(F32)
        zf = zf_ref[:, hs]
        v = i_ref[:, hs]
        lb = lb_ref[:, hs]
        e_abs = jnp.exp(-jnp.abs(zf))
        big = 1.0 / (1.0 + e_abs)
        small = e_abs * big
        nonneg = zf >= 0.0
        log_f = jnp.log(jnp.maximum(lb, TINY) + (1.0 - lb) * jnp.where(nonneg, big, small))
        k = (1.0 - lb) * jnp.where(nonneg, small, big)
        if n_valid < chunk:
            log_f = jnp.where(valid, log_f, 0.0)
            k = jnp.where(valid, k, 0.0)
        g_hi, g_mid, g_lo = _split3(log_f)
        b = (jnp.dot(tri, g_hi, preferred_element_type=F32)
             + jnp.dot(tri, g_mid, preferred_element_type=F32)
             + jnp.dot(tri, g_lo, preferred_element_type=F32))
        b_end = b[chunk - 1:chunk, :]
        a = jnp.where(on_diag, _nt_dot(q.astype(BF16), k.astype(BF16)), 0.0)
        for lv in range(n_levels):
            half = 1 << lv
            if lv == 0:
                dist = jnp.where(in_right[0], log_f, 0.0)
            elif lv == 1:
                edge = jnp.where(quad == 0, pltpu.roll(b, chunk - 1, 0),
                                 jnp.where(quad == 2, pltpu.roll(b, 1, 0),
                                           jnp.where(quad == 3, pltpu.roll(b, 2, 0), b)))
                dist = -jnp.abs(b - edge)
            else:
                nblk = chunk // (2 * half)
                edge = b.reshape(nblk, 2 * half, dk)[:, half - 1:half, :]
                edge = jnp.broadcast_to(edge, (nblk, 2 * half, dk)).reshape(chunk, dk)
                dist = -jnp.abs(b - edge)
            z = (jnp.where(in_right[lv], q, k) * jnp.exp(dist)).astype(BF16)
            a = jnp.where(at_level[lv], _nt_dot(z, z), a)
        s = st_ref[0, h]
        o = (jnp.dot(a.astype(BF16), v, preferred_element_type=F32)
             + jnp.dot((q * jnp.exp(b)).astype(BF16), s.astype(BF16), preferred_element_type=F32))
        k_d = (k * jnp.exp(b_end - b)).T.astype(BF16)
        decay = jnp.broadcast_to(jnp.exp(b_end), (dk, dk)).T
        st_ref[0, h] = decay * s + jnp.dot(k_d, v, preferred_element_type=F32)
        ms = jnp.mean(o * o, axis=-1, keepdims=True)
        gate = g_ref[:, hs].astype(F32)
        o = o * lax.rsqrt(ms + EPS) * gn_ref[...] * (gate * _sigmoid(gate))
        o_ref[:, hs] = o.astype(o_ref.dtype)


def _hgrn_scan(q, zf, i, g, lb, gnorm, s0, nseq, n_valid):
    tt, d = q.shape
    tp = tt // nseq
    nc = tp // SCAN_CHUNK
    dk = d // HG_HEADS
    row = pl.BlockSpec((SCAN_CHUNK, d), lambda b, c: (b * nc + c, 0))
    st = pl.BlockSpec((1, HG_HEADS, dk, dk), lambda b, c: (b, 0, 0, 0))
    body = functools.partial(_hgrn_scan_body, chunk=SCAN_CHUNK, n_valid=min(n_valid, SCAN_CHUNK), dk=dk)
    return pl.pallas_call(
        body,
        grid=(nseq, nc),
        in_specs=[row, row, row, row,
                  pl.BlockSpec((1, d), lambda b, c: (0, 0)),
                  pl.BlockSpec((1, dk), lambda b, c: (0, 0)),
                  st],
        out_specs=[row, st],
        out_shape=[jax.ShapeDtypeStruct((tt, d), BF16), jax.ShapeDtypeStruct(s0.shape, F32)],
        compiler_params=_params(("arbitrary", "arbitrary")),
        name="hgrn_scan",
    )(q, zf, i, g, lb.reshape(1, d), gnorm.reshape(1, dk), s0)


def _route_rows(lt):
    rows = [lt[e:e + 1, :] for e in range(N_EXPERTS)]
    m = functools.reduce(jnp.maximum, rows)
    ex = [jnp.exp(r - m) for r in rows]
    inv = 1.0 / functools.reduce(lambda x, y: x + y, ex)
    pr = [e * inv for e in ex]
    scores = []
    for gi in range(N_GROUPS):
        p4 = pr[gi * EXPERTS_PER_GROUP:(gi + 1) * EXPERTS_PER_GROUP]
        pairs = [p4[x] + p4[y] for x in range(EXPERTS_PER_GROUP) for y in range(x + 1, EXPERTS_PER_GROUP)]
        scores.append(functools.reduce(jnp.maximum, pairs))
    best = scores[0]
    sel = jnp.zeros(best.shape, jnp.int32)
    for gi in range(1, N_GROUPS):
        better = scores[gi] > best
        sel = jnp.where(better, gi, sel)
        best = jnp.where(better, scores[gi], best)
    cand = []
    for j in range(EXPERTS_PER_GROUP):
        cj = pr[(N_GROUPS - 1) * EXPERTS_PER_GROUP + j]
        for gi in range(N_GROUPS - 2, -1, -1):
            cj = jnp.where(sel == gi, pr[gi * EXPERTS_PER_GROUP + j], cj)
        cand.append(cj)

    def argmax4(vals):
        p, idx = vals[0], jnp.zeros(vals[0].shape, jnp.int32)
        for j in range(1, len(vals)):
            better = vals[j] > p
            idx = jnp.where(better, j, idx)
            p = jnp.where(better, vals[j], p)
        return p, idx

    p1, i1 = argmax4(cand)
    p2, i2 = argmax4([jnp.where(i1 == j, -1.0, cand[j]) for j in range(EXPERTS_PER_GROUP)])
    den = p1 + p2
    base = sel * EXPERTS_PER_GROUP
    return base + i1, base + i2, p1 / den, p2 / den


def _mix_router_body(a_ref, w_ref, h_ref, g1_ref, gain_ref, sh_ref, sc_ref, wrh_ref, wrl_ref, br_ref, upper_ref,
                     h_out, hn_out, gate_out, route_out, count_out):
    @pl.when(pl.program_id(0) == 0)
    def _():
        count_out[...] = jnp.zeros_like(count_out)

    mix = jnp.dot(a_ref[...].astype(BF16), w_ref[...], preferred_element_type=F32)
    h = h_ref[...] + g1_ref[...] * mix
    h_out[...] = h
    hn = _norm_mod(h, gain_ref[...], sh_ref[...], sc_ref[...])
    _store_row_tiles(hn_out, hn)
    x_hi = hn.astype(BF16)
    x_lo = (hn - x_hi.astype(F32)).astype(BF16)
    lt = (_nt_dot(wrh_ref[...], x_hi) + _nt_dot(wrl_ref[...], x_hi) + _nt_dot(wrh_ref[...], x_lo)
          + br_ref[...])
    e1, e2, g1, g2 = _route_rows(lt)
    tm = lt.shape[1]
    sub = lax.broadcasted_iota(jnp.int32, (LANES, tm), 0)
    gate_out[...] = (jnp.where(sub == 0, g1, 0.0) + jnp.where(sub == 1, g2, 0.0)).T
    eid = lax.broadcasted_iota(jnp.int32, (N_EXPERTS, tm), 0)
    pick1 = eid == e1
    pick2 = eid == e2
    hits = jnp.where(pick1, 1.0, 0.0) + jnp.where(pick2, 1.0, 0.0)
    before = jnp.dot(hits.astype(BF16), upper_ref[...], preferred_element_type=F32) + count_out[...]
    rank1 = jnp.sum(jnp.where(pick1, before, 0.0), axis=0, keepdims=True)
    rank2 = jnp.sum(jnp.where(pick2, before, 0.0), axis=0, keepdims=True)
    count_out[...] += jnp.sum(hits, axis=1, keepdims=True)
    field = lax.broadcasted_iota(jnp.int32, route_out.shape[1:], 0)
    route_out[0] = jnp.where(field == 0, e1, jnp.where(field == 1, e2, jnp.where(
        field == 2, rank1.astype(jnp.int32), rank2.astype(jnp.int32))))


def _mix_router(a, w_out, h, gain, mod, wr_hi, wr_lo, b_router, tm):
    t, d = h.shape
    row = pl.BlockSpec((tm, d), lambda i: (i, 0))
    const = lambda shape: pl.BlockSpec(shape, lambda i: (0,) * len(shape))
    upper = (jnp.arange(tm)[:, None] < jnp.arange(tm)[None, :]).astype(BF16)
    return pl.pallas_call(
        _mix_router_body,
        grid=(t // tm,),
        in_specs=[row, const((d, d)), row, mod.spec(2), const((1, d)), mod.spec(3), mod.spec(4),
                  const((N_EXPERTS, d)), const((N_EXPERTS, d)), const((N_EXPERTS, 1)), const((tm, tm))],
        out_specs=[row, pl.BlockSpec((tm * SUBLANES, LANES), lambda i: (i, 0)),
                   pl.BlockSpec((tm, LANES), lambda i: (i, 0)),
                   pl.BlockSpec((1, 4, tm), lambda i: (i, 0, 0)), const((N_EXPERTS, 1))],
        out_shape=[jax.ShapeDtypeStruct((t, d), F32), jax.ShapeDtypeStruct((t * SUBLANES, LANES), F32),
                   jax.ShapeDtypeStruct((t, LANES), F32), jax.ShapeDtypeStruct((t // tm, 4, tm), jnp.int32),
                   jax.ShapeDtypeStruct((N_EXPERTS, 1), F32)],
        compiler_params=_params(("arbitrary",)),
        name="mix_router",
    )(a, w_out, h, mod.arr, gain.reshape(1, d), mod.arr, mod.arr, wr_hi, wr_lo,
      b_router.reshape(N_EXPERTS, 1).astype(F32), upper)


def _store_row_tiles(ref, x):
    n, d = x.shape
    assert d == SUBLANES * LANES
    for c in range(SUBLANES):
        ref[pl.ds(c, n, stride=SUBLANES), :] = x[:, c * LANES:(c + 1) * LANES]


def _load_row_tiles(ref):
    n = ref.shape[0] // SUBLANES
    return jnp.concatenate([ref[pl.ds(c, n, stride=SUBLANES), :] for c in range(SUBLANES)], axis=1)


def _tile_rows(first_row, n_rows):
    return pl.ds(pl.multiple_of(first_row * SUBLANES, SUBLANES), n_rows * SUBLANES)


def _row_copy(src, src_row, dst, dst_row, sem):
    return pltpu.make_async_copy(src.at[_tile_rows(src_row, 1)], dst.at[_tile_rows(dst_row, 1)], sem)


def _index_slot(tm):
    return -(-2 * tm // SMEM_I32_CHUNK) * SMEM_I32_CHUNK


def _load_positions(pos_hbm, idx_ref, sem, step):
    n = idx_ref.shape[0]
    cp = pltpu.make_async_copy(pos_hbm.at[pl.ds(pl.multiple_of(step * n, n), n)], idx_ref, sem)
    cp.start()
    cp.wait()


def _dispatch_body(seg_end_ref, pos_hbm, x_ref, xs_out, idx_ref, zero_ref, idx_sem, row_sem, *, tm):
    @pl.when(pl.program_id(0) == 0)
    def _():
        zero_ref[...] = jnp.zeros_like(zero_ref)

        def last_tile(e):
            return pltpu.make_async_copy(
                zero_ref, xs_out.at[_tile_rows(seg_end_ref[e] - EXPERT_TILE, EXPERT_TILE)], row_sem)

        def nonempty(e):
            return seg_end_ref[e] > (seg_end_ref[e - 1] if e else 0)

        for e in range(N_EXPERTS):
            pl.when(nonempty(e))(lambda e=e: last_tile(e).start())
        for e in range(N_EXPERTS):
            pl.when(nonempty(e))(lambda e=e: last_tile(e).wait())

        def unused_tile(i, carry):
            cp = pltpu.make_async_copy(zero_ref, xs_out.at[_tile_rows(i * EXPERT_TILE, EXPERT_TILE)], row_sem)
            cp.start()
            cp.wait()
            return carry

        n_tiles = xs_out.shape[0] // (EXPERT_TILE * SUBLANES)
        lax.fori_loop(seg_end_ref[N_EXPERTS - 1] // EXPERT_TILE, n_tiles, unused_tile, 0)

    _load_positions(pos_hbm, idx_ref, idx_sem, pl.program_id(0))

    def start(t, carry):
        _row_copy(x_ref, t, xs_out, idx_ref[t], row_sem).start(priority=0)
        _row_copy(x_ref, t, xs_out, idx_ref[tm + t], row_sem).start(priority=1)
        return carry

    lax.fori_loop(0, tm, start, 0, unroll=ROW_COPY_UNROLL)
    for _ in range(2):
        pltpu.make_async_copy(x_ref, xs_out.at[_tile_rows(0, tm)], row_sem).wait()


def _dispatch(hn, pos_tiles, seg_end, n_rows, tm):
    t = hn.shape[0] // SUBLANES
    grid_spec = pltpu.PrefetchScalarGridSpec(
        num_scalar_prefetch=1,
        grid=(t // tm,),
        in_specs=[pl.BlockSpec(memory_space=pl.ANY), pl.BlockSpec((tm * SUBLANES, LANES), lambda i, se: (i, 0))],
        out_specs=pl.BlockSpec(memory_space=pl.ANY),
        scratch_shapes=[pltpu.SMEM((_index_slot(tm),), jnp.int32),
                        pltpu.VMEM((EXPERT_TILE * SUBLANES, LANES), F32),
                        pltpu.SemaphoreType.DMA, pltpu.SemaphoreType.DMA])
    return pl.pallas_call(
        functools.partial(_dispatch_body, tm=tm),
        grid_spec=grid_spec,
        out_shape=jax.ShapeDtypeStruct((n_rows * SUBLANES, LANES), F32),
        compiler_params=_params(("arbitrary",)),
        name="moe_dispatch",
    )(seg_end, pos_tiles, hn)


def _expert_body(te_ref, nu_ref, x_ref, win_ref, wout_ref, y_ref):
    @pl.when(pl.program_id(0) < nu_ref[0])
    def _():
        hid = jnp.dot(_load_row_tiles(x_ref).astype(BF16), win_ref[0], preferred_element_type=F32)
        de = hid.shape[1] // 2
        a, u = hid[:, :de], hid[:, de:]
        act = (a * _sigmoid(a) * u).astype(BF16)
        _store_row_tiles(y_ref, jnp.dot(act, wout_ref[0], preferred_element_type=F32))

    @pl.when(pl.program_id(0) >= nu_ref[0])
    def _():
        y_ref[...] = jnp.zeros_like(y_ref)


def _experts(xs, tile_expert, n_used, w_in, w_out):
    n_e, d, d2 = w_in.shape
    n_tiles = xs.shape[0] // (EXPERT_TILE * SUBLANES)
    tile = lambda i, te, nu: jnp.minimum(i, nu[0] - 1)
    rows = (EXPERT_TILE * SUBLANES, LANES)
    grid_spec = pltpu.PrefetchScalarGridSpec(
        num_scalar_prefetch=2,
        grid=(n_tiles,),
        in_specs=[pl.BlockSpec(rows, lambda i, te, nu: (tile(i, te, nu), 0)),
                  pl.BlockSpec((1, d, d2), lambda i, te, nu: (te[tile(i, te, nu)], 0, 0)),
                  pl.BlockSpec((1, d2 // 2, d), lambda i, te, nu: (te[tile(i, te, nu)], 0, 0))],
        out_specs=pl.BlockSpec(rows, lambda i, te, nu: (i, 0)))
    return pl.pallas_call(
        _expert_body,
        grid_spec=grid_spec,
        out_shape=jax.ShapeDtypeStruct(xs.shape, F32),
        compiler_params=_params(("arbitrary",)),
        name="moe_experts",
    )(tile_expert, n_used, xs, w_in, w_out)


def _combine_body(pos_hbm, ys_hbm, gate_ref, h_ref, g2_ref, o_ref, idx_ref, y1_ref, y2_ref, idx_sem, row_sem,
                  *, tm):
    _load_positions(pos_hbm, idx_ref, idx_sem, pl.program_id(0))

    def start(t, carry):
        _row_copy(ys_hbm, idx_ref[t], y1_ref, t, row_sem).start(priority=0)
        _row_copy(ys_hbm, idx_ref[tm + t], y2_ref, t, row_sem).start(priority=1)
        return carry

    lax.fori_loop(0, tm, start, 0, unroll=ROW_COPY_UNROLL)
    for y_ref in (y1_ref, y2_ref):
        pltpu.make_async_copy(ys_hbm.at[_tile_rows(0, tm)], y_ref, row_sem).wait()
    gates = gate_ref[...]
    moe = gates[:, 0:1] * _load_row_tiles(y1_ref) + gates[:, 1:2] * _load_row_tiles(y2_ref)
    o_ref[...] = h_ref[...] + g2_ref[...] * moe


def _combine(ys, pos_tiles, gates, h, mod, tm):
    t, d = h.shape
    row = pl.BlockSpec((tm, d), lambda i: (i, 0))
    return pl.pallas_call(
        functools.partial(_combine_body, tm=tm),
        grid=(t // tm,),
        in_specs=[pl.BlockSpec(memory_space=pl.ANY), pl.BlockSpec(memory_space=pl.ANY),
                  pl.BlockSpec((tm, LANES), lambda i: (i, 0)), row, mod.spec(5)],
        out_specs=row,
        out_shape=jax.ShapeDtypeStruct((t, d), F32),
        scratch_shapes=[pltpu.SMEM((_index_slot(tm),), jnp.int32),
                        pltpu.VMEM((tm * SUBLANES, LANES), F32), pltpu.VMEM((tm * SUBLANES, LANES), F32),
                        pltpu.SemaphoreType.DMA, pltpu.SemaphoreType.DMA],
        compiler_params=_params(("arbitrary",)),
        name="moe_combine",
    )(pos_tiles, ys, gates, h, mod.arr)


def _moe(hn, gates, route, counts, h, mod, w_in, w_out, tm):
    t, d = h.shape
    n_tiles = 2 * t // EXPERT_TILE + N_EXPERTS
    n_rows = n_tiles * EXPERT_TILE
    cnt = counts.reshape(N_EXPERTS).astype(jnp.int32)
    padded = (cnt + (EXPERT_TILE - 1)) // EXPERT_TILE * EXPERT_TILE
    experts = jnp.arange(N_EXPERTS, dtype=jnp.int32)
    seg_end = jnp.sum(jnp.where(experts[None, :] <= experts[:, None], padded[None, :], 0), axis=1)
    seg_start = seg_end - padded
    first_row = jnp.sum(jnp.where(route[:, :2, :, None] == experts, seg_start, 0), axis=-1)
    pos_tiles = (first_row + route[:, 2:]).reshape(t // tm, 2 * tm)
    if _index_slot(tm) != 2 * tm:
        pos_tiles = jnp.pad(pos_tiles, ((0, 0), (0, _index_slot(tm) - 2 * tm)))
    pos_tiles = pos_tiles.reshape(-1)
    tile_row = jnp.arange(n_tiles, dtype=jnp.int32) * EXPERT_TILE
    tile_expert = jnp.minimum(jnp.sum((seg_end[None, :] <= tile_row[:, None]).astype(jnp.int32), axis=1),
                              N_EXPERTS - 1)
    n_used = seg_end[-1:] // EXPERT_TILE
    xs = _dispatch(hn, pos_tiles, seg_end, n_rows, tm)
    ys = _experts(xs, tile_expert, n_used, w_in, w_out)
    return _combine(ys, pos_tiles, gates, h, mod, tm)


def _head_norm(y, gsum_ref, gain_ref, head_dim):
    ssum = jnp.dot((y * y).astype(BF16), gsum_ref[...], preferred_element_type=F32)
    return y * lax.rsqrt(ssum * (1.0 / head_dim) + EPS) * gain_ref[...]


def _head_sum_matrix(d, head_dim):
    r = jnp.arange(d) // head_dim
    return (r[:, None] == r[None, :]).astype(BF16)


def _q_proj_body(x_ref, gain_ref, sh_ref, sc_ref, w_ref, gsum_ref, qn_ref, q_ref, *, head_dim, q_scale):
    xn = _norm_mod(x_ref[...], gain_ref[...], sh_ref[...], sc_ref[...]).astype(BF16)
    y = jnp.dot(xn, w_ref[...], preferred_element_type=F32)
    q = _head_norm(y, gsum_ref, qn_ref, head_dim) * q_scale
    q_ref[...] = q.astype(q_ref.dtype)


def _q_proj(h, gain, mod, w, gsum, q_norm, tm, out_dtype, q_scale):
    t, d = h.shape
    head_dim = d // FOX_HEADS
    row = pl.BlockSpec((tm, d), lambda i: (i, 0))
    const = lambda shape: pl.BlockSpec(shape, lambda i: (0,) * len(shape))
    return pl.pallas_call(
        functools.partial(_q_proj_body, head_dim=head_dim, q_scale=q_scale),
        grid=(t // tm,),
        in_specs=[row, const((1, d)), mod.spec(0), mod.spec(1), const((d, d)), const((d, d)), const((1, d))],
        out_specs=row,
        out_shape=jax.ShapeDtypeStruct((t, d), out_dtype),
        compiler_params=_params(("arbitrary",)),
        name="q_proj",
    )(h, gain.reshape(1, d), mod.arr, mod.arr, w, gsum, jnp.tile(q_norm, FOX_HEADS).reshape(1, d))


def _kv_proj_body(x_ref, gain_ref, sh_ref, sc_ref, wk_ref, wv_ref, wf_ref, gsum_ref, kn_ref, bf_ref,
                  k_ref, v_ref, kb_ref, vb_ref, lf_ref, *, head_dim):
    xn = _norm_mod(x_ref[...], gain_ref[...], sh_ref[...], sc_ref[...]).astype(BF16)
    k = _head_norm(jnp.dot(xn, wk_ref[...], preferred_element_type=F32), gsum_ref, kn_ref, head_dim)
    k_ref[...] = k
    kb_ref[...] = k.astype(BF16)
    v = jnp.dot(xn, wv_ref[...], preferred_element_type=F32)
    v_ref[...] = v
    vb_ref[...] = v.astype(BF16)
    zf = jnp.dot(xn, wf_ref[...], preferred_element_type=F32) + bf_ref[...]
    lf_ref[...] = _log_sigmoid(zf)[:, :lf_ref.shape[1]]


def _kv_proj(h, gain, mod, wk, wv, wf, gsum, k_norm, b_fgate, tm):
    t, d = h.shape
    head_dim = d // FOX_HEADS
    row = pl.BlockSpec((tm, d), lambda i: (i, 0))
    const = lambda shape: pl.BlockSpec(shape, lambda i: (0,) * len(shape))
    bf = jnp.pad(b_fgate.astype(F32), (0, LANES - FOX_HEADS)).reshape(1, LANES)
    return pl.pallas_call(
        functools.partial(_kv_proj_body, head_dim=head_dim),
        grid=(t // tm,),
        in_specs=[row, const((1, d)), mod.spec(0), mod.spec(1), const((d, d)), const((d, d)),
                  const((d, LANES)), const((d, d)), const((1, d)), const((1, LANES))],
        out_specs=[row, row, row, row, pl.BlockSpec((tm, FOX_HEADS), lambda i: (i, 0))],
        out_shape=[jax.ShapeDtypeStruct((t, d), F32), jax.ShapeDtypeStruct((t, d), F32),
                   jax.ShapeDtypeStruct((t, d), BF16), jax.ShapeDtypeStruct((t, d), BF16),
                   jax.ShapeDtypeStruct((t, FOX_HEADS), F32)],
        compiler_params=_params(("arbitrary",)),
        name="kv_proj",
    )(h, gain.reshape(1, d), mod.arr, mod.arr, wk, wv, wf, gsum,
      jnp.tile(k_norm, FOX_HEADS).reshape(1, d), bf)


def _cumsum_lanes_body(x_ref, o_ref, carry_ref, *, carry_rows):
    j = pl.program_id(1)

    @pl.when(j == 0)
    def _():
        carry_ref[...] = jnp.zeros_like(carry_ref)

    w = x_ref.shape[-1]
    r = lax.broadcasted_iota(jnp.int32, (w, w), 0)
    c = lax.broadcasted_iota(jnp.int32, (w, w), 1)
    upper = jnp.where(r <= c, 1.0, 0.0).astype(F32)
    y = jnp.dot(x_ref[0], upper, precision=HIGHEST, preferred_element_type=F32)
    if carry_rows:
        y = y + carry_ref[...]
        carry_ref[...] = y[:, w - 1:w]
    o_ref[0] = y


def _cumsum_lanes(x, width, carry):
    b, r, t = x.shape
    spec = pl.BlockSpec((1, r, width), lambda i, j: (i, 0, j))
    return pl.pallas_call(
        functools.partial(_cumsum_lanes_body, carry_rows=carry),
        grid=(b, t // width),
        in_specs=[spec],
        out_specs=spec,
        out_shape=jax.ShapeDtypeStruct(x.shape, F32),
        scratch_shapes=[pltpu.VMEM((r, 1), F32)],
        compiler_params=_params(("arbitrary", "arbitrary")),
        name="cumsum_lanes",
    )(x)


def _attn_prompt_body(qt_ref, kt_ref, q_ref, k_ref, v_ref, f_ref, o_ref, m_ref, acc_ref,
                      *, blk, rows, head_dim):
    p = pl.program_id(2)
    qi = qt_ref[p]
    ki = kt_ref[p]

    @pl.when(ki == 0)
    def _():
        m_ref[...] = jnp.full(m_ref.shape, NEG_BIG, F32)
        acc_ref[...] = jnp.zeros_like(acc_ref)

    def process(masked):
        k = k_ref[...]
        v = v_ref[...]
        f = f_ref[...]
        first_v = lax.broadcasted_iota(jnp.int32, v.shape, 1) < head_dim
        ones = jnp.ones_like(v)
        v_heads = (jnp.where(first_v, v, ones), jnp.where(first_v, ones, v))
        m_olds = {(hh, r0): m_ref[hh, r0:r0 + rows, :] for hh in range(2) for r0 in range(0, blk, rows)}
        acc_olds = {(hh, r0): acc_ref[hh, r0:r0 + rows, :] for hh in range(2) for r0 in range(0, blk, rows)}
        m_news, acc_news = {}, {}
        for r0 in range(0, blk, rows):
            q = q_ref[r0:r0 + rows, :]
            first_q = lax.broadcasted_iota(jnp.int32, q.shape, 1) < head_dim
            zero = jnp.zeros_like(q)
            for hh, qh in enumerate((jnp.where(first_q, q, zero), jnp.where(first_q, zero, q))):
                s = _nt_dot(qh, k) - f[hh:hh + 1, :]
                if masked:
                    row = lax.broadcasted_iota(jnp.int32, s.shape, 0) + r0
                    col = lax.broadcasted_iota(jnp.int32, s.shape, 1)
                    s = jnp.where(col <= row, s, NEG_BIG)
                m_old = m_olds[hh, r0]
                m_new = jnp.maximum(m_old, jnp.max(s, axis=1, keepdims=True))
                alpha = jnp.exp2(m_old - m_new)
                pr = jnp.concatenate([jnp.exp2(s[:, c * LANES:(c + 1) * LANES] - m_new)
                                      for c in range(blk // LANES)], axis=1).astype(BF16)
                acc_news[hh, r0] = alpha * acc_olds[hh, r0] + jnp.dot(pr, v_heads[hh], preferred_element_type=F32)
                m_news[hh, r0] = m_new
        for (hh, r0), m_new in m_news.items():
            m_ref[hh, r0:r0 + rows, :] = m_new
            acc_ref[hh, r0:r0 + rows, :] = acc_news[hh, r0]

    @pl.when(ki < qi)
    def _():
        process(False)

    @pl.when(ki == qi)
    def _():
        process(True)
        a0 = acc_ref[0]
        a1 = acc_ref[1]
        first = lax.broadcasted_iota(jnp.int32, a0.shape, 1) < head_dim
        inv0 = 1.0 / a0[:, head_dim:head_dim + 1]
        inv1 = 1.0 / a1[:, 0:1]
        o_ref[...] = jnp.where(first, a0 * inv0, a1 * inv1).astype(o_ref.dtype)


def _attn_prompt(q, k, v, f_pairs, nseq):
    tt, d = q.shape
    t = tt // nseq
    blk = min(ATTN_BLOCK, t)
    rows = min(ATTN_ROWS, blk)
    nb = t // blk
    head_dim = d // FOX_HEADS
    pairs = [(qi, ki) for qi in range(nb) for ki in range(qi + 1)]
    q_tab = jnp.asarray([p[0] for p in pairs], jnp.int32)
    k_tab = jnp.asarray([p[1] for p in pairs], jnp.int32)
    grid_spec = pltpu.PrefetchScalarGridSpec(
        num_scalar_prefetch=2,
        grid=(nseq, d // LANES, len(pairs)),
        in_specs=[pl.BlockSpec((blk, LANES), lambda b, hp, p, qt, kt: (b * nb + qt[p], hp)),
                  pl.BlockSpec((blk, LANES), lambda b, hp, p, qt, kt: (b * nb + kt[p], hp)),
                  pl.BlockSpec((blk, LANES), lambda b, hp, p, qt, kt: (b * nb + kt[p], hp)),
                  pl.BlockSpec((None, None, 2, blk), lambda b, hp, p, qt, kt: (b, hp, 0, kt[p]))],
        out_specs=pl.BlockSpec((blk, LANES), lambda b, hp, p, qt, kt: (b * nb + qt[p], hp)),
        scratch_shapes=[pltpu.VMEM((2, blk, LANES), F32), pltpu.VMEM((2, blk, LANES), F32)])
    return pl.pallas_call(
        functools.partial(_attn_prompt_body, blk=blk, rows=rows, head_dim=head_dim),
        grid_spec=grid_spec,
        out_shape=jax.ShapeDtypeStruct((tt, d), BF16),
        compiler_params=_params(("arbitrary", "arbitrary", "arbitrary")),
        name="attn_prompt",
    )(q_tab, k_tab, q, k, v, f_pairs)


def _gather_cum_body(pt_ref, cp_ref, new_ref, o_ref, *, n_pages, page):
    b = pl.program_id(0)

    def step(j, carry):
        blk = cp_ref[pt_ref[b * n_pages + j]]
        o_ref[0, :, pl.ds(pl.multiple_of(j * page, page), page)] = blk + carry
        return carry + blk[:, page - 1:page]

    carry = lax.fori_loop(0, n_pages, step, jnp.zeros((cp_ref.shape[1], 1), F32), unroll=8)
    o_ref[0, :, n_pages * page:] = new_ref[0] + carry


def _gather_cum(page_table, cum_pages, cum_new):
    nseq, n_pages = page_table.shape
    n_pool, hh, page = cum_pages.shape
    grid_spec = pltpu.PrefetchScalarGridSpec(
        num_scalar_prefetch=1,
        grid=(nseq,),
        in_specs=[pl.BlockSpec((n_pool, hh, page), lambda b, pt: (0, 0, 0)),
                  pl.BlockSpec((1, hh, page), lambda b, pt: (b, 0, 0))],
        out_specs=pl.BlockSpec((1, hh, (n_pages + 1) * page), lambda b, pt: (b, 0, 0)))
    return pl.pallas_call(
        functools.partial(_gather_cum_body, n_pages=n_pages, page=page),
        grid_spec=grid_spec,
        out_shape=jax.ShapeDtypeStruct((nseq, hh, (n_pages + 1) * page), F32),
        compiler_params=_params(("arbitrary",)),
        name="gather_cum",
    )(page_table.reshape(-1), cum_pages, cum_new)


def _attn_paged_body(pt_ref, *refs, n_q, page, pps, n_steps, head_dim, from_pool):
    q_ref = refs[0]
    n_kv = pps if from_pool else 1
    k_refs = refs[1:1 + n_kv]
    v_refs = refs[1 + n_kv:1 + 2 * n_kv]
    rest = refs[1 + 2 * n_kv:]
    if from_pool:
        kn_ref, vn_ref, f_ref, o_ref, kc_ref, vc_ref, qbd_ref, m_ref, l_ref, acc_ref = rest
    else:
        kn_ref, vn_ref, f_ref, o_ref, qbd_ref, m_ref, l_ref, acc_ref = rest
    j = pl.program_id(1)
    d = q_ref.shape[1]
    rows = FOX_HEADS * n_q

    @pl.when(j == 0)
    def _():
        q = q_ref[...]
        lane_head = lax.broadcasted_iota(jnp.int32, q.shape, 1) // head_dim
        for h in range(FOX_HEADS):
            qbd_ref[h * n_q:(h + 1) * n_q, :] = jnp.where(lane_head == h, q, 0.0).astype(BF16)
        m_ref[...] = jnp.full(m_ref.shape, NEG_BIG, F32)
        l_ref[...] = jnp.zeros_like(l_ref)
        acc_ref[...] = jnp.zeros_like(acc_ref)

    def attend(k, v, f, mask):
        bias = jnp.concatenate([jnp.broadcast_to(f[h:h + 1, :], (n_q, page)) for h in range(FOX_HEADS)], axis=0)
        s = _nt_dot(qbd_ref[...], k) - bias
        if mask is not None:
            s = jnp.where(mask, s, NEG_BIG)
        m_old = m_ref[...]
        m_new = jnp.maximum(m_old, jnp.max(s, axis=1, keepdims=True))
        alpha = jnp.exp(m_old - m_new)
        pr = jnp.exp(s - m_new)
        l_ref[...] = alpha * l_ref[...] + jnp.sum(pr, axis=1, keepdims=True)
        m_ref[...] = m_new
        acc_ref[...] = alpha * acc_ref[...] + jnp.dot(pr.astype(BF16), v, preferred_element_type=F32)

    @pl.when(j < n_steps - 1)
    def _():
        for u in range(pps):
            off = pl.multiple_of((j * pps + u) * page, page)
            rows_u = slice(u * page, (u + 1) * page)
            if from_pool:
                k = pltpu.einshape("thd->t(hd)", k_refs[u][...]).astype(BF16)
                v = pltpu.einshape("thd->t(hd)", v_refs[u][...]).astype(BF16)
                kc_ref[rows_u, :] = k
                vc_ref[rows_u, :] = v
            else:
                k = k_refs[0][rows_u, :]
                v = v_refs[0][rows_u, :]
            attend(k, v, f_ref[0, :, pl.ds(off, page)], None)

    @pl.when(j == n_steps - 1)
    def _():
        pad = jnp.zeros((page - n_q, d), F32)
        k = jnp.concatenate([kn_ref[...], pad], axis=0).astype(BF16)
        v = jnp.concatenate([vn_ref[...], pad], axis=0).astype(BF16)
        t_of_row = lax.broadcasted_iota(jnp.int32, (rows, page), 0) % n_q
        key = lax.broadcasted_iota(jnp.int32, (rows, page), 1)
        attend(k, v, f_ref[0, :, (n_steps - 1) * pps * page:], key <= t_of_row)
        acc = acc_ref[...] * (1.0 / l_ref[...])
        lane_head = lax.broadcasted_iota(jnp.int32, (n_q, d), 1) // head_dim
        out = jnp.zeros((n_q, d), F32)
        for h in range(FOX_HEADS):
            out = out + jnp.where(lane_head == h, acc[h * n_q:(h + 1) * n_q, :], 0.0)
        o_ref[...] = out


def _attn_paged(q, keys, values, k_new, v_new, f_all, page_table, from_pool):
    tt, d = q.shape
    nseq, n_pages = page_table.shape
    n_q = tt // nseq
    head_dim = d // FOX_HEADS
    rows = FOX_HEADS * n_q
    if from_pool:
        page = keys.shape[1]
        pps = POOL_PAGES_PER_STEP
    else:
        page = keys.shape[1] // n_pages
        pps = PAGES_PER_STEP
    n_steps = n_pages // pps + 1
    past_step = lambda j: jnp.minimum(j, n_steps - 2)

    def page_spec(u):
        def index(b, j, pt):
            return (pt[b * n_pages + past_step(j) * pps + u], 0, 0, 0)
        return pl.BlockSpec((None, page, FOX_HEADS, head_dim), index)

    past_rows = pl.BlockSpec((None, pps * page, d), lambda b, j, pt: (b, past_step(j), 0))
    tok = pl.BlockSpec((n_q, d), lambda b, j, pt: (b, 0))
    kv_specs = [page_spec(u) for u in range(pps)] * 2 if from_pool else [past_rows, past_rows]
    kv_args = [keys] * pps + [values] * pps if from_pool else [keys, values]
    out_specs, out_shape = tok, jax.ShapeDtypeStruct((tt, d), F32)
    if from_pool:
        past = jax.ShapeDtypeStruct((nseq, n_pages * page, d), BF16)
        out_specs, out_shape = [tok, past_rows, past_rows], [out_shape, past, past]
    grid_spec = pltpu.PrefetchScalarGridSpec(
        num_scalar_prefetch=1,
        grid=(nseq, n_steps),
        in_specs=([tok] + kv_specs
                  + [tok, tok, pl.BlockSpec((1, FOX_HEADS, f_all.shape[2]), lambda b, j, pt: (b, 0, 0))]),
        out_specs=out_specs,
        scratch_shapes=[pltpu.VMEM((rows, d), BF16), pltpu.VMEM((rows, 1), F32),
                        pltpu.VMEM((rows, 1), F32), pltpu.VMEM((rows, d), F32)])
    body = functools.partial(_attn_paged_body, n_q=n_q, page=page, pps=pps, n_steps=n_steps,
                             head_dim=head_dim, from_pool=from_pool)
    return pl.pallas_call(
        body,
        grid_spec=grid_spec,
        out_shape=out_shape,
        compiler_params=_params(("arbitrary", "arbitrary")),
        name="attn_paged_pool" if from_pool else "attn_paged",
    )(page_table.reshape(-1), q, *kv_args, k_new, v_new, f_all)


def _lower_bounds(lb_logits):
    p = jax.nn.softmax(lb_logits.astype(F32), axis=0)
    return jnp.cumsum(p, axis=0) - p[0]


def _trunk(x, nseq, mods, mod_kv, s0, past, w):
    tt, d = x.shape
    t = tt // nseq
    tm = min(512, tt)
    tiles_per_seq = max(t // tm, 1)
    n_a = w["w_in_a"].shape[0]
    depth = w["w_ada"].shape[0]
    head_dim = d // FOX_HEADS
    lbs = _lower_bounds(w["lb_logits"])
    mk = lambda arr: _Mod(arr, d, tm, tiles_per_seq)

    h = x
    states = []
    for l in range(depth):
        mod = mk(mods[l])
        if l < n_a:
            q, zf, i, g = _hgrn_in_proj(h, w["norm_mix"][l], mod, w["w_in_a4"][l], tm)
            tp = -(-t // SCAN_CHUNK) * SCAN_CHUNK
            if tp != t:
                padr = lambda a: jnp.pad(a.reshape(nseq, t, d), ((0, 0), (0, tp - t), (0, 0))).reshape(nseq * tp, d)
                q, zf, i, g = padr(q), padr(zf), padr(i), padr(g)
            o, s_t = _hgrn_scan(q, zf, i, g, lbs[l], w["gnorm_a"][l], s0[l], nseq, t)
            if tp != t:
                o = o.reshape(nseq, tp, d)[:, :t].reshape(tt, d)
            states.append(s_t)
            mix_in, w_out = o, w["w_out_a"][l]
        else:
            j = l - n_a
            if past is None:
                q = _q_proj(h, w["norm_mix"][l], mod, w["w_q_b"][j], w["gsum"], w["q_norm"][j], tm, BF16,
                            head_dim ** -0.5 * LOG2E)
                mix_in = _attn_prompt(q, k_bf, v_bf, f_keys, nseq)
            else:
                q = _q_proj(h, w["norm_mix"][l], mod, w["w_q_b"][j], w["gsum"], w["q_norm"][j], tm, F32,
                            head_dim ** -0.5)
                if j == 0:
                    mix_in, past_k, past_v = _attn_paged(q, past[0], past[1], k_new, v_new, f_keys, past[3], True)
                else:
                    mix_in = _attn_paged(q, past_k, past_v, k_new, v_new, f_keys, past[3], False)
            w_out = w["w_out_b"][j]
        h, hn, gates, route, counts = _mix_router(mix_in, w_out, h, w["norm_ffn"][l], mod, w["wr_hi"],
                                                  w["wr_lo"], w["b_router"], tm)
        h = _moe(hn, gates, route, counts, h, mod, w["w_exp_in"][l], w["w_exp_out"][l], tm)
        if l == n_a - 1:
            k_new, v_new, k_bf, v_bf, logf_new = _kv_proj(
                h, w["norm_kv"], mk(mod_kv), w["w_k"], w["w_v"], w["w_f"], w["gsum"], w["k_norm"],
                w["b_fgate"], tm)
            lf_t = logf_new.reshape(nseq, t, FOX_HEADS).transpose(0, 2, 1)
            if past is None:
                f_cum = _cumsum_lanes(lf_t, min(512, t), carry=True)
                f_keys = (f_cum * LOG2E).reshape(nseq, FOX_HEADS // 2, 2, t)
            else:
                cache_logf, page_table = past[2], past[3]
                page = cache_logf.shape[1]
                pool_t = cache_logf.astype(F32).transpose(0, 2, 1)
                n_pool = pool_t.shape[0]
                rows = 64 if n_pool % 64 == 0 else 1
                cum_pages = _cumsum_lanes(pool_t.reshape(n_pool // rows, rows * FOX_HEADS, page), page,
                                          carry=False).reshape(n_pool, FOX_HEADS, page)
                new_pad = jnp.pad(lf_t, ((0, 0), (0, 0), (0, page - t)))
                cum_new = _cumsum_lanes(new_pad, page, carry=False)
                f_keys = _gather_cum(page_table, cum_pages, cum_new)
    return h, jnp.stack(states), k_new, v_new, logf_new


def kernel(x_prompt, x_sample, cache_k, cache_v, cache_logf, state_hgrn, page_table, c_prompt, c_sample,
           w_ada, b_ada, norm_mix, norm_ffn, w_in_a, lb_logits, gnorm_a, w_out_a, norm_kv, w_ada_kv,
           b_ada_kv, w_kv, b_fgate, k_norm, w_q_b, q_norm, w_out_b, w_router, b_router, w_exp_in, w_exp_out):
    nb, seq, d = x_prompt.shape
    ns, dseq, _ = x_sample.shape
    n_a = w_in_a.shape[0]
    depth = w_ada.shape[0]
    hd = FOX_HEADS * (d // FOX_HEADS)
    dk = d // HG_HEADS

    nrow = nb + ns
    nrow_pad = -(-nrow // SUBLANES) * SUBLANES
    c_all = jnp.pad(jnp.concatenate([c_prompt, c_sample], axis=0).astype(F32), ((0, nrow_pad - nrow), (0, 0)))
    mod_all = _ada(c_all, w_ada, b_ada)
    mod_kv_all = _ada(c_all, w_ada_kv[None], b_ada_kv[None])[0]
    mods_p = [mod_all[l, :nb].reshape(nb, 1, 6 * d) for l in range(depth)]
    mods_s = [jnp.repeat(mod_all[l, nb:nrow], dseq, axis=0) for l in range(depth)]
    mod_kv_p = mod_kv_all[:nb].reshape(nb, 1, 2 * d)
    mod_kv_s = jnp.repeat(mod_kv_all[nb:nrow], dseq, axis=0)

    wr_t = w_router.astype(F32).T
    wr_hi = wr_t.astype(BF16)
    wr_lo = (wr_t - wr_hi.astype(F32)).astype(BF16)
    w_in_b = w_in_a.astype(BF16)
    w = dict(
        w_ada=w_ada, lb_logits=lb_logits, norm_mix=norm_mix, norm_ffn=norm_ffn, gnorm_a=gnorm_a,
        w_in_a=w_in_a,
        w_in_a4=[[w_in_b[l, :, s * d:(s + 1) * d] for s in range(4)] for l in range(n_a)],
        w_out_a=w_out_a.astype(BF16), norm_kv=norm_kv,
        w_k=w_kv[:, :hd].astype(BF16), w_v=w_kv[:, hd:2 * hd].astype(BF16),
        w_f=jnp.pad(w_kv[:, 2 * hd:], ((0, 0), (0, LANES - FOX_HEADS))).astype(BF16),
        b_fgate=b_fgate, k_norm=k_norm, w_q_b=w_q_b.astype(BF16), q_norm=q_norm,
        w_out_b=w_out_b.astype(BF16), wr_hi=wr_hi, wr_lo=wr_lo, b_router=b_router,
        w_exp_in=w_exp_in.astype(BF16), w_exp_out=w_exp_out.astype(BF16),
        gsum=_head_sum_matrix(d, d // FOX_HEADS),
    )

    s0_p = jnp.zeros((n_a, nb, HG_HEADS, dk, dk), F32)
    y_p, st_p, k_p, v_p, lf_p = _trunk(x_prompt.reshape(nb * seq, d), nb, mods_p, mod_kv_p, s0_p, None, w)

    past = (cache_k, cache_v, cache_logf, page_table)
    y_s, st_s, k_s, v_s, lf_s = _trunk(x_sample.reshape(ns * dseq, d), ns, mods_s, mod_kv_s,
                                       state_hgrn.astype(F32), past, w)

    hs = (FOX_HEADS, d // FOX_HEADS)
    return (y_p.reshape(nb, seq, d), y_s.reshape(ns, dseq, d),
            st_p.astype(state_hgrn.dtype), st_s.astype(state_hgrn.dtype),
            k_p.reshape(nb, seq, *hs).astype(cache_k.dtype), v_p.reshape(nb, seq, *hs).astype(cache_v.dtype),
            lf_p.reshape(nb, seq, FOX_HEADS).astype(cache_logf.dtype),
            k_s.reshape(ns, dseq, *hs).astype(cache_k.dtype), v_s.reshape(ns, dseq, *hs).astype(cache_v.dtype),
            lf_s.reshape(ns, dseq, FOX_HEADS).astype(cache_logf.dtype))
```

```python
import functools
import math

import jax
import jax.numpy as jnp
from jax import lax
from jax.experimental import pallas as pl
from jax.experimental.pallas import tpu as pltpu

F32 = jnp.float32
BF16 = jnp.bfloat16
HIGHEST = lax.Precision.HIGHEST

HG_HEADS = 8
FOX_HEADS = 16
N_EXPERTS = 16
N_GROUPS = 4
EXPERTS_PER_GROUP = N_EXPERTS // N_GROUPS
EPS = 1e-6
NEG_BIG = -1e30
TINY = 1e-30

LANES = 128
SUBLANES = 8
VMEM_LIMIT_BYTES = 56 * 1024 * 1024

SCAN_CHUNK = 128
ATTN_BLOCK = 1024
ATTN_ROWS = 1024
ATTN_DIAG_ROWS = 512
LOG2E = math.log2(math.e)
PAGES_PER_STEP = 8
EXPERT_TILE = 256
SMEM_I32_CHUNK = 1024
ROW_COPY_UNROLL = 8


def _params(sem):
    return pltpu.CompilerParams(dimension_semantics=sem, vmem_limit_bytes=VMEM_LIMIT_BYTES)


def _nt_dot(a, b):
    return lax.dot_general(a, b, (((1,), (1,)), ((), ())), preferred_element_type=F32)


def _sigmoid(x):
    return 1.0 / (1.0 + jnp.exp(-x))


def _log_sigmoid(x):
    return jnp.minimum(x, 0.0) - jnp.log1p(jnp.exp(-jnp.abs(x)))


def _norm_mod(x, gain, shift, scale):
    ms = jnp.mean(x * x, axis=-1, keepdims=True)
    y = x * lax.rsqrt(ms + EPS) * gain
    return y * (1.0 + scale) + shift


class _Mod:
    def __init__(self, arr, d, tm, tiles_per_seq):
        self.arr = arr
        self.d = d
        self.tm = tm
        self.tiles_per_seq = tiles_per_seq
        self.per_token = arr.ndim == 2

    def spec(self, chunk):
        d, tps = self.d, self.tiles_per_seq
        if self.per_token:
            return pl.BlockSpec((self.tm, d), lambda i, *_: (i, chunk))
        return pl.BlockSpec((None, 1, d), lambda i, *_: (i // tps, 0, chunk))


def _ada_body(c_ref, w_ref, b_ref, o_ref):
    c = c_ref[...]
    a = c * _sigmoid(c)
    o_ref[0] = jnp.dot(a, w_ref[0], precision=HIGHEST, preferred_element_type=F32) + b_ref[0]


def _ada(c, w, b):
    n_layers, d, n = w.shape
    r = c.shape[0]
    tn = 1536 if n % 1536 == 0 else 1024
    return pl.pallas_call(
        _ada_body,
        grid=(n_layers, n // tn),
        in_specs=[pl.BlockSpec((r, d), lambda l, j: (0, 0)),
                  pl.BlockSpec((1, d, tn), lambda l, j: (l, 0, j)),
                  pl.BlockSpec((1, 1, tn), lambda l, j: (l, 0, j))],
        out_specs=pl.BlockSpec((1, r, tn), lambda l, j: (l, 0, j)),
        out_shape=jax.ShapeDtypeStruct((n_layers, r, n), F32),
        compiler_params=_params(("arbitrary", "arbitrary")),
        name="ada",
    )(c, w, b.reshape(n_layers, 1, n))


def _hgrn_in_body(x_ref, gain_ref, sh_ref, sc_ref, wq_ref, wf_ref, wi_ref, wg_ref,
                  q_ref, zf_ref, i_ref, g_ref):
    xn = _norm_mod(x_ref[...], gain_ref[...], sh_ref[...], sc_ref[...]).astype(BF16)
    q_ref[...] = jnp.dot(xn, wq_ref[...], preferred_element_type=F32).astype(q_ref.dtype)
    zf_ref[...] = jnp.dot(xn, wf_ref[...], preferred_element_type=F32)
    i_ref[...] = jnp.dot(xn, wi_ref[...], preferred_element_type=F32).astype(i_ref.dtype)
    g_ref[...] = jnp.dot(xn, wg_ref[...], preferred_element_type=F32).astype(g_ref.dtype)


def _hgrn_in_proj(h, gain, mod, w4, tm):
    t, d = h.shape
    row = pl.BlockSpec((tm, d), lambda i: (i, 0))
    full = pl.BlockSpec((d, d), lambda i: (0, 0))
    return pl.pallas_call(
        _hgrn_in_body,
        grid=(t // tm,),
        in_specs=[row, pl.BlockSpec((1, d), lambda i: (0, 0)), mod.spec(0), mod.spec(1),
                  full, full, full, full],
        out_specs=[row, row, row, row],
        out_shape=[jax.ShapeDtypeStruct((t, d), BF16), jax.ShapeDtypeStruct((t, d), F32),
                   jax.ShapeDtypeStruct((t, d), BF16), jax.ShapeDtypeStruct((t, d), BF16)],
        compiler_params=_params(("arbitrary",)),
        name="hgrn_in_proj",
    )(h, gain.reshape(1, d), mod.arr, mod.arr, *w4)


def _split3(x):
    hi = x.astype(BF16)
    r1 = x - hi.astype(F32)
    mid = r1.astype(BF16)
    lo = (r1 - mid.astype(F32)).astype(BF16)
    return hi, mid, lo


def _hgrn_scan_body(q_ref, zf_ref, i_ref, g_ref, lb_ref, gn_ref, s0_ref, o_ref, st_ref,
                    *, chunk, n_valid, dk):
    c = pl.program_id(1)

    @pl.when(c == 0)
    def _():
        st_ref[...] = s0_ref[...]

    row = lax.broadcasted_iota(jnp.int32, (chunk, chunk), 0)
    col = lax.broadcasted_iota(jnp.int32, (chunk, chunk), 1)
    tri = jnp.where(row >= col, 1.0, 0.0).astype(BF16)
    n_levels = chunk.bit_length() - 1
    differ = row ^ col
    on_diag = differ == 0
    at_level = [(lax.shift_right_logical(differ, lv) == 1) & (row > col) for lv in range(n_levels)]
    tok = lax.broadcasted_iota(jnp.int32, (chunk, dk), 0)
    in_right = [(lax.shift_right_logical(tok, lv) & 1) == 1 for lv in range(n_levels)]
    quad = tok & 3
    if n_valid < chunk:
        valid = tok < n_valid
    for h in range(HG_HEADS):
        hs = slice(h * dk, (h + 1) * dk)
        q = q_ref[:, hs].astype(F32)
        zf = zf_ref[:, hs]
        v = i_ref[:, hs]
        lb = lb_ref[:, hs]
        e_abs = jnp.exp(-jnp.abs(zf))
        big = 1.0 / (1.0 + e_abs)
        small = e_abs * big
        nonneg = zf >= 0.0
        log_f = jnp.log(jnp.maximum(lb, TINY) + (1.0 - lb) * jnp.where(nonneg, big, small))
        k = (1.0 - lb) * jnp.where(nonneg, small, big)
        if n_valid < chunk:
            log_f = jnp.where(valid, log_f, 0.0)
            k = jnp.where(valid, k, 0.0)
        g_hi, g_mid, g_lo = _split3(log_f)
        b = (jnp.dot(tri, g_hi, preferred_element_type=F32)
             + jnp.dot(tri, g_mid, preferred_element_type=F32)
             + jnp.dot(tri, g_lo, preferred_element_type=F32))
        b_end = b[chunk - 1:chunk, :]
        a = jnp.where(on_diag, _nt_dot(q.astype(BF16), k.astype(BF16)), 0.0)
        for lv in range(n_levels):
            half = 1 << lv
            if lv == 0:
                dist = jnp.where(in_right[0], log_f, 0.0)
            elif lv == 1:
                edge = jnp.where(quad == 0, pltpu.roll(b, chunk - 1, 0),
                                 jnp.where(quad == 2, pltpu.roll(b, 1, 0),
                                           jnp.where(quad == 3, pltpu.roll(b, 2, 0), b)))
                dist = -jnp.abs(b - edge)
            else:
                nblk = chunk // (2 * half)
                edge = b.reshape(nblk, 2 * half, dk)[:, half - 1:half, :]
                edge = jnp.broadcast_to(edge, (nblk, 2 * half, dk)).reshape(chunk, dk)
                dist = -jnp.abs(b - edge)
            z = (jnp.where(in_right[lv], q, k) * jnp.exp(dist)).astype(BF16)
            a = jnp.where(at_level[lv], _nt_dot(z, z), a)
        s = st_ref[0, h]
        o = (jnp.dot(a.astype(BF16), v, preferred_element_type=F32)
             + jnp.dot((q * jnp.exp(b)).astype(BF16), s.astype(BF16), preferred_element_type=F32))
        k_d = (k * jnp.exp(b_end - b)).T.astype(BF16)
        decay = jnp.broadcast_to(jnp.exp(b_end), (dk, dk)).T
        st_ref[0, h] = decay * s + jnp.dot(k_d, v, preferred_element_type=F32)
        ms = jnp.mean(o * o, axis=-1, keepdims=True)
        gate = g_ref[:, hs].astype(F32)
        o = o * lax.rsqrt(ms + EPS) * gn_ref[...] * (gate * _sigmoid(gate))
        o_ref[:, hs] = o.astype(o_ref.dtype)


def _hgrn_scan(q, zf, i, g, lb, gnorm, s0, nseq, n_valid):
    tt, d = q.shape
    tp = tt // nseq
    nc = tp // SCAN_CHUNK
    dk = d // HG_HEADS
    row = pl.BlockSpec((SCAN_CHUNK, d), lambda b, c: (b * nc + c, 0))
    st = pl.BlockSpec((1, HG_HEADS, dk, dk), lambda b, c: (b, 0, 0, 0))
    body = functools.partial(_hgrn_scan_body, chunk=SCAN_CHUNK, n_valid=min(n_valid, SCAN_CHUNK), dk=dk)
    return pl.pallas_call(
        body,
        grid=(nseq, nc),
        in_specs=[row, row, row, row,
                  pl.BlockSpec((1, d), lambda b, c: (0, 0)),
                  pl.BlockSpec((1, dk), lambda b, c: (0, 0)),
                  st],
        out_specs=[row, st],
        out_shape=[jax.ShapeDtypeStruct((tt, d), BF16), jax.ShapeDtypeStruct(s0.shape, F32)],
        compiler_params=_params(("arbitrary", "arbitrary")),
        name="hgrn_scan",
    )(q, zf, i, g, lb.reshape(1, d), gnorm.reshape(1, dk), s0)


def _route_rows(lt):
    rows = [lt[e:e + 1, :] for e in range(N_EXPERTS)]
    m = functools.reduce(jnp.maximum, rows)
    ex = [jnp.exp(r - m) for r in rows]
    inv = 1.0 / functools.reduce(lambda x, y: x + y, ex)
    pr = [e * inv for e in ex]
    scores = []
    for gi in range(N_GROUPS):
        p4 = pr[gi * EXPERTS_PER_GROUP:(gi + 1) * EXPERTS_PER_GROUP]
        pairs = [p4[x] + p4[y] for x in range(EXPERTS_PER_GROUP) for y in range(x + 1, EXPERTS_PER_GROUP)]
        scores.append(functools.reduce(jnp.maximum, pairs))
    best = scores[0]
    sel = jnp.zeros(best.shape, jnp.int32)
    for gi in range(1, N_GROUPS):
        better = scores[gi] > best
        sel = jnp.where(better, gi, sel)
        best = jnp.where(better, scores[gi], best)
    cand = []
    for j in range(EXPERTS_PER_GROUP):
        cj = pr[(N_GROUPS - 1) * EXPERTS_PER_GROUP + j]
        for gi in range(N_GROUPS - 2, -1, -1):
            cj = jnp.where(sel == gi, pr[gi * EXPERTS_PER_GROUP + j], cj)
        cand.append(cj)

    def argmax4(vals):
        p, idx = vals[0], jnp.zeros(vals[0].shape, jnp.int32)
        for j in range(1, len(vals)):
            better = vals[j] > p
            idx = jnp.where(better, j, idx)
            p = jnp.where(better, vals[j], p)
        return p, idx

    p1, i1 = argmax4(cand)
    p2, i2 = argmax4([jnp.where(i1 == j, -1.0, cand[j]) for j in range(EXPERTS_PER_GROUP)])
    den = p1 + p2
    base = sel * EXPERTS_PER_GROUP
    return base + i1, base + i2, p1 / den, p2 / den


def _mix_router_body(a_ref, w_ref, h_ref, g1_ref, gain_ref, sh_ref, sc_ref, wrh_ref, wrl_ref, br_ref, upper_ref,
                     h_out, hn_out, gate_out, route_out, count_out):
    @pl.when(pl.program_id(0) == 0)
    def _():
        count_out[...] = jnp.zeros_like(count_out)

    mix = jnp.dot(a_ref[...].astype(BF16), w_ref[...], preferred_element_type=F32)
    h = h_ref[...] + g1_ref[...] * mix
    h_out[...] = h
    hn = _norm_mod(h, gain_ref[...], sh_ref[...], sc_ref[...])
    _store_row_tiles(hn_out, hn)
    x_hi = hn.astype(BF16)
    x_lo = (hn - x_hi.astype(F32)).astype(BF16)
    lt = (_nt_dot(wrh_ref[...], x_hi) + _nt_dot(wrl_ref[...], x_hi) + _nt_dot(wrh_ref[...], x_lo)
          + br_ref[...])
    e1, e2, g1, g2 = _route_rows(lt)
    tm = lt.shape[1]
    sub = lax.broadcasted_iota(jnp.int32, (LANES, tm), 0)
    gate_out[...] = (jnp.where(sub == 0, g1, 0.0) + jnp.where(sub == 1, g2, 0.0)).T
    eid = lax.broadcasted_iota(jnp.int32, (N_EXPERTS, tm), 0)
    pick1 = eid == e1
    pick2 = eid == e2
    hits = jnp.where(pick1, 1.0, 0.0) + jnp.where(pick2, 1.0, 0.0)
    before = jnp.dot(hits.astype(BF16), upper_ref[...], preferred_element_type=F32) + count_out[...]
    rank1 = jnp.sum(jnp.where(pick1, before, 0.0), axis=0, keepdims=True)
    rank2 = jnp.sum(jnp.where(pick2, before, 0.0), axis=0, keepdims=True)
    count_out[...] += jnp.sum(hits, axis=1, keepdims=True)
    field = lax.broadcasted_iota(jnp.int32, route_out.shape[1:], 0)
    route_out[0] = jnp.where(field == 0, e1, jnp.where(field == 1, e2, jnp.where(
        field == 2, rank1.astype(jnp.int32), rank2.astype(jnp.int32))))


def _mix_router(a, w_out, h, gain, mod, wr_hi, wr_lo, b_router, tm):
    t, d = h.shape
    row = pl.BlockSpec((tm, d), lambda i: (i, 0))
    const = lambda shape: pl.BlockSpec(shape, lambda i: (0,) * len(shape))
    upper = (jnp.arange(tm)[:, None] < jnp.arange(tm)[None, :]).astype(BF16)
    return pl.pallas_call(
        _mix_router_body,
        grid=(t // tm,),
        in_specs=[row, const((d, d)), row, mod.spec(2), const((1, d)), mod.spec(3), mod.spec(4),
                  const((N_EXPERTS, d)), const((N_EXPERTS, d)), const((N_EXPERTS, 1)), const((tm, tm))],
        out_specs=[row, pl.BlockSpec((tm * SUBLANES, LANES), lambda i: (i, 0)),
                   pl.BlockSpec((tm, LANES), lambda i: (i, 0)),
                   pl.BlockSpec((1, 4, tm), lambda i: (i, 0, 0)), const((N_EXPERTS, 1))],
        out_shape=[jax.ShapeDtypeStruct((t, d), F32), jax.ShapeDtypeStruct((t * SUBLANES, LANES), F32),
                   jax.ShapeDtypeStruct((t, LANES), F32), jax.ShapeDtypeStruct((t // tm, 4, tm), jnp.int32),
                   jax.ShapeDtypeStruct((N_EXPERTS, 1), F32)],
        compiler_params=_params(("arbitrary",)),
        name="mix_router",
    )(a, w_out, h, mod.arr, gain.reshape(1, d), mod.arr, mod.arr, wr_hi, wr_lo,
      b_router.reshape(N_EXPERTS, 1).astype(F32), upper)


def _store_row_tiles(ref, x):
    n, d = x.shape
    assert d == SUBLANES * LANES
    for c in range(SUBLANES):
        ref[pl.ds(c, n, stride=SUBLANES), :] = x[:, c * LANES:(c + 1) * LANES]


def _load_row_tiles(ref):
    n = ref.shape[0] // SUBLANES
    return jnp.concatenate([ref[pl.ds(c, n, stride=SUBLANES), :] for c in range(SUBLANES)], axis=1)


def _tile_rows(first_row, n_rows):
    return pl.ds(pl.multiple_of(first_row * SUBLANES, SUBLANES), n_rows * SUBLANES)


def _row_copy(src, src_row, dst, dst_row, sem):
    return pltpu.make_async_copy(src.at[_tile_rows(src_row, 1)], dst.at[_tile_rows(dst_row, 1)], sem)


def _index_slot(tm):
    return -(-2 * tm // SMEM_I32_CHUNK) * SMEM_I32_CHUNK


def _load_positions(pos_hbm, idx_ref, sem, step):
    n = idx_ref.shape[0]
    cp = pltpu.make_async_copy(pos_hbm.at[pl.ds(pl.multiple_of(step * n, n), n)], idx_ref, sem)
    cp.start()
    cp.wait()


def _dispatch_body(seg_end_ref, pos_hbm, x_ref, xs_out, idx_ref, zero_ref, idx_sem, row_sem, *, tm):
    @pl.when(pl.program_id(0) == 0)
    def _():
        zero_ref[...] = jnp.zeros_like(zero_ref)

        def last_tile(e):
            return pltpu.make_async_copy(
                zero_ref, xs_out.at[_tile_rows(seg_end_ref[e] - EXPERT_TILE, EXPERT_TILE)], row_sem)

        def nonempty(e):
            return seg_end_ref[e] > (seg_end_ref[e - 1] if e else 0)

        for e in range(N_EXPERTS):
            pl.when(nonempty(e))(lambda e=e: last_tile(e).start())
        for e in range(N_EXPERTS):
            pl.when(nonempty(e))(lambda e=e: last_tile(e).wait())

        def unused_tile(i, carry):
            cp = pltpu.make_async_copy(zero_ref, xs_out.at[_tile_rows(i * EXPERT_TILE, EXPERT_TILE)], row_sem)
            cp.start()
            cp.wait()
            return carry

        n_tiles = xs_out.shape[0] // (EXPERT_TILE * SUBLANES)
        lax.fori_loop(seg_end_ref[N_EXPERTS - 1] // EXPERT_TILE, n_tiles, unused_tile, 0)

    _load_positions(pos_hbm, idx_ref, idx_sem, pl.program_id(0))

    def start(t, carry):
        _row_copy(x_ref, t, xs_out, idx_ref[t], row_sem).start(priority=0)
        _row_copy(x_ref, t, xs_out, idx_ref[tm + t], row_sem).start(priority=1)
        return carry

    lax.fori_loop(0, tm, start, 0, unroll=ROW_COPY_UNROLL)
    for _ in range(2):
        pltpu.make_async_copy(x_ref, xs_out.at[_tile_rows(0, tm)], row_sem).wait()


def _dispatch(hn, pos_tiles, seg_end, n_rows, tm):
    t = hn.shape[0] // SUBLANES
    grid_spec = pltpu.PrefetchScalarGridSpec(
        num_scalar_prefetch=1,
        grid=(t // tm,),
        in_specs=[pl.BlockSpec(memory_space=pl.ANY), pl.BlockSpec((tm * SUBLANES, LANES), lambda i, se: (i, 0))],
        out_specs=pl.BlockSpec(memory_space=pl.ANY),
        scratch_shapes=[pltpu.SMEM((_index_slot(tm),), jnp.int32),
                        pltpu.VMEM((EXPERT_TILE * SUBLANES, LANES), F32),
                        pltpu.SemaphoreType.DMA, pltpu.SemaphoreType.DMA])
    return pl.pallas_call(
        functools.partial(_dispatch_body, tm=tm),
        grid_spec=grid_spec,
        out_shape=jax.ShapeDtypeStruct((n_rows * SUBLANES, LANES), F32),
        compiler_params=_params(("arbitrary",)),
        name="moe_dispatch",
    )(seg_end, pos_tiles, hn)


def _expert_body(te_ref, nu_ref, x_ref, win_ref, wout_ref, y_ref):
    @pl.when(pl.program_id(0) < nu_ref[0])
    def _():
        hid = jnp.dot(_load_row_tiles(x_ref).astype(BF16), win_ref[0].astype(BF16), preferred_element_type=F32)
        de = hid.shape[1] // 2
        a, u = hid[:, :de], hid[:, de:]
        act = (a * _sigmoid(a) * u).astype(BF16)
        _store_row_tiles(y_ref, jnp.dot(act, wout_ref[0].astype(BF16), preferred_element_type=F32))

    @pl.when(pl.program_id(0) >= nu_ref[0])
    def _():
        y_ref[...] = jnp.zeros_like(y_ref)


def _experts(xs, tile_expert, n_used, w_in, w_out):
    n_e, d, d2 = w_in.shape
    n_tiles = xs.shape[0] // (EXPERT_TILE * SUBLANES)
    tile = lambda i, te, nu: jnp.minimum(i, nu[0] - 1)
    rows = (EXPERT_TILE * SUBLANES, LANES)
    grid_spec = pltpu.PrefetchScalarGridSpec(
        num_scalar_prefetch=2,
        grid=(n_tiles,),
        in_specs=[pl.BlockSpec(rows, lambda i, te, nu: (tile(i, te, nu), 0)),
                  pl.BlockSpec((1, d, d2), lambda i, te, nu: (te[tile(i, te, nu)], 0, 0)),
                  pl.BlockSpec((1, d2 // 2, d), lambda i, te, nu: (te[tile(i, te, nu)], 0, 0))],
        out_specs=pl.BlockSpec(rows, lambda i, te, nu: (i, 0)))
    return pl.pallas_call(
        _expert_body,
        grid_spec=grid_spec,
        out_shape=jax.ShapeDtypeStruct(xs.shape, F32),
        compiler_params=_params(("arbitrary",)),
        name="moe_experts",
    )(tile_expert, n_used, xs, w_in, w_out)


def _combine_body(pos_hbm, ys_hbm, gate_ref, h_ref, g2_ref, o_ref, idx_ref, y1_ref, y2_ref, idx_sem, row_sem,
                  *, tm):
    _load_positions(pos_hbm, idx_ref, idx_sem, pl.program_id(0))

    def start(t, carry):
        _row_copy(ys_hbm, idx_ref[t], y1_ref, t, row_sem).start(priority=0)
        _row_copy(ys_hbm, idx_ref[tm + t], y2_ref, t, row_sem).start(priority=1)
        return carry

    lax.fori_loop(0, tm, start, 0, unroll=ROW_COPY_UNROLL)
    for y_ref in (y1_ref, y2_ref):
        pltpu.make_async_copy(ys_hbm.at[_tile_rows(0, tm)], y_ref, row_sem).wait()
    gates = gate_ref[...]
    moe = gates[:, 0:1] * _load_row_tiles(y1_ref) + gates[:, 1:2] * _load_row_tiles(y2_ref)
    o_ref[...] = h_ref[...] + g2_ref[...] * moe


def _combine(ys, pos_tiles, gates, h, mod, tm):
    t, d = h.shape
    row = pl.BlockSpec((tm, d), lambda i: (i, 0))
    return pl.pallas_call(
        functools.partial(_combine_body, tm=tm),
        grid=(t // tm,),
        in_specs=[pl.BlockSpec(memory_space=pl.ANY), pl.BlockSpec(memory_space=pl.ANY),
                  pl.BlockSpec((tm, LANES), lambda i: (i, 0)), row, mod.spec(5)],
        out_specs=row,
        out_shape=jax.ShapeDtypeStruct((t, d), F32),
        scratch_shapes=[pltpu.SMEM((_index_slot(tm),), jnp.int32),
                        pltpu.VMEM((tm * SUBLANES, LANES), F32), pltpu.VMEM((tm * SUBLANES, LANES), F32),
                        pltpu.SemaphoreType.DMA, pltpu.SemaphoreType.DMA],
        compiler_params=_params(("arbitrary",)),
        name="moe_combine",
    )(pos_tiles, ys, gates, h, mod.arr)


def _moe(hn, gates, route, counts, h, mod, w_in, w_out, tm):
    t, d = h.shape
    n_tiles = 2 * t // EXPERT_TILE + N_EXPERTS
    n_rows = n_tiles * EXPERT_TILE
    cnt = counts.reshape(N_EXPERTS).astype(jnp.int32)
    padded = (cnt + (EXPERT_TILE - 1)) // EXPERT_TILE * EXPERT_TILE
    experts = jnp.arange(N_EXPERTS, dtype=jnp.int32)
    seg_end = jnp.sum(jnp.where(experts[None, :] <= experts[:, None], padded[None, :], 0), axis=1)
    seg_start = seg_end - padded
    first_row = jnp.sum(jnp.where(route[:, :2, :, None] == experts, seg_start, 0), axis=-1)
    pos_tiles = (first_row + route[:, 2:]).reshape(t // tm, 2 * tm)
    if _index_slot(tm) != 2 * tm:
        pos_tiles = jnp.pad(pos_tiles, ((0, 0), (0, _index_slot(tm) - 2 * tm)))
    pos_tiles = pos_tiles.reshape(-1)
    tile_row = jnp.arange(n_tiles, dtype=jnp.int32) * EXPERT_TILE
    tile_expert = jnp.minimum(jnp.sum((seg_end[None, :] <= tile_row[:, None]).astype(jnp.int32), axis=1),
                              N_EXPERTS - 1)
    n_used = seg_end[-1:] // EXPERT_TILE
    xs = _dispatch(hn, pos_tiles, seg_end, n_rows, tm)
    ys = _experts(xs, tile_expert, n_used, w_in, w_out)
    return _combine(ys, pos_tiles, gates, h, mod, tm)


def _head_norm(y, gsum_ref, gain_ref, head_dim):
    ssum = jnp.dot((y * y).astype(BF16), gsum_ref[...], preferred_element_type=F32)
    return y * lax.rsqrt(ssum * (1.0 / head_dim) + EPS) * gain_ref[...]


def _head_sum_matrix(d, head_dim):
    r = jnp.arange(d) // head_dim
    return (r[:, None] == r[None, :]).astype(BF16)


def _q_proj_body(x_ref, gain_ref, sh_ref, sc_ref, w_ref, gsum_ref, qn_ref, q_ref, *, head_dim, q_scale):
    xn = _norm_mod(x_ref[...], gain_ref[...], sh_ref[...], sc_ref[...]).astype(BF16)
    y = jnp.dot(xn, w_ref[...], preferred_element_type=F32)
    q = _head_norm(y, gsum_ref, qn_ref, head_dim) * q_scale
    q_ref[...] = q.astype(q_ref.dtype)


def _q_proj(h, gain, mod, w, gsum, q_norm, tm, out_dtype, q_scale):
    t, d = h.shape
    head_dim = d // FOX_HEADS
    row = pl.BlockSpec((tm, d), lambda i: (i, 0))
    const = lambda shape: pl.BlockSpec(shape, lambda i: (0,) * len(shape))
    return pl.pallas_call(
        functools.partial(_q_proj_body, head_dim=head_dim, q_scale=q_scale),
        grid=(t // tm,),
        in_specs=[row, const((1, d)), mod.spec(0), mod.spec(1), const((d, d)), const((d, d)), const((1, d))],
        out_specs=row,
        out_shape=jax.ShapeDtypeStruct((t, d), out_dtype),
        compiler_params=_params(("arbitrary",)),
        name="q_proj",
    )(h, gain.reshape(1, d), mod.arr, mod.arr, w, gsum, jnp.tile(q_norm, FOX_HEADS).reshape(1, d))


def _kv_proj_body(x_ref, gain_ref, sh_ref, sc_ref, wk_ref, wv_ref, wf_ref, gsum_ref, kn_ref, bf_ref,
                  k_ref, v_ref, kb_ref, vb_ref, lf_ref, *, head_dim):
    xn = _norm_mod(x_ref[...], gain_ref[...], sh_ref[...], sc_ref[...]).astype(BF16)
    k = _head_norm(jnp.dot(xn, wk_ref[...], preferred_element_type=F32), gsum_ref, kn_ref, head_dim)
    k_ref[...] = k
    kb_ref[...] = k.astype(BF16)
    v = jnp.dot(xn, wv_ref[...], preferred_element_type=F32)
    v_ref[...] = v
    vb_ref[...] = v.astype(BF16)
    zf = jnp.dot(xn, wf_ref[...], preferred_element_type=F32) + bf_ref[...]
    lf_ref[...] = _log_sigmoid(zf)[:, :lf_ref.shape[1]]


def _kv_proj(h, gain, mod, wk, wv, wf, gsum, k_norm, b_fgate, tm):
    t, d = h.shape
    head_dim = d // FOX_HEADS
    row = pl.BlockSpec((tm, d), lambda i: (i, 0))
    const = lambda shape: pl.BlockSpec(shape, lambda i: (0,) * len(shape))
    bf = jnp.pad(b_fgate.astype(F32), (0, LANES - FOX_HEADS)).reshape(1, LANES)
    return pl.pallas_call(
        functools.partial(_kv_proj_body, head_dim=head_dim),
        grid=(t // tm,),
        in_specs=[row, const((1, d)), mod.spec(0), mod.spec(1), const((d, d)), const((d, d)),
                  const((d, LANES)), const((d, d)), const((1, d)), const((1, LANES))],
        out_specs=[row, row, row, row, pl.BlockSpec((tm, FOX_HEADS), lambda i: (i, 0))],
        out_shape=[jax.ShapeDtypeStruct((t, d), F32), jax.ShapeDtypeStruct((t, d), F32),
                   jax.ShapeDtypeStruct((t, d), BF16), jax.ShapeDtypeStruct((t, d), BF16),
                   jax.ShapeDtypeStruct((t, FOX_HEADS), F32)],
        compiler_params=_params(("arbitrary",)),
        name="kv_proj",
    )(h, gain.reshape(1, d), mod.arr, mod.arr, wk, wv, wf, gsum,
      jnp.tile(k_norm, FOX_HEADS).reshape(1, d), bf)


def _cumsum_lanes_body(x_ref, o_ref, carry_ref, *, carry_rows):
    j = pl.program_id(1)

    @pl.when(j == 0)
    def _():
        carry_ref[...] = jnp.zeros_like(carry_ref)

    w = x_ref.shape[-1]
    r = lax.broadcasted_iota(jnp.int32, (w, w), 0)
    c = lax.broadcasted_iota(jnp.int32, (w, w), 1)
    upper = jnp.where(r <= c, 1.0, 0.0).astype(F32)
    y = jnp.dot(x_ref[0], upper, precision=HIGHEST, preferred_element_type=F32)
    if carry_rows:
        y = y + carry_ref[...]
        carry_ref[...] = y[:, w - 1:w]
    o_ref[0] = y


def _cumsum_lanes(x, width, carry):
    b, r, t = x.shape
    spec = pl.BlockSpec((1, r, width), lambda i, j: (i, 0, j))
    return pl.pallas_call(
        functools.partial(_cumsum_lanes_body, carry_rows=carry),
        grid=(b, t // width),
        in_specs=[spec],
        out_specs=spec,
        out_shape=jax.ShapeDtypeStruct(x.shape, F32),
        scratch_shapes=[pltpu.VMEM((r, 1), F32)],
        compiler_params=_params(("arbitrary", "arbitrary")),
        name="cumsum_lanes",
    )(x)


def _attn_prompt_body(qt_ref, kt_ref, q_ref, k_ref, v_ref, f_ref, o_ref, m_ref, acc_ref,
                      *, blk, rows, diag_rows, head_dim):
    p = pl.program_id(2)
    qi = qt_ref[p]
    ki = kt_ref[p]

    @pl.when(ki == 0)
    def _():
        m_ref[...] = jnp.full(m_ref.shape, NEG_BIG, F32)
        acc_ref[...] = jnp.zeros_like(acc_ref)

    def process(masked):
        k = k_ref[...]
        v = v_ref[...]
        f = f_ref[...]
        first_v = lax.broadcasted_iota(jnp.int32, v.shape, 1) < head_dim
        ones = jnp.ones_like(v)
        v_heads = (jnp.where(first_v, v, ones), jnp.where(first_v, ones, v))
        rb = min(rows, diag_rows) if masked else rows
        m_olds = {(hh, r0): m_ref[hh, r0:r0 + rb, :] for hh in range(2) for r0 in range(0, blk, rb)}
        acc_olds = {(hh, r0): acc_ref[hh, r0:r0 + rb, :] for hh in range(2) for r0 in range(0, blk, rb)}
        m_news, acc_news = {}, {}
        for r0 in range(0, blk, rb):
            n_keys = r0 + rb if masked else blk
            q = q_ref[r0:r0 + rb, :]
            first_q = lax.broadcasted_iota(jnp.int32, q.shape, 1) < head_dim
            zero = jnp.zeros_like(q)
            for hh, qh in enumerate((jnp.where(first_q, q, zero), jnp.where(first_q, zero, q))):
                s = _nt_dot(qh, k[:n_keys]) - f[hh:hh + 1, :n_keys]
                if masked:
                    row = lax.broadcasted_iota(jnp.int32, s.shape, 0) + r0
                    col = lax.broadcasted_iota(jnp.int32, s.shape, 1)
                    s = jnp.where(col <= row, s, NEG_BIG)
                m_old = m_olds[hh, r0]
                m_new = jnp.maximum(m_old, jnp.max(s, axis=1, keepdims=True))
                alpha = jnp.exp2(m_old - m_new)
                pr = jnp.concatenate([jnp.exp2(s[:, c * LANES:(c + 1) * LANES] - m_new)
                                      for c in range(n_keys // LANES)], axis=1).astype(BF16)
                acc_news[hh, r0] = alpha * acc_olds[hh, r0] + jnp.dot(pr, v_heads[hh][:n_keys],
                                                                      preferred_element_type=F32)
                m_news[hh, r0] = m_new
        for (hh, r0), m_new in m_news.items():
            m_ref[hh, r0:r0 + rb, :] = m_new
            acc_ref[hh, r0:r0 + rb, :] = acc_news[hh, r0]

    @pl.when(ki < qi)
    def _():
        process(False)

    @pl.when(ki == qi)
    def _():
        process(True)
        a0 = acc_ref[0]
        a1 = acc_ref[1]
        first = lax.broadcasted_iota(jnp.int32, a0.shape, 1) < head_dim
        inv0 = 1.0 / a0[:, head_dim:head_dim + 1]
        inv1 = 1.0 / a1[:, 0:1]
        o_ref[...] = jnp.where(first, a0 * inv0, a1 * inv1).astype(o_ref.dtype)


def _attn_prompt(q, k, v, f_pairs, nseq):
    tt, d = q.shape
    t = tt // nseq
    blk = min(ATTN_BLOCK, t)
    rows = min(ATTN_ROWS, blk)
    nb = t // blk
    head_dim = d // FOX_HEADS
    pairs = [(qi, ki) for qi in range(nb) for ki in range(qi + 1)]
    q_tab = jnp.asarray([p[0] for p in pairs], jnp.int32)
    k_tab = jnp.asarray([p[1] for p in pairs], jnp.int32)
    grid_spec = pltpu.PrefetchScalarGridSpec(
        num_scalar_prefetch=2,
        grid=(nseq, d // LANES, len(pairs)),
        in_specs=[pl.BlockSpec((blk, LANES), lambda b, hp, p, qt, kt: (b * nb + qt[p], hp)),
                  pl.BlockSpec((blk, LANES), lambda b, hp, p, qt, kt: (b * nb + kt[p], hp)),
                  pl.BlockSpec((blk, LANES), lambda b, hp, p, qt, kt: (b * nb + kt[p], hp)),
                  pl.BlockSpec((None, None, 2, blk), lambda b, hp, p, qt, kt: (b, hp, 0, kt[p]))],
        out_specs=pl.BlockSpec((blk, LANES), lambda b, hp, p, qt, kt: (b * nb + qt[p], hp)),
        scratch_shapes=[pltpu.VMEM((2, blk, LANES), F32), pltpu.VMEM((2, blk, LANES), F32)])
    return pl.pallas_call(
        functools.partial(_attn_prompt_body, blk=blk, rows=rows, diag_rows=min(ATTN_DIAG_ROWS, blk),
                          head_dim=head_dim),
        grid_spec=grid_spec,
        out_shape=jax.ShapeDtypeStruct((tt, d), BF16),
        compiler_params=_params(("arbitrary", "arbitrary", "arbitrary")),
        name="attn_prompt",
    )(q_tab, k_tab, q, k, v, f_pairs)


def _gather_cum_body(pt_ref, cp_ref, new_ref, o_ref, *, n_pages, page):
    b = pl.program_id(0)

    def step(j, carry):
        blk = cp_ref[pt_ref[b * n_pages + j]]
        o_ref[0, :, pl.ds(pl.multiple_of(j * page, page), page)] = blk + carry
        return carry + blk[:, page - 1:page]

    carry = lax.fori_loop(0, n_pages, step, jnp.zeros((cp_ref.shape[1], 1), F32), unroll=8)
    o_ref[0, :, n_pages * page:] = new_ref[0] + carry


def _gather_cum(page_table, cum_pages, cum_new):
    nseq, n_pages = page_table.shape
    n_pool, hh, page = cum_pages.shape
    grid_spec = pltpu.PrefetchScalarGridSpec(
        num_scalar_prefetch=1,
        grid=(nseq,),
        in_specs=[pl.BlockSpec((n_pool, hh, page), lambda b, pt: (0, 0, 0)),
                  pl.BlockSpec((1, hh, page), lambda b, pt: (b, 0, 0))],
        out_specs=pl.BlockSpec((1, hh, (n_pages + 1) * page), lambda b, pt: (b, 0, 0)))
    return pl.pallas_call(
        functools.partial(_gather_cum_body, n_pages=n_pages, page=page),
        grid_spec=grid_spec,
        out_shape=jax.ShapeDtypeStruct((nseq, hh, (n_pages + 1) * page), F32),
        compiler_params=_params(("arbitrary",)),
        name="gather_cum",
    )(page_table.reshape(-1), cum_pages, cum_new)


def _attn_paged_body(pt_ref, *refs, n_q, page, pps, n_steps, head_dim):
    q_ref = refs[0]
    k_refs = refs[1:1 + pps]
    v_refs = refs[1 + pps:1 + 2 * pps]
    kn_ref, vn_ref, f_ref, o_ref, qbd_ref, m_ref, l_ref, acc_ref = refs[1 + 2 * pps:]
    j = pl.program_id(1)
    d = q_ref.shape[1]
    rows = FOX_HEADS * n_q

    @pl.when(j == 0)
    def _():
        q = q_ref[...]
        lane_head = lax.broadcasted_iota(jnp.int32, q.shape, 1) // head_dim
        for h in range(FOX_HEADS):
            qbd_ref[h * n_q:(h + 1) * n_q, :] = jnp.where(lane_head == h, q, 0.0).astype(BF16)
        m_ref[...] = jnp.full(m_ref.shape, NEG_BIG, F32)
        l_ref[...] = jnp.zeros_like(l_ref)
        acc_ref[...] = jnp.zeros_like(acc_ref)

    def attend(k, v, f, mask):
        bias = jnp.concatenate([jnp.broadcast_to(f[h:h + 1, :], (n_q, page)) for h in range(FOX_HEADS)], axis=0)
        s = _nt_dot(qbd_ref[...], k) - bias
        if mask is not None:
            s = jnp.where(mask, s, NEG_BIG)
        m_old = m_ref[...]
        m_new = jnp.maximum(m_old, jnp.max(s, axis=1, keepdims=True))
        alpha = jnp.exp(m_old - m_new)
        pr = jnp.exp(s - m_new)
        l_ref[...] = alpha * l_ref[...] + jnp.sum(pr, axis=1, keepdims=True)
        m_ref[...] = m_new
        acc_ref[...] = alpha * acc_ref[...] + jnp.dot(pr.astype(BF16), v, preferred_element_type=F32)

    @pl.when(j < n_steps - 1)
    def _():
        for u in range(pps):
            off = pl.multiple_of((j * pps + u) * page, page)
            attend(k_refs[u][...].astype(BF16), v_refs[u][...].astype(BF16), f_ref[0, :, pl.ds(off, page)], None)

    @pl.when(j == n_steps - 1)
    def _():
        pad = jnp.zeros((page - n_q, d), F32)
        k = jnp.concatenate([kn_ref[...], pad], axis=0).astype(BF16)
        v = jnp.concatenate([vn_ref[...], pad], axis=0).astype(BF16)
        t_of_row = lax.broadcasted_iota(jnp.int32, (rows, page), 0) % n_q
        key = lax.broadcasted_iota(jnp.int32, (rows, page), 1)
        attend(k, v, f_ref[0, :, (n_steps - 1) * pps * page:], key <= t_of_row)
        acc = acc_ref[...] * (1.0 / l_ref[...])
        lane_head = lax.broadcasted_iota(jnp.int32, (n_q, d), 1) // head_dim
        out = jnp.zeros((n_q, d), F32)
        for h in range(FOX_HEADS):
            out = out + jnp.where(lane_head == h, acc[h * n_q:(h + 1) * n_q, :], 0.0)
        o_ref[...] = out


def _attn_paged(q, cache_k, cache_v, k_new, v_new, f_all, page_table):
    tt, d = q.shape
    nseq, n_pages = page_table.shape
    n_q = tt // nseq
    page = cache_k.shape[1]
    pps = PAGES_PER_STEP
    n_steps = n_pages // pps + 1
    head_dim = d // FOX_HEADS
    rows = FOX_HEADS * n_q

    def page_spec(u):
        def index(b, j, pt):
            return (pt[b * n_pages + jnp.minimum(j, n_steps - 2) * pps + u], 0, 0)
        return pl.BlockSpec((None, page, d), index)

    tok = pl.BlockSpec((n_q, d), lambda b, j, pt: (b, 0))
    grid_spec = pltpu.PrefetchScalarGridSpec(
        num_scalar_prefetch=1,
        grid=(nseq, n_steps),
        in_specs=([tok] + [page_spec(u) for u in range(pps)] + [page_spec(u) for u in range(pps)]
                  + [tok, tok, pl.BlockSpec((1, FOX_HEADS, f_all.shape[2]), lambda b, j, pt: (b, 0, 0))]),
        out_specs=tok,
        scratch_shapes=[pltpu.VMEM((rows, d), BF16), pltpu.VMEM((rows, 1), F32),
                        pltpu.VMEM((rows, 1), F32), pltpu.VMEM((rows, d), F32)])
    body = functools.partial(_attn_paged_body, n_q=n_q, page=page, pps=pps, n_steps=n_steps,
                             head_dim=head_dim)
    return pl.pallas_call(
        body,
        grid_spec=grid_spec,
        out_shape=jax.ShapeDtypeStruct((tt, d), F32),
        compiler_params=_params(("arbitrary", "arbitrary")),
        name="attn_paged",
    )(page_table.reshape(-1), q, *([cache_k] * pps), *([cache_v] * pps), k_new, v_new, f_all)


def _lower_bounds(lb_logits):
    p = jax.nn.softmax(lb_logits.astype(F32), axis=0)
    return jnp.cumsum(p, axis=0) - p[0]


def _trunk(x, nseq, mods, mod_kv, s0, past, w):
    tt, d = x.shape
    t = tt // nseq
    tm = min(512, tt)
    tiles_per_seq = max(t // tm, 1)
    n_a = w["w_in_a"].shape[0]
    depth = w["w_ada"].shape[0]
    head_dim = d // FOX_HEADS
    lbs = _lower_bounds(w["lb_logits"])
    mk = lambda arr: _Mod(arr, d, tm, tiles_per_seq)

    h = x
    states = []
    for l in range(depth):
        mod = mk(mods[l])
        if l < n_a:
            q, zf, i, g = _hgrn_in_proj(h, w["norm_mix"][l], mod, w["w_in_a4"][l], tm)
            tp = -(-t // SCAN_CHUNK) * SCAN_CHUNK
            if tp != t:
                padr = lambda a: jnp.pad(a.reshape(nseq, t, d), ((0, 0), (0, tp - t), (0, 0))).reshape(nseq * tp, d)
                q, zf, i, g = padr(q), padr(zf), padr(i), padr(g)
            o, s_t = _hgrn_scan(q, zf, i, g, lbs[l], w["gnorm_a"][l], s0[l], nseq, t)
            if tp != t:
                o = o.reshape(nseq, tp, d)[:, :t].reshape(tt, d)
            states.append(s_t)
            mix_in, w_out = o, w["w_out_a"][l]
        else:
            j = l - n_a
            if past is None:
                q = _q_proj(h, w["norm_mix"][l], mod, w["w_q_b"][j], w["gsum"], w["q_norm"][j], tm, BF16,
                            head_dim ** -0.5 * LOG2E)
                mix_in = _attn_prompt(q, k_bf, v_bf, f_keys, nseq)
            else:
                q = _q_proj(h, w["norm_mix"][l], mod, w["w_q_b"][j], w["gsum"], w["q_norm"][j], tm, F32,
                            head_dim ** -0.5)
                mix_in = _attn_paged(q, past[0], past[1], k_new, v_new, f_keys, past[3])
            w_out = w["w_out_b"][j]
        h, hn, gates, route, counts = _mix_router(mix_in, w_out, h, w["norm_ffn"][l], mod, w["wr_hi"],
                                                  w["wr_lo"], w["b_router"], tm)
        h = _moe(hn, gates, route, counts, h, mod, w["w_exp_in"][l], w["w_exp_out"][l], tm)
        if l == n_a - 1:
            k_new, v_new, k_bf, v_bf, logf_new = _kv_proj(
                h, w["norm_kv"], mk(mod_kv), w["w_k"], w["w_v"], w["w_f"], w["gsum"], w["k_norm"],
                w["b_fgate"], tm)
            lf_t = logf_new.reshape(nseq, t, FOX_HEADS).transpose(0, 2, 1)
            if past is None:
                f_cum = _cumsum_lanes(lf_t, min(512, t), carry=True)
                f_keys = (f_cum * LOG2E).reshape(nseq, FOX_HEADS // 2, 2, t)
            else:
                cache_logf, page_table = past[2], past[3]
                page = cache_logf.shape[1]
                pool_t = cache_logf.astype(F32).transpose(0, 2, 1)
                n_pool = pool_t.shape[0]
                rows = 64 if n_pool % 64 == 0 else 1
                cum_pages = _cumsum_lanes(pool_t.reshape(n_pool // rows, rows * FOX_HEADS, page), page,
                                          carry=False).reshape(n_pool, FOX_HEADS, page)
                new_pad = jnp.pad(lf_t, ((0, 0), (0, 0), (0, page - t)))
                cum_new = _cumsum_lanes(new_pad, page, carry=False)
                f_keys = _gather_cum(page_table, cum_pages, cum_new)
    return h, jnp.stack(states), k_new, v_new, logf_new


def kernel(x_prompt, x_sample, cache_k, cache_v, cache_logf, state_hgrn, page_table, c_prompt, c_sample,
           w_ada, b_ada, norm_mix, norm_ffn, w_in_a, lb_logits, gnorm_a, w_out_a, norm_kv, w_ada_kv,
           b_ada_kv, w_kv, b_fgate, k_norm, w_q_b, q_norm, w_out_b, w_router, b_router, w_exp_in, w_exp_out):
    nb, seq, d = x_prompt.shape
    ns, dseq, _ = x_sample.shape
    n_a = w_in_a.shape[0]
    depth = w_ada.shape[0]
    hd = FOX_HEADS * (d // FOX_HEADS)
    dk = d // HG_HEADS

    nrow = nb + ns
    nrow_pad = -(-nrow // SUBLANES) * SUBLANES
    c_all = jnp.pad(jnp.concatenate([c_prompt, c_sample], axis=0).astype(F32), ((0, nrow_pad - nrow), (0, 0)))
    mod_all = _ada(c_all, w_ada, b_ada)
    mod_kv_all = _ada(c_all, w_ada_kv[None], b_ada_kv[None])[0]
    mods_p = [mod_all[l, :nb].reshape(nb, 1, 6 * d) for l in range(depth)]
    mods_s = [jnp.repeat(mod_all[l, nb:nrow], dseq, axis=0) for l in range(depth)]
    mod_kv_p = mod_kv_all[:nb].reshape(nb, 1, 2 * d)
    mod_kv_s = jnp.repeat(mod_kv_all[nb:nrow], dseq, axis=0)

    wr_t = w_router.astype(F32).T
    wr_hi = wr_t.astype(BF16)
    wr_lo = (wr_t - wr_hi.astype(F32)).astype(BF16)
    w_in_b = w_in_a.astype(BF16)
    w = dict(
        w_ada=w_ada, lb_logits=lb_logits, norm_mix=norm_mix, norm_ffn=norm_ffn, gnorm_a=gnorm_a,
        w_in_a=w_in_a,
        w_in_a4=[[w_in_b[l, :, s * d:(s + 1) * d] for s in range(4)] for l in range(n_a)],
        w_out_a=w_out_a.astype(BF16), norm_kv=norm_kv,
        w_k=w_kv[:, :hd].astype(BF16), w_v=w_kv[:, hd:2 * hd].astype(BF16),
        w_f=jnp.pad(w_kv[:, 2 * hd:], ((0, 0), (0, LANES - FOX_HEADS))).astype(BF16),
        b_fgate=b_fgate, k_norm=k_norm, w_q_b=w_q_b.astype(BF16), q_norm=q_norm,
        w_out_b=w_out_b.astype(BF16), wr_hi=wr_hi, wr_lo=wr_lo, b_router=b_router,
        w_exp_in=w_exp_in, w_exp_out=w_exp_out,
        gsum=_head_sum_matrix(d, d // FOX_HEADS),
    )

    s0_p = jnp.zeros((n_a, nb, HG_HEADS, dk, dk), F32)
    y_p, st_p, k_p, v_p, lf_p = _trunk(x_prompt.reshape(nb * seq, d), nb, mods_p, mod_kv_p, s0_p, None, w)

    n_pool, page = cache_k.shape[0], cache_k.shape[1]
    past = (cache_k.reshape(n_pool, page, hd), cache_v.reshape(n_pool, page, hd), cache_logf, page_table)
    y_s, st_s, k_s, v_s, lf_s = _trunk(x_sample.reshape(ns * dseq, d), ns, mods_s, mod_kv_s,
                                       state_hgrn.astype(F32), past, w)

    hs = (FOX_HEADS, d // FOX_HEADS)
    return (y_p.reshape(nb, seq, d), y_s.reshape(ns, dseq, d),
            st_p.astype(state_hgrn.dtype), st_s.astype(state_hgrn.dtype),
            k_p.reshape(nb, seq, *hs).astype(cache_k.dtype), v_p.reshape(nb, seq, *hs).astype(cache_v.dtype),
            lf_p.reshape(nb, seq, FOX_HEADS).astype(cache_logf.dtype),
            k_s.reshape(ns, dseq, *hs).astype(cache_k.dtype), v_s.reshape(ns, dseq, *hs).astype(cache_v.dtype),
            lf_s.reshape(ns, dseq, FOX_HEADS).astype(cache_logf.dtype))
```

```python
import functools
import math

import jax
import jax.numpy as jnp
from jax import lax
from jax.experimental import pallas as pl
from jax.experimental.pallas import tpu as pltpu

F32 = jnp.float32
BF16 = jnp.bfloat16
HIGHEST = lax.Precision.HIGHEST

HG_HEADS = 8
FOX_HEADS = 16
N_EXPERTS = 16
N_GROUPS = 4
EXPERTS_PER_GROUP = N_EXPERTS // N_GROUPS
EPS = 1e-6
NEG_BIG = -1e30
TINY = 1e-30

LANES = 128
SUBLANES = 8
VMEM_LIMIT_BYTES = 56 * 1024 * 1024

SCAN_CHUNK = 128
ATTN_BLOCK = 1024
ATTN_ROWS = 1024
ATTN_DIAG_ROWS = 512
LOG2E = math.log2(math.e)
PAGES_PER_STEP = 8
EXPERT_TILE = 256
SMEM_I32_CHUNK = 1024
ROW_COPY_UNROLL = 8


def _params(sem):
    return pltpu.CompilerParams(dimension_semantics=sem, vmem_limit_bytes=VMEM_LIMIT_BYTES)


def _nt_dot(a, b):
    return lax.dot_general(a, b, (((1,), (1,)), ((), ())), preferred_element_type=F32)


def _sigmoid(x):
    return 1.0 / (1.0 + jnp.exp(-x))


def _log_sigmoid(x):
    return jnp.minimum(x, 0.0) - jnp.log1p(jnp.exp(-jnp.abs(x)))


def _norm_mod(x, gain, shift, scale):
    ms = jnp.mean(x * x, axis=-1, keepdims=True)
    y = x * lax.rsqrt(ms + EPS) * gain
    return y * (1.0 + scale) + shift


class _Mod:
    def __init__(self, arr, d, tm, tiles_per_seq):
        self.arr = arr
        self.d = d
        self.tm = tm
        self.tiles_per_seq = tiles_per_seq
        self.per_token = arr.ndim == 2

    def spec(self, chunk):
        d, tps = self.d, self.tiles_per_seq
        if self.per_token:
            return pl.BlockSpec((self.tm, d), lambda i, *_: (i, chunk))
        return pl.BlockSpec((None, 1, d), lambda i, *_: (i // tps, 0, chunk))


def _ada_body(c_ref, w_ref, b_ref, o_ref):
    c = c_ref[...]
    a = c * _sigmoid(c)
    o_ref[0] = jnp.dot(a, w_ref[0], precision=HIGHEST, preferred_element_type=F32) + b_ref[0]


def _ada(c, w, b):
    n_layers, d, n = w.shape
    r = c.shape[0]
    tn = 1536 if n % 1536 == 0 else 1024
    return pl.pallas_call(
        _ada_body,
        grid=(n_layers, n // tn),
        in_specs=[pl.BlockSpec((r, d), lambda l, j: (0, 0)),
                  pl.BlockSpec((1, d, tn), lambda l, j: (l, 0, j)),
                  pl.BlockSpec((1, 1, tn), lambda l, j: (l, 0, j))],
        out_specs=pl.BlockSpec((1, r, tn), lambda l, j: (l, 0, j)),
        out_shape=jax.ShapeDtypeStruct((n_layers, r, n), F32),
        compiler_params=_params(("arbitrary", "arbitrary")),
        name="ada",
    )(c, w, b.reshape(n_layers, 1, n))


def _hgrn_in_body(x_ref, gain_ref, sh_ref, sc_ref, wq_ref, wf_ref, wi_ref, wg_ref,
                  q_ref, zf_ref, i_ref, g_ref):
    xn = _norm_mod(x_ref[...], gain_ref[...], sh_ref[...], sc_ref[...]).astype(BF16)
    q_ref[...] = jnp.dot(xn, wq_ref[...], preferred_element_type=F32).astype(q_ref.dtype)
    zf_ref[...] = jnp.dot(xn, wf_ref[...], preferred_element_type=F32)
    i_ref[...] = jnp.dot(xn, wi_ref[...], preferred_element_type=F32).astype(i_ref.dtype)
    g_ref[...] = jnp.dot(xn, wg_ref[...], preferred_element_type=F32).astype(g_ref.dtype)


def _hgrn_in_proj(h, gain, mod, w_in, layer, tm):
    t, d = h.shape
    row = pl.BlockSpec((tm, d), lambda i: (i, 0))
    part = lambda s: pl.BlockSpec((None, d, d), lambda i: (layer, 0, s))
    return pl.pallas_call(
        _hgrn_in_body,
        grid=(t // tm,),
        in_specs=[row, pl.BlockSpec((1, d), lambda i: (0, 0)), mod.spec(0), mod.spec(1),
                  part(0), part(1), part(2), part(3)],
        out_specs=[row, row, row, row],
        out_shape=[jax.ShapeDtypeStruct((t, d), BF16), jax.ShapeDtypeStruct((t, d), F32),
                   jax.ShapeDtypeStruct((t, d), BF16), jax.ShapeDtypeStruct((t, d), BF16)],
        compiler_params=_params(("arbitrary",)),
        name="hgrn_in_proj",
    )(h, gain.reshape(1, d), mod.arr, mod.arr, w_in, w_in, w_in, w_in)


def _split3(x):
    hi = x.astype(BF16)
    r1 = x - hi.astype(F32)
    mid = r1.astype(BF16)
    lo = (r1 - mid.astype(F32)).astype(BF16)
    return hi, mid, lo


def _hgrn_scan_body(q_ref, zf_ref, i_ref, g_ref, lb_ref, gn_ref, s0_ref, o_ref, st_ref,
                    *, chunk, n_valid, dk):
    c = pl.program_id(1)

    @pl.when(c == 0)
    def _():
        st_ref[...] = s0_ref[...]

    row = lax.broadcasted_iota(jnp.int32, (chunk, chunk), 0)
    col = lax.broadcasted_iota(jnp.int32, (chunk, chunk), 1)
    tri = jnp.where(row >= col, 1.0, 0.0).astype(BF16)
    n_levels = chunk.bit_length() - 1
    differ = row ^ col
    on_diag = differ == 0
    at_level = [(lax.shift_right_logical(differ, lv) == 1) & (row > col) for lv in range(n_levels)]
    tok = lax.broadcasted_iota(jnp.int32, (chunk, dk), 0)
    in_right = [(lax.shift_right_logical(tok, lv) & 1) == 1 for lv in range(n_levels)]
    quad = tok & 3
    if n_valid < chunk:
        valid = tok < n_valid
    for h in range(HG_HEADS):
        hs = slice(h * dk, (h + 1) * dk)
        q = q_ref[:, hs].astype(F32)
        zf = zf_ref[:, hs]
        v = i_ref[:, hs]
        lb = lb_ref[:, hs]
        e_abs = jnp.exp(-jnp.abs(zf))
        big = 1.0 / (1.0 + e_abs)
        small = e_abs * big
        nonneg = zf >= 0.0
        log_f = jnp.log(jnp.maximum(lb, TINY) + (1.0 - lb) * jnp.where(nonneg, big, small))
        k = (1.0 - lb) * jnp.where(nonneg, small, big)
        if n_valid < chunk:
            log_f = jnp.where(valid, log_f, 0.0)
            k = jnp.where(valid, k, 0.0)
        g_hi, g_mid, g_lo = _split3(log_f)
        b = (jnp.dot(tri, g_hi, preferred_element_type=F32)
             + jnp.dot(tri, g_mid, preferred_element_type=F32)
             + jnp.dot(tri, g_lo, preferred_element_type=F32))
        b_end = b[chunk - 1:chunk, :]
        a = jnp.where(on_diag, _nt_dot(q.astype(BF16), k.astype(BF16)), 0.0)
        for lv in range(n_levels):
            half = 1 << lv
            if lv == 0:
                dist = jnp.where(in_right[0], log_f, 0.0)
            elif lv == 1:
                edge = jnp.where(quad == 0, pltpu.roll(b, chunk - 1, 0),
                                 jnp.where(quad == 2, pltpu.roll(b, 1, 0),
                                           jnp.where(quad == 3, pltpu.roll(b, 2, 0), b)))
                dist = -jnp.abs(b - edge)
            else:
                nblk = chunk // (2 * half)
                edge = b.reshape(nblk, 2 * half, dk)[:, half - 1:half, :]
                edge = jnp.broadcast_to(edge, (nblk, 2 * half, dk)).reshape(chunk, dk)
                dist = -jnp.abs(b - edge)
            z = (jnp.where(in_right[lv], q, k) * jnp.exp(dist)).astype(BF16)
            a = jnp.where(at_level[lv], _nt_dot(z, z), a)
        s = st_ref[0, h]
        o = (jnp.dot(a.astype(BF16), v, preferred_element_type=F32)
             + jnp.dot((q * jnp.exp(b)).astype(BF16), s.astype(BF16), preferred_element_type=F32))
        k_d = (k * jnp.exp(b_end - b)).T.astype(BF16)
        decay = jnp.broadcast_to(jnp.exp(b_end), (dk, dk)).T
        st_ref[0, h] = decay * s + jnp.dot(k_d, v, preferred_element_type=F32)
        ms = jnp.mean(o * o, axis=-1, keepdims=True)
        gate = g_ref[:, hs].astype(F32)
        o = o * lax.rsqrt(ms + EPS) * gn_ref[...] * (gate * _sigmoid(gate))
        o_ref[:, hs] = o.astype(o_ref.dtype)


def _hgrn_scan(q, zf, i, g, lb, gnorm, s0, nseq, n_valid):
    tt, d = q.shape
    tp = tt // nseq
    nc = tp // SCAN_CHUNK
    dk = d // HG_HEADS
    row = pl.BlockSpec((SCAN_CHUNK, d), lambda b, c: (b * nc + c, 0))
    st = pl.BlockSpec((1, HG_HEADS, dk, dk), lambda b, c: (b, 0, 0, 0))
    body = functools.partial(_hgrn_scan_body, chunk=SCAN_CHUNK, n_valid=min(n_valid, SCAN_CHUNK), dk=dk)
    return pl.pallas_call(
        body,
        grid=(nseq, nc),
        in_specs=[row, row, row, row,
                  pl.BlockSpec((1, d), lambda b, c: (0, 0)),
                  pl.BlockSpec((1, dk), lambda b, c: (0, 0)),
                  st],
        out_specs=[row, st],
        out_shape=[jax.ShapeDtypeStruct((tt, d), BF16), jax.ShapeDtypeStruct(s0.shape, F32)],
        compiler_params=_params(("arbitrary", "arbitrary")),
        name="hgrn_scan",
    )(q, zf, i, g, lb.reshape(1, d), gnorm.reshape(1, dk), s0)


def _route_rows(lt):
    rows = [lt[e:e + 1, :] for e in range(N_EXPERTS)]
    m = functools.reduce(jnp.maximum, rows)
    ex = [jnp.exp(r - m) for r in rows]
    inv = 1.0 / functools.reduce(lambda x, y: x + y, ex)
    pr = [e * inv for e in ex]
    scores = []
    for gi in range(N_GROUPS):
        p4 = pr[gi * EXPERTS_PER_GROUP:(gi + 1) * EXPERTS_PER_GROUP]
        pairs = [p4[x] + p4[y] for x in range(EXPERTS_PER_GROUP) for y in range(x + 1, EXPERTS_PER_GROUP)]
        scores.append(functools.reduce(jnp.maximum, pairs))
    best = scores[0]
    sel = jnp.zeros(best.shape, jnp.int32)
    for gi in range(1, N_GROUPS):
        better = scores[gi] > best
        sel = jnp.where(better, gi, sel)
        best = jnp.where(better, scores[gi], best)
    cand = []
    for j in range(EXPERTS_PER_GROUP):
        cj = pr[(N_GROUPS - 1) * EXPERTS_PER_GROUP + j]
        for gi in range(N_GROUPS - 2, -1, -1):
            cj = jnp.where(sel == gi, pr[gi * EXPERTS_PER_GROUP + j], cj)
        cand.append(cj)

    def argmax4(vals):
        p, idx = vals[0], jnp.zeros(vals[0].shape, jnp.int32)
        for j in range(1, len(vals)):
            better = vals[j] > p
            idx = jnp.where(better, j, idx)
            p = jnp.where(better, vals[j], p)
        return p, idx

    p1, i1 = argmax4(cand)
    p2, i2 = argmax4([jnp.where(i1 == j, -1.0, cand[j]) for j in range(EXPERTS_PER_GROUP)])
    den = p1 + p2
    base = sel * EXPERTS_PER_GROUP
    return base + i1, base + i2, p1 / den, p2 / den


def _mix_router_body(a_ref, w_ref, h_ref, g1_ref, gain_ref, sh_ref, sc_ref, wrh_ref, wrl_ref, br_ref, upper_ref,
                     h_out, hn_out, gate_out, route_out, count_out):
    @pl.when(pl.program_id(0) == 0)
    def _():
        count_out[...] = jnp.zeros_like(count_out)

    mix = jnp.dot(a_ref[...].astype(BF16), w_ref[...], preferred_element_type=F32)
    h = h_ref[...] + g1_ref[...] * mix
    h_out[...] = h
    hn = _norm_mod(h, gain_ref[...], sh_ref[...], sc_ref[...])
    _store_row_tiles(hn_out, hn)
    x_hi = hn.astype(BF16)
    x_lo = (hn - x_hi.astype(F32)).astype(BF16)
    lt = (_nt_dot(wrh_ref[...], x_hi) + _nt_dot(wrl_ref[...], x_hi) + _nt_dot(wrh_ref[...], x_lo)
          + br_ref[...])
    e1, e2, g1, g2 = _route_rows(lt)
    tm = lt.shape[1]
    sub = lax.broadcasted_iota(jnp.int32, (LANES, tm), 0)
    gate_out[...] = (jnp.where(sub == 0, g1, 0.0) + jnp.where(sub == 1, g2, 0.0)).T
    eid = lax.broadcasted_iota(jnp.int32, (N_EXPERTS, tm), 0)
    pick1 = eid == e1
    pick2 = eid == e2
    hits = jnp.where(pick1, 1.0, 0.0) + jnp.where(pick2, 1.0, 0.0)
    before = jnp.dot(hits.astype(BF16), upper_ref[...], preferred_element_type=F32) + count_out[...]
    rank1 = jnp.sum(jnp.where(pick1, before, 0.0), axis=0, keepdims=True)
    rank2 = jnp.sum(jnp.where(pick2, before, 0.0), axis=0, keepdims=True)
    count_out[...] += jnp.sum(hits, axis=1, keepdims=True)
    field = lax.broadcasted_iota(jnp.int32, route_out.shape[1:], 0)
    route_out[0] = jnp.where(field == 0, e1, jnp.where(field == 1, e2, jnp.where(
        field == 2, rank1.astype(jnp.int32), rank2.astype(jnp.int32))))


def _mix_router(a, w_out, h, gain, mod, wr_hi, wr_lo, b_router, tm):
    t, d = h.shape
    row = pl.BlockSpec((tm, d), lambda i: (i, 0))
    const = lambda shape: pl.BlockSpec(shape, lambda i: (0,) * len(shape))
    upper = (jnp.arange(tm)[:, None] < jnp.arange(tm)[None, :]).astype(BF16)
    return pl.pallas_call(
        _mix_router_body,
        grid=(t // tm,),
        in_specs=[row, const((d, d)), row, mod.spec(2), const((1, d)), mod.spec(3), mod.spec(4),
                  const((N_EXPERTS, d)), const((N_EXPERTS, d)), const((N_EXPERTS, 1)), const((tm, tm))],
        out_specs=[row, pl.BlockSpec((tm * SUBLANES, LANES), lambda i: (i, 0)),
                   pl.BlockSpec((tm, LANES), lambda i: (i, 0)),
                   pl.BlockSpec((1, 4, tm), lambda i: (i, 0, 0)), const((N_EXPERTS, 1))],
        out_shape=[jax.ShapeDtypeStruct((t, d), F32), jax.ShapeDtypeStruct((t * SUBLANES, LANES), F32),
                   jax.ShapeDtypeStruct((t, LANES), F32), jax.ShapeDtypeStruct((t // tm, 4, tm), jnp.int32),
                   jax.ShapeDtypeStruct((N_EXPERTS, 1), F32)],
        compiler_params=_params(("arbitrary",)),
        name="mix_router",
    )(a, w_out, h, mod.arr, gain.reshape(1, d), mod.arr, mod.arr, wr_hi, wr_lo,
      b_router.reshape(N_EXPERTS, 1).astype(F32), upper)


def _store_row_tiles(ref, x):
    n, d = x.shape
    assert d == SUBLANES * LANES
    for c in range(SUBLANES):
        ref[pl.ds(c, n, stride=SUBLANES), :] = x[:, c * LANES:(c + 1) * LANES]


def _load_row_tiles(ref):
    n = ref.shape[0] // SUBLANES
    return jnp.concatenate([ref[pl.ds(c, n, stride=SUBLANES), :] for c in range(SUBLANES)], axis=1)


def _tile_rows(first_row, n_rows):
    return pl.ds(pl.multiple_of(first_row * SUBLANES, SUBLANES), n_rows * SUBLANES)


def _row_copy(src, src_row, dst, dst_row, sem):
    return pltpu.make_async_copy(src.at[_tile_rows(src_row, 1)], dst.at[_tile_rows(dst_row, 1)], sem)


def _index_slot(tm):
    return -(-2 * tm // SMEM_I32_CHUNK) * SMEM_I32_CHUNK


def _load_positions(pos_hbm, idx_ref, sem, step):
    n = idx_ref.shape[0]
    cp = pltpu.make_async_copy(pos_hbm.at[pl.ds(pl.multiple_of(step * n, n), n)], idx_ref, sem)
    cp.start()
    cp.wait()


def _dispatch_body(seg_end_ref, pos_hbm, x_ref, xs_out, idx_ref, zero_ref, idx_sem, row_sem, *, tm):
    @pl.when(pl.program_id(0) == 0)
    def _():
        zero_ref[...] = jnp.zeros_like(zero_ref)

        def last_tile(e):
            return pltpu.make_async_copy(
                zero_ref, xs_out.at[_tile_rows(seg_end_ref[e] - EXPERT_TILE, EXPERT_TILE)], row_sem)

        def nonempty(e):
            return seg_end_ref[e] > (seg_end_ref[e - 1] if e else 0)

        for e in range(N_EXPERTS):
            pl.when(nonempty(e))(lambda e=e: last_tile(e).start())
        for e in range(N_EXPERTS):
            pl.when(nonempty(e))(lambda e=e: last_tile(e).wait())

        def unused_tile(i, carry):
            cp = pltpu.make_async_copy(zero_ref, xs_out.at[_tile_rows(i * EXPERT_TILE, EXPERT_TILE)], row_sem)
            cp.start()
            cp.wait()
            return carry

        n_tiles = xs_out.shape[0] // (EXPERT_TILE * SUBLANES)
        lax.fori_loop(seg_end_ref[N_EXPERTS - 1] // EXPERT_TILE, n_tiles, unused_tile, 0)

    _load_positions(pos_hbm, idx_ref, idx_sem, pl.program_id(0))

    def start(t, carry):
        _row_copy(x_ref, t, xs_out, idx_ref[t], row_sem).start(priority=0)
        _row_copy(x_ref, t, xs_out, idx_ref[tm + t], row_sem).start(priority=1)
        return carry

    lax.fori_loop(0, tm, start, 0, unroll=ROW_COPY_UNROLL)
    for _ in range(2):
        pltpu.make_async_copy(x_ref, xs_out.at[_tile_rows(0, tm)], row_sem).wait()


def _dispatch(hn, pos_tiles, seg_end, n_rows, tm):
    t = hn.shape[0] // SUBLANES
    grid_spec = pltpu.PrefetchScalarGridSpec(
        num_scalar_prefetch=1,
        grid=(t // tm,),
        in_specs=[pl.BlockSpec(memory_space=pl.ANY), pl.BlockSpec((tm * SUBLANES, LANES), lambda i, se: (i, 0))],
        out_specs=pl.BlockSpec(memory_space=pl.ANY),
        scratch_shapes=[pltpu.SMEM((_index_slot(tm),), jnp.int32),
                        pltpu.VMEM((EXPERT_TILE * SUBLANES, LANES), F32),
                        pltpu.SemaphoreType.DMA, pltpu.SemaphoreType.DMA])
    return pl.pallas_call(
        functools.partial(_dispatch_body, tm=tm),
        grid_spec=grid_spec,
        out_shape=jax.ShapeDtypeStruct((n_rows * SUBLANES, LANES), F32),
        compiler_params=_params(("arbitrary",)),
        name="moe_dispatch",
    )(seg_end, pos_tiles, hn)


def _expert_body(te_ref, nu_ref, x_ref, win_ref, wout_ref, y_ref):
    @pl.when(pl.program_id(0) < nu_ref[0])
    def _():
        hid = jnp.dot(_load_row_tiles(x_ref).astype(BF16), win_ref[0].astype(BF16), preferred_element_type=F32)
        de = hid.shape[1] // 2
        a, u = hid[:, :de], hid[:, de:]
        act = (a * _sigmoid(a) * u).astype(BF16)
        _store_row_tiles(y_ref, jnp.dot(act, wout_ref[0].astype(BF16), preferred_element_type=F32))

    @pl.when(pl.program_id(0) >= nu_ref[0])
    def _():
        y_ref[...] = jnp.zeros_like(y_ref)


def _experts(xs, tile_expert, n_used, w_in, w_out, layer):
    _, n_e, d, d2 = w_in.shape
    n_tiles = xs.shape[0] // (EXPERT_TILE * SUBLANES)
    tile = lambda i, te, nu: jnp.minimum(i, nu[0] - 1)
    rows = (EXPERT_TILE * SUBLANES, LANES)
    grid_spec = pltpu.PrefetchScalarGridSpec(
        num_scalar_prefetch=2,
        grid=(n_tiles,),
        in_specs=[pl.BlockSpec(rows, lambda i, te, nu: (tile(i, te, nu), 0)),
                  pl.BlockSpec((None, 1, d, d2), lambda i, te, nu: (layer, te[tile(i, te, nu)], 0, 0)),
                  pl.BlockSpec((None, 1, d2 // 2, d), lambda i, te, nu: (layer, te[tile(i, te, nu)], 0, 0))],
        out_specs=pl.BlockSpec(rows, lambda i, te, nu: (i, 0)))
    return pl.pallas_call(
        _expert_body,
        grid_spec=grid_spec,
        out_shape=jax.ShapeDtypeStruct(xs.shape, F32),
        compiler_params=_params(("arbitrary",)),
        name="moe_experts",
    )(tile_expert, n_used, xs, w_in, w_out)


def _combine_body(pos_hbm, ys_hbm, gate_ref, h_ref, g2_ref, o_ref, idx_ref, y1_ref, y2_ref, idx_sem, row_sem,
                  *, tm):
    _load_positions(pos_hbm, idx_ref, idx_sem, pl.program_id(0))

    def start(t, carry):
        _row_copy(ys_hbm, idx_ref[t], y1_ref, t, row_sem).start(priority=0)
        _row_copy(ys_hbm, idx_ref[tm + t], y2_ref, t, row_sem).start(priority=1)
        return carry

    lax.fori_loop(0, tm, start, 0, unroll=ROW_COPY_UNROLL)
    for y_ref in (y1_ref, y2_ref):
        pltpu.make_async_copy(ys_hbm.at[_tile_rows(0, tm)], y_ref, row_sem).wait()
    gates = gate_ref[...]
    moe = gates[:, 0:1] * _load_row_tiles(y1_ref) + gates[:, 1:2] * _load_row_tiles(y2_ref)
    o_ref[...] = h_ref[...] + g2_ref[...] * moe


def _combine(ys, pos_tiles, gates, h, mod, tm):
    t, d = h.shape
    row = pl.BlockSpec((tm, d), lambda i: (i, 0))
    return pl.pallas_call(
        functools.partial(_combine_body, tm=tm),
        grid=(t // tm,),
        in_specs=[pl.BlockSpec(memory_space=pl.ANY), pl.BlockSpec(memory_space=pl.ANY),
                  pl.BlockSpec((tm, LANES), lambda i: (i, 0)), row, mod.spec(5)],
        out_specs=row,
        out_shape=jax.ShapeDtypeStruct((t, d), F32),
        scratch_shapes=[pltpu.SMEM((_index_slot(tm),), jnp.int32),
                        pltpu.VMEM((tm * SUBLANES, LANES), F32), pltpu.VMEM((tm * SUBLANES, LANES), F32),
                        pltpu.SemaphoreType.DMA, pltpu.SemaphoreType.DMA],
        compiler_params=_params(("arbitrary",)),
        name="moe_combine",
    )(pos_tiles, ys, gates, h, mod.arr)


def _moe(hn, gates, route, counts, h, mod, w_in, w_out, layer, tm):
    t, d = h.shape
    n_tiles = 2 * t // EXPERT_TILE + N_EXPERTS
    n_rows = n_tiles * EXPERT_TILE
    cnt = counts.reshape(N_EXPERTS).astype(jnp.int32)
    padded = (cnt + (EXPERT_TILE - 1)) // EXPERT_TILE * EXPERT_TILE
    experts = jnp.arange(N_EXPERTS, dtype=jnp.int32)
    seg_end = jnp.sum(jnp.where(experts[None, :] <= experts[:, None], padded[None, :], 0), axis=1)
    seg_start = seg_end - padded
    first_row = jnp.sum(jnp.where(route[:, :2, :, None] == experts, seg_start, 0), axis=-1)
    pos_tiles = (first_row + route[:, 2:]).reshape(t // tm, 2 * tm)
    if _index_slot(tm) != 2 * tm:
        pos_tiles = jnp.pad(pos_tiles, ((0, 0), (0, _index_slot(tm) - 2 * tm)))
    pos_tiles = pos_tiles.reshape(-1)
    tile_row = jnp.arange(n_tiles, dtype=jnp.int32) * EXPERT_TILE
    tile_expert = jnp.minimum(jnp.sum((seg_end[None, :] <= tile_row[:, None]).astype(jnp.int32), axis=1),
                              N_EXPERTS - 1)
    n_used = seg_end[-1:] // EXPERT_TILE
    xs = _dispatch(hn, pos_tiles, seg_end, n_rows, tm)
    ys = _experts(xs, tile_expert, n_used, w_in, w_out, layer)
    return _combine(ys, pos_tiles, gates, h, mod, tm)


def _head_norm(y, gsum_ref, gain_ref, head_dim):
    ssum = jnp.dot((y * y).astype(BF16), gsum_ref[...], preferred_element_type=F32)
    return y * lax.rsqrt(ssum * (1.0 / head_dim) + EPS) * gain_ref[...]


def _head_sum_matrix(d, head_dim):
    r = jnp.arange(d) // head_dim
    return (r[:, None] == r[None, :]).astype(BF16)


def _q_proj_body(x_ref, gain_ref, sh_ref, sc_ref, w_ref, gsum_ref, qn_ref, q_ref, *, head_dim, q_scale):
    xn = _norm_mod(x_ref[...], gain_ref[...], sh_ref[...], sc_ref[...]).astype(BF16)
    y = jnp.dot(xn, w_ref[...], preferred_element_type=F32)
    q = _head_norm(y, gsum_ref, qn_ref, head_dim) * q_scale
    q_ref[...] = q.astype(q_ref.dtype)


def _q_proj(h, gain, mod, w, gsum, q_norm, tm, out_dtype, q_scale):
    t, d = h.shape
    head_dim = d // FOX_HEADS
    row = pl.BlockSpec((tm, d), lambda i: (i, 0))
    const = lambda shape: pl.BlockSpec(shape, lambda i: (0,) * len(shape))
    return pl.pallas_call(
        functools.partial(_q_proj_body, head_dim=head_dim, q_scale=q_scale),
        grid=(t // tm,),
        in_specs=[row, const((1, d)), mod.spec(0), mod.spec(1), const((d, d)), const((d, d)), const((1, d))],
        out_specs=row,
        out_shape=jax.ShapeDtypeStruct((t, d), out_dtype),
        compiler_params=_params(("arbitrary",)),
        name="q_proj",
    )(h, gain.reshape(1, d), mod.arr, mod.arr, w, gsum, jnp.tile(q_norm, FOX_HEADS).reshape(1, d))


def _kv_proj_body(x_ref, gain_ref, sh_ref, sc_ref, wk_ref, wv_ref, wf_ref, gsum_ref, kn_ref, bf_ref,
                  k_ref, v_ref, kb_ref, vb_ref, lf_ref, *, head_dim):
    xn = _norm_mod(x_ref[...], gain_ref[...], sh_ref[...], sc_ref[...]).astype(BF16)
    k = _head_norm(jnp.dot(xn, wk_ref[...], preferred_element_type=F32), gsum_ref, kn_ref, head_dim)
    k_ref[...] = k
    kb_ref[...] = k.astype(BF16)
    v = jnp.dot(xn, wv_ref[...], preferred_element_type=F32)
    v_ref[...] = v
    vb_ref[...] = v.astype(BF16)
    zf = jnp.dot(xn, wf_ref[...], preferred_element_type=F32) + bf_ref[...]
    lf_ref[...] = _log_sigmoid(zf)[:, :lf_ref.shape[1]]


def _kv_proj(h, gain, mod, wk, wv, wf, gsum, k_norm, b_fgate, tm):
    t, d = h.shape
    head_dim = d // FOX_HEADS
    row = pl.BlockSpec((tm, d), lambda i: (i, 0))
    const = lambda shape: pl.BlockSpec(shape, lambda i: (0,) * len(shape))
    bf = jnp.pad(b_fgate.astype(F32), (0, LANES - FOX_HEADS)).reshape(1, LANES)
    return pl.pallas_call(
        functools.partial(_kv_proj_body, head_dim=head_dim),
        grid=(t // tm,),
        in_specs=[row, const((1, d)), mod.spec(0), mod.spec(1), const((d, d)), const((d, d)),
                  const((d, LANES)), const((d, d)), const((1, d)), const((1, LANES))],
        out_specs=[row, row, row, row, pl.BlockSpec((tm, FOX_HEADS), lambda i: (i, 0))],
        out_shape=[jax.ShapeDtypeStruct((t, d), F32), jax.ShapeDtypeStruct((t, d), F32),
                   jax.ShapeDtypeStruct((t, d), BF16), jax.ShapeDtypeStruct((t, d), BF16),
                   jax.ShapeDtypeStruct((t, FOX_HEADS), F32)],
        compiler_params=_params(("arbitrary",)),
        name="kv_proj",
    )(h, gain.reshape(1, d), mod.arr, mod.arr, wk, wv, wf, gsum,
      jnp.tile(k_norm, FOX_HEADS).reshape(1, d), bf)


def _cumsum_lanes_body(x_ref, o_ref, carry_ref, *, carry_rows):
    j = pl.program_id(1)

    @pl.when(j == 0)
    def _():
        carry_ref[...] = jnp.zeros_like(carry_ref)

    w = x_ref.shape[-1]
    r = lax.broadcasted_iota(jnp.int32, (w, w), 0)
    c = lax.broadcasted_iota(jnp.int32, (w, w), 1)
    upper = jnp.where(r <= c, 1.0, 0.0).astype(F32)
    y = jnp.dot(x_ref[0], upper, precision=HIGHEST, preferred_element_type=F32)
    if carry_rows:
        y = y + carry_ref[...]
        carry_ref[...] = y[:, w - 1:w]
    o_ref[0] = y


def _cumsum_lanes(x, width, carry):
    b, r, t = x.shape
    spec = pl.BlockSpec((1, r, width), lambda i, j: (i, 0, j))
    return pl.pallas_call(
        functools.partial(_cumsum_lanes_body, carry_rows=carry),
        grid=(b, t // width),
        in_specs=[spec],
        out_specs=spec,
        out_shape=jax.ShapeDtypeStruct(x.shape, F32),
        scratch_shapes=[pltpu.VMEM((r, 1), F32)],
        compiler_params=_params(("arbitrary", "arbitrary")),
        name="cumsum_lanes",
    )(x)


def _attn_prompt_body(qt_ref, kt_ref, q_ref, k_ref, v_ref, f_ref, o_ref, m_ref, acc_ref,
                      *, blk, rows, diag_rows, head_dim):
    p = pl.program_id(2)
    qi = qt_ref[p]
    ki = kt_ref[p]

    @pl.when(ki == 0)
    def _():
        m_ref[...] = jnp.full(m_ref.shape, NEG_BIG, F32)
        acc_ref[...] = jnp.zeros_like(acc_ref)

    def process(masked):
        k = k_ref[...]
        v = v_ref[...]
        f = f_ref[...]
        first_v = lax.broadcasted_iota(jnp.int32, v.shape, 1) < head_dim
        ones = jnp.ones_like(v)
        v_heads = (jnp.where(first_v, v, ones), jnp.where(first_v, ones, v))
        rb = min(rows, diag_rows) if masked else rows
        m_olds = {(hh, r0): m_ref[hh, r0:r0 + rb, :] for hh in range(2) for r0 in range(0, blk, rb)}
        acc_olds = {(hh, r0): acc_ref[hh, r0:r0 + rb, :] for hh in range(2) for r0 in range(0, blk, rb)}
        m_news, acc_news = {}, {}
        for r0 in range(0, blk, rb):
            n_keys = r0 + rb if masked else blk
            q = q_ref[r0:r0 + rb, :]
            first_q = lax.broadcasted_iota(jnp.int32, q.shape, 1) < head_dim
            zero = jnp.zeros_like(q)
            for hh, qh in enumerate((jnp.where(first_q, q, zero), jnp.where(first_q, zero, q))):
                s = _nt_dot(qh, k[:n_keys]) - f[hh:hh + 1, :n_keys]
                if masked:
                    row = lax.broadcasted_iota(jnp.int32, s.shape, 0) + r0
                    col = lax.broadcasted_iota(jnp.int32, s.shape, 1)
                    s = jnp.where(col <= row, s, NEG_BIG)
                m_old = m_olds[hh, r0]
                m_new = jnp.maximum(m_old, jnp.max(s, axis=1, keepdims=True))
                alpha = jnp.exp2(m_old - m_new)
                pr = jnp.concatenate([jnp.exp2(s[:, c * LANES:(c + 1) * LANES] - m_new)
                                      for c in range(n_keys // LANES)], axis=1).astype(BF16)
                acc_news[hh, r0] = alpha * acc_olds[hh, r0] + jnp.dot(pr, v_heads[hh][:n_keys],
                                                                      preferred_element_type=F32)
                m_news[hh, r0] = m_new
        for (hh, r0), m_new in m_news.items():
            m_ref[hh, r0:r0 + rb, :] = m_new
            acc_ref[hh, r0:r0 + rb, :] = acc_news[hh, r0]

    @pl.when(ki < qi)
    def _():
        process(False)

    @pl.when(ki == qi)
    def _():
        process(True)
        a0 = acc_ref[0]
        a1 = acc_ref[1]
        first = lax.broadcasted_iota(jnp.int32, a0.shape, 1) < head_dim
        inv0 = 1.0 / a0[:, head_dim:head_dim + 1]
        inv1 = 1.0 / a1[:, 0:1]
        o_ref[...] = jnp.where(first, a0 * inv0, a1 * inv1).astype(o_ref.dtype)


def _attn_prompt(q, k, v, f_pairs, nseq):
    tt, d = q.shape
    t = tt // nseq
    blk = min(ATTN_BLOCK, t)
    rows = min(ATTN_ROWS, blk)
    nb = t // blk
    head_dim = d // FOX_HEADS
    pairs = [(qi, ki) for qi in range(nb) for ki in range(qi + 1)]
    q_tab = jnp.asarray([p[0] for p in pairs], jnp.int32)
    k_tab = jnp.asarray([p[1] for p in pairs], jnp.int32)
    grid_spec = pltpu.PrefetchScalarGridSpec(
        num_scalar_prefetch=2,
        grid=(nseq, d // LANES, len(pairs)),
        in_specs=[pl.BlockSpec((blk, LANES), lambda b, hp, p, qt, kt: (b * nb + qt[p], hp)),
                  pl.BlockSpec((blk, LANES), lambda b, hp, p, qt, kt: (b * nb + kt[p], hp)),
                  pl.BlockSpec((blk, LANES), lambda b, hp, p, qt, kt: (b * nb + kt[p], hp)),
                  pl.BlockSpec((None, None, 2, blk), lambda b, hp, p, qt, kt: (b, hp, 0, kt[p]))],
        out_specs=pl.BlockSpec((blk, LANES), lambda b, hp, p, qt, kt: (b * nb + qt[p], hp)),
        scratch_shapes=[pltpu.VMEM((2, blk, LANES), F32), pltpu.VMEM((2, blk, LANES), F32)])
    return pl.pallas_call(
        functools.partial(_attn_prompt_body, blk=blk, rows=rows, diag_rows=min(ATTN_DIAG_ROWS, blk),
                          head_dim=head_dim),
        grid_spec=grid_spec,
        out_shape=jax.ShapeDtypeStruct((tt, d), BF16),
        compiler_params=_params(("arbitrary", "arbitrary", "arbitrary")),
        name="attn_prompt",
    )(q_tab, k_tab, q, k, v, f_pairs)


def _gather_cum_body(pt_ref, cp_ref, new_ref, o_ref, *, n_pages, page):
    b = pl.program_id(0)

    def step(j, carry):
        blk = cp_ref[pt_ref[b * n_pages + j]]
        o_ref[0, :, pl.ds(pl.multiple_of(j * page, page), page)] = blk + carry
        return carry + blk[:, page - 1:page]

    carry = lax.fori_loop(0, n_pages, step, jnp.zeros((cp_ref.shape[1], 1), F32), unroll=8)
    o_ref[0, :, n_pages * page:] = new_ref[0] + carry


def _gather_cum(page_table, cum_pages, cum_new):
    nseq, n_pages = page_table.shape
    n_pool, hh, page = cum_pages.shape
    grid_spec = pltpu.PrefetchScalarGridSpec(
        num_scalar_prefetch=1,
        grid=(nseq,),
        in_specs=[pl.BlockSpec((n_pool, hh, page), lambda b, pt: (0, 0, 0)),
                  pl.BlockSpec((1, hh, page), lambda b, pt: (b, 0, 0))],
        out_specs=pl.BlockSpec((1, hh, (n_pages + 1) * page), lambda b, pt: (b, 0, 0)))
    return pl.pallas_call(
        functools.partial(_gather_cum_body, n_pages=n_pages, page=page),
        grid_spec=grid_spec,
        out_shape=jax.ShapeDtypeStruct((nseq, hh, (n_pages + 1) * page), F32),
        compiler_params=_params(("arbitrary",)),
        name="gather_cum",
    )(page_table.reshape(-1), cum_pages, cum_new)


def _attn_paged_body(pt_ref, *refs, n_q, page, pps, n_steps, head_dim):
    q_ref = refs[0]
    k_refs = refs[1:1 + pps]
    v_refs = refs[1 + pps:1 + 2 * pps]
    kn_ref, vn_ref, f_ref, o_ref, qbd_ref, m_ref, l_ref, acc_ref = refs[1 + 2 * pps:]
    j = pl.program_id(1)
    d = q_ref.shape[1]
    rows = FOX_HEADS * n_q

    @pl.when(j == 0)
    def _():
        q = q_ref[...]
        lane_head = lax.broadcasted_iota(jnp.int32, q.shape, 1) // head_dim
        for h in range(FOX_HEADS):
            qbd_ref[h * n_q:(h + 1) * n_q, :] = jnp.where(lane_head == h, q, 0.0).astype(BF16)
        m_ref[...] = jnp.full(m_ref.shape, NEG_BIG, F32)
        l_ref[...] = jnp.zeros_like(l_ref)
        acc_ref[...] = jnp.zeros_like(acc_ref)

    def attend(k, v, f, mask):
        bias = jnp.concatenate([jnp.broadcast_to(f[h:h + 1, :], (n_q, page)) for h in range(FOX_HEADS)], axis=0)
        s = _nt_dot(qbd_ref[...], k) - bias
        if mask is not None:
            s = jnp.where(mask, s, NEG_BIG)
        m_old = m_ref[...]
        m_new = jnp.maximum(m_old, jnp.max(s, axis=1, keepdims=True))
        alpha = jnp.exp(m_old - m_new)
        pr = jnp.exp(s - m_new)
        l_ref[...] = alpha * l_ref[...] + jnp.sum(pr, axis=1, keepdims=True)
        m_ref[...] = m_new
        acc_ref[...] = alpha * acc_ref[...] + jnp.dot(pr.astype(BF16), v, preferred_element_type=F32)

    @pl.when(j < n_steps - 1)
    def _():
        for u in range(pps):
            off = pl.multiple_of((j * pps + u) * page, page)
            attend(k_refs[u][...].astype(BF16), v_refs[u][...].astype(BF16), f_ref[0, :, pl.ds(off, page)], None)

    @pl.when(j == n_steps - 1)
    def _():
        pad = jnp.zeros((page - n_q, d), F32)
        k = jnp.concatenate([kn_ref[...], pad], axis=0).astype(BF16)
        v = jnp.concatenate([vn_ref[...], pad], axis=0).astype(BF16)
        t_of_row = lax.broadcasted_iota(jnp.int32, (rows, page), 0) % n_q
        key = lax.broadcasted_iota(jnp.int32, (rows, page), 1)
        attend(k, v, f_ref[0, :, (n_steps - 1) * pps * page:], key <= t_of_row)
        acc = acc_ref[...] * (1.0 / l_ref[...])
        lane_head = lax.broadcasted_iota(jnp.int32, (n_q, d), 1) // head_dim
        out = jnp.zeros((n_q, d), F32)
        for h in range(FOX_HEADS):
            out = out + jnp.where(lane_head == h, acc[h * n_q:(h + 1) * n_q, :], 0.0)
        o_ref[...] = out


def _attn_paged(q, cache_k, cache_v, k_new, v_new, f_all, page_table):
    tt, d = q.shape
    nseq, n_pages = page_table.shape
    n_q = tt // nseq
    page = cache_k.shape[1]
    pps = PAGES_PER_STEP
    n_steps = n_pages // pps + 1
    head_dim = d // FOX_HEADS
    rows = FOX_HEADS * n_q

    def page_spec(u):
        def index(b, j, pt):
            return (pt[b * n_pages + jnp.minimum(j, n_steps - 2) * pps + u], 0, 0)
        return pl.BlockSpec((None, page, d), index)

    tok = pl.BlockSpec((n_q, d), lambda b, j, pt: (b, 0))
    grid_spec = pltpu.PrefetchScalarGridSpec(
        num_scalar_prefetch=1,
        grid=(nseq, n_steps),
        in_specs=([tok] + [page_spec(u) for u in range(pps)] + [page_spec(u) for u in range(pps)]
                  + [tok, tok, pl.BlockSpec((1, FOX_HEADS, f_all.shape[2]), lambda b, j, pt: (b, 0, 0))]),
        out_specs=tok,
        scratch_shapes=[pltpu.VMEM((rows, d), BF16), pltpu.VMEM((rows, 1), F32),
                        pltpu.VMEM((rows, 1), F32), pltpu.VMEM((rows, d), F32)])
    body = functools.partial(_attn_paged_body, n_q=n_q, page=page, pps=pps, n_steps=n_steps,
                             head_dim=head_dim)
    return pl.pallas_call(
        body,
        grid_spec=grid_spec,
        out_shape=jax.ShapeDtypeStruct((tt, d), F32),
        compiler_params=_params(("arbitrary", "arbitrary")),
        name="attn_paged",
    )(page_table.reshape(-1), q, *([cache_k] * pps), *([cache_v] * pps), k_new, v_new, f_all)


def _lower_bounds(lb_logits):
    p = jax.nn.softmax(lb_logits.astype(F32), axis=0)
    return jnp.cumsum(p, axis=0) - p[0]


def _trunk(x, nseq, mods, mod_kv, s0, past, w):
    tt, d = x.shape
    t = tt // nseq
    tm = min(512, tt)
    tiles_per_seq = max(t // tm, 1)
    n_a = w["w_in_a"].shape[0]
    depth = w["w_ada"].shape[0]
    head_dim = d // FOX_HEADS
    lbs = _lower_bounds(w["lb_logits"])
    mk = lambda arr: _Mod(arr, d, tm, tiles_per_seq)

    h = x
    states = []
    for l in range(depth):
        mod = mk(mods[l])
        if l < n_a:
            q, zf, i, g = _hgrn_in_proj(h, w["norm_mix"][l], mod, w["w_in_a"], l, tm)
            tp = -(-t // SCAN_CHUNK) * SCAN_CHUNK
            if tp != t:
                padr = lambda a: jnp.pad(a.reshape(nseq, t, d), ((0, 0), (0, tp - t), (0, 0))).reshape(nseq * tp, d)
                q, zf, i, g = padr(q), padr(zf), padr(i), padr(g)
            o, s_t = _hgrn_scan(q, zf, i, g, lbs[l], w["gnorm_a"][l], s0[l], nseq, t)
            if tp != t:
                o = o.reshape(nseq, tp, d)[:, :t].reshape(tt, d)
            states.append(s_t)
            mix_in, w_out = o, w["w_out_a"][l]
        else:
            j = l - n_a
            if past is None:
                q = _q_proj(h, w["norm_mix"][l], mod, w["w_q_b"][j], w["gsum"], w["q_norm"][j], tm, BF16,
                            head_dim ** -0.5 * LOG2E)
                mix_in = _attn_prompt(q, k_bf, v_bf, f_keys, nseq)
            else:
                q = _q_proj(h, w["norm_mix"][l], mod, w["w_q_b"][j], w["gsum"], w["q_norm"][j], tm, F32,
                            head_dim ** -0.5)
                mix_in = _attn_paged(q, past[0], past[1], k_new, v_new, f_keys, past[3])
            w_out = w["w_out_b"][j]
        h, hn, gates, route, counts = _mix_router(mix_in, w_out, h, w["norm_ffn"][l], mod, w["wr_hi"],
                                                  w["wr_lo"], w["b_router"], tm)
        h = _moe(hn, gates, route, counts, h, mod, w["w_exp_in"], w["w_exp_out"], l, tm)
        if l == n_a - 1:
            k_new, v_new, k_bf, v_bf, logf_new = _kv_proj(
                h, w["norm_kv"], mk(mod_kv), w["w_k"], w["w_v"], w["w_f"], w["gsum"], w["k_norm"],
                w["b_fgate"], tm)
            lf_t = logf_new.reshape(nseq, t, FOX_HEADS).transpose(0, 2, 1)
            if past is None:
                f_cum = _cumsum_lanes(lf_t, min(512, t), carry=True)
                f_keys = (f_cum * LOG2E).reshape(nseq, FOX_HEADS // 2, 2, t)
            else:
                cache_logf, page_table = past[2], past[3]
                page = cache_logf.shape[1]
                pool_t = cache_logf.astype(F32).transpose(0, 2, 1)
                n_pool = pool_t.shape[0]
                rows = 64 if n_pool % 64 == 0 else 1
                cum_pages = _cumsum_lanes(pool_t.reshape(n_pool // rows, rows * FOX_HEADS, page), page,
                                          carry=False).reshape(n_pool, FOX_HEADS, page)
                new_pad = jnp.pad(lf_t, ((0, 0), (0, 0), (0, page - t)))
                cum_new = _cumsum_lanes(new_pad, page, carry=False)
                f_keys = _gather_cum(page_table, cum_pages, cum_new)
    return h, jnp.stack(states), k_new, v_new, logf_new


def kernel(x_prompt, x_sample, cache_k, cache_v, cache_logf, state_hgrn, page_table, c_prompt, c_sample,
           w_ada, b_ada, norm_mix, norm_ffn, w_in_a, lb_logits, gnorm_a, w_out_a, norm_kv, w_ada_kv,
           b_ada_kv, w_kv, b_fgate, k_norm, w_q_b, q_norm, w_out_b, w_router, b_router, w_exp_in, w_exp_out):
    nb, seq, d = x_prompt.shape
    ns, dseq, _ = x_sample.shape
    n_a = w_in_a.shape[0]
    depth = w_ada.shape[0]
    hd = FOX_HEADS * (d // FOX_HEADS)
    dk = d // HG_HEADS

    nrow = nb + ns
    nrow_pad = -(-nrow // SUBLANES) * SUBLANES
    c_all = jnp.pad(jnp.concatenate([c_prompt, c_sample], axis=0).astype(F32), ((0, nrow_pad - nrow), (0, 0)))
    mod_all = _ada(c_all, w_ada, b_ada)
    mod_kv_all = _ada(c_all, w_ada_kv[None], b_ada_kv[None])[0]
    mods_p = [mod_all[l, :nb].reshape(nb, 1, 6 * d) for l in range(depth)]
    mods_s = [jnp.repeat(mod_all[l, nb:nrow], dseq, axis=0) for l in range(depth)]
    mod_kv_p = mod_kv_all[:nb].reshape(nb, 1, 2 * d)
    mod_kv_s = jnp.repeat(mod_kv_all[nb:nrow], dseq, axis=0)

    wr_t = w_router.astype(F32).T
    wr_hi = wr_t.astype(BF16)
    wr_lo = (wr_t - wr_hi.astype(F32)).astype(BF16)
    w = dict(
        w_ada=w_ada, lb_logits=lb_logits, norm_mix=norm_mix, norm_ffn=norm_ffn, gnorm_a=gnorm_a,
        w_in_a=w_in_a.astype(BF16), w_out_a=w_out_a.astype(BF16), norm_kv=norm_kv,
        w_k=w_kv[:, :hd].astype(BF16), w_v=w_kv[:, hd:2 * hd].astype(BF16),
        w_f=jnp.pad(w_kv[:, 2 * hd:], ((0, 0), (0, LANES - FOX_HEADS))).astype(BF16),
        b_fgate=b_fgate, k_norm=k_norm, w_q_b=w_q_b.astype(BF16), q_norm=q_norm,
        w_out_b=w_out_b.astype(BF16), wr_hi=wr_hi, wr_lo=wr_lo, b_router=b_router,
        w_exp_in=w_exp_in, w_exp_out=w_exp_out,
        gsum=_head_sum_matrix(d, d // FOX_HEADS),
    )

    s0_p = jnp.zeros((n_a, nb, HG_HEADS, dk, dk), F32)
    y_p, st_p, k_p, v_p, lf_p = _trunk(x_prompt.reshape(nb * seq, d), nb, mods_p, mod_kv_p, s0_p, None, w)

    n_pool, page = cache_k.shape[0], cache_k.shape[1]
    past = (cache_k.reshape(n_pool, page, hd), cache_v.reshape(n_pool, page, hd), cache_logf, page_table)
    y_s, st_s, k_s, v_s, lf_s = _trunk(x_sample.reshape(ns * dseq, d), ns, mods_s, mod_kv_s,
                                       state_hgrn.astype(F32), past, w)

    hs = (FOX_HEADS, d // FOX_HEADS)
    return (y_p.reshape(nb, seq, d), y_s.reshape(ns, dseq, d),
            st_p.astype(state_hgrn.dtype), st_s.astype(state_hgrn.dtype),
            k_p.reshape(nb, seq, *hs).astype(cache_k.dtype), v_p.reshape(nb, seq, *hs).astype(cache_v.dtype),
            lf_p.reshape(nb, seq, FOX_HEADS).astype(cache_logf.dtype),
            k_s.reshape(ns, dseq, *hs).astype(cache_k.dtype), v_s.reshape(ns, dseq, *hs).astype(cache_v.dtype),
            lf_s.reshape(ns, dseq, FOX_HEADS).astype(cache_logf.dtype))
```

```python
import functools
import math

import jax
import jax.numpy as jnp
from jax import lax
from jax.experimental import pallas as pl
from jax.experimental.pallas import tpu as pltpu

F32 = jnp.float32
BF16 = jnp.bfloat16
HIGHEST = lax.Precision.HIGHEST

HG_HEADS = 8
FOX_HEADS = 16
N_EXPERTS = 16
N_GROUPS = 4
EXPERTS_PER_GROUP = N_EXPERTS // N_GROUPS
EPS = 1e-6
NEG_BIG = -1e30
TINY = 1e-30

LANES = 128
SUBLANES = 8
VMEM_LIMIT_BYTES = 56 * 1024 * 1024

SCAN_CHUNK = 128
ATTN_BLOCK = 1024
ATTN_ROWS = 1024
ATTN_DIAG_ROWS = 512
LOG2E = math.log2(math.e)
PAGES_PER_STEP = 8
EXPERT_TILE = 512
EXPERT_TILE_SMALL = 256
SMEM_I32_CHUNK = 1024
ROW_COPY_UNROLL = 8


def _params(sem):
    return pltpu.CompilerParams(dimension_semantics=sem, vmem_limit_bytes=VMEM_LIMIT_BYTES)


def _nt_dot(a, b):
    return lax.dot_general(a, b, (((1,), (1,)), ((), ())), preferred_element_type=F32)


def _sigmoid(x):
    return 1.0 / (1.0 + jnp.exp(-x))


def _log_sigmoid(x):
    return jnp.minimum(x, 0.0) - jnp.log1p(jnp.exp(-jnp.abs(x)))


def _norm_mod(x, gain, shift, scale):
    ms = jnp.mean(x * x, axis=-1, keepdims=True)
    y = x * lax.rsqrt(ms + EPS) * gain
    return y * (1.0 + scale) + shift


class _Mod:
    def __init__(self, arr, d, tm, tiles_per_seq):
        self.arr = arr
        self.d = d
        self.tm = tm
        self.tiles_per_seq = tiles_per_seq
        self.per_token = arr.ndim == 2

    def spec(self, chunk):
        d, tps = self.d, self.tiles_per_seq
        if self.per_token:
            return pl.BlockSpec((self.tm, d), lambda i, *_: (i, chunk))
        return pl.BlockSpec((None, 1, d), lambda i, *_: (i // tps, 0, chunk))


def _ada_body(c_ref, w_ref, b_ref, o_ref):
    c = c_ref[...]
    a = c * _sigmoid(c)
    o_ref[0] = jnp.dot(a, w_ref[0], precision=HIGHEST, preferred_element_type=F32) + b_ref[0]


def _ada(c, w, b):
    n_layers, d, n = w.shape
    r = c.shape[0]
    tn = 1536 if n % 1536 == 0 else 1024
    return pl.pallas_call(
        _ada_body,
        grid=(n_layers, n // tn),
        in_specs=[pl.BlockSpec((r, d), lambda l, j: (0, 0)),
                  pl.BlockSpec((1, d, tn), lambda l, j: (l, 0, j)),
                  pl.BlockSpec((1, 1, tn), lambda l, j: (l, 0, j))],
        out_specs=pl.BlockSpec((1, r, tn), lambda l, j: (l, 0, j)),
        out_shape=jax.ShapeDtypeStruct((n_layers, r, n), F32),
        compiler_params=_params(("arbitrary", "arbitrary")),
        name="ada",
    )(c, w, b.reshape(n_layers, 1, n))


def _hgrn_in_body(x_ref, gain_ref, sh_ref, sc_ref, wq_ref, wf_ref, wi_ref, wg_ref,
                  q_ref, zf_ref, i_ref, g_ref):
    xn = _norm_mod(x_ref[...], gain_ref[...], sh_ref[...], sc_ref[...]).astype(BF16)
    q_ref[...] = jnp.dot(xn, wq_ref[...], preferred_element_type=F32).astype(q_ref.dtype)
    zf_ref[...] = jnp.dot(xn, wf_ref[...], preferred_element_type=F32)
    i_ref[...] = jnp.dot(xn, wi_ref[...], preferred_element_type=F32).astype(i_ref.dtype)
    g_ref[...] = jnp.dot(xn, wg_ref[...], preferred_element_type=F32).astype(g_ref.dtype)


def _hgrn_in_proj(h, gain, mod, w_in, layer, tm):
    t, d = h.shape
    row = pl.BlockSpec((tm, d), lambda i: (i, 0))
    part = lambda s: pl.BlockSpec((None, d, d), lambda i: (layer, 0, s))
    return pl.pallas_call(
        _hgrn_in_body,
        grid=(t // tm,),
        in_specs=[row, pl.BlockSpec((1, d), lambda i: (0, 0)), mod.spec(0), mod.spec(1),
                  part(0), part(1), part(2), part(3)],
        out_specs=[row, row, row, row],
        out_shape=[jax.ShapeDtypeStruct((t, d), BF16), jax.ShapeDtypeStruct((t, d), F32),
                   jax.ShapeDtypeStruct((t, d), BF16), jax.ShapeDtypeStruct((t, d), BF16)],
        compiler_params=_params(("arbitrary",)),
        name="hgrn_in_proj",
    )(h, gain.reshape(1, d), mod.arr, mod.arr, w_in, w_in, w_in, w_in)


def _split3(x):
    hi = x.astype(BF16)
    r1 = x - hi.astype(F32)
    mid = r1.astype(BF16)
    lo = (r1 - mid.astype(F32)).astype(BF16)
    return hi, mid, lo


def _hgrn_scan_body(q_ref, zf_ref, i_ref, g_ref, lb_ref, gn_ref, s0_ref, o_ref, st_ref,
                    *, chunk, n_valid, dk):
    c = pl.program_id(1)

    @pl.when(c == 0)
    def _():
        st_ref[...] = s0_ref[...]

    row = lax.broadcasted_iota(jnp.int32, (chunk, chunk), 0)
    col = lax.broadcasted_iota(jnp.int32, (chunk, chunk), 1)
    tri = jnp.where(row >= col, 1.0, 0.0).astype(BF16)
    n_levels = chunk.bit_length() - 1
    differ = row ^ col
    on_diag = differ == 0
    at_level = [(lax.shift_right_logical(differ, lv) == 1) & (row > col) for lv in range(n_levels)]
    tok = lax.broadcasted_iota(jnp.int32, (chunk, dk), 0)
    in_right = [(lax.shift_right_logical(tok, lv) & 1) == 1 for lv in range(n_levels)]
    quad = tok & 3
    if n_valid < chunk:
        valid = tok < n_valid
    for h in range(HG_HEADS):
        hs = slice(h * dk, (h + 1) * dk)
        q = q_ref[:, hs].astype(F32)
        zf = zf_ref[:, hs]
        v = i_ref[:, hs]
        lb = lb_ref[:, hs]
        e_abs = jnp.exp(-jnp.abs(zf))
        big = 1.0 / (1.0 + e_abs)
        small = e_abs * big
        nonneg = zf >= 0.0
        log_f = jnp.log(jnp.maximum(lb, TINY) + (1.0 - lb) * jnp.where(nonneg, big, small))
        k = (1.0 - lb) * jnp.where(nonneg, small, big)
        if n_valid < chunk:
            log_f = jnp.where(valid, log_f, 0.0)
            k = jnp.where(valid, k, 0.0)
        g_hi, g_mid, g_lo = _split3(log_f)
        b = (jnp.dot(tri, g_hi, preferred_element_type=F32)
             + jnp.dot(tri, g_mid, preferred_element_type=F32)
             + jnp.dot(tri, g_lo, preferred_element_type=F32))
        b_end = b[chunk - 1:chunk, :]
        a = jnp.where(on_diag, _nt_dot(q.astype(BF16), k.astype(BF16)), 0.0)
        for lv in range(n_levels):
            half = 1 << lv
            if lv == 0:
                dist = jnp.where(in_right[0], log_f, 0.0)
            elif lv == 1:
                edge = jnp.where(quad == 0, pltpu.roll(b, chunk - 1, 0),
                                 jnp.where(quad == 2, pltpu.roll(b, 1, 0),
                                           jnp.where(quad == 3, pltpu.roll(b, 2, 0), b)))
                dist = -jnp.abs(b - edge)
            else:
                nblk = chunk // (2 * half)
                edge = b.reshape(nblk, 2 * half, dk)[:, half - 1:half, :]
                edge = jnp.broadcast_to(edge, (nblk, 2 * half, dk)).reshape(chunk, dk)
                dist = -jnp.abs(b - edge)
            z = (jnp.where(in_right[lv], q, k) * jnp.exp(dist)).astype(BF16)
            a = jnp.where(at_level[lv], _nt_dot(z, z), a)
        s = st_ref[0, h]
        o = (jnp.dot(a.astype(BF16), v, preferred_element_type=F32)
             + jnp.dot((q * jnp.exp(b)).astype(BF16), s.astype(BF16), preferred_element_type=F32))
        k_d = (k * jnp.exp(b_end - b)).T.astype(BF16)
        decay = jnp.broadcast_to(jnp.exp(b_end), (dk, dk)).T
        st_ref[0, h] = decay * s + jnp.dot(k_d, v, preferred_element_type=F32)
        ms = jnp.mean(o * o, axis=-1, keepdims=True)
        gate = g_ref[:, hs].astype(F32)
        o = o * lax.rsqrt(ms + EPS) * gn_ref[...] * (gate * _sigmoid(gate))
        o_ref[:, hs] = o.astype(o_ref.dtype)


def _hgrn_scan(q, zf, i, g, lb, gnorm, s0, nseq, n_valid):
    tt, d = q.shape
    tp = tt // nseq
    nc = tp // SCAN_CHUNK
    dk = d // HG_HEADS
    row = pl.BlockSpec((SCAN_CHUNK, d), lambda b, c: (b * nc + c, 0))
    st = pl.BlockSpec((1, HG_HEADS, dk, dk), lambda b, c: (b, 0, 0, 0))
    body = functools.partial(_hgrn_scan_body, chunk=SCAN_CHUNK, n_valid=min(n_valid, SCAN_CHUNK), dk=dk)
    return pl.pallas_call(
        body,
        grid=(nseq, nc),
        in_specs=[row, row, row, row,
                  pl.BlockSpec((1, d), lambda b, c: (0, 0)),
                  pl.BlockSpec((1, dk), lambda b, c: (0, 0)),
                  st],
        out_specs=[row, st],
        out_shape=[jax.ShapeDtypeStruct((tt, d), BF16), jax.ShapeDtypeStruct(s0.shape, F32)],
        compiler_params=_params(("arbitrary", "arbitrary")),
        name="hgrn_scan",
    )(q, zf, i, g, lb.reshape(1, d), gnorm.reshape(1, dk), s0)


def _route_rows(lt):
    rows = [lt[e:e + 1, :] for e in range(N_EXPERTS)]
    m = functools.reduce(jnp.maximum, rows)
    ex = [jnp.exp(r - m) for r in rows]
    inv = 1.0 / functools.reduce(lambda x, y: x + y, ex)
    pr = [e * inv for e in ex]
    scores = []
    for gi in range(N_GROUPS):
        p4 = pr[gi * EXPERTS_PER_GROUP:(gi + 1) * EXPERTS_PER_GROUP]
        pairs = [p4[x] + p4[y] for x in range(EXPERTS_PER_GROUP) for y in range(x + 1, EXPERTS_PER_GROUP)]
        scores.append(functools.reduce(jnp.maximum, pairs))
    best = scores[0]
    sel = jnp.zeros(best.shape, jnp.int32)
    for gi in range(1, N_GROUPS):
        better = scores[gi] > best
        sel = jnp.where(better, gi, sel)
        best = jnp.where(better, scores[gi], best)
    cand = []
    for j in range(EXPERTS_PER_GROUP):
        cj = pr[(N_GROUPS - 1) * EXPERTS_PER_GROUP + j]
        for gi in range(N_GROUPS - 2, -1, -1):
            cj = jnp.where(sel == gi, pr[gi * EXPERTS_PER_GROUP + j], cj)
        cand.append(cj)

    def argmax4(vals):
        p, idx = vals[0], jnp.zeros(vals[0].shape, jnp.int32)
        for j in range(1, len(vals)):
            better = vals[j] > p
            idx = jnp.where(better, j, idx)
            p = jnp.where(better, vals[j], p)
        return p, idx

    p1, i1 = argmax4(cand)
    p2, i2 = argmax4([jnp.where(i1 == j, -1.0, cand[j]) for j in range(EXPERTS_PER_GROUP)])
    den = p1 + p2
    base = sel * EXPERTS_PER_GROUP
    return base + i1, base + i2, p1 / den, p2 / den


def _mix_router_body(a_ref, w_ref, h_ref, g1_ref, gain_ref, sh_ref, sc_ref, wrh_ref, wrl_ref, br_ref, upper_ref,
                     h_out, hn_out, gate_out, route_out, count_out):
    @pl.when(pl.program_id(0) == 0)
    def _():
        count_out[...] = jnp.zeros_like(count_out)

    mix = jnp.dot(a_ref[...].astype(BF16), w_ref[...], preferred_element_type=F32)
    h = h_ref[...] + g1_ref[...] * mix
    h_out[...] = h
    hn = _norm_mod(h, gain_ref[...], sh_ref[...], sc_ref[...])
    _store_row_tiles(hn_out, hn)
    x_hi = hn.astype(BF16)
    x_lo = (hn - x_hi.astype(F32)).astype(BF16)
    lt = (_nt_dot(wrh_ref[...], x_hi) + _nt_dot(wrl_ref[...], x_hi) + _nt_dot(wrh_ref[...], x_lo)
          + br_ref[...])
    e1, e2, g1, g2 = _route_rows(lt)
    tm = lt.shape[1]
    sub = lax.broadcasted_iota(jnp.int32, (LANES, tm), 0)
    gate_out[...] = (jnp.where(sub == 0, g1, 0.0) + jnp.where(sub == 1, g2, 0.0)).T
    eid = lax.broadcasted_iota(jnp.int32, (N_EXPERTS, tm), 0)
    pick1 = eid == e1
    pick2 = eid == e2
    hits = jnp.where(pick1, 1.0, 0.0) + jnp.where(pick2, 1.0, 0.0)
    before = jnp.dot(hits.astype(BF16), upper_ref[...], preferred_element_type=F32) + count_out[...]
    rank1 = jnp.sum(jnp.where(pick1, before, 0.0), axis=0, keepdims=True)
    rank2 = jnp.sum(jnp.where(pick2, before, 0.0), axis=0, keepdims=True)
    count_out[...] += jnp.sum(hits, axis=1, keepdims=True)
    field = lax.broadcasted_iota(jnp.int32, route_out.shape[1:], 0)
    route_out[0] = jnp.where(field == 0, e1, jnp.where(field == 1, e2, jnp.where(
        field == 2, rank1.astype(jnp.int32), rank2.astype(jnp.int32))))


def _mix_router(a, w_out, h, gain, mod, wr_hi, wr_lo, b_router, tm):
    t, d = h.shape
    row = pl.BlockSpec((tm, d), lambda i: (i, 0))
    const = lambda shape: pl.BlockSpec(shape, lambda i: (0,) * len(shape))
    upper = (jnp.arange(tm)[:, None] < jnp.arange(tm)[None, :]).astype(BF16)
    return pl.pallas_call(
        _mix_router_body,
        grid=(t // tm,),
        in_specs=[row, const((d, d)), row, mod.spec(2), const((1, d)), mod.spec(3), mod.spec(4),
                  const((N_EXPERTS, d)), const((N_EXPERTS, d)), const((N_EXPERTS, 1)), const((tm, tm))],
        out_specs=[row, pl.BlockSpec((tm * SUBLANES, LANES), lambda i: (i, 0)),
                   pl.BlockSpec((tm, LANES), lambda i: (i, 0)),
                   pl.BlockSpec((1, 4, tm), lambda i: (i, 0, 0)), const((N_EXPERTS, 1))],
        out_shape=[jax.ShapeDtypeStruct((t, d), F32), jax.ShapeDtypeStruct((t * SUBLANES, LANES), F32),
                   jax.ShapeDtypeStruct((t, LANES), F32), jax.ShapeDtypeStruct((t // tm, 4, tm), jnp.int32),
                   jax.ShapeDtypeStruct((N_EXPERTS, 1), F32)],
        compiler_params=_params(("arbitrary",)),
        name="mix_router",
    )(a, w_out, h, mod.arr, gain.reshape(1, d), mod.arr, mod.arr, wr_hi, wr_lo,
      b_router.reshape(N_EXPERTS, 1).astype(F32), upper)


def _store_row_tiles(ref, x):
    n, d = x.shape
    assert d == SUBLANES * LANES
    for c in range(SUBLANES):
        ref[pl.ds(c, n, stride=SUBLANES), :] = x[:, c * LANES:(c + 1) * LANES]


def _load_row_tiles(ref):
    n = ref.shape[0] // SUBLANES
    return jnp.concatenate([ref[pl.ds(c, n, stride=SUBLANES), :] for c in range(SUBLANES)], axis=1)


def _tile_rows(first_row, n_rows):
    return pl.ds(pl.multiple_of(first_row * SUBLANES, SUBLANES), n_rows * SUBLANES)


def _row_copy(src, src_row, dst, dst_row, sem):
    return pltpu.make_async_copy(src.at[_tile_rows(src_row, 1)], dst.at[_tile_rows(dst_row, 1)], sem)


def _index_slot(tm):
    return -(-2 * tm // SMEM_I32_CHUNK) * SMEM_I32_CHUNK


def _load_positions(pos_hbm, idx_ref, sem, step):
    n = idx_ref.shape[0]
    cp = pltpu.make_async_copy(pos_hbm.at[pl.ds(pl.multiple_of(step * n, n), n)], idx_ref, sem)
    cp.start()
    cp.wait()


def _dispatch_body(seg_end_ref, pos_hbm, x_ref, xs_out, idx_ref, zero_ref, idx_sem, row_sem, *, tm):
    etile = zero_ref.shape[0] // SUBLANES

    @pl.when(pl.program_id(0) == 0)
    def _():
        zero_ref[...] = jnp.zeros_like(zero_ref)

        def last_tile(e):
            return pltpu.make_async_copy(zero_ref, xs_out.at[_tile_rows(seg_end_ref[e] - etile, etile)], row_sem)

        def nonempty(e):
            return seg_end_ref[e] > (seg_end_ref[e - 1] if e else 0)

        for e in range(N_EXPERTS):
            pl.when(nonempty(e))(lambda e=e: last_tile(e).start())
        for e in range(N_EXPERTS):
            pl.when(nonempty(e))(lambda e=e: last_tile(e).wait())

        def unused_tile(i, carry):
            cp = pltpu.make_async_copy(zero_ref, xs_out.at[_tile_rows(i * etile, etile)], row_sem)
            cp.start()
            cp.wait()
            return carry

        n_tiles = xs_out.shape[0] // (etile * SUBLANES)
        lax.fori_loop(seg_end_ref[N_EXPERTS - 1] // etile, n_tiles, unused_tile, 0)

    _load_positions(pos_hbm, idx_ref, idx_sem, pl.program_id(0))

    def start(t, carry):
        _row_copy(x_ref, t, xs_out, idx_ref[t], row_sem).start(priority=0)
        _row_copy(x_ref, t, xs_out, idx_ref[tm + t], row_sem).start(priority=1)
        return carry

    lax.fori_loop(0, tm, start, 0, unroll=ROW_COPY_UNROLL)
    for _ in range(2):
        pltpu.make_async_copy(x_ref, xs_out.at[_tile_rows(0, tm)], row_sem).wait()


def _dispatch(hn, pos_tiles, seg_end, n_rows, tm, etile):
    t = hn.shape[0] // SUBLANES
    grid_spec = pltpu.PrefetchScalarGridSpec(
        num_scalar_prefetch=1,
        grid=(t // tm,),
        in_specs=[pl.BlockSpec(memory_space=pl.ANY), pl.BlockSpec((tm * SUBLANES, LANES), lambda i, se: (i, 0))],
        out_specs=pl.BlockSpec(memory_space=pl.ANY),
        scratch_shapes=[pltpu.SMEM((_index_slot(tm),), jnp.int32),
                        pltpu.VMEM((etile * SUBLANES, LANES), F32),
                        pltpu.SemaphoreType.DMA, pltpu.SemaphoreType.DMA])
    return pl.pallas_call(
        functools.partial(_dispatch_body, tm=tm),
        grid_spec=grid_spec,
        out_shape=jax.ShapeDtypeStruct((n_rows * SUBLANES, LANES), F32),
        compiler_params=_params(("arbitrary",)),
        name="moe_dispatch",
    )(seg_end, pos_tiles, hn)


def _expert_body(te_ref, nu_ref, x_ref, win_ref, wout_ref, y_ref):
    @pl.when(pl.program_id(0) < nu_ref[0])
    def _():
        hid = jnp.dot(_load_row_tiles(x_ref).astype(BF16), win_ref[0].astype(BF16), preferred_element_type=F32)
        de = hid.shape[1] // 2
        a, u = hid[:, :de], hid[:, de:]
        act = (a * _sigmoid(a) * u).astype(BF16)
        _store_row_tiles(y_ref, jnp.dot(act, wout_ref[0].astype(BF16), preferred_element_type=F32))

    @pl.when(pl.program_id(0) >= nu_ref[0])
    def _():
        y_ref[...] = jnp.zeros_like(y_ref)


def _experts(xs, tile_expert, n_used, w_in, w_out, layer, etile):
    _, n_e, d, d2 = w_in.shape
    n_tiles = xs.shape[0] // (etile * SUBLANES)
    tile = lambda i, te, nu: jnp.minimum(i, nu[0] - 1)
    rows = (etile * SUBLANES, LANES)
    grid_spec = pltpu.PrefetchScalarGridSpec(
        num_scalar_prefetch=2,
        grid=(n_tiles,),
        in_specs=[pl.BlockSpec(rows, lambda i, te, nu: (tile(i, te, nu), 0)),
                  pl.BlockSpec((None, 1, d, d2), lambda i, te, nu: (layer, te[tile(i, te, nu)], 0, 0)),
                  pl.BlockSpec((None, 1, d2 // 2, d), lambda i, te, nu: (layer, te[tile(i, te, nu)], 0, 0))],
        out_specs=pl.BlockSpec(rows, lambda i, te, nu: (i, 0)))
    return pl.pallas_call(
        _expert_body,
        grid_spec=grid_spec,
        out_shape=jax.ShapeDtypeStruct(xs.shape, F32),
        compiler_params=_params(("arbitrary",)),
        name="moe_experts",
    )(tile_expert, n_used, xs, w_in, w_out)


def _combine_body(pos_hbm, ys_hbm, gate_ref, h_ref, g2_ref, o_ref, idx_ref, y1_ref, y2_ref, idx_sem, row_sem,
                  *, tm):
    _load_positions(pos_hbm, idx_ref, idx_sem, pl.program_id(0))

    def start(t, carry):
        _row_copy(ys_hbm, idx_ref[t], y1_ref, t, row_sem).start(priority=0)
        _row_copy(ys_hbm, idx_ref[tm + t], y2_ref, t, row_sem).start(priority=1)
        return carry

    lax.fori_loop(0, tm, start, 0, unroll=ROW_COPY_UNROLL)
    for y_ref in (y1_ref, y2_ref):
        pltpu.make_async_copy(ys_hbm.at[_tile_rows(0, tm)], y_ref, row_sem).wait()
    gates = gate_ref[...]
    moe = gates[:, 0:1] * _load_row_tiles(y1_ref) + gates[:, 1:2] * _load_row_tiles(y2_ref)
    o_ref[...] = h_ref[...] + g2_ref[...] * moe


def _combine(ys, pos_tiles, gates, h, mod, tm):
    t, d = h.shape
    row = pl.BlockSpec((tm, d), lambda i: (i, 0))
    return pl.pallas_call(
        functools.partial(_combine_body, tm=tm),
        grid=(t // tm,),
        in_specs=[pl.BlockSpec(memory_space=pl.ANY), pl.BlockSpec(memory_space=pl.ANY),
                  pl.BlockSpec((tm, LANES), lambda i: (i, 0)), row, mod.spec(5)],
        out_specs=row,
        out_shape=jax.ShapeDtypeStruct((t, d), F32),
        scratch_shapes=[pltpu.SMEM((_index_slot(tm),), jnp.int32),
                        pltpu.VMEM((tm * SUBLANES, LANES), F32), pltpu.VMEM((tm * SUBLANES, LANES), F32),
                        pltpu.SemaphoreType.DMA, pltpu.SemaphoreType.DMA],
        compiler_params=_params(("arbitrary",)),
        name="moe_combine",
    )(pos_tiles, ys, gates, h, mod.arr)


def _moe(hn, gates, route, counts, h, mod, w_in, w_out, layer, tm):
    t, d = h.shape
    etile = EXPERT_TILE if 2 * t >= 4 * N_EXPERTS * EXPERT_TILE else EXPERT_TILE_SMALL
    n_tiles = 2 * t // etile + N_EXPERTS
    n_rows = n_tiles * etile
    cnt = counts.reshape(N_EXPERTS).astype(jnp.int32)
    padded = (cnt + (etile - 1)) // etile * etile
    experts = jnp.arange(N_EXPERTS, dtype=jnp.int32)
    seg_end = jnp.sum(jnp.where(experts[None, :] <= experts[:, None], padded[None, :], 0), axis=1)
    seg_start = seg_end - padded
    first_row = jnp.sum(jnp.where(route[:, :2, :, None] == experts, seg_start, 0), axis=-1)
    pos_tiles = (first_row + route[:, 2:]).reshape(t // tm, 2 * tm)
    if _index_slot(tm) != 2 * tm:
        pos_tiles = jnp.pad(pos_tiles, ((0, 0), (0, _index_slot(tm) - 2 * tm)))
    pos_tiles = pos_tiles.reshape(-1)
    tile_row = jnp.arange(n_tiles, dtype=jnp.int32) * etile
    tile_expert = jnp.minimum(jnp.sum((seg_end[None, :] <= tile_row[:, None]).astype(jnp.int32), axis=1),
                              N_EXPERTS - 1)
    n_used = seg_end[-1:] // etile
    xs = _dispatch(hn, pos_tiles, seg_end, n_rows, tm, etile)
    ys = _experts(xs, tile_expert, n_used, w_in, w_out, layer, etile)
    return _combine(ys, pos_tiles, gates, h, mod, tm)


def _head_norm(y, gsum_ref, gain_ref, head_dim):
    ssum = jnp.dot((y * y).astype(BF16), gsum_ref[...], preferred_element_type=F32)
    return y * lax.rsqrt(ssum * (1.0 / head_dim) + EPS) * gain_ref[...]


def _head_sum_matrix(d, head_dim):
    r = jnp.arange(d) // head_dim
    return (r[:, None] == r[None, :]).astype(BF16)


def _q_proj_body(x_ref, gain_ref, sh_ref, sc_ref, w_ref, gsum_ref, qn_ref, q_ref, *, head_dim, q_scale):
    xn = _norm_mod(x_ref[...], gain_ref[...], sh_ref[...], sc_ref[...]).astype(BF16)
    y = jnp.dot(xn, w_ref[...], preferred_element_type=F32)
    q = _head_norm(y, gsum_ref, qn_ref, head_dim) * q_scale
    q_ref[...] = q.astype(q_ref.dtype)


def _q_proj(h, gain, mod, w, gsum, q_norm, tm, out_dtype, q_scale):
    t, d = h.shape
    head_dim = d // FOX_HEADS
    row = pl.BlockSpec((tm, d), lambda i: (i, 0))
    const = lambda shape: pl.BlockSpec(shape, lambda i: (0,) * len(shape))
    return pl.pallas_call(
        functools.partial(_q_proj_body, head_dim=head_dim, q_scale=q_scale),
        grid=(t // tm,),
        in_specs=[row, const((1, d)), mod.spec(0), mod.spec(1), const((d, d)), const((d, d)), const((1, d))],
        out_specs=row,
        out_shape=jax.ShapeDtypeStruct((t, d), out_dtype),
        compiler_params=_params(("arbitrary",)),
        name="q_proj",
    )(h, gain.reshape(1, d), mod.arr, mod.arr, w, gsum, jnp.tile(q_norm, FOX_HEADS).reshape(1, d))


def _kv_proj_body(x_ref, gain_ref, sh_ref, sc_ref, wk_ref, wv_ref, wf_ref, gsum_ref, kn_ref, bf_ref,
                  k_ref, v_ref, kb_ref, vb_ref, lf_ref, *, head_dim):
    xn = _norm_mod(x_ref[...], gain_ref[...], sh_ref[...], sc_ref[...]).astype(BF16)
    k = _head_norm(jnp.dot(xn, wk_ref[...], preferred_element_type=F32), gsum_ref, kn_ref, head_dim)
    k_ref[...] = k
    kb_ref[...] = k.astype(BF16)
    v = jnp.dot(xn, wv_ref[...], preferred_element_type=F32)
    v_ref[...] = v
    vb_ref[...] = v.astype(BF16)
    zf = jnp.dot(xn, wf_ref[...], preferred_element_type=F32) + bf_ref[...]
    lf_ref[...] = _log_sigmoid(zf)[:, :lf_ref.shape[1]]


def _kv_proj(h, gain, mod, wk, wv, wf, gsum, k_norm, b_fgate, tm):
    t, d = h.shape
    head_dim = d // FOX_HEADS
    row = pl.BlockSpec((tm, d), lambda i: (i, 0))
    const = lambda shape: pl.BlockSpec(shape, lambda i: (0,) * len(shape))
    bf = jnp.pad(b_fgate.astype(F32), (0, LANES - FOX_HEADS)).reshape(1, LANES)
    return pl.pallas_call(
        functools.partial(_kv_proj_body, head_dim=head_dim),
        grid=(t // tm,),
        in_specs=[row, const((1, d)), mod.spec(0), mod.spec(1), const((d, d)), const((d, d)),
                  const((d, LANES)), const((d, d)), const((1, d)), const((1, LANES))],
        out_specs=[row, row, row, row, pl.BlockSpec((tm, FOX_HEADS), lambda i: (i, 0))],
        out_shape=[jax.ShapeDtypeStruct((t, d), F32), jax.ShapeDtypeStruct((t, d), F32),
                   jax.ShapeDtypeStruct((t, d), BF16), jax.ShapeDtypeStruct((t, d), BF16),
                   jax.ShapeDtypeStruct((t, FOX_HEADS), F32)],
        compiler_params=_params(("arbitrary",)),
        name="kv_proj",
    )(h, gain.reshape(1, d), mod.arr, mod.arr, wk, wv, wf, gsum,
      jnp.tile(k_norm, FOX_HEADS).reshape(1, d), bf)


def _cumsum_lanes_body(x_ref, o_ref, carry_ref, *, carry_rows):
    j = pl.program_id(1)

    @pl.when(j == 0)
    def _():
        carry_ref[...] = jnp.zeros_like(carry_ref)

    w = x_ref.shape[-1]
    r = lax.broadcasted_iota(jnp.int32, (w, w), 0)
    c = lax.broadcasted_iota(jnp.int32, (w, w), 1)
    upper = jnp.where(r <= c, 1.0, 0.0).astype(F32)
    y = jnp.dot(x_ref[0], upper, precision=HIGHEST, preferred_element_type=F32)
    if carry_rows:
        y = y + carry_ref[...]
        carry_ref[...] = y[:, w - 1:w]
    o_ref[0] = y


def _cumsum_lanes(x, width, carry):
    b, r, t = x.shape
    spec = pl.BlockSpec((1, r, width), lambda i, j: (i, 0, j))
    return pl.pallas_call(
        functools.partial(_cumsum_lanes_body, carry_rows=carry),
        grid=(b, t // width),
        in_specs=[spec],
        out_specs=spec,
        out_shape=jax.ShapeDtypeStruct(x.shape, F32),
        scratch_shapes=[pltpu.VMEM((r, 1), F32)],
        compiler_params=_params(("arbitrary", "arbitrary")),
        name="cumsum_lanes",
    )(x)


def _attn_prompt_body(qt_ref, kt_ref, q_ref, k_ref, v_ref, f_ref, o_ref, m_ref, acc_ref,
                      *, blk, rows, diag_rows, head_dim):
    p = pl.program_id(2)
    qi = qt_ref[p]
    ki = kt_ref[p]

    @pl.when(ki == 0)
    def _():
        m_ref[...] = jnp.full(m_ref.shape, NEG_BIG, F32)
        acc_ref[...] = jnp.zeros_like(acc_ref)

    def process(masked):
        k = k_ref[...]
        v = v_ref[...]
        f = f_ref[...]
        first_v = lax.broadcasted_iota(jnp.int32, v.shape, 1) < head_dim
        ones = jnp.ones_like(v)
        v_heads = (jnp.where(first_v, v, ones), jnp.where(first_v, ones, v))
        rb = min(rows, diag_rows) if masked else rows
        m_olds = {(hh, r0): m_ref[hh, r0:r0 + rb, :] for hh in range(2) for r0 in range(0, blk, rb)}
        acc_olds = {(hh, r0): acc_ref[hh, r0:r0 + rb, :] for hh in range(2) for r0 in range(0, blk, rb)}
        m_news, acc_news = {}, {}
        for r0 in range(0, blk, rb):
            n_keys = r0 + rb if masked else blk
            q = q_ref[r0:r0 + rb, :]
            first_q = lax.broadcasted_iota(jnp.int32, q.shape, 1) < head_dim
            zero = jnp.zeros_like(q)
            for hh, qh in enumerate((jnp.where(first_q, q, zero), jnp.where(first_q, zero, q))):
                s = _nt_dot(qh, k[:n_keys]) - f[hh:hh + 1, :n_keys]
                if masked:
                    row = lax.broadcasted_iota(jnp.int32, s.shape, 0) + r0
                    col = lax.broadcasted_iota(jnp.int32, s.shape, 1)
                    s = jnp.where(col <= row, s, NEG_BIG)
                m_old = m_olds[hh, r0]
                m_new = jnp.maximum(m_old, jnp.max(s, axis=1, keepdims=True))
                alpha = jnp.exp2(m_old - m_new)
                pr = jnp.concatenate([jnp.exp2(s[:, c * LANES:(c + 1) * LANES] - m_new)
                                      for c in range(n_keys // LANES)], axis=1).astype(BF16)
                acc_news[hh, r0] = alpha * acc_olds[hh, r0] + jnp.dot(pr, v_heads[hh][:n_keys],
                                                                      preferred_element_type=F32)
                m_news[hh, r0] = m_new
        for (hh, r0), m_new in m_news.items():
            m_ref[hh, r0:r0 + rb, :] = m_new
            acc_ref[hh, r0:r0 + rb, :] = acc_news[hh, r0]

    @pl.when(ki < qi)
    def _():
        process(False)

    @pl.when(ki == qi)
    def _():
        process(True)
        a0 = acc_ref[0]
        a1 = acc_ref[1]
        first = lax.broadcasted_iota(jnp.int32, a0.shape, 1) < head_dim
        inv0 = 1.0 / a0[:, head_dim:head_dim + 1]
        inv1 = 1.0 / a1[:, 0:1]
        o_ref[...] = jnp.where(first, a0 * inv0, a1 * inv1).astype(o_ref.dtype)


def _attn_prompt(q, k, v, f_pairs, nseq):
    tt, d = q.shape
    t = tt // nseq
    blk = min(ATTN_BLOCK, t)
    rows = min(ATTN_ROWS, blk)
    nb = t // blk
    head_dim = d // FOX_HEADS
    pairs = [(qi, ki) for qi in range(nb) for ki in range(qi + 1)]
    q_tab = jnp.asarray([p[0] for p in pairs], jnp.int32)
    k_tab = jnp.asarray([p[1] for p in pairs], jnp.int32)
    grid_spec = pltpu.PrefetchScalarGridSpec(
        num_scalar_prefetch=2,
        grid=(nseq, d // LANES, len(pairs)),
        in_specs=[pl.BlockSpec((blk, LANES), lambda b, hp, p, qt, kt: (b * nb + qt[p], hp)),
                  pl.BlockSpec((blk, LANES), lambda b, hp, p, qt, kt: (b * nb + kt[p], hp)),
                  pl.BlockSpec((blk, LANES), lambda b, hp, p, qt, kt: (b * nb + kt[p], hp)),
                  pl.BlockSpec((None, None, 2, blk), lambda b, hp, p, qt, kt: (b, hp, 0, kt[p]))],
        out_specs=pl.BlockSpec((blk, LANES), lambda b, hp, p, qt, kt: (b * nb + qt[p], hp)),
        scratch_shapes=[pltpu.VMEM((2, blk, LANES), F32), pltpu.VMEM((2, blk, LANES), F32)])
    return pl.pallas_call(
        functools.partial(_attn_prompt_body, blk=blk, rows=rows, diag_rows=min(ATTN_DIAG_ROWS, blk),
                          head_dim=head_dim),
        grid_spec=grid_spec,
        out_shape=jax.ShapeDtypeStruct((tt, d), BF16),
        compiler_params=_params(("arbitrary", "arbitrary", "arbitrary")),
        name="attn_prompt",
    )(q_tab, k_tab, q, k, v, f_pairs)


def _gather_cum_body(pt_ref, cp_ref, new_ref, o_ref, *, n_pages, page):
    b = pl.program_id(0)

    def step(j, carry):
        blk = cp_ref[pt_ref[b * n_pages + j]]
        o_ref[0, :, pl.ds(pl.multiple_of(j * page, page), page)] = blk + carry
        return carry + blk[:, page - 1:page]

    carry = lax.fori_loop(0, n_pages, step, jnp.zeros((cp_ref.shape[1], 1), F32), unroll=8)
    o_ref[0, :, n_pages * page:] = new_ref[0] + carry


def _gather_cum(page_table, cum_pages, cum_new):
    nseq, n_pages = page_table.shape
    n_pool, hh, page = cum_pages.shape
    grid_spec = pltpu.PrefetchScalarGridSpec(
        num_scalar_prefetch=1,
        grid=(nseq,),
        in_specs=[pl.BlockSpec((n_pool, hh, page), lambda b, pt: (0, 0, 0)),
                  pl.BlockSpec((1, hh, page), lambda b, pt: (b, 0, 0))],
        out_specs=pl.BlockSpec((1, hh, (n_pages + 1) * page), lambda b, pt: (b, 0, 0)))
    return pl.pallas_call(
        functools.partial(_gather_cum_body, n_pages=n_pages, page=page),
        grid_spec=grid_spec,
        out_shape=jax.ShapeDtypeStruct((nseq, hh, (n_pages + 1) * page), F32),
        compiler_params=_params(("arbitrary",)),
        name="gather_cum",
    )(page_table.reshape(-1), cum_pages, cum_new)


def _attn_paged_body(pt_ref, *refs, n_q, page, pps, n_steps, head_dim):
    q_ref = refs[0]
    k_refs = refs[1:1 + pps]
    v_refs = refs[1 + pps:1 + 2 * pps]
    kn_ref, vn_ref, f_ref, o_ref, qbd_ref, m_ref, l_ref, acc_ref = refs[1 + 2 * pps:]
    j = pl.program_id(1)
    d = q_ref.shape[1]
    rows = FOX_HEADS * n_q

    @pl.when(j == 0)
    def _():
        q = q_ref[...]
        lane_head = lax.broadcasted_iota(jnp.int32, q.shape, 1) // head_dim
        for h in range(FOX_HEADS):
            qbd_ref[h * n_q:(h + 1) * n_q, :] = jnp.where(lane_head == h, q, 0.0).astype(BF16)
        m_ref[...] = jnp.full(m_ref.shape, NEG_BIG, F32)
        l_ref[...] = jnp.zeros_like(l_ref)
        acc_ref[...] = jnp.zeros_like(acc_ref)

    def attend(k, v, f, mask):
        bias = jnp.concatenate([jnp.broadcast_to(f[h:h + 1, :], (n_q, page)) for h in range(FOX_HEADS)], axis=0)
        s = _nt_dot(qbd_ref[...], k) - bias
        if mask is not None:
            s = jnp.where(mask, s, NEG_BIG)
        m_old = m_ref[...]
        m_new = jnp.maximum(m_old, jnp.max(s, axis=1, keepdims=True))
        alpha = jnp.exp(m_old - m_new)
        pr = jnp.exp(s - m_new)
        l_ref[...] = alpha * l_ref[...] + jnp.sum(pr, axis=1, keepdims=True)
        m_ref[...] = m_new
        acc_ref[...] = alpha * acc_ref[...] + jnp.dot(pr.astype(BF16), v, preferred_element_type=F32)

    @pl.when(j < n_steps - 1)
    def _():
        for u in range(pps):
            off = pl.multiple_of((j * pps + u) * page, page)
            attend(k_refs[u][...].astype(BF16), v_refs[u][...].astype(BF16), f_ref[0, :, pl.ds(off, page)], None)

    @pl.when(j == n_steps - 1)
    def _():
        pad = jnp.zeros((page - n_q, d), F32)
        k = jnp.concatenate([kn_ref[...], pad], axis=0).astype(BF16)
        v = jnp.concatenate([vn_ref[...], pad], axis=0).astype(BF16)
        t_of_row = lax.broadcasted_iota(jnp.int32, (rows, page), 0) % n_q
        key = lax.broadcasted_iota(jnp.int32, (rows, page), 1)
        attend(k, v, f_ref[0, :, (n_steps - 1) * pps * page:], key <= t_of_row)
        acc = acc_ref[...] * (1.0 / l_ref[...])
        lane_head = lax.broadcasted_iota(jnp.int32, (n_q, d), 1) // head_dim
        out = jnp.zeros((n_q, d), F32)
        for h in range(FOX_HEADS):
            out = out + jnp.where(lane_head == h, acc[h * n_q:(h + 1) * n_q, :], 0.0)
        o_ref[...] = out


def _attn_paged(q, cache_k, cache_v, k_new, v_new, f_all, page_table):
    tt, d = q.shape
    nseq, n_pages = page_table.shape
    n_q = tt // nseq
    page = cache_k.shape[1]
    pps = PAGES_PER_STEP
    n_steps = n_pages // pps + 1
    head_dim = d // FOX_HEADS
    rows = FOX_HEADS * n_q

    def page_spec(u):
        def index(b, j, pt):
            return (pt[b * n_pages + jnp.minimum(j, n_steps - 2) * pps + u], 0, 0)
        return pl.BlockSpec((None, page, d), index)

    tok = pl.BlockSpec((n_q, d), lambda b, j, pt: (b, 0))
    grid_spec = pltpu.PrefetchScalarGridSpec(
        num_scalar_prefetch=1,
        grid=(nseq, n_steps),
        in_specs=([tok] + [page_spec(u) for u in range(pps)] + [page_spec(u) for u in range(pps)]
                  + [tok, tok, pl.BlockSpec((1, FOX_HEADS, f_all.shape[2]), lambda b, j, pt: (b, 0, 0))]),
        out_specs=tok,
        scratch_shapes=[pltpu.VMEM((rows, d), BF16), pltpu.VMEM((rows, 1), F32),
                        pltpu.VMEM((rows, 1), F32), pltpu.VMEM((rows, d), F32)])
    body = functools.partial(_attn_paged_body, n_q=n_q, page=page, pps=pps, n_steps=n_steps,
                             head_dim=head_dim)
    return pl.pallas_call(
        body,
        grid_spec=grid_spec,
        out_shape=jax.ShapeDtypeStruct((tt, d), F32),
        compiler_params=_params(("arbitrary", "arbitrary")),
        name="attn_paged",
    )(page_table.reshape(-1), q, *([cache_k] * pps), *([cache_v] * pps), k_new, v_new, f_all)


def _lower_bounds(lb_logits):
    p = jax.nn.softmax(lb_logits.astype(F32), axis=0)
    return jnp.cumsum(p, axis=0) - p[0]


def _trunk(x, nseq, mods, mod_kv, s0, past, w):
    tt, d = x.shape
    t = tt // nseq
    tm = min(512, tt)
    tiles_per_seq = max(t // tm, 1)
    n_a = w["w_in_a"].shape[0]
    depth = w["w_ada"].shape[0]
    head_dim = d // FOX_HEADS
    lbs = _lower_bounds(w["lb_logits"])
    mk = lambda arr: _Mod(arr, d, tm, tiles_per_seq)

    h = x
    states = []
    for l in range(depth):
        mod = mk(mods[l])
        if l < n_a:
            q, zf, i, g = _hgrn_in_proj(h, w["norm_mix"][l], mod, w["w_in_a"], l, tm)
            tp = -(-t // SCAN_CHUNK) * SCAN_CHUNK
            if tp != t:
                padr = lambda a: jnp.pad(a.reshape(nseq, t, d), ((0, 0), (0, tp - t), (0, 0))).reshape(nseq * tp, d)
                q, zf, i, g = padr(q), padr(zf), padr(i), padr(g)
            o, s_t = _hgrn_scan(q, zf, i, g, lbs[l], w["gnorm_a"][l], s0[l], nseq, t)
            if tp != t:
                o = o.reshape(nseq, tp, d)[:, :t].reshape(tt, d)
            states.append(s_t)
            mix_in, w_out = o, w["w_out_a"][l]
        else:
            j = l - n_a
            if past is None:
                q = _q_proj(h, w["norm_mix"][l], mod, w["w_q_b"][j], w["gsum"], w["q_norm"][j], tm, BF16,
                            head_dim ** -0.5 * LOG2E)
                mix_in = _attn_prompt(q, k_bf, v_bf, f_keys, nseq)
            else:
                q = _q_proj(h, w["norm_mix"][l], mod, w["w_q_b"][j], w["gsum"], w["q_norm"][j], tm, F32,
                            head_dim ** -0.5)
                mix_in = _attn_paged(q, past[0], past[1], k_new, v_new, f_keys, past[3])
            w_out = w["w_out_b"][j]
        h, hn, gates, route, counts = _mix_router(mix_in, w_out, h, w["norm_ffn"][l], mod, w["wr_hi"],
                                                  w["wr_lo"], w["b_router"], tm)
        h = _moe(hn, gates, route, counts, h, mod, w["w_exp_in"], w["w_exp_out"], l, tm)
        if l == n_a - 1:
            k_new, v_new, k_bf, v_bf, logf_new = _kv_proj(
                h, w["norm_kv"], mk(mod_kv), w["w_k"], w["w_v"], w["w_f"], w["gsum"], w["k_norm"],
                w["b_fgate"], tm)
            lf_t = logf_new.reshape(nseq, t, FOX_HEADS).transpose(0, 2, 1)
            if past is None:
                f_cum = _cumsum_lanes(lf_t, min(512, t), carry=True)
                f_keys = (f_cum * LOG2E).reshape(nseq, FOX_HEADS // 2, 2, t)
            else:
                cache_logf, page_table = past[2], past[3]
                page = cache_logf.shape[1]
                pool_t = cache_logf.astype(F32).transpose(0, 2, 1)
                n_pool = pool_t.shape[0]
                rows = 64 if n_pool % 64 == 0 else 1
                cum_pages = _cumsum_lanes(pool_t.reshape(n_pool // rows, rows * FOX_HEADS, page), page,
                                          carry=False).reshape(n_pool, FOX_HEADS, page)
                new_pad = jnp.pad(lf_t, ((0, 0), (0, 0), (0, page - t)))
                cum_new = _cumsum_lanes(new_pad, page, carry=False)
                f_keys = _gather_cum(page_table, cum_pages, cum_new)
    return h, jnp.stack(states), k_new, v_new, logf_new


def kernel(x_prompt, x_sample, cache_k, cache_v, cache_logf, state_hgrn, page_table, c_prompt, c_sample,
           w_ada, b_ada, norm_mix, norm_ffn, w_in_a, lb_logits, gnorm_a, w_out_a, norm_kv, w_ada_kv,
           b_ada_kv, w_kv, b_fgate, k_norm, w_q_b, q_norm, w_out_b, w_router, b_router, w_exp_in, w_exp_out):
    nb, seq, d = x_prompt.shape
    ns, dseq, _ = x_sample.shape
    n_a = w_in_a.shape[0]
    depth = w_ada.shape[0]
    hd = FOX_HEADS * (d // FOX_HEADS)
    dk = d // HG_HEADS

    nrow = nb + ns
    nrow_pad = -(-nrow // SUBLANES) * SUBLANES
    c_all = jnp.pad(jnp.concatenate([c_prompt, c_sample], axis=0).astype(F32), ((0, nrow_pad - nrow), (0, 0)))
    mod_all = _ada(c_all, w_ada, b_ada)
    mod_kv_all = _ada(c_all, w_ada_kv[None], b_ada_kv[None])[0]
    mods_p = [mod_all[l, :nb].reshape(nb, 1, 6 * d) for l in range(depth)]
    mods_s = [jnp.repeat(mod_all[l, nb:nrow], dseq, axis=0) for l in range(depth)]
    mod_kv_p = mod_kv_all[:nb].reshape(nb, 1, 2 * d)
    mod_kv_s = jnp.repeat(mod_kv_all[nb:nrow], dseq, axis=0)

    wr_t = w_router.astype(F32).T
    wr_hi = wr_t.astype(BF16)
    wr_lo = (wr_t - wr_hi.astype(F32)).astype(BF16)
    w = dict(
        w_ada=w_ada, lb_logits=lb_logits, norm_mix=norm_mix, norm_ffn=norm_ffn, gnorm_a=gnorm_a,
        w_in_a=w_in_a.astype(BF16), w_out_a=w_out_a.astype(BF16), norm_kv=norm_kv,
        w_k=w_kv[:, :hd].astype(BF16), w_v=w_kv[:, hd:2 * hd].astype(BF16),
        w_f=jnp.pad(w_kv[:, 2 * hd:], ((0, 0), (0, LANES - FOX_HEADS))).astype(BF16),
        b_fgate=b_fgate, k_norm=k_norm, w_q_b=w_q_b.astype(BF16), q_norm=q_norm,
        w_out_b=w_out_b.astype(BF16), wr_hi=wr_hi, wr_lo=wr_lo, b_router=b_router,
        w_exp_in=w_exp_in, w_exp_out=w_exp_out,
        gsum=_head_sum_matrix(d, d // FOX_HEADS),
    )

    s0_p = jnp.zeros((n_a, nb, HG_HEADS, dk, dk), F32)
    y_p, st_p, k_p, v_p, lf_p = _trunk(x_prompt.reshape(nb * seq, d), nb, mods_p, mod_kv_p, s0_p, None, w)

    n_pool, page = cache_k.shape[0], cache_k.shape[1]
    past = (cache_k.reshape(n_pool, page, hd), cache_v.reshape(n_pool, page, hd), cache_logf, page_table)
    y_s, st_s, k_s, v_s, lf_s = _trunk(x_sample.reshape(ns * dseq, d), ns, mods_s, mod_kv_s,
                                       state_hgrn.astype(F32), past, w)

    hs = (FOX_HEADS, d // FOX_HEADS)
    return (y_p.reshape(nb, seq, d), y_s.reshape(ns, dseq, d),
            st_p.astype(state_hgrn.dtype), st_s.astype(state_hgrn.dtype),
            k_p.reshape(nb, seq, *hs).astype(cache_k.dtype), v_p.reshape(nb, seq, *hs).astype(cache_v.dtype),
            lf_p.reshape(nb, seq, FOX_HEADS).astype(cache_logf.dtype),
            k_s.reshape(ns, dseq, *hs).astype(cache_k.dtype), v_s.reshape(ns, dseq, *hs).astype(cache_v.dtype),
            lf_s.reshape(ns, dseq, FOX_HEADS).astype(cache_logf.dtype))
```

```python
import functools
import math

import jax
import jax.numpy as jnp
from jax import lax
from jax.experimental import pallas as pl
from jax.experimental.pallas import tpu as pltpu

F32 = jnp.float32
BF16 = jnp.bfloat16
HIGHEST = lax.Precision.HIGHEST

HG_HEADS = 8
FOX_HEADS = 16
N_EXPERTS = 16
N_GROUPS = 4
EXPERTS_PER_GROUP = N_EXPERTS // N_GROUPS
EPS = 1e-6
NEG_BIG = -1e30
TINY = 1e-30

LANES = 128
SUBLANES = 8
VMEM_LIMIT_BYTES = 56 * 1024 * 1024

SCAN_CHUNK = 128
ATTN_BLOCK = 1024
ATTN_ROWS = 1024
ATTN_DIAG_ROWS = 512
LOG2E = math.log2(math.e)
PAGES_PER_STEP = 8
EXPERT_TILE = 512
EXPERT_TILE_SMALL = 256
ROW_COPY_UNROLL = 8


def _params(sem):
    return pltpu.CompilerParams(dimension_semantics=sem, vmem_limit_bytes=VMEM_LIMIT_BYTES)


def _nt_dot(a, b):
    return lax.dot_general(a, b, (((1,), (1,)), ((), ())), preferred_element_type=F32)


def _sigmoid(x):
    return 1.0 / (1.0 + jnp.exp(-x))


def _log_sigmoid(x):
    return jnp.minimum(x, 0.0) - jnp.log1p(jnp.exp(-jnp.abs(x)))


def _norm_mod(x, gain, shift, scale):
    ms = jnp.mean(x * x, axis=-1, keepdims=True)
    y = x * lax.rsqrt(ms + EPS) * gain
    return y * (1.0 + scale) + shift


class _Mod:
    def __init__(self, arr, d, tm, tiles_per_seq):
        self.arr = arr
        self.d = d
        self.tm = tm
        self.tiles_per_seq = tiles_per_seq
        self.per_token = arr.ndim == 2

    def spec(self, chunk):
        d, tps = self.d, self.tiles_per_seq
        if self.per_token:
            return pl.BlockSpec((self.tm, d), lambda i, *_: (i, chunk))
        return pl.BlockSpec((None, 1, d), lambda i, *_: (i // tps, 0, chunk))


def _ada_body(c_ref, w_ref, b_ref, o_ref):
    c = c_ref[...]
    a = c * _sigmoid(c)
    o_ref[0] = jnp.dot(a, w_ref[0], precision=HIGHEST, preferred_element_type=F32) + b_ref[0]


def _ada(c, w, b):
    n_layers, d, n = w.shape
    r = c.shape[0]
    tn = 1536 if n % 1536 == 0 else 1024
    return pl.pallas_call(
        _ada_body,
        grid=(n_layers, n // tn),
        in_specs=[pl.BlockSpec((r, d), lambda l, j: (0, 0)),
                  pl.BlockSpec((1, d, tn), lambda l, j: (l, 0, j)),
                  pl.BlockSpec((1, 1, tn), lambda l, j: (l, 0, j))],
        out_specs=pl.BlockSpec((1, r, tn), lambda l, j: (l, 0, j)),
        out_shape=jax.ShapeDtypeStruct((n_layers, r, n), F32),
        compiler_params=_params(("arbitrary", "arbitrary")),
        name="ada",
    )(c, w, b.reshape(n_layers, 1, n))


def _hgrn_in_body(x_ref, gain_ref, sh_ref, sc_ref, wq_ref, wf_ref, wi_ref, wg_ref,
                  q_ref, zf_ref, i_ref, g_ref):
    xn = _norm_mod(x_ref[...], gain_ref[...], sh_ref[...], sc_ref[...]).astype(BF16)
    q_ref[...] = jnp.dot(xn, wq_ref[...], preferred_element_type=F32).astype(q_ref.dtype)
    zf_ref[...] = jnp.dot(xn, wf_ref[...], preferred_element_type=F32)
    i_ref[...] = jnp.dot(xn, wi_ref[...], preferred_element_type=F32).astype(i_ref.dtype)
    g_ref[...] = jnp.dot(xn, wg_ref[...], preferred_element_type=F32).astype(g_ref.dtype)


def _hgrn_in_proj(h, gain, mod, w_in, layer, tm):
    t, d = h.shape
    row = pl.BlockSpec((tm, d), lambda i: (i, 0))
    part = lambda s: pl.BlockSpec((None, d, d), lambda i: (layer, 0, s))
    return pl.pallas_call(
        _hgrn_in_body,
        grid=(t // tm,),
        in_specs=[row, pl.BlockSpec((1, d), lambda i: (0, 0)), mod.spec(0), mod.spec(1),
                  part(0), part(1), part(2), part(3)],
        out_specs=[row, row, row, row],
        out_shape=[jax.ShapeDtypeStruct((t, d), BF16), jax.ShapeDtypeStruct((t, d), F32),
                   jax.ShapeDtypeStruct((t, d), BF16), jax.ShapeDtypeStruct((t, d), BF16)],
        compiler_params=_params(("arbitrary",)),
        name="hgrn_in_proj",
    )(h, gain.reshape(1, d), mod.arr, mod.arr, w_in, w_in, w_in, w_in)


def _split3(x):
    hi = x.astype(BF16)
    r1 = x - hi.astype(F32)
    mid = r1.astype(BF16)
    lo = (r1 - mid.astype(F32)).astype(BF16)
    return hi, mid, lo


def _hgrn_scan_body(q_ref, zf_ref, i_ref, g_ref, lb_ref, gn_ref, s0_ref, o_ref, st_ref,
                    *, chunk, n_valid, dk):
    c = pl.program_id(1)

    @pl.when(c == 0)
    def _():
        st_ref[...] = s0_ref[...]

    row = lax.broadcasted_iota(jnp.int32, (chunk, chunk), 0)
    col = lax.broadcasted_iota(jnp.int32, (chunk, chunk), 1)
    tri = jnp.where(row >= col, 1.0, 0.0).astype(BF16)
    n_levels = chunk.bit_length() - 1
    differ = row ^ col
    on_diag = differ == 0
    at_level = [(lax.shift_right_logical(differ, lv) == 1) & (row > col) for lv in range(n_levels)]
    tok = lax.broadcasted_iota(jnp.int32, (chunk, dk), 0)
    in_right = [(lax.shift_right_logical(tok, lv) & 1) == 1 for lv in range(n_levels)]
    quad = tok & 3
    if n_valid < chunk:
        valid = tok < n_valid
    for h in range(HG_HEADS):
        hs = slice(h * dk, (h + 1) * dk)
        q = q_ref[:, hs].astype(F32)
        zf = zf_ref[:, hs]
        v = i_ref[:, hs]
        lb = lb_ref[:, hs]
        e_abs = jnp.exp(-jnp.abs(zf))
        big = 1.0 / (1.0 + e_abs)
        small = e_abs * big
        nonneg = zf >= 0.0
        log_f = jnp.log(jnp.maximum(lb, TINY) + (1.0 - lb) * jnp.where(nonneg, big, small))
        k = (1.0 - lb) * jnp.where(nonneg, small, big)
        if n_valid < chunk:
            log_f = jnp.where(valid, log_f, 0.0)
            k = jnp.where(valid, k, 0.0)
        g_hi, g_mid, g_lo = _split3(log_f)
        b = (jnp.dot(tri, g_hi, preferred_element_type=F32)
             + jnp.dot(tri, g_mid, preferred_element_type=F32)
             + jnp.dot(tri, g_lo, preferred_element_type=F32))
        b_end = b[chunk - 1:chunk, :]
        a = jnp.where(on_diag, _nt_dot(q.astype(BF16), k.astype(BF16)), 0.0)
        for lv in range(n_levels):
            half = 1 << lv
            if lv == 0:
                dist = jnp.where(in_right[0], log_f, 0.0)
            elif lv == 1:
                edge = jnp.where(quad == 0, pltpu.roll(b, chunk - 1, 0),
                                 jnp.where(quad == 2, pltpu.roll(b, 1, 0),
                                           jnp.where(quad == 3, pltpu.roll(b, 2, 0), b)))
                dist = -jnp.abs(b - edge)
            else:
                nblk = chunk // (2 * half)
                edge = b.reshape(nblk, 2 * half, dk)[:, half - 1:half, :]
                edge = jnp.broadcast_to(edge, (nblk, 2 * half, dk)).reshape(chunk, dk)
                dist = -jnp.abs(b - edge)
            z = (jnp.where(in_right[lv], q, k) * jnp.exp(dist)).astype(BF16)
            a = jnp.where(at_level[lv], _nt_dot(z, z), a)
        s = st_ref[0, h]
        o = (jnp.dot(a.astype(BF16), v, preferred_element_type=F32)
             + jnp.dot((q * jnp.exp(b)).astype(BF16), s.astype(BF16), preferred_element_type=F32))
        k_d = (k * jnp.exp(b_end - b)).T.astype(BF16)
        decay = jnp.broadcast_to(jnp.exp(b_end), (dk, dk)).T
        st_ref[0, h] = decay * s + jnp.dot(k_d, v, preferred_element_type=F32)
        ms = jnp.mean(o * o, axis=-1, keepdims=True)
        gate = g_ref[:, hs].astype(F32)
        o = o * lax.rsqrt(ms + EPS) * gn_ref[...] * (gate * _sigmoid(gate))
        o_ref[:, hs] = o.astype(o_ref.dtype)


def _hgrn_scan(q, zf, i, g, lb, gnorm, s0, nseq, n_valid):
    tt, d = q.shape
    tp = tt // nseq
    nc = tp // SCAN_CHUNK
    dk = d // HG_HEADS
    row = pl.BlockSpec((SCAN_CHUNK, d), lambda b, c: (b * nc + c, 0))
    st = pl.BlockSpec((1, HG_HEADS, dk, dk), lambda b, c: (b, 0, 0, 0))
    body = functools.partial(_hgrn_scan_body, chunk=SCAN_CHUNK, n_valid=min(n_valid, SCAN_CHUNK), dk=dk)
    return pl.pallas_call(
        body,
        grid=(nseq, nc),
        in_specs=[row, row, row, row,
                  pl.BlockSpec((1, d), lambda b, c: (0, 0)),
                  pl.BlockSpec((1, dk), lambda b, c: (0, 0)),
                  st],
        out_specs=[row, st],
        out_shape=[jax.ShapeDtypeStruct((tt, d), BF16), jax.ShapeDtypeStruct(s0.shape, F32)],
        compiler_params=_params(("arbitrary", "arbitrary")),
        name="hgrn_scan",
    )(q, zf, i, g, lb.reshape(1, d), gnorm.reshape(1, dk), s0)


def _route_rows(lt):
    rows = [lt[e:e + 1, :] for e in range(N_EXPERTS)]
    m = functools.reduce(jnp.maximum, rows)
    ex = [jnp.exp(r - m) for r in rows]
    inv = 1.0 / functools.reduce(lambda x, y: x + y, ex)
    pr = [e * inv for e in ex]
    scores = []
    for gi in range(N_GROUPS):
        p4 = pr[gi * EXPERTS_PER_GROUP:(gi + 1) * EXPERTS_PER_GROUP]
        pairs = [p4[x] + p4[y] for x in range(EXPERTS_PER_GROUP) for y in range(x + 1, EXPERTS_PER_GROUP)]
        scores.append(functools.reduce(jnp.maximum, pairs))
    best = scores[0]
    sel = jnp.zeros(best.shape, jnp.int32)
    for gi in range(1, N_GROUPS):
        better = scores[gi] > best
        sel = jnp.where(better, gi, sel)
        best = jnp.where(better, scores[gi], best)
    cand = []
    for j in range(EXPERTS_PER_GROUP):
        cj = pr[(N_GROUPS - 1) * EXPERTS_PER_GROUP + j]
        for gi in range(N_GROUPS - 2, -1, -1):
            cj = jnp.where(sel == gi, pr[gi * EXPERTS_PER_GROUP + j], cj)
        cand.append(cj)

    def argmax4(vals):
        p, idx = vals[0], jnp.zeros(vals[0].shape, jnp.int32)
        for j in range(1, len(vals)):
            better = vals[j] > p
            idx = jnp.where(better, j, idx)
            p = jnp.where(better, vals[j], p)
        return p, idx

    p1, i1 = argmax4(cand)
    p2, i2 = argmax4([jnp.where(i1 == j, -1.0, cand[j]) for j in range(EXPERTS_PER_GROUP)])
    den = p1 + p2
    base = sel * EXPERTS_PER_GROUP
    return base + i1, base + i2, p1 / den, p2 / den


def _mix_router_body(a_ref, w_ref, h_ref, g1_ref, gain_ref, sh_ref, sc_ref, wrh_ref, wrl_ref, br_ref, upper_ref,
                     h_out, hn_out, gate_out, route_out, count_out):
    @pl.when(pl.program_id(0) == 0)
    def _():
        count_out[...] = jnp.zeros_like(count_out)

    mix = jnp.dot(a_ref[...].astype(BF16), w_ref[...], preferred_element_type=F32)
    h = h_ref[...] + g1_ref[...] * mix
    h_out[...] = h
    hn = _norm_mod(h, gain_ref[...], sh_ref[...], sc_ref[...])
    _store_row_tiles(hn_out, hn)
    x_hi = hn.astype(BF16)
    x_lo = (hn - x_hi.astype(F32)).astype(BF16)
    lt = (_nt_dot(wrh_ref[...], x_hi) + _nt_dot(wrl_ref[...], x_hi) + _nt_dot(wrh_ref[...], x_lo)
          + br_ref[...])
    e1, e2, g1, g2 = _route_rows(lt)
    tm = lt.shape[1]
    sub = lax.broadcasted_iota(jnp.int32, (LANES, tm), 0)
    gate_out[...] = (jnp.where(sub == 0, g1, 0.0) + jnp.where(sub == 1, g2, 0.0)).T
    eid = lax.broadcasted_iota(jnp.int32, (N_EXPERTS, tm), 0)
    pick1 = eid == e1
    pick2 = eid == e2
    hits = jnp.where(pick1, 1.0, 0.0) + jnp.where(pick2, 1.0, 0.0)
    before = jnp.dot(hits.astype(BF16), upper_ref[...], preferred_element_type=F32) + count_out[...]
    rank1 = jnp.sum(jnp.where(pick1, before, 0.0), axis=0, keepdims=True)
    rank2 = jnp.sum(jnp.where(pick2, before, 0.0), axis=0, keepdims=True)
    count_out[...] += jnp.sum(hits, axis=1, keepdims=True)
    field = lax.broadcasted_iota(jnp.int32, route_out.shape[1:], 0)
    route_out[0] = jnp.where(field == 0, e1, jnp.where(field == 1, e2, jnp.where(
        field == 2, rank1.astype(jnp.int32), rank2.astype(jnp.int32))))


def _mix_router(a, w_out, h, gain, mod, wr_hi, wr_lo, b_router, tm):
    t, d = h.shape
    row = pl.BlockSpec((tm, d), lambda i: (i, 0))
    const = lambda shape: pl.BlockSpec(shape, lambda i: (0,) * len(shape))
    upper = (jnp.arange(tm)[:, None] < jnp.arange(tm)[None, :]).astype(BF16)
    return pl.pallas_call(
        _mix_router_body,
        grid=(t // tm,),
        in_specs=[row, const((d, d)), row, mod.spec(2), const((1, d)), mod.spec(3), mod.spec(4),
                  const((N_EXPERTS, d)), const((N_EXPERTS, d)), const((N_EXPERTS, 1)), const((tm, tm))],
        out_specs=[row, pl.BlockSpec((tm * SUBLANES, LANES), lambda i: (i, 0)),
                   pl.BlockSpec((tm, LANES), lambda i: (i, 0)),
                   pl.BlockSpec((1, 4, tm), lambda i: (i, 0, 0)), const((N_EXPERTS, 1))],
        out_shape=[jax.ShapeDtypeStruct((t, d), F32), jax.ShapeDtypeStruct((t * SUBLANES, LANES), F32),
                   jax.ShapeDtypeStruct((t, LANES), F32), jax.ShapeDtypeStruct((t // tm, 4, tm), jnp.int32),
                   jax.ShapeDtypeStruct((N_EXPERTS, 1), F32)],
        compiler_params=_params(("arbitrary",)),
        name="mix_router",
    )(a, w_out, h, mod.arr, gain.reshape(1, d), mod.arr, mod.arr, wr_hi, wr_lo,
      b_router.reshape(N_EXPERTS, 1).astype(F32), upper)


def _store_row_tiles(ref, x):
    n, d = x.shape
    assert d == SUBLANES * LANES
    for c in range(SUBLANES):
        ref[pl.ds(c, n, stride=SUBLANES), :] = x[:, c * LANES:(c + 1) * LANES]


def _load_row_tiles(ref):
    n = ref.shape[0] // SUBLANES
    return jnp.concatenate([ref[pl.ds(c, n, stride=SUBLANES), :] for c in range(SUBLANES)], axis=1)


def _tile_rows(first_row, n_rows):
    return pl.ds(pl.multiple_of(first_row * SUBLANES, SUBLANES), n_rows * SUBLANES)


def _row_copy(src, src_row, dst, dst_row, sem):
    return pltpu.make_async_copy(src.at[_tile_rows(src_row, 1)], dst.at[_tile_rows(dst_row, 1)], sem)


def _dispatch_body(seg_end_ref, pos_ref, x_ref, xs_out, zero_ref, row_sem, *, tm):
    etile = zero_ref.shape[0] // SUBLANES

    @pl.when(pl.program_id(0) == 0)
    def _():
        zero_ref[...] = jnp.zeros_like(zero_ref)

        def last_tile(e):
            return pltpu.make_async_copy(zero_ref, xs_out.at[_tile_rows(seg_end_ref[e] - etile, etile)], row_sem)

        def nonempty(e):
            return seg_end_ref[e] > (seg_end_ref[e - 1] if e else 0)

        for e in range(N_EXPERTS):
            pl.when(nonempty(e))(lambda e=e: last_tile(e).start())
        for e in range(N_EXPERTS):
            pl.when(nonempty(e))(lambda e=e: last_tile(e).wait())

        def unused_tile(i, carry):
            cp = pltpu.make_async_copy(zero_ref, xs_out.at[_tile_rows(i * etile, etile)], row_sem)
            cp.start()
            cp.wait()
            return carry

        n_tiles = xs_out.shape[0] // (etile * SUBLANES)
        lax.fori_loop(seg_end_ref[N_EXPERTS - 1] // etile, n_tiles, unused_tile, 0)

    base = pl.program_id(0) * (2 * tm)

    def start(t, carry):
        _row_copy(x_ref, t, xs_out, pos_ref[base + t], row_sem).start(priority=0)
        _row_copy(x_ref, t, xs_out, pos_ref[base + tm + t], row_sem).start(priority=1)
        return carry

    lax.fori_loop(0, tm, start, 0, unroll=ROW_COPY_UNROLL)
    for _ in range(2):
        pltpu.make_async_copy(x_ref, xs_out.at[_tile_rows(0, tm)], row_sem).wait()


def _dispatch(hn, pos_tiles, seg_end, n_rows, tm, etile):
    t = hn.shape[0] // SUBLANES
    grid_spec = pltpu.PrefetchScalarGridSpec(
        num_scalar_prefetch=2,
        grid=(t // tm,),
        in_specs=[pl.BlockSpec((tm * SUBLANES, LANES), lambda i, se, pos: (i, 0))],
        out_specs=pl.BlockSpec(memory_space=pl.ANY),
        scratch_shapes=[pltpu.VMEM((etile * SUBLANES, LANES), F32), pltpu.SemaphoreType.DMA])
    return pl.pallas_call(
        functools.partial(_dispatch_body, tm=tm),
        grid_spec=grid_spec,
        out_shape=jax.ShapeDtypeStruct((n_rows * SUBLANES, LANES), F32),
        compiler_params=_params(("arbitrary",)),
        name="moe_dispatch",
    )(seg_end, pos_tiles, hn)


def _expert_body(te_ref, nu_ref, x_ref, win_ref, wout_ref, y_ref):
    @pl.when(pl.program_id(0) < nu_ref[0])
    def _():
        hid = jnp.dot(_load_row_tiles(x_ref).astype(BF16), win_ref[0].astype(BF16), preferred_element_type=F32)
        de = hid.shape[1] // 2
        a, u = hid[:, :de], hid[:, de:]
        act = (a * _sigmoid(a) * u).astype(BF16)
        _store_row_tiles(y_ref, jnp.dot(act, wout_ref[0].astype(BF16), preferred_element_type=F32))

    @pl.when(pl.program_id(0) >= nu_ref[0])
    def _():
        y_ref[...] = jnp.zeros_like(y_ref)


def _experts(xs, tile_expert, n_used, w_in, w_out, layer, etile):
    _, n_e, d, d2 = w_in.shape
    n_tiles = xs.shape[0] // (etile * SUBLANES)
    tile = lambda i, te, nu: jnp.minimum(i, nu[0] - 1)
    rows = (etile * SUBLANES, LANES)
    grid_spec = pltpu.PrefetchScalarGridSpec(
        num_scalar_prefetch=2,
        grid=(n_tiles,),
        in_specs=[pl.BlockSpec(rows, lambda i, te, nu: (tile(i, te, nu), 0)),
                  pl.BlockSpec((None, 1, d, d2), lambda i, te, nu: (layer, te[tile(i, te, nu)], 0, 0)),
                  pl.BlockSpec((None, 1, d2 // 2, d), lambda i, te, nu: (layer, te[tile(i, te, nu)], 0, 0))],
        out_specs=pl.BlockSpec(rows, lambda i, te, nu: (i, 0)))
    return pl.pallas_call(
        _expert_body,
        grid_spec=grid_spec,
        out_shape=jax.ShapeDtypeStruct(xs.shape, F32),
        compiler_params=_params(("arbitrary",)),
        name="moe_experts",
    )(tile_expert, n_used, xs, w_in, w_out)


def _combine_body(pos_ref, ys_hbm, gate_ref, h_ref, g2_ref, o_ref, y1_ref, y2_ref, row_sem, *, tm):
    base = pl.program_id(0) * (2 * tm)

    def start(t, carry):
        _row_copy(ys_hbm, pos_ref[base + t], y1_ref, t, row_sem).start(priority=0)
        _row_copy(ys_hbm, pos_ref[base + tm + t], y2_ref, t, row_sem).start(priority=1)
        return carry

    lax.fori_loop(0, tm, start, 0, unroll=ROW_COPY_UNROLL)
    for y_ref in (y1_ref, y2_ref):
        pltpu.make_async_copy(ys_hbm.at[_tile_rows(0, tm)], y_ref, row_sem).wait()
    gates = gate_ref[...]
    moe = gates[:, 0:1] * _load_row_tiles(y1_ref) + gates[:, 1:2] * _load_row_tiles(y2_ref)
    o_ref[...] = h_ref[...] + g2_ref[...] * moe


def _combine(ys, pos_tiles, gates, h, mod, tm):
    t, d = h.shape
    row = pl.BlockSpec((tm, d), lambda i, pos: (i, 0))
    grid_spec = pltpu.PrefetchScalarGridSpec(
        num_scalar_prefetch=1,
        grid=(t // tm,),
        in_specs=[pl.BlockSpec(memory_space=pl.ANY), pl.BlockSpec((tm, LANES), lambda i, pos: (i, 0)),
                  row, mod.spec(5)],
        out_specs=row,
        scratch_shapes=[pltpu.VMEM((tm * SUBLANES, LANES), F32), pltpu.VMEM((tm * SUBLANES, LANES), F32),
                        pltpu.SemaphoreType.DMA])
    return pl.pallas_call(
        functools.partial(_combine_body, tm=tm),
        grid_spec=grid_spec,
        out_shape=jax.ShapeDtypeStruct((t, d), F32),
        compiler_params=_params(("arbitrary",)),
        name="moe_combine",
    )(pos_tiles, ys, gates, h, mod.arr)


def _moe(hn, gates, route, counts, h, mod, w_in, w_out, layer, tm):
    t, d = h.shape
    etile = EXPERT_TILE if 2 * t >= 4 * N_EXPERTS * EXPERT_TILE else EXPERT_TILE_SMALL
    n_tiles = 2 * t // etile + N_EXPERTS
    n_rows = n_tiles * etile
    cnt = counts.reshape(N_EXPERTS).astype(jnp.int32)
    padded = (cnt + (etile - 1)) // etile * etile
    experts = jnp.arange(N_EXPERTS, dtype=jnp.int32)
    seg_end = jnp.sum(jnp.where(experts[None, :] <= experts[:, None], padded[None, :], 0), axis=1)
    seg_start = seg_end - padded
    first_row = jnp.sum(jnp.where(route[:, :2, :, None] == experts, seg_start, 0), axis=-1)
    pos_tiles = (first_row + route[:, 2:]).reshape(-1)
    tile_row = jnp.arange(n_tiles, dtype=jnp.int32) * etile
    tile_expert = jnp.minimum(jnp.sum((seg_end[None, :] <= tile_row[:, None]).astype(jnp.int32), axis=1),
                              N_EXPERTS - 1)
    n_used = seg_end[-1:] // etile
    xs = _dispatch(hn, pos_tiles, seg_end, n_rows, tm, etile)
    ys = _experts(xs, tile_expert, n_used, w_in, w_out, layer, etile)
    return _combine(ys, pos_tiles, gates, h, mod, tm)


def _head_norm(y, gsum_ref, gain_ref, head_dim):
    ssum = jnp.dot((y * y).astype(BF16), gsum_ref[...], preferred_element_type=F32)
    return y * lax.rsqrt(ssum * (1.0 / head_dim) + EPS) * gain_ref[...]


def _head_sum_matrix(d, head_dim):
    r = jnp.arange(d) // head_dim
    return (r[:, None] == r[None, :]).astype(BF16)


def _q_proj_body(x_ref, gain_ref, sh_ref, sc_ref, w_ref, gsum_ref, qn_ref, q_ref, *, head_dim, q_scale):
    xn = _norm_mod(x_ref[...], gain_ref[...], sh_ref[...], sc_ref[...]).astype(BF16)
    y = jnp.dot(xn, w_ref[...], preferred_element_type=F32)
    q = _head_norm(y, gsum_ref, qn_ref, head_dim) * q_scale
    q_ref[...] = q.astype(q_ref.dtype)


def _q_proj(h, gain, mod, w, gsum, q_norm, tm, out_dtype, q_scale):
    t, d = h.shape
    head_dim = d // FOX_HEADS
    row = pl.BlockSpec((tm, d), lambda i: (i, 0))
    const = lambda shape: pl.BlockSpec(shape, lambda i: (0,) * len(shape))
    return pl.pallas_call(
        functools.partial(_q_proj_body, head_dim=head_dim, q_scale=q_scale),
        grid=(t // tm,),
        in_specs=[row, const((1, d)), mod.spec(0), mod.spec(1), const((d, d)), const((d, d)), const((1, d))],
        out_specs=row,
        out_shape=jax.ShapeDtypeStruct((t, d), out_dtype),
        compiler_params=_params(("arbitrary",)),
        name="q_proj",
    )(h, gain.reshape(1, d), mod.arr, mod.arr, w, gsum, jnp.tile(q_norm, FOX_HEADS).reshape(1, d))


def _kv_proj_body(x_ref, gain_ref, sh_ref, sc_ref, wk_ref, wv_ref, wf_ref, gsum_ref, kn_ref, bf_ref,
                  k_ref, v_ref, kb_ref, vb_ref, lf_ref, *, head_dim):
    xn = _norm_mod(x_ref[...], gain_ref[...], sh_ref[...], sc_ref[...]).astype(BF16)
    k = _head_norm(jnp.dot(xn, wk_ref[...], preferred_element_type=F32), gsum_ref, kn_ref, head_dim)
    k_ref[...] = k
    kb_ref[...] = k.astype(BF16)
    v = jnp.dot(xn, wv_ref[...], preferred_element_type=F32)
    v_ref[...] = v
    vb_ref[...] = v.astype(BF16)
    zf = jnp.dot(xn, wf_ref[...], preferred_element_type=F32) + bf_ref[...]
    lf_ref[...] = _log_sigmoid(zf)[:, :lf_ref.shape[1]]


def _kv_proj(h, gain, mod, wk, wv, wf, gsum, k_norm, b_fgate, tm):
    t, d = h.shape
    head_dim = d // FOX_HEADS
    row = pl.BlockSpec((tm, d), lambda i: (i, 0))
    const = lambda shape: pl.BlockSpec(shape, lambda i: (0,) * len(shape))
    bf = jnp.pad(b_fgate.astype(F32), (0, LANES - FOX_HEADS)).reshape(1, LANES)
    return pl.pallas_call(
        functools.partial(_kv_proj_body, head_dim=head_dim),
        grid=(t // tm,),
        in_specs=[row, const((1, d)), mod.spec(0), mod.spec(1), const((d, d)), const((d, d)),
                  const((d, LANES)), const((d, d)), const((1, d)), const((1, LANES))],
        out_specs=[row, row, row, row, pl.BlockSpec((tm, FOX_HEADS), lambda i: (i, 0))],
        out_shape=[jax.ShapeDtypeStruct((t, d), F32), jax.ShapeDtypeStruct((t, d), F32),
                   jax.ShapeDtypeStruct((t, d), BF16), jax.ShapeDtypeStruct((t, d), BF16),
                   jax.ShapeDtypeStruct((t, FOX_HEADS), F32)],
        compiler_params=_params(("arbitrary",)),
        name="kv_proj",
    )(h, gain.reshape(1, d), mod.arr, mod.arr, wk, wv, wf, gsum,
      jnp.tile(k_norm, FOX_HEADS).reshape(1, d), bf)


def _cumsum_lanes_body(x_ref, o_ref, carry_ref, *, carry_rows):
    j = pl.program_id(1)

    @pl.when(j == 0)
    def _():
        carry_ref[...] = jnp.zeros_like(carry_ref)

    w = x_ref.shape[-1]
    r = lax.broadcasted_iota(jnp.int32, (w, w), 0)
    c = lax.broadcasted_iota(jnp.int32, (w, w), 1)
    upper = jnp.where(r <= c, 1.0, 0.0).astype(F32)
    y = jnp.dot(x_ref[0], upper, precision=HIGHEST, preferred_element_type=F32)
    if carry_rows:
        y = y + carry_ref[...]
        carry_ref[...] = y[:, w - 1:w]
    o_ref[0] = y


def _cumsum_lanes(x, width, carry):
    b, r, t = x.shape
    spec = pl.BlockSpec((1, r, width), lambda i, j: (i, 0, j))
    return pl.pallas_call(
        functools.partial(_cumsum_lanes_body, carry_rows=carry),
        grid=(b, t // width),
        in_specs=[spec],
        out_specs=spec,
        out_shape=jax.ShapeDtypeStruct(x.shape, F32),
        scratch_shapes=[pltpu.VMEM((r, 1), F32)],
        compiler_params=_params(("arbitrary", "arbitrary")),
        name="cumsum_lanes",
    )(x)


def _attn_prompt_body(qt_ref, kt_ref, q_ref, k_ref, v_ref, f_ref, o_ref, m_ref, acc_ref,
                      *, blk, rows, diag_rows, head_dim):
    p = pl.program_id(2)
    qi = qt_ref[p]
    ki = kt_ref[p]

    @pl.when(ki == 0)
    def _():
        m_ref[...] = jnp.full(m_ref.shape, NEG_BIG, F32)
        acc_ref[...] = jnp.zeros_like(acc_ref)

    def process(masked):
        k = k_ref[...]
        v = v_ref[...]
        f = f_ref[...]
        first_v = lax.broadcasted_iota(jnp.int32, v.shape, 1) < head_dim
        ones = jnp.ones_like(v)
        v_heads = (jnp.where(first_v, v, ones), jnp.where(first_v, ones, v))
        rb = min(rows, diag_rows) if masked else rows
        m_olds = {(hh, r0): m_ref[hh, r0:r0 + rb, :] for hh in range(2) for r0 in range(0, blk, rb)}
        acc_olds = {(hh, r0): acc_ref[hh, r0:r0 + rb, :] for hh in range(2) for r0 in range(0, blk, rb)}
        m_news, acc_news = {}, {}
        for r0 in range(0, blk, rb):
            n_keys = r0 + rb if masked else blk
            q = q_ref[r0:r0 + rb, :]
            first_q = lax.broadcasted_iota(jnp.int32, q.shape, 1) < head_dim
            zero = jnp.zeros_like(q)
            for hh, qh in enumerate((jnp.where(first_q, q, zero), jnp.where(first_q, zero, q))):
                s = _nt_dot(qh, k[:n_keys]) - f[hh:hh + 1, :n_keys]
                if masked:
                    row = lax.broadcasted_iota(jnp.int32, s.shape, 0) + r0
                    col = lax.broadcasted_iota(jnp.int32, s.shape, 1)
                    s = jnp.where(col <= row, s, NEG_BIG)
                m_old = m_olds[hh, r0]
                m_new = jnp.maximum(m_old, jnp.max(s, axis=1, keepdims=True))
                alpha = jnp.exp2(m_old - m_new)
                pr = jnp.concatenate([jnp.exp2(s[:, c * LANES:(c + 1) * LANES] - m_new)
                                      for c in range(n_keys // LANES)], axis=1).astype(BF16)
                acc_news[hh, r0] = alpha * acc_olds[hh, r0] + jnp.dot(pr, v_heads[hh][:n_keys],
                                                                      preferred_element_type=F32)
                m_news[hh, r0] = m_new
        for (hh, r0), m_new in m_news.items():
            m_ref[hh, r0:r0 + rb, :] = m_new
            acc_ref[hh, r0:r0 + rb, :] = acc_news[hh, r0]

    @pl.when(ki < qi)
    def _():
        process(False)

    @pl.when(ki == qi)
    def _():
        process(True)
        a0 = acc_ref[0]
        a1 = acc_ref[1]
        first = lax.broadcasted_iota(jnp.int32, a0.shape, 1) < head_dim
        inv0 = 1.0 / a0[:, head_dim:head_dim + 1]
        inv1 = 1.0 / a1[:, 0:1]
        o_ref[...] = jnp.where(first, a0 * inv0, a1 * inv1).astype(o_ref.dtype)


def _attn_prompt(q, k, v, f_pairs, nseq):
    tt, d = q.shape
    t = tt // nseq
    blk = min(ATTN_BLOCK, t)
    rows = min(ATTN_ROWS, blk)
    nb = t // blk
    head_dim = d // FOX_HEADS
    pairs = [(qi, ki) for qi in range(nb) for ki in range(qi + 1)]
    q_tab = jnp.asarray([p[0] for p in pairs], jnp.int32)
    k_tab = jnp.asarray([p[1] for p in pairs], jnp.int32)
    grid_spec = pltpu.PrefetchScalarGridSpec(
        num_scalar_prefetch=2,
        grid=(nseq, d // LANES, len(pairs)),
        in_specs=[pl.BlockSpec((blk, LANES), lambda b, hp, p, qt, kt: (b * nb + qt[p], hp)),
                  pl.BlockSpec((blk, LANES), lambda b, hp, p, qt, kt: (b * nb + kt[p], hp)),
                  pl.BlockSpec((blk, LANES), lambda b, hp, p, qt, kt: (b * nb + kt[p], hp)),
                  pl.BlockSpec((None, None, 2, blk), lambda b, hp, p, qt, kt: (b, hp, 0, kt[p]))],
        out_specs=pl.BlockSpec((blk, LANES), lambda b, hp, p, qt, kt: (b * nb + qt[p], hp)),
        scratch_shapes=[pltpu.VMEM((2, blk, LANES), F32), pltpu.VMEM((2, blk, LANES), F32)])
    return pl.pallas_call(
        functools.partial(_attn_prompt_body, blk=blk, rows=rows, diag_rows=min(ATTN_DIAG_ROWS, blk),
                          head_dim=head_dim),
        grid_spec=grid_spec,
        out_shape=jax.ShapeDtypeStruct((tt, d), BF16),
        compiler_params=_params(("arbitrary", "arbitrary", "arbitrary")),
        name="attn_prompt",
    )(q_tab, k_tab, q, k, v, f_pairs)


def _gather_cum_body(pt_ref, cp_ref, new_ref, o_ref, *, n_pages, page):
    b = pl.program_id(0)

    def step(j, carry):
        blk = cp_ref[pt_ref[b * n_pages + j]]
        o_ref[0, :, pl.ds(pl.multiple_of(j * page, page), page)] = blk + carry
        return carry + blk[:, page - 1:page]

    carry = lax.fori_loop(0, n_pages, step, jnp.zeros((cp_ref.shape[1], 1), F32), unroll=8)
    o_ref[0, :, n_pages * page:] = new_ref[0] + carry


def _gather_cum(page_table, cum_pages, cum_new):
    nseq, n_pages = page_table.shape
    n_pool, hh, page = cum_pages.shape
    grid_spec = pltpu.PrefetchScalarGridSpec(
        num_scalar_prefetch=1,
        grid=(nseq,),
        in_specs=[pl.BlockSpec((n_pool, hh, page), lambda b, pt: (0, 0, 0)),
                  pl.BlockSpec((1, hh, page), lambda b, pt: (b, 0, 0))],
        out_specs=pl.BlockSpec((1, hh, (n_pages + 1) * page), lambda b, pt: (b, 0, 0)))
    return pl.pallas_call(
        functools.partial(_gather_cum_body, n_pages=n_pages, page=page),
        grid_spec=grid_spec,
        out_shape=jax.ShapeDtypeStruct((nseq, hh, (n_pages + 1) * page), F32),
        compiler_params=_params(("arbitrary",)),
        name="gather_cum",
    )(page_table.reshape(-1), cum_pages, cum_new)


def _attn_paged_body(pt_ref, *refs, n_q, page, pps, n_steps, head_dim):
    q_ref = refs[0]
    k_refs = refs[1:1 + pps]
    v_refs = refs[1 + pps:1 + 2 * pps]
    kn_ref, vn_ref, f_ref, o_ref, qbd_ref, m_ref, l_ref, acc_ref = refs[1 + 2 * pps:]
    j = pl.program_id(1)
    d = q_ref.shape[1]
    rows = FOX_HEADS * n_q

    @pl.when(j == 0)
    def _():
        q = q_ref[...]
        lane_head = lax.broadcasted_iota(jnp.int32, q.shape, 1) // head_dim
        for h in range(FOX_HEADS):
            qbd_ref[h * n_q:(h + 1) * n_q, :] = jnp.where(lane_head == h, q, 0.0).astype(BF16)
        m_ref[...] = jnp.full(m_ref.shape, NEG_BIG, F32)
        l_ref[...] = jnp.zeros_like(l_ref)
        acc_ref[...] = jnp.zeros_like(acc_ref)

    def attend(k, v, f, mask):
        bias = jnp.concatenate([jnp.broadcast_to(f[h:h + 1, :], (n_q, page)) for h in range(FOX_HEADS)], axis=0)
        s = _nt_dot(qbd_ref[...], k) - bias
        if mask is not None:
            s = jnp.where(mask, s, NEG_BIG)
        m_old = m_ref[...]
        m_new = jnp.maximum(m_old, jnp.max(s, axis=1, keepdims=True))
        alpha = jnp.exp(m_old - m_new)
        pr = jnp.exp(s - m_new)
        l_ref[...] = alpha * l_ref[...] + jnp.sum(pr, axis=1, keepdims=True)
        m_ref[...] = m_new
        acc_ref[...] = alpha * acc_ref[...] + jnp.dot(pr.astype(BF16), v, preferred_element_type=F32)

    @pl.when(j < n_steps - 1)
    def _():
        for u in range(pps):
            off = pl.multiple_of((j * pps + u) * page, page)
            attend(k_refs[u][...].astype(BF16), v_refs[u][...].astype(BF16), f_ref[0, :, pl.ds(off, page)], None)

    @pl.when(j == n_steps - 1)
    def _():
        pad = jnp.zeros((page - n_q, d), F32)
        k = jnp.concatenate([kn_ref[...], pad], axis=0).astype(BF16)
        v = jnp.concatenate([vn_ref[...], pad], axis=0).astype(BF16)
        t_of_row = lax.broadcasted_iota(jnp.int32, (rows, page), 0) % n_q
        key = lax.broadcasted_iota(jnp.int32, (rows, page), 1)
        attend(k, v, f_ref[0, :, (n_steps - 1) * pps * page:], key <= t_of_row)
        acc = acc_ref[...] * (1.0 / l_ref[...])
        lane_head = lax.broadcasted_iota(jnp.int32, (n_q, d), 1) // head_dim
        out = jnp.zeros((n_q, d), F32)
        for h in range(FOX_HEADS):
            out = out + jnp.where(lane_head == h, acc[h * n_q:(h + 1) * n_q, :], 0.0)
        o_ref[...] = out


def _attn_paged(q, cache_k, cache_v, k_new, v_new, f_all, page_table):
    tt, d = q.shape
    nseq, n_pages = page_table.shape
    n_q = tt // nseq
    page = cache_k.shape[1]
    pps = PAGES_PER_STEP
    n_steps = n_pages // pps + 1
    head_dim = d // FOX_HEADS
    rows = FOX_HEADS * n_q

    def page_spec(u):
        def index(b, j, pt):
            return (pt[b * n_pages + jnp.minimum(j, n_steps - 2) * pps + u], 0, 0)
        return pl.BlockSpec((None, page, d), index)

    tok = pl.BlockSpec((n_q, d), lambda b, j, pt: (b, 0))
    grid_spec = pltpu.PrefetchScalarGridSpec(
        num_scalar_prefetch=1,
        grid=(nseq, n_steps),
        in_specs=([tok] + [page_spec(u) for u in range(pps)] + [page_spec(u) for u in range(pps)]
                  + [tok, tok, pl.BlockSpec((1, FOX_HEADS, f_all.shape[2]), lambda b, j, pt: (b, 0, 0))]),
        out_specs=tok,
        scratch_shapes=[pltpu.VMEM((rows, d), BF16), pltpu.VMEM((rows, 1), F32),
                        pltpu.VMEM((rows, 1), F32), pltpu.VMEM((rows, d), F32)])
    body = functools.partial(_attn_paged_body, n_q=n_q, page=page, pps=pps, n_steps=n_steps,
                             head_dim=head_dim)
    return pl.pallas_call(
        body,
        grid_spec=grid_spec,
        out_shape=jax.ShapeDtypeStruct((tt, d), F32),
        compiler_params=_params(("arbitrary", "arbitrary")),
        name="attn_paged",
    )(page_table.reshape(-1), q, *([cache_k] * pps), *([cache_v] * pps), k_new, v_new, f_all)


def _lower_bounds(lb_logits):
    p = jax.nn.softmax(lb_logits.astype(F32), axis=0)
    return jnp.cumsum(p, axis=0) - p[0]


def _trunk(x, nseq, mods, mod_kv, s0, past, w):
    tt, d = x.shape
    t = tt // nseq
    tm = min(512, tt)
    tiles_per_seq = max(t // tm, 1)
    n_a = w["w_in_a"].shape[0]
    depth = w["w_ada"].shape[0]
    head_dim = d // FOX_HEADS
    lbs = _lower_bounds(w["lb_logits"])
    mk = lambda arr: _Mod(arr, d, tm, tiles_per_seq)

    h = x
    states = []
    for l in range(depth):
        mod = mk(mods[l])
        if l < n_a:
            q, zf, i, g = _hgrn_in_proj(h, w["norm_mix"][l], mod, w["w_in_a"], l, tm)
            tp = -(-t // SCAN_CHUNK) * SCAN_CHUNK
            if tp != t:
                padr = lambda a: jnp.pad(a.reshape(nseq, t, d), ((0, 0), (0, tp - t), (0, 0))).reshape(nseq * tp, d)
                q, zf, i, g = padr(q), padr(zf), padr(i), padr(g)
            o, s_t = _hgrn_scan(q, zf, i, g, lbs[l], w["gnorm_a"][l], s0[l], nseq, t)
            if tp != t:
                o = o.reshape(nseq, tp, d)[:, :t].reshape(tt, d)
            states.append(s_t)
            mix_in, w_out = o, w["w_out_a"][l]
        else:
            j = l - n_a
            if past is None:
                q = _q_proj(h, w["norm_mix"][l], mod, w["w_q_b"][j], w["gsum"], w["q_norm"][j], tm, BF16,
                            head_dim ** -0.5 * LOG2E)
                mix_in = _attn_prompt(q, k_bf, v_bf, f_keys, nseq)
            else:
                q = _q_proj(h, w["norm_mix"][l], mod, w["w_q_b"][j], w["gsum"], w["q_norm"][j], tm, F32,
                            head_dim ** -0.5)
                mix_in = _attn_paged(q, past[0], past[1], k_new, v_new, f_keys, past[3])
            w_out = w["w_out_b"][j]
        h, hn, gates, route, counts = _mix_router(mix_in, w_out, h, w["norm_ffn"][l], mod, w["wr_hi"],
                                                  w["wr_lo"], w["b_router"], tm)
        h = _moe(hn, gates, route, counts, h, mod, w["w_exp_in"], w["w_exp_out"], l, tm)
        if l == n_a - 1:
            k_new, v_new, k_bf, v_bf, logf_new = _kv_proj(
                h, w["norm_kv"], mk(mod_kv), w["w_k"], w["w_v"], w["w_f"], w["gsum"], w["k_norm"],
                w["b_fgate"], tm)
            lf_t = logf_new.reshape(nseq, t, FOX_HEADS).transpose(0, 2, 1)
            if past is None:
                f_cum = _cumsum_lanes(lf_t, min(512, t), carry=True)
                f_keys = (f_cum * LOG2E).reshape(nseq, FOX_HEADS // 2, 2, t)
            else:
                cache_logf, page_table = past[2], past[3]
                page = cache_logf.shape[1]
                pool_t = cache_logf.astype(F32).transpose(0, 2, 1)
                n_pool = pool_t.shape[0]
                rows = 64 if n_pool % 64 == 0 else 1
                cum_pages = _cumsum_lanes(pool_t.reshape(n_pool // rows, rows * FOX_HEADS, page), page,
                                          carry=False).reshape(n_pool, FOX_HEADS, page)
                new_pad = jnp.pad(lf_t, ((0, 0), (0, 0), (0, page - t)))
                cum_new = _cumsum_lanes(new_pad, page, carry=False)
                f_keys = _gather_cum(page_table, cum_pages, cum_new)
    return h, jnp.stack(states), k_new, v_new, logf_new


def kernel(x_prompt, x_sample, cache_k, cache_v, cache_logf, state_hgrn, page_table, c_prompt, c_sample,
           w_ada, b_ada, norm_mix, norm_ffn, w_in_a, lb_logits, gnorm_a, w_out_a, norm_kv, w_ada_kv,
           b_ada_kv, w_kv, b_fgate, k_norm, w_q_b, q_norm, w_out_b, w_router, b_router, w_exp_in, w_exp_out):
    nb, seq, d = x_prompt.shape
    ns, dseq, _ = x_sample.shape
    n_a = w_in_a.shape[0]
    depth = w_ada.shape[0]
    hd = FOX_HEADS * (d // FOX_HEADS)
    dk = d // HG_HEADS

    nrow = nb + ns
    nrow_pad = -(-nrow // SUBLANES) * SUBLANES
    c_all = jnp.pad(jnp.concatenate([c_prompt, c_sample], axis=0).astype(F32), ((0, nrow_pad - nrow), (0, 0)))
    mod_all = _ada(c_all, w_ada, b_ada)
    mod_kv_all = _ada(c_all, w_ada_kv[None], b_ada_kv[None])[0]
    mods_p = [mod_all[l, :nb].reshape(nb, 1, 6 * d) for l in range(depth)]
    mods_s = [jnp.repeat(mod_all[l, nb:nrow], dseq, axis=0) for l in range(depth)]
    mod_kv_p = mod_kv_all[:nb].reshape(nb, 1, 2 * d)
    mod_kv_s = jnp.repeat(mod_kv_all[nb:nrow], dseq, axis=0)

    wr_t = w_router.astype(F32).T
    wr_hi = wr_t.astype(BF16)
    wr_lo = (wr_t - wr_hi.astype(F32)).astype(BF16)
    w = dict(
        w_ada=w_ada, lb_logits=lb_logits, norm_mix=norm_mix, norm_ffn=norm_ffn, gnorm_a=gnorm_a,
        w_in_a=w_in_a.astype(BF16), w_out_a=w_out_a.astype(BF16), norm_kv=norm_kv,
        w_k=w_kv[:, :hd].astype(BF16), w_v=w_kv[:, hd:2 * hd].astype(BF16),
        w_f=jnp.pad(w_kv[:, 2 * hd:], ((0, 0), (0, LANES - FOX_HEADS))).astype(BF16),
        b_fgate=b_fgate, k_norm=k_norm, w_q_b=w_q_b.astype(BF16), q_norm=q_norm,
        w_out_b=w_out_b.astype(BF16), wr_hi=wr_hi, wr_lo=wr_lo, b_router=b_router,
        w_exp_in=w_exp_in, w_exp_out=w_exp_out,
        gsum=_head_sum_matrix(d, d // FOX_HEADS),
    )

    s0_p = jnp.zeros((n_a, nb, HG_HEADS, dk, dk), F32)
    y_p, st_p, k_p, v_p, lf_p = _trunk(x_prompt.reshape(nb * seq, d), nb, mods_p, mod_kv_p, s0_p, None, w)

    n_pool, page = cache_k.shape[0], cache_k.shape[1]
    past = (cache_k.reshape(n_pool, page, hd), cache_v.reshape(n_pool, page, hd), cache_logf, page_table)
    y_s, st_s, k_s, v_s, lf_s = _trunk(x_sample.reshape(ns * dseq, d), ns, mods_s, mod_kv_s,
                                       state_hgrn.astype(F32), past, w)

    hs = (FOX_HEADS, d // FOX_HEADS)
    return (y_p.reshape(nb, seq, d), y_s.reshape(ns, dseq, d),
            st_p.astype(state_hgrn.dtype), st_s.astype(state_hgrn.dtype),
            k_p.reshape(nb, seq, *hs).astype(cache_k.dtype), v_p.reshape(nb, seq, *hs).astype(cache_v.dtype),
            lf_p.reshape(nb, seq, FOX_HEADS).astype(cache_logf.dtype),
            k_s.reshape(ns, dseq, *hs).astype(cache_k.dtype), v_s.reshape(ns, dseq, *hs).astype(cache_v.dtype),
            lf_s.reshape(ns, dseq, FOX_HEADS).astype(cache_logf.dtype))
```

```python
import functools
import math

import jax
import jax.numpy as jnp
from jax import lax
from jax.experimental import pallas as pl
from jax.experimental.pallas import tpu as pltpu

F32 = jnp.float32
BF16 = jnp.bfloat16
HIGHEST = lax.Precision.HIGHEST

HG_HEADS = 8
FOX_HEADS = 16
N_EXPERTS = 16
N_GROUPS = 4
EXPERTS_PER_GROUP = N_EXPERTS // N_GROUPS
EPS = 1e-6
NEG_BIG = -1e30
TINY = 1e-30

LANES = 128
SUBLANES = 8
VMEM_LIMIT_BYTES = 56 * 1024 * 1024

SCAN_CHUNK = 128
ATTN_BLOCK = 1024
ATTN_ROWS = 1024
ATTN_DIAG_ROWS = 512
LOG2E = math.log2(math.e)
PAGES_PER_STEP = 16
EXPERT_TILE = 512
EXPERT_TILE_SMALL = 256
ROW_COPY_UNROLL = 8


def _params(sem):
    return pltpu.CompilerParams(dimension_semantics=sem, vmem_limit_bytes=VMEM_LIMIT_BYTES)


def _nt_dot(a, b):
    return lax.dot_general(a, b, (((1,), (1,)), ((), ())), preferred_element_type=F32)


def _sigmoid(x):
    return 1.0 / (1.0 + jnp.exp(-x))


def _log_sigmoid(x):
    return jnp.minimum(x, 0.0) - jnp.log1p(jnp.exp(-jnp.abs(x)))


def _norm_mod(x, gain, shift, scale):
    ms = jnp.mean(x * x, axis=-1, keepdims=True)
    y = x * lax.rsqrt(ms + EPS) * gain
    return y * (1.0 + scale) + shift


class _Mod:
    def __init__(self, arr, d, tm, tiles_per_seq):
        self.arr = arr
        self.d = d
        self.tm = tm
        self.tiles_per_seq = tiles_per_seq
        self.per_token = arr.ndim == 2

    def spec(self, chunk):
        d, tps = self.d, self.tiles_per_seq
        if self.per_token:
            return pl.BlockSpec((self.tm, d), lambda i, *_: (i, chunk))
        return pl.BlockSpec((None, 1, d), lambda i, *_: (i // tps, 0, chunk))


def _ada_body(c_ref, w_ref, b_ref, o_ref):
    c = c_ref[...]
    a = c * _sigmoid(c)
    o_ref[0] = jnp.dot(a, w_ref[0], precision=HIGHEST, preferred_element_type=F32) + b_ref[0]


def _ada(c, w, b):
    n_layers, d, n = w.shape
    r = c.shape[0]
    tn = 1536 if n % 1536 == 0 else 1024
    return pl.pallas_call(
        _ada_body,
        grid=(n_layers, n // tn),
        in_specs=[pl.BlockSpec((r, d), lambda l, j: (0, 0)),
                  pl.BlockSpec((1, d, tn), lambda l, j: (l, 0, j)),
                  pl.BlockSpec((1, 1, tn), lambda l, j: (l, 0, j))],
        out_specs=pl.BlockSpec((1, r, tn), lambda l, j: (l, 0, j)),
        out_shape=jax.ShapeDtypeStruct((n_layers, r, n), F32),
        compiler_params=_params(("arbitrary", "arbitrary")),
        name="ada",
    )(c, w, b.reshape(n_layers, 1, n))


def _hgrn_in_body(x_ref, gain_ref, sh_ref, sc_ref, wq_ref, wf_ref, wi_ref, wg_ref,
                  q_ref, zf_ref, i_ref, g_ref):
    xn = _norm_mod(x_ref[...], gain_ref[...], sh_ref[...], sc_ref[...]).astype(BF16)
    q_ref[...] = jnp.dot(xn, wq_ref[...], preferred_element_type=F32).astype(q_ref.dtype)
    zf_ref[...] = jnp.dot(xn, wf_ref[...], preferred_element_type=F32)
    i_ref[...] = jnp.dot(xn, wi_ref[...], preferred_element_type=F32).astype(i_ref.dtype)
    g_ref[...] = jnp.dot(xn, wg_ref[...], preferred_element_type=F32).astype(g_ref.dtype)


def _hgrn_in_proj(h, gain, mod, w_in, layer, tm):
    t, d = h.shape
    row = pl.BlockSpec((tm, d), lambda i: (i, 0))
    part = lambda s: pl.BlockSpec((None, d, d), lambda i: (layer, 0, s))
    return pl.pallas_call(
        _hgrn_in_body,
        grid=(t // tm,),
        in_specs=[row, pl.BlockSpec((1, d), lambda i: (0, 0)), mod.spec(0), mod.spec(1),
                  part(0), part(1), part(2), part(3)],
        out_specs=[row, row, row, row],
        out_shape=[jax.ShapeDtypeStruct((t, d), BF16), jax.ShapeDtypeStruct((t, d), F32),
                   jax.ShapeDtypeStruct((t, d), BF16), jax.ShapeDtypeStruct((t, d), BF16)],
        compiler_params=_params(("arbitrary",)),
        name="hgrn_in_proj",
    )(h, gain.reshape(1, d), mod.arr, mod.arr, w_in, w_in, w_in, w_in)


def _split3(x):
    hi = x.astype(BF16)
    r1 = x - hi.astype(F32)
    mid = r1.astype(BF16)
    lo = (r1 - mid.astype(F32)).astype(BF16)
    return hi, mid, lo


def _hgrn_scan_body(q_ref, zf_ref, i_ref, g_ref, lb_ref, gn_ref, s0_ref, o_ref, st_ref,
                    *, chunk, n_valid, dk):
    c = pl.program_id(1)

    @pl.when(c == 0)
    def _():
        st_ref[...] = s0_ref[...]

    row = lax.broadcasted_iota(jnp.int32, (chunk, chunk), 0)
    col = lax.broadcasted_iota(jnp.int32, (chunk, chunk), 1)
    tri = jnp.where(row >= col, 1.0, 0.0).astype(BF16)
    n_levels = chunk.bit_length() - 1
    differ = row ^ col
    on_diag = differ == 0
    at_level = [(lax.shift_right_logical(differ, lv) == 1) & (row > col) for lv in range(n_levels)]
    tok = lax.broadcasted_iota(jnp.int32, (chunk, dk), 0)
    in_right = [(lax.shift_right_logical(tok, lv) & 1) == 1 for lv in range(n_levels)]
    quad = tok & 3
    if n_valid < chunk:
        valid = tok < n_valid
    for h in range(HG_HEADS):
        hs = slice(h * dk, (h + 1) * dk)
        q = q_ref[:, hs].astype(F32)
        zf = zf_ref[:, hs]
        v = i_ref[:, hs]
        lb = lb_ref[:, hs]
        e_abs = jnp.exp(-jnp.abs(zf))
        big = 1.0 / (1.0 + e_abs)
        small = e_abs * big
        nonneg = zf >= 0.0
        log_f = jnp.log(jnp.maximum(lb, TINY) + (1.0 - lb) * jnp.where(nonneg, big, small))
        k = (1.0 - lb) * jnp.where(nonneg, small, big)
        if n_valid < chunk:
            log_f = jnp.where(valid, log_f, 0.0)
            k = jnp.where(valid, k, 0.0)
        g_hi, g_mid, g_lo = _split3(log_f)
        b = (jnp.dot(tri, g_hi, preferred_element_type=F32)
             + jnp.dot(tri, g_mid, preferred_element_type=F32)
             + jnp.dot(tri, g_lo, preferred_element_type=F32))
        b_end = b[chunk - 1:chunk, :]
        a = jnp.where(on_diag, _nt_dot(q.astype(BF16), k.astype(BF16)), 0.0)
        for lv in range(n_levels):
            half = 1 << lv
            if lv == 0:
                dist = jnp.where(in_right[0], log_f, 0.0)
            elif lv == 1:
                edge = jnp.where(quad == 0, pltpu.roll(b, chunk - 1, 0),
                                 jnp.where(quad == 2, pltpu.roll(b, 1, 0),
                                           jnp.where(quad == 3, pltpu.roll(b, 2, 0), b)))
                dist = -jnp.abs(b - edge)
            else:
                nblk = chunk // (2 * half)
                edge = b.reshape(nblk, 2 * half, dk)[:, half - 1:half, :]
                edge = jnp.broadcast_to(edge, (nblk, 2 * half, dk)).reshape(chunk, dk)
                dist = -jnp.abs(b - edge)
            z = (jnp.where(in_right[lv], q, k) * jnp.exp(dist)).astype(BF16)
            a = jnp.where(at_level[lv], _nt_dot(z, z), a)
        s = st_ref[0, h]
        o = (jnp.dot(a.astype(BF16), v, preferred_element_type=F32)
             + jnp.dot((q * jnp.exp(b)).astype(BF16), s.astype(BF16), preferred_element_type=F32))
        k_d = (k * jnp.exp(b_end - b)).T.astype(BF16)
        decay = jnp.broadcast_to(jnp.exp(b_end), (dk, dk)).T
        st_ref[0, h] = decay * s + jnp.dot(k_d, v, preferred_element_type=F32)
        ms = jnp.mean(o * o, axis=-1, keepdims=True)
        gate = g_ref[:, hs].astype(F32)
        o = o * lax.rsqrt(ms + EPS) * gn_ref[...] * (gate * _sigmoid(gate))
        o_ref[:, hs] = o.astype(o_ref.dtype)


def _hgrn_scan(q, zf, i, g, lb, gnorm, s0, nseq, n_valid):
    tt, d = q.shape
    tp = tt // nseq
    nc = tp // SCAN_CHUNK
    dk = d // HG_HEADS
    row = pl.BlockSpec((SCAN_CHUNK, d), lambda b, c: (b * nc + c, 0))
    st = pl.BlockSpec((1, HG_HEADS, dk, dk), lambda b, c: (b, 0, 0, 0))
    body = functools.partial(_hgrn_scan_body, chunk=SCAN_CHUNK, n_valid=min(n_valid, SCAN_CHUNK), dk=dk)
    return pl.pallas_call(
        body,
        grid=(nseq, nc),
        in_specs=[row, row, row, row,
                  pl.BlockSpec((1, d), lambda b, c: (0, 0)),
                  pl.BlockSpec((1, dk), lambda b, c: (0, 0)),
                  st],
        out_specs=[row, st],
        out_shape=[jax.ShapeDtypeStruct((tt, d), BF16), jax.ShapeDtypeStruct(s0.shape, F32)],
        compiler_params=_params(("arbitrary", "arbitrary")),
        name="hgrn_scan",
    )(q, zf, i, g, lb.reshape(1, d), gnorm.reshape(1, dk), s0)


def _route_rows(lt):
    rows = [lt[e:e + 1, :] for e in range(N_EXPERTS)]
    m = functools.reduce(jnp.maximum, rows)
    ex = [jnp.exp(r - m) for r in rows]
    inv = 1.0 / functools.reduce(lambda x, y: x + y, ex)
    pr = [e * inv for e in ex]
    scores = []
    for gi in range(N_GROUPS):
        p4 = pr[gi * EXPERTS_PER_GROUP:(gi + 1) * EXPERTS_PER_GROUP]
        pairs = [p4[x] + p4[y] for x in range(EXPERTS_PER_GROUP) for y in range(x + 1, EXPERTS_PER_GROUP)]
        scores.append(functools.reduce(jnp.maximum, pairs))
    best = scores[0]
    sel = jnp.zeros(best.shape, jnp.int32)
    for gi in range(1, N_GROUPS):
        better = scores[gi] > best
        sel = jnp.where(better, gi, sel)
        best = jnp.where(better, scores[gi], best)
    cand = []
    for j in range(EXPERTS_PER_GROUP):
        cj = pr[(N_GROUPS - 1) * EXPERTS_PER_GROUP + j]
        for gi in range(N_GROUPS - 2, -1, -1):
            cj = jnp.where(sel == gi, pr[gi * EXPERTS_PER_GROUP + j], cj)
        cand.append(cj)

    def argmax4(vals):
        p, idx = vals[0], jnp.zeros(vals[0].shape, jnp.int32)
        for j in range(1, len(vals)):
            better = vals[j] > p
            idx = jnp.where(better, j, idx)
            p = jnp.where(better, vals[j], p)
        return p, idx

    p1, i1 = argmax4(cand)
    p2, i2 = argmax4([jnp.where(i1 == j, -1.0, cand[j]) for j in range(EXPERTS_PER_GROUP)])
    den = p1 + p2
    base = sel * EXPERTS_PER_GROUP
    return base + i1, base + i2, p1 / den, p2 / den


def _mix_router_body(a_ref, w_ref, h_ref, g1_ref, gain_ref, sh_ref, sc_ref, wrh_ref, wrl_ref, br_ref, upper_ref,
                     h_out, hn_out, gate_out, route_out, count_out):
    @pl.when(pl.program_id(0) == 0)
    def _():
        count_out[...] = jnp.zeros_like(count_out)

    mix = jnp.dot(a_ref[...].astype(BF16), w_ref[...], preferred_element_type=F32)
    h = h_ref[...] + g1_ref[...] * mix
    h_out[...] = h
    hn = _norm_mod(h, gain_ref[...], sh_ref[...], sc_ref[...])
    _store_row_tiles(hn_out, hn)
    x_hi = hn.astype(BF16)
    x_lo = (hn - x_hi.astype(F32)).astype(BF16)
    lt = (_nt_dot(wrh_ref[...], x_hi) + _nt_dot(wrl_ref[...], x_hi) + _nt_dot(wrh_ref[...], x_lo)
          + br_ref[...])
    e1, e2, g1, g2 = _route_rows(lt)
    tm = lt.shape[1]
    sub = lax.broadcasted_iota(jnp.int32, (LANES, tm), 0)
    gate_out[...] = (jnp.where(sub == 0, g1, 0.0) + jnp.where(sub == 1, g2, 0.0)).T
    eid = lax.broadcasted_iota(jnp.int32, (N_EXPERTS, tm), 0)
    pick1 = eid == e1
    pick2 = eid == e2
    hits = jnp.where(pick1, 1.0, 0.0) + jnp.where(pick2, 1.0, 0.0)
    before = jnp.dot(hits.astype(BF16), upper_ref[...], preferred_element_type=F32) + count_out[...]
    rank1 = jnp.sum(jnp.where(pick1, before, 0.0), axis=0, keepdims=True)
    rank2 = jnp.sum(jnp.where(pick2, before, 0.0), axis=0, keepdims=True)
    count_out[...] += jnp.sum(hits, axis=1, keepdims=True)
    field = lax.broadcasted_iota(jnp.int32, route_out.shape[1:], 0)
    route_out[0] = jnp.where(field == 0, e1, jnp.where(field == 1, e2, jnp.where(
        field == 2, rank1.astype(jnp.int32), rank2.astype(jnp.int32))))


def _mix_router(a, w_out, h, gain, mod, wr_hi, wr_lo, b_router, tm):
    t, d = h.shape
    row = pl.BlockSpec((tm, d), lambda i: (i, 0))
    const = lambda shape: pl.BlockSpec(shape, lambda i: (0,) * len(shape))
    upper = (jnp.arange(tm)[:, None] < jnp.arange(tm)[None, :]).astype(BF16)
    return pl.pallas_call(
        _mix_router_body,
        grid=(t // tm,),
        in_specs=[row, const((d, d)), row, mod.spec(2), const((1, d)), mod.spec(3), mod.spec(4),
                  const((N_EXPERTS, d)), const((N_EXPERTS, d)), const((N_EXPERTS, 1)), const((tm, tm))],
        out_specs=[row, pl.BlockSpec((tm * SUBLANES, LANES), lambda i: (i, 0)),
                   pl.BlockSpec((tm, LANES), lambda i: (i, 0)),
                   pl.BlockSpec((1, 4, tm), lambda i: (i, 0, 0)), const((N_EXPERTS, 1))],
        out_shape=[jax.ShapeDtypeStruct((t, d), F32), jax.ShapeDtypeStruct((t * SUBLANES, LANES), F32),
                   jax.ShapeDtypeStruct((t, LANES), F32), jax.ShapeDtypeStruct((t // tm, 4, tm), jnp.int32),
                   jax.ShapeDtypeStruct((N_EXPERTS, 1), F32)],
        compiler_params=_params(("arbitrary",)),
        name="mix_router",
    )(a, w_out, h, mod.arr, gain.reshape(1, d), mod.arr, mod.arr, wr_hi, wr_lo,
      b_router.reshape(N_EXPERTS, 1).astype(F32), upper)


def _store_row_tiles(ref, x):
    n, d = x.shape
    assert d == SUBLANES * LANES
    for c in range(SUBLANES):
        ref[pl.ds(c, n, stride=SUBLANES), :] = x[:, c * LANES:(c + 1) * LANES]


def _load_row_tiles(ref):
    n = ref.shape[0] // SUBLANES
    return jnp.concatenate([ref[pl.ds(c, n, stride=SUBLANES), :] for c in range(SUBLANES)], axis=1)


def _tile_rows(first_row, n_rows):
    return pl.ds(pl.multiple_of(first_row * SUBLANES, SUBLANES), n_rows * SUBLANES)


def _row_copy(src, src_row, dst, dst_row, sem):
    return pltpu.make_async_copy(src.at[_tile_rows(src_row, 1)], dst.at[_tile_rows(dst_row, 1)], sem)


def _dispatch_body(seg_end_ref, pos_ref, x_ref, xs_out, zero_ref, row_sem, *, tm):
    etile = zero_ref.shape[0] // SUBLANES

    @pl.when(pl.program_id(0) == 0)
    def _():
        zero_ref[...] = jnp.zeros_like(zero_ref)

        def last_tile(e):
            return pltpu.make_async_copy(zero_ref, xs_out.at[_tile_rows(seg_end_ref[e] - etile, etile)], row_sem)

        def nonempty(e):
            return seg_end_ref[e] > (seg_end_ref[e - 1] if e else 0)

        for e in range(N_EXPERTS):
            pl.when(nonempty(e))(lambda e=e: last_tile(e).start())
        for e in range(N_EXPERTS):
            pl.when(nonempty(e))(lambda e=e: last_tile(e).wait())

        def unused_tile(i, carry):
            cp = pltpu.make_async_copy(zero_ref, xs_out.at[_tile_rows(i * etile, etile)], row_sem)
            cp.start()
            cp.wait()
            return carry

        n_tiles = xs_out.shape[0] // (etile * SUBLANES)
        lax.fori_loop(seg_end_ref[N_EXPERTS - 1] // etile, n_tiles, unused_tile, 0)

    base = pl.program_id(0) * (2 * tm)

    def start(t, carry):
        _row_copy(x_ref, t, xs_out, pos_ref[base + t], row_sem).start(priority=0)
        _row_copy(x_ref, t, xs_out, pos_ref[base + tm + t], row_sem).start(priority=1)
        return carry

    lax.fori_loop(0, tm, start, 0, unroll=ROW_COPY_UNROLL)
    for _ in range(2):
        pltpu.make_async_copy(x_ref, xs_out.at[_tile_rows(0, tm)], row_sem).wait()


def _dispatch(hn, pos_tiles, seg_end, n_rows, tm, etile):
    t = hn.shape[0] // SUBLANES
    grid_spec = pltpu.PrefetchScalarGridSpec(
        num_scalar_prefetch=2,
        grid=(t // tm,),
        in_specs=[pl.BlockSpec((tm * SUBLANES, LANES), lambda i, se, pos: (i, 0))],
        out_specs=pl.BlockSpec(memory_space=pl.ANY),
        scratch_shapes=[pltpu.VMEM((etile * SUBLANES, LANES), F32), pltpu.SemaphoreType.DMA])
    return pl.pallas_call(
        functools.partial(_dispatch_body, tm=tm),
        grid_spec=grid_spec,
        out_shape=jax.ShapeDtypeStruct((n_rows * SUBLANES, LANES), F32),
        compiler_params=_params(("arbitrary",)),
        name="moe_dispatch",
    )(seg_end, pos_tiles, hn)


def _expert_body(te_ref, nu_ref, x_ref, win_ref, wout_ref, y_ref):
    @pl.when(pl.program_id(0) < nu_ref[0])
    def _():
        hid = jnp.dot(_load_row_tiles(x_ref).astype(BF16), win_ref[0].astype(BF16), preferred_element_type=F32)
        de = hid.shape[1] // 2
        a, u = hid[:, :de], hid[:, de:]
        act = (a * _sigmoid(a) * u).astype(BF16)
        _store_row_tiles(y_ref, jnp.dot(act, wout_ref[0].astype(BF16), preferred_element_type=F32))

    @pl.when(pl.program_id(0) >= nu_ref[0])
    def _():
        y_ref[...] = jnp.zeros_like(y_ref)


def _experts(xs, tile_expert, n_used, w_in, w_out, layer, etile):
    _, n_e, d, d2 = w_in.shape
    n_tiles = xs.shape[0] // (etile * SUBLANES)
    tile = lambda i, te, nu: jnp.minimum(i, nu[0] - 1)
    rows = (etile * SUBLANES, LANES)
    grid_spec = pltpu.PrefetchScalarGridSpec(
        num_scalar_prefetch=2,
        grid=(n_tiles,),
        in_specs=[pl.BlockSpec(rows, lambda i, te, nu: (tile(i, te, nu), 0)),
                  pl.BlockSpec((None, 1, d, d2), lambda i, te, nu: (layer, te[tile(i, te, nu)], 0, 0)),
                  pl.BlockSpec((None, 1, d2 // 2, d), lambda i, te, nu: (layer, te[tile(i, te, nu)], 0, 0))],
        out_specs=pl.BlockSpec(rows, lambda i, te, nu: (i, 0)))
    return pl.pallas_call(
        _expert_body,
        grid_spec=grid_spec,
        out_shape=jax.ShapeDtypeStruct(xs.shape, F32),
        compiler_params=_params(("arbitrary",)),
        name="moe_experts",
    )(tile_expert, n_used, xs, w_in, w_out)


def _combine_body(pos_ref, ys_hbm, gate_ref, h_ref, g2_ref, o_ref, y1_ref, y2_ref, row_sem, *, tm):
    base = pl.program_id(0) * (2 * tm)

    def start(t, carry):
        _row_copy(ys_hbm, pos_ref[base + t], y1_ref, t, row_sem).start(priority=0)
        _row_copy(ys_hbm, pos_ref[base + tm + t], y2_ref, t, row_sem).start(priority=1)
        return carry

    lax.fori_loop(0, tm, start, 0, unroll=ROW_COPY_UNROLL)
    for y_ref in (y1_ref, y2_ref):
        pltpu.make_async_copy(ys_hbm.at[_tile_rows(0, tm)], y_ref, row_sem).wait()
    gates = gate_ref[...]
    moe = gates[:, 0:1] * _load_row_tiles(y1_ref) + gates[:, 1:2] * _load_row_tiles(y2_ref)
    o_ref[...] = h_ref[...] + g2_ref[...] * moe


def _combine(ys, pos_tiles, gates, h, mod, tm):
    t, d = h.shape
    row = pl.BlockSpec((tm, d), lambda i, pos: (i, 0))
    grid_spec = pltpu.PrefetchScalarGridSpec(
        num_scalar_prefetch=1,
        grid=(t // tm,),
        in_specs=[pl.BlockSpec(memory_space=pl.ANY), pl.BlockSpec((tm, LANES), lambda i, pos: (i, 0)),
                  row, mod.spec(5)],
        out_specs=row,
        scratch_shapes=[pltpu.VMEM((tm * SUBLANES, LANES), F32), pltpu.VMEM((tm * SUBLANES, LANES), F32),
                        pltpu.SemaphoreType.DMA])
    return pl.pallas_call(
        functools.partial(_combine_body, tm=tm),
        grid_spec=grid_spec,
        out_shape=jax.ShapeDtypeStruct((t, d), F32),
        compiler_params=_params(("arbitrary",)),
        name="moe_combine",
    )(pos_tiles, ys, gates, h, mod.arr)


def _moe(hn, gates, route, counts, h, mod, w_in, w_out, layer, tm):
    t, d = h.shape
    etile = EXPERT_TILE if 2 * t >= 4 * N_EXPERTS * EXPERT_TILE else EXPERT_TILE_SMALL
    n_tiles = 2 * t // etile + N_EXPERTS
    n_rows = n_tiles * etile
    cnt = counts.reshape(N_EXPERTS).astype(jnp.int32)
    padded = (cnt + (etile - 1)) // etile * etile
    experts = jnp.arange(N_EXPERTS, dtype=jnp.int32)
    seg_end = jnp.sum(jnp.where(experts[None, :] <= experts[:, None], padded[None, :], 0), axis=1)
    seg_start = seg_end - padded
    first_row = jnp.sum(jnp.where(route[:, :2, :, None] == experts, seg_start, 0), axis=-1)
    pos_tiles = (first_row + route[:, 2:]).reshape(-1)
    tile_row = jnp.arange(n_tiles, dtype=jnp.int32) * etile
    tile_expert = jnp.minimum(jnp.sum((seg_end[None, :] <= tile_row[:, None]).astype(jnp.int32), axis=1),
                              N_EXPERTS - 1)
    n_used = seg_end[-1:] // etile
    xs = _dispatch(hn, pos_tiles, seg_end, n_rows, tm, etile)
    ys = _experts(xs, tile_expert, n_used, w_in, w_out, layer, etile)
    return _combine(ys, pos_tiles, gates, h, mod, tm)


def _head_norm(y, gsum_ref, gain_ref, head_dim):
    ssum = jnp.dot((y * y).astype(BF16), gsum_ref[...], preferred_element_type=F32)
    return y * lax.rsqrt(ssum * (1.0 / head_dim) + EPS) * gain_ref[...]


def _head_sum_matrix(d, head_dim):
    r = jnp.arange(d) // head_dim
    return (r[:, None] == r[None, :]).astype(BF16)


def _q_proj_body(x_ref, gain_ref, sh_ref, sc_ref, w_ref, gsum_ref, qn_ref, q_ref, *, head_dim, q_scale):
    xn = _norm_mod(x_ref[...], gain_ref[...], sh_ref[...], sc_ref[...]).astype(BF16)
    y = jnp.dot(xn, w_ref[...], preferred_element_type=F32)
    q = _head_norm(y, gsum_ref, qn_ref, head_dim) * q_scale
    q_ref[...] = q.astype(q_ref.dtype)


def _q_proj(h, gain, mod, w, gsum, q_norm, tm, out_dtype, q_scale):
    t, d = h.shape
    head_dim = d // FOX_HEADS
    row = pl.BlockSpec((tm, d), lambda i: (i, 0))
    const = lambda shape: pl.BlockSpec(shape, lambda i: (0,) * len(shape))
    return pl.pallas_call(
        functools.partial(_q_proj_body, head_dim=head_dim, q_scale=q_scale),
        grid=(t // tm,),
        in_specs=[row, const((1, d)), mod.spec(0), mod.spec(1), const((d, d)), const((d, d)), const((1, d))],
        out_specs=row,
        out_shape=jax.ShapeDtypeStruct((t, d), out_dtype),
        compiler_params=_params(("arbitrary",)),
        name="q_proj",
    )(h, gain.reshape(1, d), mod.arr, mod.arr, w, gsum, jnp.tile(q_norm, FOX_HEADS).reshape(1, d))


def _kv_proj_body(x_ref, gain_ref, sh_ref, sc_ref, wk_ref, wv_ref, wf_ref, gsum_ref, kn_ref, bf_ref,
                  k_ref, v_ref, kb_ref, vb_ref, lf_ref, *, head_dim):
    xn = _norm_mod(x_ref[...], gain_ref[...], sh_ref[...], sc_ref[...]).astype(BF16)
    k = _head_norm(jnp.dot(xn, wk_ref[...], preferred_element_type=F32), gsum_ref, kn_ref, head_dim)
    k_ref[...] = k
    kb_ref[...] = k.astype(BF16)
    v = jnp.dot(xn, wv_ref[...], preferred_element_type=F32)
    v_ref[...] = v
    vb_ref[...] = v.astype(BF16)
    zf = jnp.dot(xn, wf_ref[...], preferred_element_type=F32) + bf_ref[...]
    lf_ref[...] = _log_sigmoid(zf)[:, :lf_ref.shape[1]]


def _kv_proj(h, gain, mod, wk, wv, wf, gsum, k_norm, b_fgate, tm):
    t, d = h.shape
    head_dim = d // FOX_HEADS
    row = pl.BlockSpec((tm, d), lambda i: (i, 0))
    const = lambda shape: pl.BlockSpec(shape, lambda i: (0,) * len(shape))
    bf = jnp.pad(b_fgate.astype(F32), (0, LANES - FOX_HEADS)).reshape(1, LANES)
    return pl.pallas_call(
        functools.partial(_kv_proj_body, head_dim=head_dim),
        grid=(t // tm,),
        in_specs=[row, const((1, d)), mod.spec(0), mod.spec(1), const((d, d)), const((d, d)),
                  const((d, LANES)), const((d, d)), const((1, d)), const((1, LANES))],
        out_specs=[row, row, row, row, pl.BlockSpec((tm, FOX_HEADS), lambda i: (i, 0))],
        out_shape=[jax.ShapeDtypeStruct((t, d), F32), jax.ShapeDtypeStruct((t, d), F32),
                   jax.ShapeDtypeStruct((t, d), BF16), jax.ShapeDtypeStruct((t, d), BF16),
                   jax.ShapeDtypeStruct((t, FOX_HEADS), F32)],
        compiler_params=_params(("arbitrary",)),
        name="kv_proj",
    )(h, gain.reshape(1, d), mod.arr, mod.arr, wk, wv, wf, gsum,
      jnp.tile(k_norm, FOX_HEADS).reshape(1, d), bf)


def _cumsum_lanes_body(x_ref, o_ref, carry_ref, *, carry_rows):
    j = pl.program_id(1)

    @pl.when(j == 0)
    def _():
        carry_ref[...] = jnp.zeros_like(carry_ref)

    w = x_ref.shape[-1]
    r = lax.broadcasted_iota(jnp.int32, (w, w), 0)
    c = lax.broadcasted_iota(jnp.int32, (w, w), 1)
    upper = jnp.where(r <= c, 1.0, 0.0).astype(F32)
    y = jnp.dot(x_ref[0], upper, precision=HIGHEST, preferred_element_type=F32)
    if carry_rows:
        y = y + carry_ref[...]
        carry_ref[...] = y[:, w - 1:w]
    o_ref[0] = y


def _cumsum_lanes(x, width, carry):
    b, r, t = x.shape
    spec = pl.BlockSpec((1, r, width), lambda i, j: (i, 0, j))
    return pl.pallas_call(
        functools.partial(_cumsum_lanes_body, carry_rows=carry),
        grid=(b, t // width),
        in_specs=[spec],
        out_specs=spec,
        out_shape=jax.ShapeDtypeStruct(x.shape, F32),
        scratch_shapes=[pltpu.VMEM((r, 1), F32)],
        compiler_params=_params(("arbitrary", "arbitrary")),
        name="cumsum_lanes",
    )(x)


def _attn_prompt_body(qt_ref, kt_ref, q_ref, k_ref, v_ref, f_ref, o_ref, m_ref, acc_ref,
                      *, blk, rows, diag_rows, head_dim):
    p = pl.program_id(2)
    qi = qt_ref[p]
    ki = kt_ref[p]

    @pl.when(ki == 0)
    def _():
        m_ref[...] = jnp.full(m_ref.shape, NEG_BIG, F32)
        acc_ref[...] = jnp.zeros_like(acc_ref)

    def process(masked):
        k = k_ref[...]
        v = v_ref[...]
        f = f_ref[...]
        first_v = lax.broadcasted_iota(jnp.int32, v.shape, 1) < head_dim
        ones = jnp.ones_like(v)
        v_heads = (jnp.where(first_v, v, ones), jnp.where(first_v, ones, v))
        rb = min(rows, diag_rows) if masked else rows
        m_olds = {(hh, r0): m_ref[hh, r0:r0 + rb, :] for hh in range(2) for r0 in range(0, blk, rb)}
        acc_olds = {(hh, r0): acc_ref[hh, r0:r0 + rb, :] for hh in range(2) for r0 in range(0, blk, rb)}
        m_news, acc_news = {}, {}
        for r0 in range(0, blk, rb):
            n_keys = r0 + rb if masked else blk
            q = q_ref[r0:r0 + rb, :]
            first_q = lax.broadcasted_iota(jnp.int32, q.shape, 1) < head_dim
            zero = jnp.zeros_like(q)
            for hh, qh in enumerate((jnp.where(first_q, q, zero), jnp.where(first_q, zero, q))):
                s = _nt_dot(qh, k[:n_keys]) - f[hh:hh + 1, :n_keys]
                if masked:
                    row = lax.broadcasted_iota(jnp.int32, s.shape, 0) + r0
                    col = lax.broadcasted_iota(jnp.int32, s.shape, 1)
                    s = jnp.where(col <= row, s, NEG_BIG)
                m_old = m_olds[hh, r0]
                m_new = jnp.maximum(m_old, jnp.max(s, axis=1, keepdims=True))
                alpha = jnp.exp2(m_old - m_new)
                pr = jnp.concatenate([jnp.exp2(s[:, c * LANES:(c + 1) * LANES] - m_new)
                                      for c in range(n_keys // LANES)], axis=1).astype(BF16)
                acc_news[hh, r0] = alpha * acc_olds[hh, r0] + jnp.dot(pr, v_heads[hh][:n_keys],
                                                                      preferred_element_type=F32)
                m_news[hh, r0] = m_new
        for (hh, r0), m_new in m_news.items():
            m_ref[hh, r0:r0 + rb, :] = m_new
            acc_ref[hh, r0:r0 + rb, :] = acc_news[hh, r0]

    @pl.when(ki < qi)
    def _():
        process(False)

    @pl.when(ki == qi)
    def _():
        process(True)
        a0 = acc_ref[0]
        a1 = acc_ref[1]
        first = lax.broadcasted_iota(jnp.int32, a0.shape, 1) < head_dim
        inv0 = 1.0 / a0[:, head_dim:head_dim + 1]
        inv1 = 1.0 / a1[:, 0:1]
        o_ref[...] = jnp.where(first, a0 * inv0, a1 * inv1).astype(o_ref.dtype)


def _attn_prompt(q, k, v, f_pairs, nseq):
    tt, d = q.shape
    t = tt // nseq
    blk = min(ATTN_BLOCK, t)
    rows = min(ATTN_ROWS, blk)
    nb = t // blk
    head_dim = d // FOX_HEADS
    pairs = [(qi, ki) for qi in range(nb) for ki in range(qi + 1)]
    q_tab = jnp.asarray([p[0] for p in pairs], jnp.int32)
    k_tab = jnp.asarray([p[1] for p in pairs], jnp.int32)
    grid_spec = pltpu.PrefetchScalarGridSpec(
        num_scalar_prefetch=2,
        grid=(nseq, d // LANES, len(pairs)),
        in_specs=[pl.BlockSpec((blk, LANES), lambda b, hp, p, qt, kt: (b * nb + qt[p], hp)),
                  pl.BlockSpec((blk, LANES), lambda b, hp, p, qt, kt: (b * nb + kt[p], hp)),
                  pl.BlockSpec((blk, LANES), lambda b, hp, p, qt, kt: (b * nb + kt[p], hp)),
                  pl.BlockSpec((None, None, 2, blk), lambda b, hp, p, qt, kt: (b, hp, 0, kt[p]))],
        out_specs=pl.BlockSpec((blk, LANES), lambda b, hp, p, qt, kt: (b * nb + qt[p], hp)),
        scratch_shapes=[pltpu.VMEM((2, blk, LANES), F32), pltpu.VMEM((2, blk, LANES), F32)])
    return pl.pallas_call(
        functools.partial(_attn_prompt_body, blk=blk, rows=rows, diag_rows=min(ATTN_DIAG_ROWS, blk),
                          head_dim=head_dim),
        grid_spec=grid_spec,
        out_shape=jax.ShapeDtypeStruct((tt, d), BF16),
        compiler_params=_params(("arbitrary", "arbitrary", "arbitrary")),
        name="attn_prompt",
    )(q_tab, k_tab, q, k, v, f_pairs)


def _gather_cum_body(pt_ref, cp_ref, new_ref, o_ref, *, n_pages, page):
    b = pl.program_id(0)

    def step(j, carry):
        blk = cp_ref[pt_ref[b * n_pages + j]]
        o_ref[0, :, pl.ds(pl.multiple_of(j * page, page), page)] = blk + carry
        return carry + blk[:, page - 1:page]

    carry = lax.fori_loop(0, n_pages, step, jnp.zeros((cp_ref.shape[1], 1), F32), unroll=8)
    o_ref[0, :, n_pages * page:] = new_ref[0] + carry


def _gather_cum(page_table, cum_pages, cum_new):
    nseq, n_pages = page_table.shape
    n_pool, hh, page = cum_pages.shape
    grid_spec = pltpu.PrefetchScalarGridSpec(
        num_scalar_prefetch=1,
        grid=(nseq,),
        in_specs=[pl.BlockSpec((n_pool, hh, page), lambda b, pt: (0, 0, 0)),
                  pl.BlockSpec((1, hh, page), lambda b, pt: (b, 0, 0))],
        out_specs=pl.BlockSpec((1, hh, (n_pages + 1) * page), lambda b, pt: (b, 0, 0)))
    return pl.pallas_call(
        functools.partial(_gather_cum_body, n_pages=n_pages, page=page),
        grid_spec=grid_spec,
        out_shape=jax.ShapeDtypeStruct((nseq, hh, (n_pages + 1) * page), F32),
        compiler_params=_params(("arbitrary",)),
        name="gather_cum",
    )(page_table.reshape(-1), cum_pages, cum_new)


def _attn_paged_body(pt_ref, *refs, n_q, page, pps, n_steps, head_dim):
    q_ref = refs[0]
    k_refs = refs[1:1 + pps]
    v_refs = refs[1 + pps:1 + 2 * pps]
    kn_ref, vn_ref, f_ref, o_ref, qbd_ref, m_ref, l_ref, acc_ref = refs[1 + 2 * pps:]
    j = pl.program_id(1)
    d = q_ref.shape[1]
    rows = FOX_HEADS * n_q

    @pl.when(j == 0)
    def _():
        q = q_ref[...]
        lane_head = lax.broadcasted_iota(jnp.int32, q.shape, 1) // head_dim
        for h in range(FOX_HEADS):
            qbd_ref[h * n_q:(h + 1) * n_q, :] = jnp.where(lane_head == h, q, 0.0).astype(BF16)
        m_ref[...] = jnp.full(m_ref.shape, NEG_BIG, F32)
        l_ref[...] = jnp.zeros_like(l_ref)
        acc_ref[...] = jnp.zeros_like(acc_ref)

    def attend(k, v, f, mask):
        bias = jnp.concatenate([jnp.broadcast_to(f[h:h + 1, :], (n_q, page)) for h in range(FOX_HEADS)], axis=0)
        s = _nt_dot(qbd_ref[...], k) - bias
        if mask is not None:
            s = jnp.where(mask, s, NEG_BIG)
        m_old = m_ref[...]
        m_new = jnp.maximum(m_old, jnp.max(s, axis=1, keepdims=True))
        alpha = jnp.exp(m_old - m_new)
        pr = jnp.exp(s - m_new)
        l_ref[...] = alpha * l_ref[...] + jnp.sum(pr, axis=1, keepdims=True)
        m_ref[...] = m_new
        acc_ref[...] = alpha * acc_ref[...] + jnp.dot(pr.astype(BF16), v, preferred_element_type=F32)

    @pl.when(j < n_steps - 1)
    def _():
        for u in range(pps):
            off = pl.multiple_of((j * pps + u) * page, page)
            attend(k_refs[u][...].astype(BF16), v_refs[u][...].astype(BF16), f_ref[0, :, pl.ds(off, page)], None)

    @pl.when(j == n_steps - 1)
    def _():
        pad = jnp.zeros((page - n_q, d), F32)
        k = jnp.concatenate([kn_ref[...], pad], axis=0).astype(BF16)
        v = jnp.concatenate([vn_ref[...], pad], axis=0).astype(BF16)
        t_of_row = lax.broadcasted_iota(jnp.int32, (rows, page), 0) % n_q
        key = lax.broadcasted_iota(jnp.int32, (rows, page), 1)
        attend(k, v, f_ref[0, :, (n_steps - 1) * pps * page:], key <= t_of_row)
        acc = acc_ref[...] * (1.0 / l_ref[...])
        lane_head = lax.broadcasted_iota(jnp.int32, (n_q, d), 1) // head_dim
        out = jnp.zeros((n_q, d), F32)
        for h in range(FOX_HEADS):
            out = out + jnp.where(lane_head == h, acc[h * n_q:(h + 1) * n_q, :], 0.0)
        o_ref[...] = out


def _attn_paged(q, cache_k, cache_v, k_new, v_new, f_all, page_table):
    tt, d = q.shape
    nseq, n_pages = page_table.shape
    n_q = tt // nseq
    page = cache_k.shape[1]
    pps = PAGES_PER_STEP
    n_steps = n_pages // pps + 1
    head_dim = d // FOX_HEADS
    rows = FOX_HEADS * n_q

    def page_spec(u):
        def index(b, j, pt):
            return (pt[b * n_pages + jnp.minimum(j, n_steps - 2) * pps + u], 0, 0)
        return pl.BlockSpec((None, page, d), index)

    tok = pl.BlockSpec((n_q, d), lambda b, j, pt: (b, 0))
    grid_spec = pltpu.PrefetchScalarGridSpec(
        num_scalar_prefetch=1,
        grid=(nseq, n_steps),
        in_specs=([tok] + [page_spec(u) for u in range(pps)] + [page_spec(u) for u in range(pps)]
                  + [tok, tok, pl.BlockSpec((1, FOX_HEADS, f_all.shape[2]), lambda b, j, pt: (b, 0, 0))]),
        out_specs=tok,
        scratch_shapes=[pltpu.VMEM((rows, d), BF16), pltpu.VMEM((rows, 1), F32),
                        pltpu.VMEM((rows, 1), F32), pltpu.VMEM((rows, d), F32)])
    body = functools.partial(_attn_paged_body, n_q=n_q, page=page, pps=pps, n_steps=n_steps,
                             head_dim=head_dim)
    return pl.pallas_call(
        body,
        grid_spec=grid_spec,
        out_shape=jax.ShapeDtypeStruct((tt, d), F32),
        compiler_params=_params(("arbitrary", "arbitrary")),
        name="attn_paged",
    )(page_table.reshape(-1), q, *([cache_k] * pps), *([cache_v] * pps), k_new, v_new, f_all)


def _lower_bounds(lb_logits):
    p = jax.nn.softmax(lb_logits.astype(F32), axis=0)
    return jnp.cumsum(p, axis=0) - p[0]


def _trunk(x, nseq, mods, mod_kv, s0, past, w):
    tt, d = x.shape
    t = tt // nseq
    tm = min(512, tt)
    tiles_per_seq = max(t // tm, 1)
    n_a = w["w_in_a"].shape[0]
    depth = w["w_ada"].shape[0]
    head_dim = d // FOX_HEADS
    lbs = _lower_bounds(w["lb_logits"])
    mk = lambda arr: _Mod(arr, d, tm, tiles_per_seq)

    h = x
    states = []
    for l in range(depth):
        mod = mk(mods[l])
        if l < n_a:
            q, zf, i, g = _hgrn_in_proj(h, w["norm_mix"][l], mod, w["w_in_a"], l, tm)
            tp = -(-t // SCAN_CHUNK) * SCAN_CHUNK
            if tp != t:
                padr = lambda a: jnp.pad(a.reshape(nseq, t, d), ((0, 0), (0, tp - t), (0, 0))).reshape(nseq * tp, d)
                q, zf, i, g = padr(q), padr(zf), padr(i), padr(g)
            o, s_t = _hgrn_scan(q, zf, i, g, lbs[l], w["gnorm_a"][l], s0[l], nseq, t)
            if tp != t:
                o = o.reshape(nseq, tp, d)[:, :t].reshape(tt, d)
            states.append(s_t)
            mix_in, w_out = o, w["w_out_a"][l]
        else:
            j = l - n_a
            if past is None:
                q = _q_proj(h, w["norm_mix"][l], mod, w["w_q_b"][j], w["gsum"], w["q_norm"][j], tm, BF16,
                            head_dim ** -0.5 * LOG2E)
                mix_in = _attn_prompt(q, k_bf, v_bf, f_keys, nseq)
            else:
                q = _q_proj(h, w["norm_mix"][l], mod, w["w_q_b"][j], w["gsum"], w["q_norm"][j], tm, F32,
                            head_dim ** -0.5)
                mix_in = _attn_paged(q, past[0], past[1], k_new, v_new, f_keys, past[3])
            w_out = w["w_out_b"][j]
        h, hn, gates, route, counts = _mix_router(mix_in, w_out, h, w["norm_ffn"][l], mod, w["wr_hi"],
                                                  w["wr_lo"], w["b_router"], tm)
        h = _moe(hn, gates, route, counts, h, mod, w["w_exp_in"], w["w_exp_out"], l, tm)
        if l == n_a - 1:
            k_new, v_new, k_bf, v_bf, logf_new = _kv_proj(
                h, w["norm_kv"], mk(mod_kv), w["w_k"], w["w_v"], w["w_f"], w["gsum"], w["k_norm"],
                w["b_fgate"], tm)
            lf_t = logf_new.reshape(nseq, t, FOX_HEADS).transpose(0, 2, 1)
            if past is None:
                f_cum = _cumsum_lanes(lf_t, min(512, t), carry=True)
                f_keys = (f_cum * LOG2E).reshape(nseq, FOX_HEADS // 2, 2, t)
            else:
                cache_logf, page_table = past[2], past[3]
                page = cache_logf.shape[1]
                pool_t = cache_logf.astype(F32).transpose(0, 2, 1)
                n_pool = pool_t.shape[0]
                rows = 64 if n_pool % 64 == 0 else 1
                cum_pages = _cumsum_lanes(pool_t.reshape(n_pool // rows, rows * FOX_HEADS, page), page,
                                          carry=False).reshape(n_pool, FOX_HEADS, page)
                new_pad = jnp.pad(lf_t, ((0, 0), (0, 0), (0, page - t)))
                cum_new = _cumsum_lanes(new_pad, page, carry=False)
                f_keys = _gather_cum(page_table, cum_pages, cum_new)
    return h, jnp.stack(states), k_new, v_new, logf_new


def kernel(x_prompt, x_sample, cache_k, cache_v, cache_logf, state_hgrn, page_table, c_prompt, c_sample,
           w_ada, b_ada, norm_mix, norm_ffn, w_in_a, lb_logits, gnorm_a, w_out_a, norm_kv, w_ada_kv,
           b_ada_kv, w_kv, b_fgate, k_norm, w_q_b, q_norm, w_out_b, w_router, b_router, w_exp_in, w_exp_out):
    nb, seq, d = x_prompt.shape
    ns, dseq, _ = x_sample.shape
    n_a = w_in_a.shape[0]
    depth = w_ada.shape[0]
    hd = FOX_HEADS * (d // FOX_HEADS)
    dk = d // HG_HEADS

    nrow = nb + ns
    nrow_pad = -(-nrow // SUBLANES) * SUBLANES
    c_all = jnp.pad(jnp.concatenate([c_prompt, c_sample], axis=0).astype(F32), ((0, nrow_pad - nrow), (0, 0)))
    mod_all = _ada(c_all, w_ada, b_ada)
    mod_kv_all = _ada(c_all, w_ada_kv[None], b_ada_kv[None])[0]
    mods_p = [mod_all[l, :nb].reshape(nb, 1, 6 * d) for l in range(depth)]
    mods_s = [jnp.repeat(mod_all[l, nb:nrow], dseq, axis=0) for l in range(depth)]
    mod_kv_p = mod_kv_all[:nb].reshape(nb, 1, 2 * d)
    mod_kv_s = jnp.repeat(mod_kv_all[nb:nrow], dseq, axis=0)

    wr_t = w_router.astype(F32).T
    wr_hi = wr_t.astype(BF16)
    wr_lo = (wr_t - wr_hi.astype(F32)).astype(BF16)
    w = dict(
        w_ada=w_ada, lb_logits=lb_logits, norm_mix=norm_mix, norm_ffn=norm_ffn, gnorm_a=gnorm_a,
        w_in_a=w_in_a.astype(BF16), w_out_a=w_out_a.astype(BF16), norm_kv=norm_kv,
        w_k=w_kv[:, :hd].astype(BF16), w_v=w_kv[:, hd:2 * hd].astype(BF16),
        w_f=jnp.pad(w_kv[:, 2 * hd:], ((0, 0), (0, LANES - FOX_HEADS))).astype(BF16),
        b_fgate=b_fgate, k_norm=k_norm, w_q_b=w_q_b.astype(BF16), q_norm=q_norm,
        w_out_b=w_out_b.astype(BF16), wr_hi=wr_hi, wr_lo=wr_lo, b_router=b_router,
        w_exp_in=w_exp_in, w_exp_out=w_exp_out,
        gsum=_head_sum_matrix(d, d // FOX_HEADS),
    )

    s0_p = jnp.zeros((n_a, nb, HG_HEADS, dk, dk), F32)
    y_p, st_p, k_p, v_p, lf_p = _trunk(x_prompt.reshape(nb * seq, d), nb, mods_p, mod_kv_p, s0_p, None, w)

    n_pool, page = cache_k.shape[0], cache_k.shape[1]
    past = (cache_k.reshape(n_pool, page, hd), cache_v.reshape(n_pool, page, hd), cache_logf, page_table)
    y_s, st_s, k_s, v_s, lf_s = _trunk(x_sample.reshape(ns * dseq, d), ns, mods_s, mod_kv_s,
                                       state_hgrn.astype(F32), past, w)

    hs = (FOX_HEADS, d // FOX_HEADS)
    return (y_p.reshape(nb, seq, d), y_s.reshape(ns, dseq, d),
            st_p.astype(state_hgrn.dtype), st_s.astype(state_hgrn.dtype),
            k_p.reshape(nb, seq, *hs).astype(cache_k.dtype), v_p.reshape(nb, seq, *hs).astype(cache_v.dtype),
            lf_p.reshape(nb, seq, FOX_HEADS).astype(cache_logf.dtype),
            k_s.reshape(ns, dseq, *hs).astype(cache_k.dtype), v_s.reshape(ns, dseq, *hs).astype(cache_v.dtype),
            lf_s.reshape(ns, dseq, FOX_HEADS).astype(cache_logf.dtype))
```
